```python
import math
import jax, jax.numpy as jnp
from jax import lax
import numpy as np

D_MODEL = 1024
BATCH = 32
SEQ = 2048
DEPTH = 4

N_MIXERS = 2
N_S5_LAYERS = (DEPTH + 1) // 2
N_POOL_LAYERS = DEPTH // 2
S5_GROUP_CH = 16
S5_GROUPS = D_MODEL // S5_GROUP_CH
S5_STATE = 64
S5_CHUNK = 128
DT_MIN = 1e-3
DT_MAX = 1e-1
POOL_WINDOWS = (2, 4, 8, 16)
POOL_GROUPS = len(POOL_WINDOWS)
POOL_GROUP_CH = D_MODEL // POOL_GROUPS
D_FF = 2816
CONV_WIDTH = 3
RMS_EPS = 1e-6

kernel_name = "hybrid_s5_pool_convffn"


def rms_norm(x, gain):
    xf = x.astype(jnp.float32)
    y = xf * lax.rsqrt(jnp.mean(xf * xf, axis=-1, keepdims=True) + RMS_EPS)
    return (y * gain.astype(jnp.float32)).astype(x.dtype)


def _complex_affine_combine(e1, e2):
    a1r, a1i, b1r, b1i = e1
    a2r, a2i, b2r, b2i = e2
    ar = a2r * a1r - a2i * a1i
    ai = a2r * a1i + a2i * a1r
    br = a2r * b1r - a2i * b1i + b2r
    bi = a2r * b1i + a2i * b1r + b2i
    return ar, ai, br, bi


def s5_mixer(u, lam_re, lam_im, log_dt, b_re, b_im, c_re, c_im, d_skip, w_glu, b_glu):
    f32 = jnp.float32
    bsz, l, d = u.shape
    lr, li = lam_re.astype(f32), lam_im.astype(f32)
    dt = jnp.exp(log_dt.astype(f32))[:, None]
    mag = jnp.exp(lr * dt)
    ab_re = mag * jnp.cos(li * dt)
    ab_im = mag * jnp.sin(li * dt)
    den = lr * lr + li * li
    nr = ab_re - 1.0
    ni = ab_im
    f_re = ((nr * lr + ni * li) / den)[..., None]
    f_im = ((ni * lr - nr * li) / den)[..., None]
    br, bi = b_re.astype(f32), b_im.astype(f32)
    bb_re = f_re * br - f_im * bi
    bb_im = f_re * bi + f_im * br
    cr, ci = c_re.astype(f32), c_im.astype(f32)

    n_chunks = l // S5_CHUNK
    uf = u.astype(f32)
    u_chunks = uf.reshape(bsz, n_chunks, S5_CHUNK, S5_GROUPS, S5_GROUP_CH).transpose(1, 0, 2, 3, 4)
    a_re = jnp.broadcast_to(ab_re, (1, S5_CHUNK, S5_GROUPS, S5_STATE))
    a_im = jnp.broadcast_to(ab_im, (1, S5_CHUNK, S5_GROUPS, S5_STATE))

    def step(carry, u_c):
        h0r, h0i = carry
        xr = jnp.einsum('bcgh,gph->bcgp', u_c, bb_re)
        xi = jnp.einsum('bcgh,gph->bcgp', u_c, bb_im)
        pr, pi_, sr, si = lax.associative_scan(_complex_affine_combine, (a_re, a_im, xr, xi), axis=1)
        h0r_, h0i_ = h0r[:, None], h0i[:, None]
        hr = sr + pr * h0r_ - pi_ * h0i_
        hi = si + pr * h0i_ + pi_ * h0r_
        y = jnp.einsum('bcgp,ghp->bcgh', hr, cr) - jnp.einsum('bcgp,ghp->bcgh', hi, ci)
        return (hr[:, -1], hi[:, -1]), y

    init = (jnp.zeros((bsz, S5_GROUPS, S5_STATE), f32), jnp.zeros((bsz, S5_GROUPS, S5_STATE), f32))
    _, ys = lax.scan(step, init, u_chunks)
    y = ys.transpose(1, 0, 2, 3, 4).reshape(bsz, l, d) + d_skip.astype(f32) * uf
    y = jax.nn.gelu(y)
    out = y * jax.nn.sigmoid(y @ w_glu.astype(f32) + b_glu.astype(f32))
    return out.astype(u.dtype)


def pool_mixer(u, w_group, scale):
    f32 = jnp.float32
    bsz, l, d = u.shape
    uf = u.astype(f32)
    cs = jnp.concatenate([jnp.zeros((bsz, 1, d), f32), jnp.cumsum(uf, axis=1)], axis=1)
    t = jnp.arange(1, l + 1, dtype=f32)
    pooled = []
    for g, win in enumerate(POOL_WINDOWS):
        csg = cs[:, :, g * POOL_GROUP_CH:(g + 1) * POOL_GROUP_CH]
        upper = csg[:, 1:]
        lower = jnp.pad(csg, ((0, 0), (win - 1, 0), (0, 0)))[:, :l]
        count = jnp.minimum(t, float(win))[None, :, None]
        pooled.append((upper - lower) / count)
    pooled = jnp.stack(pooled, axis=2)
    diff = pooled - uf.reshape(bsz, l, POOL_GROUPS, POOL_GROUP_CH)
    out = jnp.einsum('bsgc,gcd->bsgd', diff, w_group.astype(f32)).reshape(bsz, l, d)
    return (out * scale.astype(f32)).astype(u.dtype)


def conv_ffn(u, w_gate, w_val, conv_w, conv_b, w_down):
    l = u.shape[1]
    g = u @ w_gate
    gp = jnp.pad(g, ((0, 0), (CONV_WIDTH - 1, 0), (0, 0)))
    gc = conv_b + conv_w[0] * gp[:, 0:l]
    for k in range(1, CONV_WIDTH):
        gc = gc + conv_w[k] * gp[:, k:k + l]
    hdn = jax.nn.gelu(gc) * (u @ w_val)
    return hdn @ w_down


def _fwd_setup_inputs(seed: int = 0) -> dict:
    key = jax.random.key(seed)
    ks = jax.random.split(key, 22)
    f32 = jnp.float32
    na, nb = N_S5_LAYERS, N_POOL_LAYERS
    G, P, H, D, F = S5_GROUPS, S5_STATE, S5_GROUP_CH, D_MODEL, D_FF
    nrm = lambda k, s: jax.random.normal(k, s, f32)
    n_idx = jnp.arange(P, dtype=f32)
    return {
        "x": nrm(ks[0], (BATCH, SEQ, D)),
        "s5_lambda_re": -0.5 + 0.01 * nrm(ks[1], (na, G, P)),
        "s5_lambda_im": math.pi * n_idx + 0.01 * nrm(ks[2], (na, G, P)),
        "s5_log_dt": jax.random.uniform(ks[3], (na, G), f32, math.log(DT_MIN), math.log(DT_MAX)),
        "s5_b_re": nrm(ks[4], (na, G, P, H)) * (2 * H) ** -0.5,
        "s5_b_im": nrm(ks[5], (na, G, P, H)) * (2 * H) ** -0.5,
        "s5_c_re": nrm(ks[6], (na, G, H, P)) * P ** -0.5,
        "s5_c_im": nrm(ks[7], (na, G, H, P)) * P ** -0.5,
        "s5_d": nrm(ks[8], (na, D)),
        "s5_w_glu": nrm(ks[9], (na, D, D)) * D ** -0.5,
        "s5_b_glu": 0.01 * nrm(ks[10], (na, D)),
        "pool_w": nrm(ks[11], (nb, POOL_GROUPS, POOL_GROUP_CH, POOL_GROUP_CH)) * POOL_GROUP_CH ** -0.5,
        "pool_scale": 1.0 + 0.1 * nrm(ks[12], (nb, D)),
        "ffn_w_gate": nrm(ks[13], (DEPTH, D, F)) * D ** -0.5,
        "ffn_w_val": nrm(ks[14], (DEPTH, D, F)) * D ** -0.5,
        "ffn_conv_w": nrm(ks[15], (DEPTH, CONV_WIDTH, F)) * CONV_WIDTH ** -0.5,
        "ffn_conv_b": 0.01 * nrm(ks[16], (DEPTH, F)),
        "ffn_w_down": nrm(ks[17], (DEPTH, F, D)) * F ** -0.5,
        "norm_mix_pre": 1.0 + 0.05 * nrm(ks[18], (DEPTH, D)),
        "norm_mix_post": 1.0 + 0.05 * nrm(ks[19], (DEPTH, D)),
        "norm_ffn_pre": 1.0 + 0.05 * nrm(ks[20], (DEPTH, D)),
        "norm_ffn_post": 1.0 + 0.05 * nrm(ks[21], (DEPTH, D)),
    }


def _fwd_reference(x, s5_lambda_re, s5_lambda_im, s5_log_dt, s5_b_re, s5_b_im, s5_c_re, s5_c_im,
              s5_d, s5_w_glu, s5_b_glu, pool_w, pool_scale, ffn_w_gate, ffn_w_val, ffn_conv_w,
              ffn_conv_b, ffn_w_down, norm_mix_pre, norm_mix_post, norm_ffn_pre, norm_ffn_post):
    for i in range(DEPTH):
        j = i // N_MIXERS
        h = rms_norm(x, norm_mix_pre[i])
        if i % N_MIXERS == 0:
            m = s5_mixer(h, s5_lambda_re[j], s5_lambda_im[j], s5_log_dt[j], s5_b_re[j], s5_b_im[j],
                         s5_c_re[j], s5_c_im[j], s5_d[j], s5_w_glu[j], s5_b_glu[j])
        else:
            m = pool_mixer(h, pool_w[j], pool_scale[j])
        x = x + rms_norm(m, norm_mix_post[i])
        h = rms_norm(x, norm_ffn_pre[i])
        f = conv_ffn(h, ffn_w_gate[i], ffn_w_val[i], ffn_conv_w[i], ffn_conv_b[i], ffn_w_down[i])
        x = x + rms_norm(f, norm_ffn_post[i])
    return x


import jax as _jax
import jax.numpy as _jnp

TWIN_FORMAT = 'train_step'
FWD_PARAMS = ['x', 's5_lambda_re', 's5_lambda_im', 's5_log_dt', 's5_b_re', 's5_b_im', 's5_c_re', 's5_c_im', 's5_d', 's5_w_glu', 's5_b_glu', 'pool_w', 'pool_scale', 'ffn_w_gate', 'ffn_w_val', 'ffn_conv_w', 'ffn_conv_b', 'ffn_w_down', 'norm_mix_pre', 'norm_mix_post', 'norm_ffn_pre', 'norm_ffn_post']
TWIN_WEIGHTS = ['s5_lambda_re', 's5_lambda_im', 's5_log_dt', 's5_b_re', 's5_b_im', 's5_c_re', 's5_c_im', 's5_d', 's5_w_glu', 's5_b_glu', 'pool_w', 'pool_scale', 'ffn_w_gate', 'ffn_w_val', 'ffn_conv_w', 'ffn_conv_b', 'ffn_w_down', 'norm_mix_pre', 'norm_mix_post', 'norm_ffn_pre', 'norm_ffn_post']
TWIN_DIFF_INPUT = 'x'
TWIN_INPUTS = ['x', 's5_lambda_re', 's5_lambda_im', 's5_log_dt', 's5_b_re', 's5_b_im', 's5_c_re', 's5_c_im', 's5_d', 's5_w_glu', 's5_b_glu', 'pool_w', 'pool_scale', 'ffn_w_gate', 'ffn_w_val', 'ffn_conv_w', 'ffn_conv_b', 'ffn_w_down', 'norm_mix_pre', 'norm_mix_post', 'norm_ffn_pre', 'norm_ffn_post', 'loss_target', 'm_s5_lambda_re', 'm_s5_lambda_im', 'm_s5_log_dt', 'm_s5_b_re', 'm_s5_b_im', 'm_s5_c_re', 'm_s5_c_im', 'm_s5_d', 'm_s5_w_glu', 'm_s5_b_glu', 'm_pool_w', 'm_pool_scale', 'm_ffn_w_gate', 'm_ffn_w_val', 'm_ffn_conv_w', 'm_ffn_conv_b', 'm_ffn_w_down', 'm_norm_mix_pre', 'm_norm_mix_post', 'm_norm_ffn_pre', 'm_norm_ffn_post', 'v_s5_lambda_re', 'v_s5_lambda_im', 'v_s5_log_dt', 'v_s5_b_re', 'v_s5_b_im', 'v_s5_c_re', 'v_s5_c_im', 'v_s5_d', 'v_s5_w_glu', 'v_s5_b_glu', 'v_pool_w', 'v_pool_scale', 'v_ffn_w_gate', 'v_ffn_w_val', 'v_ffn_conv_w', 'v_ffn_conv_b', 'v_ffn_w_down', 'v_norm_mix_pre', 'v_norm_mix_post', 'v_norm_ffn_pre', 'v_norm_ffn_post']
TWIN_OUTPUTS = ['loss', 'grad_x', 'grad_s5_lambda_re', 'grad_s5_lambda_im', 'grad_s5_log_dt', 'grad_s5_b_re', 'grad_s5_b_im', 'grad_s5_c_re', 'grad_s5_c_im', 'grad_s5_d', 'grad_s5_w_glu', 'grad_s5_b_glu', 'grad_pool_w', 'grad_pool_scale', 'grad_ffn_w_gate', 'grad_ffn_w_val', 'grad_ffn_conv_w', 'grad_ffn_conv_b', 'grad_ffn_w_down', 'grad_norm_mix_pre', 'grad_norm_mix_post', 'grad_norm_ffn_pre', 'grad_norm_ffn_post', 'delta_s5_lambda_re', 'delta_s5_lambda_im', 'delta_s5_log_dt', 'delta_s5_b_re', 'delta_s5_b_im', 'delta_s5_c_re', 'delta_s5_c_im', 'delta_s5_d', 'delta_s5_w_glu', 'delta_s5_b_glu', 'delta_pool_w', 'delta_pool_scale', 'delta_ffn_w_gate', 'delta_ffn_w_val', 'delta_ffn_conv_w', 'delta_ffn_conv_b', 'delta_ffn_w_down', 'delta_norm_mix_pre', 'delta_norm_mix_post', 'delta_norm_ffn_pre', 'delta_norm_ffn_post', 'new_m_s5_lambda_re', 'new_m_s5_lambda_im', 'new_m_s5_log_dt', 'new_m_s5_b_re', 'new_m_s5_b_im', 'new_m_s5_c_re', 'new_m_s5_c_im', 'new_m_s5_d', 'new_m_s5_w_glu', 'new_m_s5_b_glu', 'new_m_pool_w', 'new_m_pool_scale', 'new_m_ffn_w_gate', 'new_m_ffn_w_val', 'new_m_ffn_conv_w', 'new_m_ffn_conv_b', 'new_m_ffn_w_down', 'new_m_norm_mix_pre', 'new_m_norm_mix_post', 'new_m_norm_ffn_pre', 'new_m_norm_ffn_post', 'new_v_s5_lambda_re', 'new_v_s5_lambda_im', 'new_v_s5_log_dt', 'new_v_s5_b_re', 'new_v_s5_b_im', 'new_v_s5_c_re', 'new_v_s5_c_im', 'new_v_s5_d', 'new_v_s5_w_glu', 'new_v_s5_b_glu', 'new_v_pool_w', 'new_v_pool_scale', 'new_v_ffn_w_gate', 'new_v_ffn_w_val', 'new_v_ffn_conv_w', 'new_v_ffn_conv_b', 'new_v_ffn_w_down', 'new_v_norm_mix_pre', 'new_v_norm_mix_post', 'new_v_norm_ffn_pre', 'new_v_norm_ffn_post']
TWIN_LEAF_KINDS = {'loss': 'loss', 'grad_x': 'grad_x', 'grad_s5_lambda_re': 'grad_w', 'grad_s5_lambda_im': 'grad_w', 'grad_s5_log_dt': 'grad_w', 'grad_s5_b_re': 'grad_w', 'grad_s5_b_im': 'grad_w', 'grad_s5_c_re': 'grad_w', 'grad_s5_c_im': 'grad_w', 'grad_s5_d': 'grad_w', 'grad_s5_w_glu': 'grad_w', 'grad_s5_b_glu': 'grad_w', 'grad_pool_w': 'grad_w', 'grad_pool_scale': 'grad_w', 'grad_ffn_w_gate': 'grad_w', 'grad_ffn_w_val': 'grad_w', 'grad_ffn_conv_w': 'grad_w', 'grad_ffn_conv_b': 'grad_w', 'grad_ffn_w_down': 'grad_w', 'grad_norm_mix_pre': 'grad_w', 'grad_norm_mix_post': 'grad_w', 'grad_norm_ffn_pre': 'grad_w', 'grad_norm_ffn_post': 'grad_w', 'delta_s5_lambda_re': 'delta_w', 'delta_s5_lambda_im': 'delta_w', 'delta_s5_log_dt': 'delta_w', 'delta_s5_b_re': 'delta_w', 'delta_s5_b_im': 'delta_w', 'delta_s5_c_re': 'delta_w', 'delta_s5_c_im': 'delta_w', 'delta_s5_d': 'delta_w', 'delta_s5_w_glu': 'delta_w', 'delta_s5_b_glu': 'delta_w', 'delta_pool_w': 'delta_w', 'delta_pool_scale': 'delta_w', 'delta_ffn_w_gate': 'delta_w', 'delta_ffn_w_val': 'delta_w', 'delta_ffn_conv_w': 'delta_w', 'delta_ffn_conv_b': 'delta_w', 'delta_ffn_w_down': 'delta_w', 'delta_norm_mix_pre': 'delta_w', 'delta_norm_mix_post': 'delta_w', 'delta_norm_ffn_pre': 'delta_w', 'delta_norm_ffn_post': 'delta_w', 'new_m_s5_lambda_re': 'new_m', 'new_m_s5_lambda_im': 'new_m', 'new_m_s5_log_dt': 'new_m', 'new_m_s5_b_re': 'new_m', 'new_m_s5_b_im': 'new_m', 'new_m_s5_c_re': 'new_m', 'new_m_s5_c_im': 'new_m', 'new_m_s5_d': 'new_m', 'new_m_s5_w_glu': 'new_m', 'new_m_s5_b_glu': 'new_m', 'new_m_pool_w': 'new_m', 'new_m_pool_scale': 'new_m', 'new_m_ffn_w_gate': 'new_m', 'new_m_ffn_w_val': 'new_m', 'new_m_ffn_conv_w': 'new_m', 'new_m_ffn_conv_b': 'new_m', 'new_m_ffn_w_down': 'new_m', 'new_m_norm_mix_pre': 'new_m', 'new_m_norm_mix_post': 'new_m', 'new_m_norm_ffn_pre': 'new_m', 'new_m_norm_ffn_post': 'new_m', 'new_v_s5_lambda_re': 'new_v', 'new_v_s5_lambda_im': 'new_v', 'new_v_s5_log_dt': 'new_v', 'new_v_s5_b_re': 'new_v', 'new_v_s5_b_im': 'new_v', 'new_v_s5_c_re': 'new_v', 'new_v_s5_c_im': 'new_v', 'new_v_s5_d': 'new_v', 'new_v_s5_w_glu': 'new_v', 'new_v_s5_b_glu': 'new_v', 'new_v_pool_w': 'new_v', 'new_v_pool_scale': 'new_v', 'new_v_ffn_w_gate': 'new_v', 'new_v_ffn_w_val': 'new_v', 'new_v_ffn_conv_w': 'new_v', 'new_v_ffn_conv_b': 'new_v', 'new_v_ffn_w_down': 'new_v', 'new_v_norm_mix_pre': 'new_v', 'new_v_norm_mix_post': 'new_v', 'new_v_norm_ffn_pre': 'new_v', 'new_v_norm_ffn_post': 'new_v'}


def _forward(args):
    return _fwd_reference(*[args[k] for k in FWD_PARAMS])


def _output_shape():
    out = _jax.eval_shape(lambda: _forward(_fwd_setup_inputs(0)))
    return out.shape, out.dtype

N_MICROBATCH = 1
ADAM_LR = 0.001
ADAM_B1 = 0.9
ADAM_B2 = 0.999
ADAM_EPS = 1e-08
ADAM_WD = 0.01
ADAM_STEP = 10
PER_EXAMPLE_BATCH_AXIS = {'x': 0, 'loss_target': 0}
SHARED_INPUTS = []
_WEIGHT_DTYPES = {'s5_lambda_re': _jnp.float32, 's5_lambda_im': _jnp.float32, 's5_log_dt': _jnp.float32, 's5_b_re': _jnp.float32, 's5_b_im': _jnp.float32, 's5_c_re': _jnp.float32, 's5_c_im': _jnp.float32, 's5_d': _jnp.float32, 's5_w_glu': _jnp.float32, 's5_b_glu': _jnp.float32, 'pool_w': _jnp.float32, 'pool_scale': _jnp.float32, 'ffn_w_gate': _jnp.float32, 'ffn_w_val': _jnp.float32, 'ffn_conv_w': _jnp.float32, 'ffn_conv_b': _jnp.float32, 'ffn_w_down': _jnp.float32, 'norm_mix_pre': _jnp.float32, 'norm_mix_post': _jnp.float32, 'norm_ffn_pre': _jnp.float32, 'norm_ffn_post': _jnp.float32}
MOMENT_SCALE = {'s5_lambda_re': 1.326035e+00, 's5_lambda_im': 1.518152e+00, 's5_log_dt': 1.156743e+02, 's5_b_re': 1.168149e+00, 's5_b_im': 1.222877e+00, 's5_c_re': 1.523605e+00, 's5_c_im': 1.732840e+00, 's5_d': 9.950132e+01, 's5_w_glu': 1.363812e+01, 's5_b_glu': 4.316114e+01, 'pool_w': 1.014459e+01, 'pool_scale': 3.158051e+01, 'ffn_w_gate': 2.131704e+00, 'ffn_w_val': 5.702995e+00, 'ffn_conv_w': 4.266754e+00, 'ffn_conv_b': 9.475842e+00, 'ffn_w_down': 9.492805e+00, 'norm_mix_pre': 7.257260e+01, 'norm_mix_post': 1.515503e+02, 'norm_ffn_pre': 9.960445e+00, 'norm_ffn_post': 7.056428e+01}


def _to_microbatches(a, axis):
    t = _jnp.moveaxis(a, axis, 0)
    t = t.reshape((N_MICROBATCH, t.shape[0] // N_MICROBATCH) + t.shape[1:])
    return _jnp.moveaxis(t, 1, axis + 1)


def setup_inputs(seed: int = 0) -> dict:
    inp = _fwd_setup_inputs(seed)
    key = _jax.random.fold_in(_jax.random.key(seed), 7919)
    shape, _ = _output_shape()
    out = dict(inp)
    out["loss_target"] = _jax.random.normal(_jax.random.fold_in(key, 0), shape, _jnp.float32)
    for i, name in enumerate(TWIN_WEIGHTS):
        w = inp[name].astype(_jnp.float32)
        if MOMENT_SCALE is None:
            s = _jnp.sqrt(_jnp.mean(_jnp.square(w)) + 1e-30)
        else:
            s = MOMENT_SCALE[name]
        km, kv = _jax.random.split(_jax.random.fold_in(key, i + 1))
        out[name] = w
        out["m_" + name] = s * _jax.random.normal(km, w.shape, _jnp.float32)
        out["v_" + name] = (s * s) * _jax.random.uniform(kv, w.shape, _jnp.float32, 0.5, 1.5)
    if N_MICROBATCH > 1:
        for name, axis in PER_EXAMPLE_BATCH_AXIS.items():
            out[name] = _to_microbatches(out[name], axis)
    return {'x': out['x'], 's5_lambda_re': out['s5_lambda_re'], 's5_lambda_im': out['s5_lambda_im'], 's5_log_dt': out['s5_log_dt'], 's5_b_re': out['s5_b_re'], 's5_b_im': out['s5_b_im'], 's5_c_re': out['s5_c_re'], 's5_c_im': out['s5_c_im'], 's5_d': out['s5_d'], 's5_w_glu': out['s5_w_glu'], 's5_b_glu': out['s5_b_glu'], 'pool_w': out['pool_w'], 'pool_scale': out['pool_scale'], 'ffn_w_gate': out['ffn_w_gate'], 'ffn_w_val': out['ffn_w_val'], 'ffn_conv_w': out['ffn_conv_w'], 'ffn_conv_b': out['ffn_conv_b'], 'ffn_w_down': out['ffn_w_down'], 'norm_mix_pre': out['norm_mix_pre'], 'norm_mix_post': out['norm_mix_post'], 'norm_ffn_pre': out['norm_ffn_pre'], 'norm_ffn_post': out['norm_ffn_post'], 'loss_target': out['loss_target'], 'm_s5_lambda_re': out['m_s5_lambda_re'], 'm_s5_lambda_im': out['m_s5_lambda_im'], 'm_s5_log_dt': out['m_s5_log_dt'], 'm_s5_b_re': out['m_s5_b_re'], 'm_s5_b_im': out['m_s5_b_im'], 'm_s5_c_re': out['m_s5_c_re'], 'm_s5_c_im': out['m_s5_c_im'], 'm_s5_d': out['m_s5_d'], 'm_s5_w_glu': out['m_s5_w_glu'], 'm_s5_b_glu': out['m_s5_b_glu'], 'm_pool_w': out['m_pool_w'], 'm_pool_scale': out['m_pool_scale'], 'm_ffn_w_gate': out['m_ffn_w_gate'], 'm_ffn_w_val': out['m_ffn_w_val'], 'm_ffn_conv_w': out['m_ffn_conv_w'], 'm_ffn_conv_b': out['m_ffn_conv_b'], 'm_ffn_w_down': out['m_ffn_w_down'], 'm_norm_mix_pre': out['m_norm_mix_pre'], 'm_norm_mix_post': out['m_norm_mix_post'], 'm_norm_ffn_pre': out['m_norm_ffn_pre'], 'm_norm_ffn_post': out['m_norm_ffn_post'], 'v_s5_lambda_re': out['v_s5_lambda_re'], 'v_s5_lambda_im': out['v_s5_lambda_im'], 'v_s5_log_dt': out['v_s5_log_dt'], 'v_s5_b_re': out['v_s5_b_re'], 'v_s5_b_im': out['v_s5_b_im'], 'v_s5_c_re': out['v_s5_c_re'], 'v_s5_c_im': out['v_s5_c_im'], 'v_s5_d': out['v_s5_d'], 'v_s5_w_glu': out['v_s5_w_glu'], 'v_s5_b_glu': out['v_s5_b_glu'], 'v_pool_w': out['v_pool_w'], 'v_pool_scale': out['v_pool_scale'], 'v_ffn_w_gate': out['v_ffn_w_gate'], 'v_ffn_w_val': out['v_ffn_w_val'], 'v_ffn_conv_w': out['v_ffn_conv_w'], 'v_ffn_conv_b': out['v_ffn_conv_b'], 'v_ffn_w_down': out['v_ffn_w_down'], 'v_norm_mix_pre': out['v_norm_mix_pre'], 'v_norm_mix_post': out['v_norm_mix_post'], 'v_norm_ffn_pre': out['v_norm_ffn_pre'], 'v_norm_ffn_post': out['v_norm_ffn_post']}


def _loss(weights, diff, rest, loss_target):
    with _jax.named_scope("forward"):
        args = {**rest, TWIN_DIFF_INPUT: diff, **{k: w.astype(_WEIGHT_DTYPES[k]) for k, w in weights.items()}}
        y = _forward(args)
    with _jax.named_scope("loss_head"):
        err = _jnp.square(y.astype(_jnp.float32) - loss_target)
        return 0.5 * _jnp.sum(_jnp.mean(err, axis=-1)) if err.ndim else 0.5 * err


def _adamw(w, g, m, v):
    m = ADAM_B1 * m + (1.0 - ADAM_B1) * g
    v = ADAM_B2 * v + (1.0 - ADAM_B2) * _jnp.square(g)
    m_hat = m / (1.0 - ADAM_B1 ** ADAM_STEP)
    v_hat = v / (1.0 - ADAM_B2 ** ADAM_STEP)
    delta = -ADAM_LR * (m_hat / (_jnp.sqrt(v_hat) + ADAM_EPS) + ADAM_WD * w)
    return delta, m, v


def reference(x, s5_lambda_re, s5_lambda_im, s5_log_dt, s5_b_re, s5_b_im, s5_c_re, s5_c_im, s5_d, s5_w_glu, s5_b_glu, pool_w, pool_scale, ffn_w_gate, ffn_w_val, ffn_conv_w, ffn_conv_b, ffn_w_down, norm_mix_pre, norm_mix_post, norm_ffn_pre, norm_ffn_post, loss_target, m_s5_lambda_re, m_s5_lambda_im, m_s5_log_dt, m_s5_b_re, m_s5_b_im, m_s5_c_re, m_s5_c_im, m_s5_d, m_s5_w_glu, m_s5_b_glu, m_pool_w, m_pool_scale, m_ffn_w_gate, m_ffn_w_val, m_ffn_conv_w, m_ffn_conv_b, m_ffn_w_down, m_norm_mix_pre, m_norm_mix_post, m_norm_ffn_pre, m_norm_ffn_post, v_s5_lambda_re, v_s5_lambda_im, v_s5_log_dt, v_s5_b_re, v_s5_b_im, v_s5_c_re, v_s5_c_im, v_s5_d, v_s5_w_glu, v_s5_b_glu, v_pool_w, v_pool_scale, v_ffn_w_gate, v_ffn_w_val, v_ffn_conv_w, v_ffn_conv_b, v_ffn_w_down, v_norm_mix_pre, v_norm_mix_post, v_norm_ffn_pre, v_norm_ffn_post):
    given = dict(x=x, s5_lambda_re=s5_lambda_re, s5_lambda_im=s5_lambda_im, s5_log_dt=s5_log_dt, s5_b_re=s5_b_re, s5_b_im=s5_b_im, s5_c_re=s5_c_re, s5_c_im=s5_c_im, s5_d=s5_d, s5_w_glu=s5_w_glu, s5_b_glu=s5_b_glu, pool_w=pool_w, pool_scale=pool_scale, ffn_w_gate=ffn_w_gate, ffn_w_val=ffn_w_val, ffn_conv_w=ffn_conv_w, ffn_conv_b=ffn_conv_b, ffn_w_down=ffn_w_down, norm_mix_pre=norm_mix_pre, norm_mix_post=norm_mix_post, norm_ffn_pre=norm_ffn_pre, norm_ffn_post=norm_ffn_post, loss_target=loss_target, m_s5_lambda_re=m_s5_lambda_re, m_s5_lambda_im=m_s5_lambda_im, m_s5_log_dt=m_s5_log_dt, m_s5_b_re=m_s5_b_re, m_s5_b_im=m_s5_b_im, m_s5_c_re=m_s5_c_re, m_s5_c_im=m_s5_c_im, m_s5_d=m_s5_d, m_s5_w_glu=m_s5_w_glu, m_s5_b_glu=m_s5_b_glu, m_pool_w=m_pool_w, m_pool_scale=m_pool_scale, m_ffn_w_gate=m_ffn_w_gate, m_ffn_w_val=m_ffn_w_val, m_ffn_conv_w=m_ffn_conv_w, m_ffn_conv_b=m_ffn_conv_b, m_ffn_w_down=m_ffn_w_down, m_norm_mix_pre=m_norm_mix_pre, m_norm_mix_post=m_norm_mix_post, m_norm_ffn_pre=m_norm_ffn_pre, m_norm_ffn_post=m_norm_ffn_post, v_s5_lambda_re=v_s5_lambda_re, v_s5_lambda_im=v_s5_lambda_im, v_s5_log_dt=v_s5_log_dt, v_s5_b_re=v_s5_b_re, v_s5_b_im=v_s5_b_im, v_s5_c_re=v_s5_c_re, v_s5_c_im=v_s5_c_im, v_s5_d=v_s5_d, v_s5_w_glu=v_s5_w_glu, v_s5_b_glu=v_s5_b_glu, v_pool_w=v_pool_w, v_pool_scale=v_pool_scale, v_ffn_w_gate=v_ffn_w_gate, v_ffn_w_val=v_ffn_w_val, v_ffn_conv_w=v_ffn_conv_w, v_ffn_conv_b=v_ffn_conv_b, v_ffn_w_down=v_ffn_w_down, v_norm_mix_pre=v_norm_mix_pre, v_norm_mix_post=v_norm_mix_post, v_norm_ffn_pre=v_norm_ffn_pre, v_norm_ffn_post=v_norm_ffn_post)
    weights = {n: given[n] for n in TWIN_WEIGHTS}
    shared = {n: given[n] for n in SHARED_INPUTS}
    per_example = {n: given[n] for n in ['x']}
    grad_fn = _jax.value_and_grad(_loss, argnums=(0, 1))

    def one_microbatch(ex, loss_target):
        ex = dict(ex)
        diff = ex.pop(TWIN_DIFF_INPUT)
        return grad_fn(weights, diff, {**shared, **ex}, loss_target)

    if N_MICROBATCH == 1:
        loss, (grad_w, grad_x) = one_microbatch(per_example, given["loss_target"])
    else:
        def body(carry, xs):
            loss_sum, grad_sum = carry
            l_k, (gw_k, gx_k) = one_microbatch(xs[0], xs[1])
            with _jax.named_scope("update"):
                return (loss_sum + l_k, _jax.tree.map(_jnp.add, grad_sum, gw_k)), gx_k

        init = (_jnp.zeros((), _jnp.float32), _jax.tree.map(_jnp.zeros_like, weights))
        (loss, grad_w), grad_x = _jax.lax.scan(body, init, (per_example, given["loss_target"]))
    with _jax.named_scope("update"):
        delta_w, new_m, new_v = {}, {}, {}
        for n in TWIN_WEIGHTS:
            delta_w[n], new_m[n], new_v[n] = _adamw(weights[n], grad_w[n], given["m_" + n], given["v_" + n])
    return (loss, grad_x, *[grad_w[n] for n in TWIN_WEIGHTS], *[delta_w[n] for n in TWIN_WEIGHTS],
            *[new_m[n] for n in TWIN_WEIGHTS], *[new_v[n] for n in TWIN_WEIGHTS])
```

```python
import math

import jax
import jax.numpy as jnp
from jax import lax
from jax.experimental import pallas as pl
from jax.experimental.pallas import tpu as pltpu

f32, bf16 = jnp.float32, jnp.bfloat16
SDS = jax.ShapeDtypeStruct
MESH = pl.DeviceIdType.MESH

RMS_EPS = 1e-6
GELU_C = math.sqrt(2.0 / math.pi)
GELU_K = 0.044715
ADAM_LR, ADAM_B1, ADAM_B2, ADAM_EPS, ADAM_WD, ADAM_STEP = 0.001, 0.9, 0.999, 1e-08, 0.01, 10
POOL_WINDOWS = (2, 4, 8, 16)
POOL_HALO = 16
S5_GROUP_CH = 16
S5_STATE = 64
S5_KB_CH = 256
N_CHIPS = 4

FFN_TL = 512
FFN_FT = 256
WG_TR = 2048
POOL_TL = 512
S5_TQ = 512
S5_CB = 512
TOK_TR = 512
VMEM_LIMIT = 56 * 1024 * 1024

HBM = pl.BlockSpec(memory_space=pltpu.HBM)


def _cparams(n_axes=0, side_effects=False):
    kw = dict(vmem_limit_bytes=VMEM_LIMIT)
    if n_axes:
        kw["dimension_semantics"] = ("arbitrary",) * n_axes
    if side_effects:
        kw["has_side_effects"] = True
    return pltpu.CompilerParams(**kw)


def _dot(a, b):
    return jnp.dot(a, b, preferred_element_type=f32)


def _dot_nt(a, b):
    return lax.dot_general(a, b, (((1,), (1,)), ((), ())), preferred_element_type=f32)


def _dot_tn(a, b):
    return lax.dot_general(a, b, (((0,), (0,)), ((), ())), preferred_element_type=f32)


def _rms_scale(x):
    return lax.rsqrt(jnp.mean(x * x, axis=-1, keepdims=True) + RMS_EPS)


def _rms_fwd(x, gain):
    return x * _rms_scale(x) * gain


def _rms_bwd(x, gain, dy):
    r = _rms_scale(x)
    xn = x * r
    dgain = jnp.sum(dy * xn, axis=0, keepdims=True)
    dxn = dy * gain
    dx = r * (dxn - xn * jnp.mean(dxn * xn, axis=-1, keepdims=True))
    return dx, dgain


def _gelu(x):
    return 0.5 * x * (1.0 + jnp.tanh(GELU_C * (x + GELU_K * x * x * x)))


def _gelu_grad(x):
    t = jnp.tanh(GELU_C * (x + GELU_K * x * x * x))
    return 0.5 * (1.0 + t) + 0.5 * x * (1.0 - t * t) * GELU_C * (1.0 + 3.0 * GELU_K * x * x)


def _sigmoid(x):
    return 1.0 / (1.0 + jnp.exp(-x))


def _cmul(ar, ai, br, bi):
    return ar * br - ai * bi, ar * bi + ai * br


def _row_block(n, cap):
    best = None
    for d in range(16, min(n, cap) + 1, 16):
        if n % d == 0:
            best = d
    assert best is not None, n
    return best


def _mesh_pos():
    return lax.axis_index("x"), lax.axis_index("y"), lax.axis_index("c")


def _other_chips(x, y):
    return [(1 - x, y), (x, 1 - y), (1 - x, 1 - y)]


def _allgather_weights(s_ffn, s_glu, s_small):
    nm, fs, d = s_ffn.shape
    ng, gs, _ = s_glu.shape
    rs = s_small.shape[0]
    n_arr = 3

    def body(sf, sg, ss, gf, gg, gsm, send_sems, recv_sems, loc_sems):
        x, y, c = _mesh_pos()
        srcs = (sf, sg, ss)

        def dsts(kk):
            return (gf.at[:, pl.ds(kk * fs, fs), :], gg.at[:, pl.ds(kk * gs, gs), :], gsm.at[kk])

        me = 2 * x + y
        own = [pltpu.make_async_copy(s, t, loc_sems.at[i]) for i, (s, t) in enumerate(zip(srcs, dsts(me)))]
        for cp in own:
            cp.start()
        chips = _other_chips(x, y)
        sends = []
        for j, chip in enumerate(chips):
            for i, (s, t) in enumerate(zip(srcs, dsts(me))):
                cp = pltpu.make_async_remote_copy(src_ref=s, dst_ref=t, send_sem=send_sems.at[n_arr * j + i],
                                                  recv_sem=recv_sems.at[n_arr * j + i],
                                                  device_id=(chip[0], chip[1], c), device_id_type=MESH)
                cp.start()
                sends.append(cp)
        for j, chip in enumerate(chips):
            for i, (s, t) in enumerate(zip(srcs, dsts(2 * chip[0] + chip[1]))):
                pltpu.make_async_remote_copy(src_ref=s, dst_ref=t, send_sem=send_sems.at[n_arr * j + i],
                                             recv_sem=recv_sems.at[n_arr * j + i],
                                             device_id=(chip[0], chip[1], c), device_id_type=MESH).wait_recv()
        for cp in sends:
            cp.wait_send()
        for cp in own:
            cp.wait()

    return pl.pallas_call(
        body, name="allgather_weights",
        out_shape=(SDS((nm, N_CHIPS * fs, d), s_ffn.dtype), SDS((ng, N_CHIPS * gs, d), s_glu.dtype),
                   SDS((N_CHIPS, rs, d), s_small.dtype)),
        in_specs=[HBM, HBM, HBM], out_specs=(HBM, HBM, HBM),
        scratch_shapes=[pltpu.SemaphoreType.DMA((3 * n_arr,)), pltpu.SemaphoreType.DMA((3 * n_arr,)),
                        pltpu.SemaphoreType.DMA((n_arr,))],
        compiler_params=_cparams(side_effects=True),
    )(s_ffn, s_glu, s_small)


def _exchange_grad_slabs(dw_layers, dglu, q):
    nl = len(dw_layers)
    _, f, d = dw_layers[0].shape
    fs = f // N_CHIPS
    ng, dg_rows, _ = dglu.shape
    gs = dg_rows // N_CHIPS
    rs = q.shape[0]
    n_arr = nl + 2

    def body(*refs):
        dws = refs[:nl]
        dgl, qq = refs[nl], refs[nl + 1]
        r_ffn, r_glu, r_q = refs[nl + 2:nl + 5]
        send_sems, recv_sems, loc_sems = refs[nl + 5:]
        x, y, c = _mesh_pos()
        me = 2 * x + y

        def srcs(kk):
            return [w.at[:, pl.ds(kk * fs, fs), :] for w in dws] + [dgl.at[:, pl.ds(kk * gs, gs), :], qq]

        def dsts(slot):
            return [r_ffn.at[slot, l] for l in range(nl)] + [r_glu.at[slot], r_q.at[slot]]

        own = [pltpu.make_async_copy(s, t, loc_sems.at[i]) for i, (s, t) in enumerate(zip(srcs(me), dsts(me)))]
        for cp in own:
            cp.start()
        chips = _other_chips(x, y)
        sends = []
        for j, chip in enumerate(chips):
            for i, (s, t) in enumerate(zip(srcs(2 * chip[0] + chip[1]), dsts(me))):
                cp = pltpu.make_async_remote_copy(src_ref=s, dst_ref=t, send_sem=send_sems.at[n_arr * j + i],
                                                  recv_sem=recv_sems.at[n_arr * j + i],
                                                  device_id=(chip[0], chip[1], c), device_id_type=MESH)
                cp.start()
                sends.append(cp)
        for j, chip in enumerate(chips):
            kk = 2 * chip[0] + chip[1]
            for i, (s, t) in enumerate(zip(srcs(me), dsts(kk))):
                pltpu.make_async_remote_copy(src_ref=s, dst_ref=t, send_sem=send_sems.at[n_arr * j + i],
                                             recv_sem=recv_sems.at[n_arr * j + i],
                                             device_id=(chip[0], chip[1], c), device_id_type=MESH).wait_recv()
        for cp in sends:
            cp.wait_send()
        for cp in own:
            cp.wait()

    return pl.pallas_call(
        body, name="exchange_grad_slabs",
        out_shape=(SDS((N_CHIPS, nl, 3, fs, d), bf16), SDS((N_CHIPS, ng, gs, d), bf16), SDS((N_CHIPS, rs, d), f32)),
        in_specs=[HBM] * (nl + 2), out_specs=(HBM, HBM, HBM),
        scratch_shapes=[pltpu.SemaphoreType.DMA((3 * n_arr,)), pltpu.SemaphoreType.DMA((3 * n_arr,)),
                        pltpu.SemaphoreType.DMA((n_arr,))],
        compiler_params=_cparams(side_effects=True),
    )(*dw_layers, dglu, q)


def _exchange_with_sibling(parts):
    n = len(parts)

    def body(*refs):
        ins, outs = refs[:n], refs[n:2 * n]
        send_sems, recv_sems = refs[2 * n:]
        x, y, c = _mesh_pos()
        cps = [pltpu.make_async_remote_copy(src_ref=s, dst_ref=t, send_sem=send_sems.at[i], recv_sem=recv_sems.at[i],
                                            device_id=(x, y, 1 - c), device_id_type=MESH)
               for i, (s, t) in enumerate(zip(ins, outs))]
        for cp in cps:
            cp.start()
        for cp in cps:
            cp.wait()

    return pl.pallas_call(
        body, name="exchange_with_sibling",
        out_shape=tuple(SDS(p.shape, p.dtype) for p in parts),
        in_specs=[HBM] * n, out_specs=tuple([HBM] * n),
        scratch_shapes=[pltpu.SemaphoreType.DMA((n,)), pltpu.SemaphoreType.DMA((n,))],
        compiler_params=_cparams(side_effects=True),
    )(*parts)


def _sum_slots(r, name):
    _, rows, cols = r.shape
    tr = _row_block(rows, 512)

    def body(r_ref, o_ref):
        o_ref[...] = ((r_ref[0].astype(f32) + r_ref[1].astype(f32)) + r_ref[2].astype(f32)) + r_ref[3].astype(f32)

    return pl.pallas_call(
        body, name=name, grid=(rows // tr,), out_shape=SDS((rows, cols), f32),
        in_specs=[pl.BlockSpec((N_CHIPS, tr, cols), lambda i: (0, i, 0))],
        out_specs=pl.BlockSpec((tr, cols), lambda i: (i, 0)),
        compiler_params=_cparams(1),
    )(r)


def _add2(a, b, name):
    rows, cols = a.shape
    tr = _row_block(rows, 512)

    def body(a_ref, b_ref, o_ref):
        o_ref[...] = a_ref[...] + b_ref[...]

    spec = pl.BlockSpec((tr, cols), lambda i: (i, 0))
    return pl.pallas_call(body, name=name, grid=(rows // tr,), out_shape=SDS((rows, cols), f32),
                          in_specs=[spec, spec], out_specs=spec, compiler_params=_cparams(1))(a, b)


def _adamw(g_parts, w, m, v, name):
    rows, cols = w.shape
    tr = _row_block(rows, 512)
    n_g = len(g_parts)
    c1 = 1.0 / (1.0 - ADAM_B1 ** ADAM_STEP)
    c2 = 1.0 / (1.0 - ADAM_B2 ** ADAM_STEP)

    def body(*refs):
        g_refs = refs[:n_g]
        w_ref, m_ref, v_ref, go_ref, d_ref, mo_ref, vo_ref = refs[n_g:]
        g = g_refs[0][...]
        for r in g_refs[1:]:
            g = g + r[...]
        mn = ADAM_B1 * m_ref[...] + (1.0 - ADAM_B1) * g
        vn = ADAM_B2 * v_ref[...] + (1.0 - ADAM_B2) * (g * g)
        go_ref[...] = g
        mo_ref[...] = mn
        vo_ref[...] = vn
        d_ref[...] = -ADAM_LR * ((mn * c1) / (jnp.sqrt(vn * c2) + ADAM_EPS) + ADAM_WD * w_ref[...])

    spec = pl.BlockSpec((tr, cols), lambda i: (i, 0))
    return pl.pallas_call(
        body, name=name, grid=(rows // tr,), out_shape=tuple(SDS((rows, cols), f32) for _ in range(4)),
        in_specs=[spec] * (n_g + 3), out_specs=(spec,) * 4, compiler_params=_cparams(1),
    )(*g_parts, w, m, v)


def _causal_conv(g, prev, w, bias):
    row = lax.broadcasted_iota(jnp.int32, g.shape, 0)
    p7, p6 = prev[7:8, :], prev[6:7, :]
    g1 = jnp.where(row == 0, p7, pltpu.roll(g, 1, axis=0))
    g2 = jnp.where(row == 0, p6, jnp.where(row == 1, p7, pltpu.roll(g, 2, axis=0)))
    return bias + w[0:1, :] * g2 + w[1:2, :] * g1 + w[2:3, :] * g, g1, g2


def _ffn_fwd(x_mid, gain_pre, gain_post, w_all, conv_w, conv_b, li):
    bsz, l, d = x_mid.shape
    f = w_all.shape[1]
    tl, ft = min(FFN_TL, l), min(FFN_FT, f)
    nt, nf = l // tl, f // ft

    def body(x_ref, gpre_ref, gpost_ref, wg_ref, wv_ref, wd_ref, cw_ref, cb_ref,
             xo_ref, g_ref, v_ref, f_ref, h_sc, facc, gprev):
        t, j = pl.program_id(1), pl.program_id(2)

        @pl.when(j == 0)
        def _():
            h_sc[...] = _rms_fwd(x_ref[...], gpre_ref[...]).astype(bf16)
            facc[...] = jnp.zeros_like(facc)

        @pl.when(t == 0)
        def _():
            gprev[j] = jnp.zeros((8, ft), f32)

        h = h_sc[...]
        g = _dot_nt(h, wg_ref[...])
        v = _dot_nt(h, wv_ref[...])
        g_ref[...] = g.astype(bf16)
        v_ref[...] = v.astype(bf16)
        gc, _, _ = _causal_conv(g, gprev[j], cw_ref[...], cb_ref[...])
        gprev[j] = g[tl - 8:tl, :]
        hdn = _gelu(gc) * v
        facc[...] += _dot(hdn.astype(bf16), wd_ref[...])

        @pl.when(j == nf - 1)
        def _():
            fv = facc[...]
            f_ref[...] = fv
            xo_ref[...] = x_ref[...] + _rms_fwd(fv, gpost_ref[...])

    tok = pl.BlockSpec((None, tl, d), lambda b, t, j: (b, t, 0))
    hid = pl.BlockSpec((None, tl, ft), lambda b, t, j: (b, t, j))
    gain = pl.BlockSpec((1, d), lambda b, t, j: (0, 0))

    def wspec(kind):
        return pl.BlockSpec((None, ft, d), lambda b, t, j: (3 * li + kind, j, 0))

    return pl.pallas_call(
        body, name=f"ffn_fwd_{li}", grid=(bsz, nt, nf),
        out_shape=(SDS((bsz, l, d), f32), SDS((bsz, l, f), bf16), SDS((bsz, l, f), bf16), SDS((bsz, l, d), f32)),
        in_specs=[tok, gain, gain, wspec(0), wspec(1), wspec(2),
                  pl.BlockSpec((None, 3, ft), lambda b, t, j: (li, 0, j)),
                  pl.BlockSpec((None, 1, ft), lambda b, t, j: (li, 0, j))],
        out_specs=(tok, hid, hid, tok),
        scratch_shapes=[pltpu.VMEM((tl, d), bf16), pltpu.VMEM((tl, d), f32), pltpu.VMEM((nf, 8, ft), f32)],
        compiler_params=_cparams(3),
    )(x_mid, gain_pre, gain_post, w_all, w_all, w_all, conv_w, conv_b.reshape(conv_b.shape[0], 1, f))


def _ffn_bwd_act(x_mid, f_sv, g_sv, v_sv, dxo, gain_pre, gain_post, w_all, conv_w, conv_b, li):
    bsz, l, d = x_mid.shape
    f = w_all.shape[1]
    tl, ft = min(FFN_TL, l), min(FFN_FT, f)
    nt, nf = l // tl, f // ft

    def body(x_ref, f_ref, dxo_ref, g_ref, v_ref, gh_ref, gpre_ref, gpost_ref, wg_ref, wv_ref, wd_ref, cw_ref, cb_ref,
             dx_ref, dg_ref, dv_ref, hdn_ref, h_ref, df_ref, dgain_ref, dconv_ref, h_sc, df_sc, dh_acc, dgc_next):
        b, t, j = pl.program_id(0), pl.program_id(1), pl.program_id(2)
        tt = nt - 1 - t

        @pl.when((b == 0) & (t == 0) & (j == 0))
        def _():
            dgain_ref[...] = jnp.zeros_like(dgain_ref)
            dconv_ref[...] = jnp.zeros_like(dconv_ref)

        @pl.when(j == 0)
        def _():
            hb = _rms_fwd(x_ref[...], gpre_ref[...]).astype(bf16)
            h_sc[...] = hb
            h_ref[...] = hb
            df, dgp = _rms_bwd(f_ref[...], gpost_ref[...], dxo_ref[...])
            dfb = df.astype(bf16)
            df_sc[...] = dfb
            df_ref[...] = dfb
            dgain_ref[1:2, :] += dgp
            dh_acc[...] = jnp.zeros_like(dh_acc)

        @pl.when(t == 0)
        def _():
            dgc_next[j] = jnp.zeros((8, ft), f32)

        dhdn = _dot_nt(df_sc[...], wd_ref[...])
        g = g_ref[...].astype(f32)
        v = v_ref[...].astype(f32)
        prev = jnp.where(tt > 0, gh_ref[...].astype(f32)[8:16, :], 0.0)
        w = cw_ref[...]
        gc, g1, g2 = _causal_conv(g, prev, w, cb_ref[...])
        u = _gelu(gc)
        hdn_ref[...] = (u * v).astype(bf16)
        dv = dhdn * u
        dgc = dhdn * v * _gelu_grad(gc)
        nxt = dgc_next[j]
        row = lax.broadcasted_iota(jnp.int32, dgc.shape, 0)
        n0, n1 = nxt[0:1, :], nxt[1:2, :]
        d1 = jnp.where(row == tl - 1, n0, pltpu.roll(dgc, tl - 1, axis=0))
        d2 = jnp.where(row == tl - 1, n1, jnp.where(row == tl - 2, n0, pltpu.roll(dgc, tl - 2, axis=0)))
        dg = w[2:3, :] * dgc + w[1:2, :] * d1 + w[0:1, :] * d2
        dgc_next[j] = dgc[0:8, :]
        dgb, dvb = dg.astype(bf16), dv.astype(bf16)
        dg_ref[...] = dgb
        dv_ref[...] = dvb
        dh_acc[...] += _dot(dgb, wg_ref[...]) + _dot(dvb, wv_ref[...])
        dconv_ref[j, 0:1, :] += jnp.sum(dgc * g2, axis=0, keepdims=True)
        dconv_ref[j, 1:2, :] += jnp.sum(dgc * g1, axis=0, keepdims=True)
        dconv_ref[j, 2:3, :] += jnp.sum(dgc * g, axis=0, keepdims=True)
        dconv_ref[j, 3:4, :] += jnp.sum(dgc, axis=0, keepdims=True)

        @pl.when(j == nf - 1)
        def _():
            dxp, dgp = _rms_bwd(x_ref[...], gpre_ref[...], dh_acc[...])
            dx_ref[...] = dxo_ref[...] + dxp
            dgain_ref[0:1, :] += dgp

    tok = pl.BlockSpec((None, tl, d), lambda b, t, j: (b, nt - 1 - t, 0))
    hid = pl.BlockSpec((None, tl, ft), lambda b, t, j: (b, nt - 1 - t, j))
    halo = pl.BlockSpec((None, 16, ft), lambda b, t, j: (b, jnp.maximum((nt - 1 - t) * (tl // 16) - 1, 0), j))
    gain = pl.BlockSpec((1, d), lambda b, t, j: (0, 0))

    def wspec(kind):
        return pl.BlockSpec((None, ft, d), lambda b, t, j: (3 * li + kind, j, 0))

    return pl.pallas_call(
        body, name=f"ffn_bwd_act_{li}", grid=(bsz, nt, nf),
        out_shape=(SDS((bsz, l, d), f32), SDS((bsz, l, f), bf16), SDS((bsz, l, f), bf16), SDS((bsz, l, f), bf16),
                   SDS((bsz, l, d), bf16), SDS((bsz, l, d), bf16), SDS((8, d), f32), SDS((nf, 8, ft), f32)),
        in_specs=[tok, tok, tok, hid, hid, halo, gain, gain, wspec(0), wspec(1), wspec(2),
                  pl.BlockSpec((None, 3, ft), lambda b, t, j: (li, 0, j)),
                  pl.BlockSpec((None, 1, ft), lambda b, t, j: (li, 0, j))],
        out_specs=(tok, hid, hid, hid, tok, tok,
                   pl.BlockSpec((8, d), lambda b, t, j: (0, 0)), pl.BlockSpec((nf, 8, ft), lambda b, t, j: (0, 0, 0))),
        scratch_shapes=[pltpu.VMEM((tl, d), bf16), pltpu.VMEM((tl, d), bf16), pltpu.VMEM((tl, d), f32),
                        pltpu.VMEM((nf, 8, ft), f32)],
        compiler_params=_cparams(3),
    )(x_mid, f_sv, dxo, g_sv, v_sv, g_sv, gain_pre, gain_post, w_all, w_all, w_all, conv_w,
      conv_b.reshape(conv_b.shape[0], 1, f))


def _ffn_bwd_weights(dg, dv, hdn, h, df, li):
    t, f = dg.shape
    d = h.shape[1]
    ft, tr = min(FFN_FT, f), min(WG_TR, t)
    nf, nr = f // ft, t // tr

    def body(dg_ref, dv_ref, hdn_ref, h_ref, df_ref, o_ref, acc):
        r = pl.program_id(1)

        @pl.when(r == 0)
        def _():
            acc[...] = jnp.zeros_like(acc)

        hb = h_ref[...]
        acc[0] += _dot_tn(dg_ref[...], hb)
        acc[1] += _dot_tn(dv_ref[...], hb)
        acc[2] += _dot_tn(hdn_ref[...], df_ref[...])

        @pl.when(r == nr - 1)
        def _():
            o_ref[...] = acc[...].astype(bf16)

    hid = pl.BlockSpec((tr, ft), lambda j, r: (r, j))
    tok = pl.BlockSpec((tr, d), lambda j, r: (r, 0))
    return pl.pallas_call(
        body, name=f"ffn_bwd_weights_{li}", grid=(nf, nr), out_shape=SDS((3, f, d), bf16),
        in_specs=[hid, hid, hid, tok, tok], out_specs=pl.BlockSpec((3, ft, d), lambda j, r: (0, j, 0)),
        scratch_shapes=[pltpu.VMEM((3, ft, d), f32)], compiler_params=_cparams(2),
    )(dg, dv, hdn, h, df)


def _pool_counts(t0, tl, d):
    gch = d // len(POOL_WINDOWS)
    tpos = (t0 + lax.broadcasted_iota(jnp.int32, (tl, d), 0) + 1).astype(f32)
    lane = lax.broadcasted_iota(jnp.int32, (tl, d), 1)
    win = jnp.full((tl, d), float(POOL_WINDOWS[-1]), f32)
    for gi in range(len(POOL_WINDOWS) - 2, -1, -1):
        win = jnp.where(lane < (gi + 1) * gch, float(POOL_WINDOWS[gi]), win)
    return jnp.minimum(tpos, win)


def _pool_select(parts, tl, d):
    gch = d // len(POOL_WINDOWS)
    lane = lax.broadcasted_iota(jnp.int32, (tl, d), 1)
    out = parts[-1]
    for gi in range(len(parts) - 2, -1, -1):
        out = jnp.where(lane < (gi + 1) * gch, parts[gi], out)
    return out


def _pool_window_sums(u, halo, tl):
    ext = jnp.concatenate([halo, u], axis=0)
    sums, cur = [], ext
    for k in (1, 2, 4, 8):
        cur = cur + pltpu.roll(cur, k, axis=0)
        sums.append(cur[POOL_HALO:POOL_HALO + tl, :])
    return sums


def _pool_mix(u, halo, cnt, pw_ref, scale, tl, d):
    gch = d // len(POOL_WINDOWS)
    pooled = _pool_select(_pool_window_sums(u, halo, tl), tl, d) / cnt
    diff = pooled - u
    outs = [_dot(diff[:, gi * gch:(gi + 1) * gch].astype(bf16), pw_ref[gi]) for gi in range(len(POOL_WINDOWS))]
    return diff, jnp.concatenate(outs, axis=1)


def _pool_fwd(x, gain_pre, gain_post, pw, scale, li):
    bsz, l, d = x.shape
    tl = min(POOL_TL, l)
    nt = l // tl

    def body(x_ref, gpre_ref, gpost_ref, pw_ref, sc_ref, xo_ref, halo):
        t = pl.program_id(1)

        @pl.when(t == 0)
        def _():
            halo[...] = jnp.zeros_like(halo)

        xv = x_ref[...]
        u = _rms_fwd(xv, gpre_ref[...])
        _, out = _pool_mix(u, halo[...], _pool_counts(t * tl, tl, d), pw_ref, sc_ref[...], tl, d)
        halo[...] = u[tl - POOL_HALO:tl, :]
        xo_ref[...] = xv + _rms_fwd(out * sc_ref[...], gpost_ref[...])

    tok = pl.BlockSpec((None, tl, d), lambda b, t: (b, t, 0))
    gain = pl.BlockSpec((1, d), lambda b, t: (0, 0))
    return pl.pallas_call(
        body, name=f"pool_fwd_{li}", grid=(bsz, nt), out_shape=SDS((bsz, l, d), f32),
        in_specs=[tok, gain, gain, pl.BlockSpec(pw.shape, lambda b, t: (0, 0, 0)), gain], out_specs=tok,
        scratch_shapes=[pltpu.VMEM((POOL_HALO, d), f32)], compiler_params=_cparams(2),
    )(x, gain_pre, gain_post, pw, scale)


def _pool_bwd(x, dxo, gain_pre, gain_post, pw, scale, li):
    bsz, l, d = x.shape
    tl = min(POOL_TL, l)
    nt = l // tl
    ng = len(POOL_WINDOWS)
    gch = d // ng
    n_ext = tl + POOL_HALO

    def body(x_ref, xh_ref, dxo_ref, gpre_ref, gpost_ref, pw_ref, sc_ref, dx_ref, dpw_ref, ds_ref, qnext):
        b, t = pl.program_id(0), pl.program_id(1)
        tt = nt - 1 - t

        @pl.when((b == 0) & (t == 0))
        def _():
            dpw_ref[...] = jnp.zeros_like(dpw_ref)
            ds_ref[...] = jnp.zeros_like(ds_ref)

        @pl.when(t == 0)
        def _():
            qnext[...] = jnp.zeros_like(qnext)

        xv = x_ref[...]
        gpre = gpre_ref[...]
        u = _rms_fwd(xv, gpre)
        uh = _rms_fwd(jnp.where(tt > 0, xh_ref[...], 0.0), gpre)
        cnt = _pool_counts(tt * tl, tl, d)
        scale_v = sc_ref[...]
        diff, out = _pool_mix(u, uh, cnt, pw_ref, scale_v, tl, d)
        dxo_v = dxo_ref[...]
        dm, dgpost = _rms_bwd(out * scale_v, gpost_ref[...], dxo_v)
        ds_ref[1:2, :] += dgpost
        ds_ref[2:3, :] += jnp.sum(dm * out, axis=0, keepdims=True)
        dout = (dm * scale_v).astype(bf16)
        ddiffs = []
        for gi in range(ng):
            sl = slice(gi * gch, (gi + 1) * gch)
            dpw_ref[gi] += _dot_tn(diff[:, sl].astype(bf16), dout[:, sl])
            ddiffs.append(_dot_nt(dout[:, sl], pw_ref[gi]))
        ddiff = jnp.concatenate(ddiffs, axis=1)
        q = ddiff / cnt
        ext = jnp.concatenate([q, qnext[...]], axis=0)
        sums, cur = [], ext
        for k in (1, 2, 4, 8):
            cur = cur + pltpu.roll(cur, n_ext - k, axis=0)
            sums.append(cur[0:tl, :])
        du = _pool_select(sums, tl, d) - ddiff
        qnext[...] = q[0:POOL_HALO, :]
        dxp, dgpre = _rms_bwd(xv, gpre, du)
        ds_ref[0:1, :] += dgpre
        dx_ref[...] = dxo_v + dxp

    tok = pl.BlockSpec((None, tl, d), lambda b, t: (b, nt - 1 - t, 0))
    halo = pl.BlockSpec((None, POOL_HALO, d),
                        lambda b, t: (b, jnp.maximum((nt - 1 - t) * (tl // POOL_HALO) - 1, 0), 0))
    gain = pl.BlockSpec((1, d), lambda b, t: (0, 0))
    return pl.pallas_call(
        body, name=f"pool_bwd_{li}", grid=(bsz, nt),
        out_shape=(SDS((bsz, l, d), f32), SDS((ng, gch, gch), f32), SDS((8, d), f32)),
        in_specs=[tok, halo, tok, gain, gain, pl.BlockSpec(pw.shape, lambda b, t: (0, 0, 0)), gain],
        out_specs=(tok, pl.BlockSpec((ng, gch, gch), lambda b, t: (0, 0, 0)), pl.BlockSpec((8, d), lambda b, t: (0, 0))),
        scratch_shapes=[pltpu.VMEM((POOL_HALO, d), f32)], compiler_params=_cparams(2),
    )(x, x, dxo, gain_pre, gain_post, pw, scale)


def _s5_discretise(lam_re, lam_im, log_dt, b_re, b_im):
    def body(lr_ref, li_ref, ld_ref, br_ref, bi_ref, ar_ref, ai_ref, bbr_ref, bbi_ref):
        lr, li = lr_ref[...], li_ref[...]
        dt = jnp.exp(ld_ref[...])
        mag = jnp.exp(lr * dt)
        ar = mag * jnp.cos(li * dt)
        ai = mag * jnp.sin(li * dt)
        den = lr * lr + li * li
        nr, ni = ar - 1.0, ai
        fr = (nr * lr + ni * li) / den
        fi = (ni * lr - nr * li) / den
        br, bi = br_ref[...], bi_ref[...]
        ar_ref[...] = ar
        ai_ref[...] = ai
        bbr_ref[...] = fr * br - fi * bi
        bbi_ref[...] = fr * bi + fi * br

    return pl.pallas_call(
        body, name="s5_discretise",
        out_shape=(SDS(lam_re.shape, f32), SDS(lam_re.shape, f32), SDS(b_re.shape, f32), SDS(b_re.shape, f32)),
        compiler_params=_cparams(),
    )(lam_re, lam_im, log_dt, b_re, b_im)


def _s5_discretise_bwd(lam_re, lam_im, log_dt, b_re, b_im, g_ar, g_ai, g_bbr, g_bbi):
    def body(lr_ref, li_ref, ld_ref, br_ref, bi_ref, gar_ref, gai_ref, gbbr_ref, gbbi_ref,
             dlr_ref, dli_ref, dld_ref, dbr_ref, dbi_ref):
        lr, li = lr_ref[...], li_ref[...]
        dt = jnp.exp(ld_ref[...])
        mag = jnp.exp(lr * dt)
        cs, sn = jnp.cos(li * dt), jnp.sin(li * dt)
        ar, ai = mag * cs, mag * sn
        den = lr * lr + li * li
        nr, ni = ar - 1.0, ai
        fr = (nr * lr + ni * li) / den
        fi = (ni * lr - nr * li) / den
        br, bi = br_ref[...], bi_ref[...]
        gbbr, gbbi = gbbr_ref[...], gbbi_ref[...]
        dbr_ref[...] = fr * gbbr + fi * gbbi
        dbi_ref[...] = fr * gbbi - fi * gbbr
        gfr = jnp.sum(br * gbbr + bi * gbbi, axis=2, keepdims=True)
        gfi = jnp.sum(br * gbbi - bi * gbbr, axis=2, keepdims=True)
        gnr_num, gni_num = gfr / den, gfi / den
        gden = -(gfr * fr + gfi * fi) / den
        g_nr = gnr_num * lr - gni_num * li
        g_ni = gnr_num * li + gni_num * lr
        dlr = gnr_num * nr + gni_num * ni + gden * 2.0 * lr
        dli = gnr_num * ni - gni_num * nr + gden * 2.0 * li
        gar = gar_ref[...] + g_nr
        gai = gai_ref[...] + g_ni
        gq = (gar * cs + gai * sn) * mag
        gth = (gai * cs - gar * sn) * mag
        dlr_ref[...] = dlr + gq * dt
        dli_ref[...] = dli + gth * dt
        dld_ref[...] = jnp.sum(gq * lr + gth * li, axis=3, keepdims=True) * dt

    return pl.pallas_call(
        body, name="s5_discretise_bwd",
        out_shape=(SDS(lam_re.shape, f32), SDS(lam_re.shape, f32), SDS(log_dt.shape, f32),
                   SDS(b_re.shape, f32), SDS(b_re.shape, f32)),
        compiler_params=_cparams(),
    )(lam_re, lam_im, log_dt, b_re, b_im, g_ar, g_ai, g_bbr, g_bbi)


def _pow_2k(ar, ai, n):
    k = n
    while k > 1:
        ar, ai = _cmul(ar, ai, ar, ai)
        k //= 2
    return ar, ai


def _interleaved_scan(xr_sc, xi_sc, ar, ai, carry_r, carry_i, nj, width, reverse, h_sc=None):
    cb = min(S5_CB, width)
    acc_out = []
    for c0 in range(0, width, cb):
        sl = pl.ds(c0, cb)
        a_r, a_i = ar[:, c0:c0 + cb], ai[:, c0:c0 + cb]
        aj_r, aj_i = _pow_2k(a_r, a_i, nj)

        def pos(i):
            return nj - 1 - i if reverse else i

        def local_step(i, st):
            j = pos(i)
            hr, hi = _cmul(a_r, a_i, st[0], st[1])
            hr, hi = hr + xr_sc[j, :, sl], hi + xi_sc[j, :, sl]
            xr_sc[j, :, sl] = hr
            xi_sc[j, :, sl] = hi
            return hr, hi

        zero = jnp.zeros((8, cb), f32)
        fin_r, fin_i = lax.fori_loop(0, nj, local_step, (zero, zero))
        row = lax.broadcasted_iota(jnp.int32, (8, cb), 0)
        c_r, c_i = carry_r[0:1, sl], carry_i[0:1, sl]
        ent_r, ent_i = zero, zero
        order = range(7, -1, -1) if reverse else range(8)
        for s in order:
            ent_r = jnp.where(row == s, c_r, ent_r)
            ent_i = jnp.where(row == s, c_i, ent_i)
            pr, pi_ = _cmul(aj_r[0:1, :], aj_i[0:1, :], c_r, c_i)
            c_r, c_i = fin_r[s:s + 1, :] + pr, fin_i[s:s + 1, :] + pi_
        carry_r[:, sl] = jnp.broadcast_to(c_r, (8, cb))
        carry_i[:, sl] = jnp.broadcast_to(c_i, (8, cb))

        def fix_step(i, st):
            j = pos(i)
            p_r, p_i, nx_r, nx_i, acc_r, acc_i = st
            cr_, ci_ = _cmul(p_r, p_i, ent_r, ent_i)
            hr, hi = xr_sc[j, :, sl] + cr_, xi_sc[j, :, sl] + ci_
            xr_sc[j, :, sl] = hr
            xi_sc[j, :, sl] = hi
            if h_sc is not None:
                sr, si = h_sc[0][j, :, sl], h_sc[1][j, :, sl]
                acc_r = acc_r + nx_r * sr + nx_i * si
                acc_i = acc_i + nx_i * sr - nx_r * si
                nx_r, nx_i = hr, hi
            p_r, p_i = _cmul(p_r, p_i, a_r, a_i)
            return p_r, p_i, nx_r, nx_i, acc_r, acc_i

        st = lax.fori_loop(0, nj, fix_step, (a_r, a_i, ent_r, ent_i, zero, zero))
        acc_out.append((st[4], st[5]))
    return acc_out


def _s5_scan_fwd(x_il, gain_pre, d_skip, ab_r, ab_i, bb_r, bb_i, c_r, c_in, li):
    bsz, nblk, tq, d = x_il.shape
    nkb, kc, ks = bb_r.shape
    nj = tq // 8

    def body(xf_ref, xk_ref, gk_ref, dk_ref, ar_ref, ai_ref, bbr_ref, bbi_ref, cr_ref, ci_ref,
             y_ref, hr_ref, hi_ref, xr_sc, xi_sc, carry_r, carry_i):
        n = pl.program_id(2)

        @pl.when(n == 0)
        def _():
            carry_r[...] = jnp.zeros_like(carry_r)
            carry_i[...] = jnp.zeros_like(carry_i)

        u = xk_ref[...] * _rms_scale(xf_ref[...]) * gk_ref[...]
        ub = u.astype(bf16)
        xr_sc[...] = _dot(ub, bbr_ref[...]).reshape(nj, 8, ks)
        xi_sc[...] = _dot(ub, bbi_ref[...]).reshape(nj, 8, ks)
        ar = jnp.broadcast_to(ar_ref[...], (8, ks))
        ai = jnp.broadcast_to(ai_ref[...], (8, ks))
        _interleaved_scan(xr_sc, xi_sc, ar, ai, carry_r, carry_i, nj, ks, reverse=False)
        hrb = xr_sc[...].reshape(tq, ks).astype(bf16)
        hib = xi_sc[...].reshape(tq, ks).astype(bf16)
        hr_ref[...] = hrb
        hi_ref[...] = hib
        y_ref[...] = _dot(hrb, cr_ref[...]) + _dot(hib, ci_ref[...]) + dk_ref[...] * u

    full = pl.BlockSpec((None, None, tq, d), lambda k, b, n: (b, n, 0, 0))
    chan = pl.BlockSpec((None, None, tq, kc), lambda k, b, n: (b, n, 0, k))
    stat = pl.BlockSpec((None, None, tq, ks), lambda k, b, n: (b, n, 0, k))
    vec_c = pl.BlockSpec((1, kc), lambda k, b, n: (0, k))
    vec_s = pl.BlockSpec((1, ks), lambda k, b, n: (0, k))
    bmat = pl.BlockSpec((None, kc, ks), lambda k, b, n: (k, 0, 0))
    cmat = pl.BlockSpec((None, ks, kc), lambda k, b, n: (k, 0, 0))
    s_tot = nkb * ks
    return pl.pallas_call(
        body, name=f"s5_scan_fwd_{li}", grid=(nkb, bsz, nblk),
        out_shape=(SDS((bsz, nblk, tq, d), f32), SDS((bsz, nblk, tq, s_tot), bf16), SDS((bsz, nblk, tq, s_tot), bf16)),
        in_specs=[full, chan, vec_c, vec_c, vec_s, vec_s, bmat, bmat, cmat, cmat],
        out_specs=(chan, stat, stat),
        scratch_shapes=[pltpu.VMEM((nj, 8, ks), f32), pltpu.VMEM((nj, 8, ks), f32),
                        pltpu.VMEM((8, ks), f32), pltpu.VMEM((8, ks), f32)],
        compiler_params=_cparams(3),
    )(x_il, x_il, gain_pre, d_skip, ab_r, ab_i, bb_r, bb_i, c_r, c_in)


def _s5_glu_fwd(y, x_il, w_glu, b_glu, gain_post, li):
    t, d = y.shape
    tr = min(TOK_TR, t)

    def body(y_ref, x_ref, w_ref, b_ref, gp_ref, xo_ref):
        z = _gelu(y_ref[...])
        a = _dot(z.astype(bf16), w_ref[...]) + b_ref[...]
        xo_ref[...] = x_ref[...] + _rms_fwd(z * _sigmoid(a), gp_ref[...])

    tok = pl.BlockSpec((tr, d), lambda i: (i, 0))
    vec = pl.BlockSpec((1, d), lambda i: (0, 0))
    return pl.pallas_call(
        body, name=f"s5_glu_fwd_{li}", grid=(t // tr,), out_shape=SDS((t, d), f32),
        in_specs=[tok, tok, pl.BlockSpec((d, d), lambda i: (0, 0)), vec, vec], out_specs=tok,
        compiler_params=_cparams(1),
    )(y, x_il, w_glu, b_glu, gain_post)


def _s5_glu_bwd(y, dxo, w_glu, b_glu, gain_post, li):
    t, d = y.shape
    tr = min(TOK_TR, t)

    def body(y_ref, dxo_ref, w_ref, b_ref, gp_ref, dy_ref, dw_ref, ds_ref):
        @pl.when(pl.program_id(0) == 0)
        def _():
            dw_ref[...] = jnp.zeros_like(dw_ref)
            ds_ref[...] = jnp.zeros_like(ds_ref)

        yv = y_ref[...]
        z = _gelu(yv)
        zb = z.astype(bf16)
        w = w_ref[...]
        s = _sigmoid(_dot(zb, w) + b_ref[...])
        dm, dgpost = _rms_bwd(z * s, gp_ref[...], dxo_ref[...])
        da = dm * z * s * (1.0 - s)
        dab = da.astype(bf16)
        dz = dm * s + _dot_nt(dab, w)
        dw_ref[...] += _dot_tn(zb, dab)
        ds_ref[0:1, :] += dgpost
        ds_ref[1:2, :] += jnp.sum(da, axis=0, keepdims=True)
        dy_ref[...] = dz * _gelu_grad(yv)

    tok = pl.BlockSpec((tr, d), lambda i: (i, 0))
    vec = pl.BlockSpec((1, d), lambda i: (0, 0))
    mat = pl.BlockSpec((d, d), lambda i: (0, 0))
    return pl.pallas_call(
        body, name=f"s5_glu_bwd_{li}", grid=(t // tr,),
        out_shape=(SDS((t, d), f32), SDS((d, d), f32), SDS((8, d), f32)),
        in_specs=[tok, tok, mat, vec, vec], out_specs=(tok, mat, pl.BlockSpec((8, d), lambda i: (0, 0))),
        compiler_params=_cparams(1),
    )(y, dxo, w_glu, b_glu, gain_post)


def _s5_scan_bwd(x_il, dy, h_r, h_i, gain_pre, d_skip, ab_r, ab_i, bb_r, bb_i, c_r, c_in, li):
    bsz, nblk, tq, d = x_il.shape
    nkb, kc, ks = bb_r.shape
    nj = tq // 8

    def body(xf_ref, xk_ref, dy_ref, hr_ref, hi_ref, gk_ref, dk_ref, ar_ref, ai_ref, bbr_ref, bbi_ref, cr_ref, ci_ref,
             du_ref, dbbr_ref, dbbi_ref, dcr_ref, dci_ref, dar_ref, dai_ref, dd_ref,
             gr_sc, gi_sc, hr_sc, hi_sc, carry_r, carry_i, acc_r, acc_i):
        b, n = pl.program_id(1), pl.program_id(2)

        @pl.when((b == 0) & (n == 0))
        def _():
            for ref in (dbbr_ref, dbbi_ref, dcr_ref, dci_ref, dar_ref, dai_ref, dd_ref, acc_r, acc_i):
                ref[...] = jnp.zeros_like(ref)

        @pl.when(n == 0)
        def _():
            carry_r[...] = jnp.zeros_like(carry_r)
            carry_i[...] = jnp.zeros_like(carry_i)

        u = xk_ref[...] * _rms_scale(xf_ref[...]) * gk_ref[...]
        ub = u.astype(bf16)
        dyv = dy_ref[...]
        dyb = dyv.astype(bf16)
        dd_ref[0:1, :] += jnp.sum(dyv * u, axis=0, keepdims=True)
        hrb, hib = hr_ref[...], hi_ref[...]
        dcr_ref[...] += _dot_tn(hrb, dyb)
        dci_ref[...] += _dot_tn(hib, dyb)
        hr_sc[...] = hrb.astype(f32).reshape(nj, 8, ks)
        hi_sc[...] = hib.astype(f32).reshape(nj, 8, ks)
        gr_sc[...] = _dot_nt(dyb, cr_ref[...]).reshape(nj, 8, ks)
        gi_sc[...] = _dot_nt(dyb, ci_ref[...]).reshape(nj, 8, ks)
        ar = jnp.broadcast_to(ar_ref[...], (8, ks))
        ai = jnp.broadcast_to(-ai_ref[...], (8, ks))
        accs = _interleaved_scan(gr_sc, gi_sc, ar, ai, carry_r, carry_i, nj, ks, reverse=True, h_sc=(hr_sc, hi_sc))
        cb = min(S5_CB, ks)
        for q, (a_r, a_i) in enumerate(accs):
            acc_r[:, q * cb:(q + 1) * cb] += a_r
            acc_i[:, q * cb:(q + 1) * cb] += a_i
        grb = gr_sc[...].reshape(tq, ks).astype(bf16)
        gib = gi_sc[...].reshape(tq, ks).astype(bf16)
        dbbr_ref[...] += _dot_tn(ub, grb)
        dbbi_ref[...] += _dot_tn(ub, gib)
        du_ref[...] = dyv * dk_ref[...] + _dot_nt(grb, bbr_ref[...]) + _dot_nt(gib, bbi_ref[...])

        @pl.when((b == bsz - 1) & (n == nblk - 1))
        def _():
            dar_ref[0:1, :] = jnp.sum(acc_r[...], axis=0, keepdims=True)
            dai_ref[0:1, :] = jnp.sum(acc_i[...], axis=0, keepdims=True)

    full = pl.BlockSpec((None, None, tq, d), lambda k, b, n: (b, nblk - 1 - n, 0, 0))
    chan = pl.BlockSpec((None, None, tq, kc), lambda k, b, n: (b, nblk - 1 - n, 0, k))
    stat = pl.BlockSpec((None, None, tq, ks), lambda k, b, n: (b, nblk - 1 - n, 0, k))
    vec_c = pl.BlockSpec((1, kc), lambda k, b, n: (0, k))
    vec_s = pl.BlockSpec((1, ks), lambda k, b, n: (0, k))
    bmat = pl.BlockSpec((None, kc, ks), lambda k, b, n: (k, 0, 0))
    cmat = pl.BlockSpec((None, ks, kc), lambda k, b, n: (k, 0, 0))
    acc_s = pl.BlockSpec((None, 8, ks), lambda k, b, n: (k, 0, 0))
    acc_c = pl.BlockSpec((None, 8, kc), lambda k, b, n: (k, 0, 0))
    return pl.pallas_call(
        body, name=f"s5_scan_bwd_{li}", grid=(nkb, bsz, nblk),
        out_shape=(SDS((bsz, nblk, tq, d), f32), SDS((nkb, kc, ks), f32), SDS((nkb, kc, ks), f32),
                   SDS((nkb, ks, kc), f32), SDS((nkb, ks, kc), f32), SDS((nkb, 8, ks), f32), SDS((nkb, 8, ks), f32),
                   SDS((nkb, 8, kc), f32)),
        in_specs=[full, chan, chan, stat, stat, vec_c, vec_c, vec_s, vec_s, bmat, bmat, cmat, cmat],
        out_specs=(chan, bmat, bmat, cmat, cmat, acc_s, acc_s, acc_c),
        scratch_shapes=[pltpu.VMEM((nj, 8, ks), f32)] * 4 + [pltpu.VMEM((8, ks), f32)] * 4,
        compiler_params=_cparams(3),
    )(x_il, x_il, dy, h_r, h_i, gain_pre, d_skip, ab_r, ab_i, bb_r, bb_i, c_r, c_in)


def _norm_residual_bwd(x, du, dxo, gain, name):
    t, d = x.shape
    tr = min(TOK_TR, t)

    def body(x_ref, du_ref, dxo_ref, g_ref, dx_ref, dg_ref):
        @pl.when(pl.program_id(0) == 0)
        def _():
            dg_ref[...] = jnp.zeros_like(dg_ref)

        dxp, dgain = _rms_bwd(x_ref[...], g_ref[...], du_ref[...])
        dx_ref[...] = dxo_ref[...] + dxp
        dg_ref[0:1, :] += dgain

    tok = pl.BlockSpec((tr, d), lambda i: (i, 0))
    return pl.pallas_call(
        body, name=name, grid=(t // tr,), out_shape=(SDS((t, d), f32), SDS((8, d), f32)),
        in_specs=[tok, tok, tok, pl.BlockSpec((1, d), lambda i: (0, 0))],
        out_specs=(tok, pl.BlockSpec((8, d), lambda i: (0, 0))), compiler_params=_cparams(1),
    )(x, du, dxo, gain)


def _loss_head(y, target):
    t, d = y.shape
    tr = min(TOK_TR, t)

    def body(y_ref, t_ref, l_ref, dy_ref):
        @pl.when(pl.program_id(0) == 0)
        def _():
            l_ref[...] = jnp.zeros_like(l_ref)

        err = y_ref[...] - t_ref[...]
        dy_ref[...] = err * (1.0 / d)
        l_ref[...] += jnp.sum(jnp.sum(err * err, axis=1, keepdims=True), axis=0, keepdims=True)

    tok = pl.BlockSpec((tr, d), lambda i: (i, 0))
    return pl.pallas_call(
        body, name="loss_head", grid=(t // tr,), out_shape=(SDS((8, 128), f32), SDS((t, d), f32)),
        in_specs=[tok, tok], out_specs=(pl.BlockSpec((8, 128), lambda i: (0, 0)), tok), compiler_params=_cparams(1),
    )(y, target)


def _interleave(a, tq):
    bsz, l, d = a.shape
    return a.reshape(bsz, l // tq, 8, tq // 8, d).transpose(0, 1, 3, 2, 4).reshape(bsz, l // tq, tq, d)


def _deinterleave(a, l):
    bsz, nblk, tq, d = a.shape
    return a.reshape(bsz, nblk, tq // 8, 8, d).transpose(0, 1, 3, 2, 4).reshape(bsz, l, d)


def _block_diag_in(bb, nkb):
    g, h, p = bb.shape
    gl = g // nkb
    eye = jnp.eye(gl, dtype=bb.dtype)
    m = bb.reshape(nkb, gl, h, p)[:, :, :, None, :] * eye[None, :, None, :, None]
    return m.reshape(nkb, gl * h, gl * p)


def _block_diag_in_grad(m, g, h, p):
    nkb = m.shape[0]
    gl = g // nkb
    dg = jnp.diagonal(m.reshape(nkb, gl, h, gl, p), axis1=1, axis2=3)
    return dg.transpose(0, 3, 1, 2).reshape(g, h, p)


def _block_diag_out(c, nkb):
    g, h, p = c.shape
    gl = g // nkb
    eye = jnp.eye(gl, dtype=c.dtype)
    m = c.reshape(nkb, gl, h, p).transpose(0, 1, 3, 2)[:, :, :, None, :] * eye[None, :, None, :, None]
    return m.reshape(nkb, gl * p, gl * h)


def _block_diag_out_grad(m, g, h, p):
    nkb = m.shape[0]
    gl = g // nkb
    dg = jnp.diagonal(m.reshape(nkb, gl, p, gl, h), axis1=1, axis2=3)
    return dg.transpose(0, 3, 2, 1).reshape(g, h, p)


def _pack_rows(arrs, cols, row_mult):
    flat = jnp.concatenate([a.reshape(-1).astype(f32) for a in arrs])
    rows = -(-flat.shape[0] // cols)
    rows = -(-rows // row_mult) * row_mult
    return jnp.pad(flat, (0, rows * cols - flat.shape[0])).reshape(rows, cols)


def _unpack_rows(buf, shapes):
    flat = buf.reshape(-1)
    out, off = [], 0
    for s in shapes:
        n = math.prod(s)
        out.append(flat[off:off + n].reshape(s))
        off += n
    return out


W_NAMES = ['s5_lambda_re', 's5_lambda_im', 's5_log_dt', 's5_b_re', 's5_b_im', 's5_c_re', 's5_c_im', 's5_d', 's5_w_glu',
           's5_b_glu', 'pool_w', 'pool_scale', 'ffn_w_gate', 'ffn_w_val', 'ffn_conv_w', 'ffn_conv_b', 'ffn_w_down',
           'norm_mix_pre', 'norm_mix_post', 'norm_ffn_pre', 'norm_ffn_post']
BIG = ('s5_w_glu', 'ffn_w_gate', 'ffn_w_val', 'ffn_w_down')


def kernel(x, s5_lambda_re, s5_lambda_im, s5_log_dt, s5_b_re, s5_b_im, s5_c_re, s5_c_im, s5_d, s5_w_glu, s5_b_glu, pool_w, pool_scale, ffn_w_gate, ffn_w_val, ffn_conv_w, ffn_conv_b, ffn_w_down, norm_mix_pre, norm_mix_post, norm_ffn_pre, norm_ffn_post, loss_target, m_s5_lambda_re, m_s5_lambda_im, m_s5_log_dt, m_s5_b_re, m_s5_b_im, m_s5_c_re, m_s5_c_im, m_s5_d, m_s5_w_glu, m_s5_b_glu, m_pool_w, m_pool_scale, m_ffn_w_gate, m_ffn_w_val, m_ffn_conv_w, m_ffn_conv_b, m_ffn_w_down, m_norm_mix_pre, m_norm_mix_post, m_norm_ffn_pre, m_norm_ffn_post, v_s5_lambda_re, v_s5_lambda_im, v_s5_log_dt, v_s5_b_re, v_s5_b_im, v_s5_c_re, v_s5_c_im, v_s5_d, v_s5_w_glu, v_s5_b_glu, v_pool_w, v_pool_scale, v_ffn_w_gate, v_ffn_w_val, v_ffn_conv_w, v_ffn_conv_b, v_ffn_w_down, v_norm_mix_pre, v_norm_mix_post, v_norm_ffn_pre, v_norm_ffn_post):
    w_in = dict(zip(W_NAMES, (s5_lambda_re, s5_lambda_im, s5_log_dt, s5_b_re, s5_b_im, s5_c_re, s5_c_im, s5_d, s5_w_glu,
                              s5_b_glu, pool_w, pool_scale, ffn_w_gate, ffn_w_val, ffn_conv_w, ffn_conv_b, ffn_w_down,
                              norm_mix_pre, norm_mix_post, norm_ffn_pre, norm_ffn_post)))
    m_in = dict(zip(W_NAMES, (m_s5_lambda_re, m_s5_lambda_im, m_s5_log_dt, m_s5_b_re, m_s5_b_im, m_s5_c_re, m_s5_c_im,
                              m_s5_d, m_s5_w_glu, m_s5_b_glu, m_pool_w, m_pool_scale, m_ffn_w_gate, m_ffn_w_val,
                              m_ffn_conv_w, m_ffn_conv_b, m_ffn_w_down, m_norm_mix_pre, m_norm_mix_post,
                              m_norm_ffn_pre, m_norm_ffn_post)))
    v_in = dict(zip(W_NAMES, (v_s5_lambda_re, v_s5_lambda_im, v_s5_log_dt, v_s5_b_re, v_s5_b_im, v_s5_c_re, v_s5_c_im,
                              v_s5_d, v_s5_w_glu, v_s5_b_glu, v_pool_w, v_pool_scale, v_ffn_w_gate, v_ffn_w_val,
                              v_ffn_conv_w, v_ffn_conv_b, v_ffn_w_down, v_norm_mix_pre, v_norm_mix_post,
                              v_norm_ffn_pre, v_norm_ffn_post)))

    bsz, seq, d = x.shape
    depth = ffn_w_gate.shape[0]
    n_s5, n_grp, n_state = s5_lambda_re.shape
    n_gch = s5_b_re.shape[3]
    n_pool = pool_w.shape[0]
    fs = ffn_w_gate.shape[2]
    f = N_CHIPS * fs
    gs = s5_w_glu.shape[1]
    nkb = d // S5_KB_CH
    tq = min(S5_TQ, seq)
    chip = 2 * lax.axis_index("x") + lax.axis_index("y")

    s_ffn = jnp.stack([ffn_w_gate.transpose(0, 2, 1), ffn_w_val.transpose(0, 2, 1), ffn_w_down], axis=1)
    s_ffn = s_ffn.astype(bf16).reshape(3 * depth, fs, d)
    small_shard = [pool_w, pool_scale, ffn_conv_w]
    s_small = _pack_rows(small_shard, d, 16)
    g_ffn, g_glu, g_small = _allgather_weights(s_ffn, s5_w_glu.astype(bf16), s_small)
    parts = [_unpack_rows(g_small[k], [a.shape for a in small_shard]) for k in range(N_CHIPS)]
    pool_w_full = jnp.concatenate([p[0] for p in parts], axis=2).astype(bf16)
    pool_scale_full = jnp.concatenate([p[1] for p in parts], axis=1)
    conv_w_full = jnp.concatenate([p[2] for p in parts], axis=2)

    lam_r4 = s5_lambda_re.reshape(n_s5, n_grp, 1, n_state)
    lam_i4 = s5_lambda_im.reshape(n_s5, n_grp, 1, n_state)
    ldt4 = s5_log_dt.reshape(n_s5, n_grp, 1, 1)
    b_r4 = s5_b_re.transpose(0, 1, 3, 2)
    b_i4 = s5_b_im.transpose(0, 1, 3, 2)
    ab_r4, ab_i4, bb_r4, bb_i4 = _s5_discretise(lam_r4, lam_i4, ldt4, b_r4, b_i4)
    n_st_tot = n_grp * n_state

    def s5_params(j):
        return dict(
            ab_r=ab_r4[j].reshape(1, n_st_tot), ab_i=ab_i4[j].reshape(1, n_st_tot),
            bb_r=_block_diag_in(bb_r4[j], nkb).astype(bf16), bb_i=_block_diag_in(bb_i4[j], nkb).astype(bf16),
            c_r=_block_diag_out(s5_c_re[j], nkb).astype(bf16), c_in=_block_diag_out(-s5_c_im[j], nkb).astype(bf16),
            d_skip=s5_d[j].reshape(1, d), w_glu=g_glu[j], b_glu=s5_b_glu[j].reshape(1, d))

    def row(a, i):
        return a[i].reshape(1, -1)

    saved = []
    xc = x
    for i in range(depth):
        j = i // 2
        sv = dict(x_in=xc)
        if i % 2 == 0:
            sp = s5_params(j)
            x_il = _interleave(xc, tq)
            y_il, h_r, h_i = _s5_scan_fwd(x_il, row(norm_mix_pre, i), sp["d_skip"], sp["ab_r"], sp["ab_i"],
                                          sp["bb_r"], sp["bb_i"], sp["c_r"], sp["c_in"], i)
            xo_il = _s5_glu_fwd(y_il.reshape(bsz * seq, d), x_il.reshape(bsz * seq, d), sp["w_glu"], sp["b_glu"],
                                row(norm_mix_post, i), i)
            x_mid = _deinterleave(xo_il.reshape(x_il.shape), seq)
            sv.update(sp=sp, x_il=x_il, y_il=y_il, h_r=h_r, h_i=h_i)
        else:
            x_mid = _pool_fwd(xc, row(norm_mix_pre, i), row(norm_mix_post, i), pool_w_full[j], row(pool_scale_full, j), i)
        xc, g_sv, v_sv, f_sv = _ffn_fwd(x_mid, row(norm_ffn_pre, i), row(norm_ffn_post, i), g_ffn, conv_w_full,
                                        ffn_conv_b, i)
        sv.update(x_mid=x_mid, g=g_sv, v=v_sv, f=f_sv)
        saved.append(sv)

    sq, dy = _loss_head(xc.reshape(bsz * seq, d), loss_target.reshape(bsz * seq, d))
    loss = lax.psum(sq[0, 0] * (0.5 / d), ("x", "y", "c"))

    dx = dy.reshape(bsz, seq, d)
    dw_layers = [None] * depth
    g_small_params = {n: [None] * w_in[n].shape[0] for n in W_NAMES if n not in BIG}
    dglu = [None] * n_s5
    for i in range(depth - 1, -1, -1):
        j = i // 2
        sv = saved[i]
        dx, dg, dv, hdn, hb, dfb, dgain, dconv = _ffn_bwd_act(
            sv["x_mid"], sv["f"], sv["g"], sv["v"], dx, row(norm_ffn_pre, i), row(norm_ffn_post, i), g_ffn,
            conv_w_full, ffn_conv_b, i)
        dw_layers[i] = _ffn_bwd_weights(dg.reshape(bsz * seq, f), dv.reshape(bsz * seq, f), hdn.reshape(bsz * seq, f),
                                        hb.reshape(bsz * seq, d), dfb.reshape(bsz * seq, d), i)
        g_small_params["norm_ffn_pre"][i] = dgain[0]
        g_small_params["norm_ffn_post"][i] = dgain[1]
        dconv = dconv.transpose(1, 0, 2).reshape(8, f)
        g_small_params["ffn_conv_w"][i] = dconv[0:3]
        g_small_params["ffn_conv_b"][i] = dconv[3]
        if i % 2 == 0:
            sp = sv["sp"]
            dxo_il = _interleave(dx, tq)
            dy_s, dglu[j], ds_glu = _s5_glu_bwd(sv["y_il"].reshape(bsz * seq, d), dxo_il.reshape(bsz * seq, d),
                                                sp["w_glu"], sp["b_glu"], row(norm_mix_post, i), i)
            du, dbb_r, dbb_i, dc_r, dc_in, dab_r, dab_i, dd = _s5_scan_bwd(
                sv["x_il"], dy_s.reshape(sv["x_il"].shape), sv["h_r"], sv["h_i"], row(norm_mix_pre, i), sp["d_skip"],
                sp["ab_r"], sp["ab_i"], sp["bb_r"], sp["bb_i"], sp["c_r"], sp["c_in"], i)
            dx_il, dgpre = _norm_residual_bwd(sv["x_il"].reshape(bsz * seq, d), du.reshape(bsz * seq, d),
                                              dxo_il.reshape(bsz * seq, d), row(norm_mix_pre, i), f"s5_pre_bwd_{i}")
            dx = _deinterleave(dx_il.reshape(sv["x_il"].shape), seq)
            g_small_params["norm_mix_pre"][i] = dgpre[0]
            g_small_params["norm_mix_post"][i] = ds_glu[0]
            g_small_params["s5_b_glu"][j] = ds_glu[1]
            g_small_params["s5_d"][j] = dd[:, 0, :].reshape(d)
            g_small_params["s5_c_re"][j] = _block_diag_out_grad(dc_r, n_grp, n_gch, n_state)
            g_small_params["s5_c_im"][j] = -_block_diag_out_grad(dc_in, n_grp, n_gch, n_state)
            sv["g_ab"] = (dab_r[:, 0, :].reshape(n_grp, 1, n_state), dab_i[:, 0, :].reshape(n_grp, 1, n_state),
                          _block_diag_in_grad(dbb_r, n_grp, n_gch, n_state),
                          _block_diag_in_grad(dbb_i, n_grp, n_gch, n_state))
        else:
            dx, dpw, dsm = _pool_bwd(sv["x_in"], dx, row(norm_mix_pre, i), row(norm_mix_post, i), pool_w_full[j],
                                     row(pool_scale_full, j), i)
            g_small_params["norm_mix_pre"][i] = dsm[0]
            g_small_params["norm_mix_post"][i] = dsm[1]
            g_small_params["pool_scale"][j] = dsm[2]
            g_small_params["pool_w"][j] = dpw
    grad_x = dx

    g_ab = [saved[2 * j]["g_ab"] for j in range(n_s5)]
    d_lr, d_li, d_ld, d_br, d_bi = _s5_discretise_bwd(
        lam_r4, lam_i4, ldt4, b_r4, b_i4, jnp.stack([g[0] for g in g_ab]), jnp.stack([g[1] for g in g_ab]),
        jnp.stack([g[2] for g in g_ab]), jnp.stack([g[3] for g in g_ab]))
    small_full = {n: (jnp.stack(v) if v[0] is not None else None) for n, v in g_small_params.items()}
    small_full["s5_lambda_re"] = d_lr.reshape(n_s5, n_grp, n_state)
    small_full["s5_lambda_im"] = d_li.reshape(n_s5, n_grp, n_state)
    small_full["s5_log_dt"] = d_ld.reshape(n_s5, n_grp)
    small_full["s5_b_re"] = d_br.transpose(0, 1, 3, 2)
    small_full["s5_b_im"] = d_bi.transpose(0, 1, 3, 2)
    small_names = [n for n in W_NAMES if n not in BIG]
    small_shapes = [small_full[n].shape for n in small_names]
    q = _pack_rows([small_full[n] for n in small_names], d, 64)

    r_ffn, r_glu, r_q = _exchange_grad_slabs(dw_layers, jnp.stack(dglu).astype(bf16), q)
    p_ffn = _sum_slots(r_ffn.reshape(N_CHIPS, depth * 3 * fs, d), "sum_slots_ffn")
    p_glu = _sum_slots(r_glu.reshape(N_CHIPS, n_s5 * gs, d), "sum_slots_glu")
    p_q = _sum_slots(r_q, "sum_slots_small")
    o_ffn, o_glu, o_q = _exchange_with_sibling([p_ffn, p_glu, p_q])

    outs = {}

    def put(name, res, shape):
        outs[name] = tuple(r.reshape(shape) for r in res)

    pf, of = p_ffn.reshape(depth, 3, fs, d), o_ffn.reshape(depth, 3, fs, d)
    for kind, name in ((0, "ffn_w_gate"), (1, "ffn_w_val")):
        parts_g = [a[:, kind].transpose(0, 2, 1).reshape(depth * d, fs) for a in (pf, of)]
        put(name, _adamw(parts_g, w_in[name].reshape(depth * d, fs), m_in[name].reshape(depth * d, fs),
                         v_in[name].reshape(depth * d, fs), f"adamw_{name}"), w_in[name].shape)
    put("ffn_w_down", _adamw([pf[:, 2].reshape(depth * fs, d), of[:, 2].reshape(depth * fs, d)],
                             ffn_w_down.reshape(depth * fs, d), m_ffn_w_down.reshape(depth * fs, d),
                             v_ffn_w_down.reshape(depth * fs, d), "adamw_ffn_w_down"), ffn_w_down.shape)
    put("s5_w_glu", _adamw([p_glu, o_glu], s5_w_glu.reshape(n_s5 * gs, d), m_s5_w_glu.reshape(n_s5 * gs, d),
                           v_s5_w_glu.reshape(n_s5 * gs, d), "adamw_s5_w_glu"), s5_w_glu.shape)

    q_tot = _add2(p_q, o_q, "sum_small")
    g_small = dict(zip(small_names, _unpack_rows(q_tot, small_shapes)))
    g_small["pool_w"] = lax.dynamic_slice_in_dim(g_small["pool_w"], chip * pool_w.shape[2], pool_w.shape[2], axis=2)
    g_small["pool_scale"] = lax.dynamic_slice_in_dim(g_small["pool_scale"], chip * pool_scale.shape[1],
                                                     pool_scale.shape[1], axis=1)
    g_small["ffn_conv_w"] = lax.dynamic_slice_in_dim(g_small["ffn_conv_w"], chip * fs, fs, axis=2)
    local_shapes = [w_in[n].shape for n in small_names]
    res = _adamw([_pack_rows([g_small[n] for n in small_names], d, 64)],
                 _pack_rows([w_in[n] for n in small_names], d, 64), _pack_rows([m_in[n] for n in small_names], d, 64),
                 _pack_rows([v_in[n] for n in small_names], d, 64), "adamw_small")
    unpacked = [_unpack_rows(r, local_shapes) for r in res]
    for idx, n in enumerate(small_names):
        outs[n] = (g_small[n], unpacked[1][idx], unpacked[2][idx], unpacked[3][idx])

    return (loss, grad_x, *[outs[n][0] for n in W_NAMES], *[outs[n][1] for n in W_NAMES],
            *[outs[n][2] for n in W_NAMES], *[outs[n][3] for n in W_NAMES])
```

```python
import math

import jax
import jax.numpy as jnp
from jax import lax
from jax.experimental import pallas as pl
from jax.experimental.pallas import tpu as pltpu

f32, bf16 = jnp.float32, jnp.bfloat16
SDS = jax.ShapeDtypeStruct
MESH = pl.DeviceIdType.MESH

RMS_EPS = 1e-6
GELU_C = math.sqrt(2.0 / math.pi)
GELU_K = 0.044715
ADAM_LR, ADAM_B1, ADAM_B2, ADAM_EPS, ADAM_WD, ADAM_STEP = 0.001, 0.9, 0.999, 1e-08, 0.01, 10
POOL_WINDOWS = (2, 4, 8, 16)
POOL_HALO = 16
S5_GROUP_CH = 16
S5_STATE = 64
S5_KB_CH = 256
N_CHIPS = 4

FFN_TL = 512
FFN_FT = 256
WG_TR = 2048
POOL_TL = 512
S5_TQ = 512
S5_CB = 512
TOK_TR = 512
VMEM_LIMIT = 56 * 1024 * 1024

HBM = pl.BlockSpec(memory_space=pltpu.HBM)


def _cparams(n_axes=0, side_effects=False):
    kw = dict(vmem_limit_bytes=VMEM_LIMIT)
    if n_axes:
        kw["dimension_semantics"] = ("arbitrary",) * n_axes
    if side_effects:
        kw["has_side_effects"] = True
    return pltpu.CompilerParams(**kw)


def _dot(a, b):
    return jnp.dot(a, b, preferred_element_type=f32)


def _dot_nt(a, b):
    return lax.dot_general(a, b, (((1,), (1,)), ((), ())), preferred_element_type=f32)


def _dot_tn(a, b):
    return lax.dot_general(a, b, (((0,), (0,)), ((), ())), preferred_element_type=f32)


def _rms_scale(x):
    return lax.rsqrt(jnp.mean(x * x, axis=-1, keepdims=True) + RMS_EPS)


def _rms_fwd(x, gain):
    return x * _rms_scale(x) * gain


def _rms_bwd(x, gain, dy):
    r = _rms_scale(x)
    xn = x * r
    dgain = jnp.sum(dy * xn, axis=0, keepdims=True)
    dxn = dy * gain
    dx = r * (dxn - xn * jnp.mean(dxn * xn, axis=-1, keepdims=True))
    return dx, dgain


def _gelu(x):
    t = jnp.tanh(x * (GELU_C + (GELU_C * GELU_K) * (x * x)))
    hx = 0.5 * x
    return hx + hx * t


def _gelu_and_grad(x):
    x2 = x * x
    t = jnp.tanh(x * (GELU_C + (GELU_C * GELU_K) * x2))
    hx = 0.5 * x
    return hx + hx * t, (0.5 + 0.5 * t) + hx * (1.0 - t * t) * (GELU_C + (3.0 * GELU_C * GELU_K) * x2)


def _sigmoid(x):
    return 1.0 / (1.0 + jnp.exp(-x))


def _cmul(ar, ai, br, bi):
    return ar * br - ai * bi, ar * bi + ai * br


def _row_block(n, cap):
    best = None
    for d in range(16, min(n, cap) + 1, 16):
        if n % d == 0:
            best = d
    assert best is not None, n
    return best


def _mesh_pos():
    return lax.axis_index("x"), lax.axis_index("y"), lax.axis_index("c")


def _other_chips(x, y):
    return [(1 - x, y), (x, 1 - y), (1 - x, 1 - y)]


def _slab_index(ndim, axis, start, size):
    return tuple(pl.ds(start, size) if a == axis else slice(None) for a in range(ndim))


class _ChipExchange:
    def __init__(self, kind, axis, src, dst, send_sems, recv_sems, loc_sem):
        x, y, c = _mesh_pos()
        me = 2 * x + y
        nd = len(src.shape)
        size = src.shape[axis] if kind == "gather" else src.shape[axis] // N_CHIPS

        def src_for(kk):
            return src.at[_slab_index(nd, axis, kk * size, size)] if kind == "slab" else src

        def dst_for(kk):
            return dst.at[_slab_index(nd, axis, kk * size, size)] if kind == "gather" else dst.at[kk]

        self.own = pltpu.make_async_copy(src_for(me), dst_for(me), loc_sem)
        self.sends, self.recvs = [], []
        for j, chip in enumerate(_other_chips(x, y)):
            kk = 2 * chip[0] + chip[1]
            peer = dict(send_sem=send_sems.at[j], recv_sem=recv_sems.at[j], device_id=(chip[0], chip[1], c),
                        device_id_type=MESH)
            self.sends.append(pltpu.make_async_remote_copy(src_ref=src_for(kk), dst_ref=dst_for(me), **peer))
            self.recvs.append(pltpu.make_async_remote_copy(src_ref=src_for(me), dst_ref=dst_for(kk), **peer))

    def start(self):
        self.own.start()
        for cp in self.sends:
            cp.start()

    def finish(self):
        for cp in self.recvs:
            cp.wait_recv()
        for cp in self.sends:
            cp.wait_send()
        self.own.wait()


def _exchange_out_shape(kind, axis, src):
    if kind == "gather":
        shape = tuple(N_CHIPS * n if a == axis else n for a, n in enumerate(src.shape))
    elif kind == "slab":
        shape = (N_CHIPS,) + tuple(n // N_CHIPS if a == axis else n for a, n in enumerate(src.shape))
    else:
        shape = (N_CHIPS,) + tuple(src.shape)
    return SDS(shape, src.dtype)


EXCHANGE_SEMS = [pltpu.SemaphoreType.DMA((3,)), pltpu.SemaphoreType.DMA((3,)), pltpu.SemaphoreType.DMA((1,))]


def _chip_exchange(items, name):
    n = len(items)

    def body(*refs):
        ins, outs, sems = refs[:n], refs[n:2 * n], refs[2 * n:]
        exs = [_ChipExchange(kind, axis, ins[i], outs[i], sems[3 * i], sems[3 * i + 1], sems[3 * i + 2].at[0])
               for i, (kind, axis, _) in enumerate(items)]
        for ex in exs:
            ex.start()
        for ex in exs:
            ex.finish()

    return pl.pallas_call(
        body, name=name, out_shape=tuple(_exchange_out_shape(k, a, arr) for k, a, arr in items),
        in_specs=[HBM] * n, out_specs=tuple([HBM] * n), scratch_shapes=EXCHANGE_SEMS * n,
        compiler_params=_cparams(side_effects=True),
    )(*[arr for _, _, arr in items])


def _hosted(host, host_in, host_out, sems, first, last):
    if host is None:
        return lambda: None
    ex = _ChipExchange(host[0], host[1], host_in, host_out, sems[0], sems[1], sems[2].at[0])

    @pl.when(first)
    def _():
        ex.start()

    def finish():
        @pl.when(last)
        def _():
            ex.finish()

    return finish


def _host_args(host):
    if host is None:
        return [], [], [], [], []
    return [HBM], [_exchange_out_shape(*host)], [HBM], EXCHANGE_SEMS, [host[2]]


def _exchange_with_sibling(parts):
    n = len(parts)

    def body(*refs):
        ins, outs = refs[:n], refs[n:2 * n]
        send_sems, recv_sems = refs[2 * n:]
        x, y, c = _mesh_pos()
        cps = [pltpu.make_async_remote_copy(src_ref=s, dst_ref=t, send_sem=send_sems.at[i], recv_sem=recv_sems.at[i],
                                            device_id=(x, y, 1 - c), device_id_type=MESH)
               for i, (s, t) in enumerate(zip(ins, outs))]
        for cp in cps:
            cp.start()
        for cp in cps:
            cp.wait()

    return pl.pallas_call(
        body, name="exchange_with_sibling",
        out_shape=tuple(SDS(p.shape, p.dtype) for p in parts),
        in_specs=[HBM] * n, out_specs=tuple([HBM] * n),
        scratch_shapes=[pltpu.SemaphoreType.DMA((n,)), pltpu.SemaphoreType.DMA((n,))],
        compiler_params=_cparams(side_effects=True),
    )(*parts)


def _sum_slots(r, name):
    _, rows, cols = r.shape
    tr = _row_block(rows, 512)

    def body(r_ref, o_ref):
        o_ref[...] = ((r_ref[0].astype(f32) + r_ref[1].astype(f32)) + r_ref[2].astype(f32)) + r_ref[3].astype(f32)

    return pl.pallas_call(
        body, name=name, grid=(rows // tr,), out_shape=SDS((rows, cols), f32),
        in_specs=[pl.BlockSpec((N_CHIPS, tr, cols), lambda i: (0, i, 0))],
        out_specs=pl.BlockSpec((tr, cols), lambda i: (i, 0)),
        compiler_params=_cparams(1),
    )(r)


def _sum_slots_layers(rs, name):
    _, rows, cols = rs[0].shape
    nl = len(rs)
    tr = _row_block(rows, 256)

    def body(*refs):
        o_ref = refs[nl]
        for l in range(nl):
            r = refs[l]
            o_ref[l] = ((r[0].astype(f32) + r[1].astype(f32)) + r[2].astype(f32)) + r[3].astype(f32)

    return pl.pallas_call(
        body, name=name, grid=(rows // tr,), out_shape=SDS((nl, rows, cols), f32),
        in_specs=[pl.BlockSpec((N_CHIPS, tr, cols), lambda i: (0, i, 0))] * nl,
        out_specs=pl.BlockSpec((nl, tr, cols), lambda i: (0, i, 0)),
        compiler_params=_cparams(1),
    )(*rs)


def _add_pairs(pairs, name):
    n = len(pairs)

    def body(*refs):
        for i in range(n):
            refs[2 * n + i][...] = refs[2 * i][...] + refs[2 * i + 1][...]

    return pl.pallas_call(body, name=name, out_shape=tuple(SDS(a.shape, f32) for a, _ in pairs),
                          compiler_params=_cparams())(*[t for pair in pairs for t in pair])


def _adamw(g_parts, w, m, v, name):
    rows, cols = w.shape
    tr = _row_block(rows, 512)
    n_g = len(g_parts)
    c1 = 1.0 / (1.0 - ADAM_B1 ** ADAM_STEP)
    c2 = 1.0 / (1.0 - ADAM_B2 ** ADAM_STEP)

    def body(*refs):
        g_refs = refs[:n_g]
        w_ref, m_ref, v_ref, go_ref, d_ref, mo_ref, vo_ref = refs[n_g:]
        g = g_refs[0][...]
        for r in g_refs[1:]:
            g = g + r[...]
        mn = ADAM_B1 * m_ref[...] + (1.0 - ADAM_B1) * g
        vn = ADAM_B2 * v_ref[...] + (1.0 - ADAM_B2) * (g * g)
        go_ref[...] = g
        mo_ref[...] = mn
        vo_ref[...] = vn
        d_ref[...] = -ADAM_LR * ((mn * c1) / (jnp.sqrt(vn * c2) + ADAM_EPS) + ADAM_WD * w_ref[...])

    spec = pl.BlockSpec((tr, cols), lambda i: (i, 0))
    return pl.pallas_call(
        body, name=name, grid=(rows // tr,), out_shape=tuple(SDS((rows, cols), f32) for _ in range(4)),
        in_specs=[spec] * (n_g + 3), out_specs=(spec,) * 4, compiler_params=_cparams(1),
    )(*g_parts, w, m, v)


def _causal_conv(g, prev, w, bias):
    row = lax.broadcasted_iota(jnp.int32, g.shape, 0)
    p7, p6 = prev[7:8, :], prev[6:7, :]
    g1 = jnp.where(row == 0, p7, pltpu.roll(g, 1, axis=0))
    g2 = jnp.where(row == 0, p6, jnp.where(row == 1, p7, pltpu.roll(g, 2, axis=0)))
    return bias + w[0:1, :] * g2 + w[1:2, :] * g1 + w[2:3, :] * g, g1, g2


def _ffn_fwd(x_mid, gain_pre, gain_post, w_l, conv_w, conv_b, li, host=None):
    bsz, l, d = x_mid.shape
    f = w_l.shape[1]
    tl, ft = min(FFN_TL, l), min(FFN_FT, f)
    nt, nf = l // tl, f // ft
    h_in, h_shape, h_out, h_scr, h_ops = _host_args(host)

    def body(x_ref, gpre_ref, gpost_ref, wg_ref, wv_ref, wd_ref, cw_ref, cb_ref, *rest):
        xo_ref, g_ref, v_ref, f_ref = rest[len(h_in):len(h_in) + 4]
        h_sc, facc, gprev = rest[len(h_in) + 4 + len(h_out):len(h_in) + 7 + len(h_out)]
        b, t, j = pl.program_id(0), pl.program_id(1), pl.program_id(2)
        finish = _hosted(host, rest[0], rest[len(h_in) + 4], rest[len(h_in) + 7 + len(h_out):],
                         (b == 0) & (t == 0) & (j == 0), (b == bsz - 1) & (t == nt - 1) & (j == nf - 1))

        @pl.when(j == 0)
        def _():
            h_sc[...] = _rms_fwd(x_ref[...], gpre_ref[...]).astype(bf16)
            facc[...] = jnp.zeros_like(facc)

        @pl.when(t == 0)
        def _():
            gprev[j] = jnp.zeros((8, ft), f32)

        h = h_sc[...]
        g = _dot_nt(h, wg_ref[...])
        v = _dot_nt(h, wv_ref[...])
        g_ref[...] = g.astype(bf16)
        v_ref[...] = v.astype(bf16)
        gc, _, _ = _causal_conv(g, gprev[j], cw_ref[...], cb_ref[...])
        gprev[j] = g[tl - 8:tl, :]
        hdn = _gelu(gc) * v
        facc[...] += _dot(hdn.astype(bf16), wd_ref[...])

        @pl.when(j == nf - 1)
        def _():
            fv = facc[...]
            f_ref[...] = fv
            xo_ref[...] = x_ref[...] + _rms_fwd(fv, gpost_ref[...])

        finish()

    tok = pl.BlockSpec((None, tl, d), lambda b, t, j: (b, t, 0))
    hid = pl.BlockSpec((None, tl, ft), lambda b, t, j: (b, t, j))
    gain = pl.BlockSpec((1, d), lambda b, t, j: (0, 0))

    def wspec(kind):
        return pl.BlockSpec((None, ft, d), lambda b, t, j: (kind, j, 0))

    return pl.pallas_call(
        body, name=f"ffn_fwd_{li}", grid=(bsz, nt, nf),
        out_shape=(SDS((bsz, l, d), f32), SDS((bsz, l, f), bf16), SDS((bsz, l, f), bf16), SDS((bsz, l, d), f32),
                   *h_shape),
        in_specs=[tok, gain, gain, wspec(0), wspec(1), wspec(2),
                  pl.BlockSpec((None, 3, ft), lambda b, t, j: (li, 0, j)),
                  pl.BlockSpec((None, 1, ft), lambda b, t, j: (li, 0, j)), *h_in],
        out_specs=(tok, hid, hid, tok, *h_out),
        scratch_shapes=[pltpu.VMEM((tl, d), bf16), pltpu.VMEM((tl, d), f32), pltpu.VMEM((nf, 8, ft), f32), *h_scr],
        compiler_params=_cparams(3, side_effects=host is not None),
    )(x_mid, gain_pre, gain_post, w_l, w_l, w_l, conv_w, conv_b.reshape(conv_b.shape[0], 1, f), *h_ops)


def _ffn_bwd_act(x_mid, f_sv, g_sv, v_sv, dxo, gain_pre, gain_post, w_l, conv_w, conv_b, li, host=None):
    bsz, l, d = x_mid.shape
    f = w_l.shape[1]
    tl, ft = min(FFN_TL, l), min(FFN_FT, f)
    nt, nf = l // tl, f // ft
    h_in, h_shape, h_out, h_scr, h_ops = _host_args(host)

    def body(x_ref, f_ref, dxo_ref, g_ref, v_ref, gh_ref, gpre_ref, gpost_ref, wg_ref, wv_ref, wd_ref, cw_ref, cb_ref,
             *rest):
        o0 = len(h_in)
        dx_ref, dg_ref, dv_ref, hdn_ref, h_ref, df_ref, dgain_ref, dconv_ref = rest[o0:o0 + 8]
        s0 = o0 + 8 + len(h_out)
        h_sc, df_sc, dh_acc, dgc_next = rest[s0:s0 + 4]
        b, t, j = pl.program_id(0), pl.program_id(1), pl.program_id(2)
        tt = nt - 1 - t
        finish = _hosted(host, rest[0], rest[o0 + 8], rest[s0 + 4:],
                         (b == 0) & (t == 0) & (j == 0), (b == bsz - 1) & (t == nt - 1) & (j == nf - 1))

        @pl.when((b == 0) & (t == 0) & (j == 0))
        def _():
            dgain_ref[...] = jnp.zeros_like(dgain_ref)
            dconv_ref[...] = jnp.zeros_like(dconv_ref)

        @pl.when(j == 0)
        def _():
            hb = _rms_fwd(x_ref[...], gpre_ref[...]).astype(bf16)
            h_sc[...] = hb
            h_ref[...] = hb
            df, dgp = _rms_bwd(f_ref[...], gpost_ref[...], dxo_ref[...])
            dfb = df.astype(bf16)
            df_sc[...] = dfb
            df_ref[...] = dfb
            dgain_ref[1:2, :] += dgp
            dh_acc[...] = jnp.zeros_like(dh_acc)

        @pl.when(t == 0)
        def _():
            dgc_next[j] = jnp.zeros((8, ft), f32)

        dhdn = _dot_nt(df_sc[...], wd_ref[...])
        g = g_ref[...].astype(f32)
        v = v_ref[...].astype(f32)
        prev = jnp.where(tt > 0, gh_ref[...].astype(f32)[8:16, :], 0.0)
        w = cw_ref[...]
        gc, g1, g2 = _causal_conv(g, prev, w, cb_ref[...])
        u, ug = _gelu_and_grad(gc)
        hdn_ref[...] = (u * v).astype(bf16)
        dv = dhdn * u
        dgc = dhdn * v * ug
        nxt = dgc_next[j]
        row = lax.broadcasted_iota(jnp.int32, dgc.shape, 0)
        n0, n1 = nxt[0:1, :], nxt[1:2, :]
        d1 = jnp.where(row == tl - 1, n0, pltpu.roll(dgc, tl - 1, axis=0))
        d2 = jnp.where(row == tl - 1, n1, jnp.where(row == tl - 2, n0, pltpu.roll(dgc, tl - 2, axis=0)))
        dg = w[2:3, :] * dgc + w[1:2, :] * d1 + w[0:1, :] * d2
        dgc_next[j] = dgc[0:8, :]
        dgb, dvb = dg.astype(bf16), dv.astype(bf16)
        dg_ref[...] = dgb
        dv_ref[...] = dvb
        dh_acc[...] += _dot(dgb, wg_ref[...]) + _dot(dvb, wv_ref[...])
        dconv_ref[j, 0:1, :] += jnp.sum(dgc * g2, axis=0, keepdims=True)
        dconv_ref[j, 1:2, :] += jnp.sum(dgc * g1, axis=0, keepdims=True)
        dconv_ref[j, 2:3, :] += jnp.sum(dgc * g, axis=0, keepdims=True)
        dconv_ref[j, 3:4, :] += jnp.sum(dgc, axis=0, keepdims=True)

        @pl.when(j == nf - 1)
        def _():
            dxp, dgp = _rms_bwd(x_ref[...], gpre_ref[...], dh_acc[...])
            dx_ref[...] = dxo_ref[...] + dxp
            dgain_ref[0:1, :] += dgp

        finish()

    tok = pl.BlockSpec((None, tl, d), lambda b, t, j: (b, nt - 1 - t, 0))
    hid = pl.BlockSpec((None, tl, ft), lambda b, t, j: (b, nt - 1 - t, j))
    halo = pl.BlockSpec((None, 16, ft), lambda b, t, j: (b, jnp.maximum((nt - 1 - t) * (tl // 16) - 1, 0), j))
    gain = pl.BlockSpec((1, d), lambda b, t, j: (0, 0))

    def wspec(kind):
        return pl.BlockSpec((None, ft, d), lambda b, t, j: (kind, j, 0))

    return pl.pallas_call(
        body, name=f"ffn_bwd_act_{li}", grid=(bsz, nt, nf),
        out_shape=(SDS((bsz, l, d), f32), SDS((bsz, l, f), bf16), SDS((bsz, l, f), bf16), SDS((bsz, l, f), bf16),
                   SDS((bsz, l, d), bf16), SDS((bsz, l, d), bf16), SDS((8, d), f32), SDS((nf, 8, ft), f32), *h_shape),
        in_specs=[tok, tok, tok, hid, hid, halo, gain, gain, wspec(0), wspec(1), wspec(2),
                  pl.BlockSpec((None, 3, ft), lambda b, t, j: (li, 0, j)),
                  pl.BlockSpec((None, 1, ft), lambda b, t, j: (li, 0, j)), *h_in],
        out_specs=(tok, hid, hid, hid, tok, tok,
                   pl.BlockSpec((8, d), lambda b, t, j: (0, 0)), pl.BlockSpec((nf, 8, ft), lambda b, t, j: (0, 0, 0)),
                   *h_out),
        scratch_shapes=[pltpu.VMEM((tl, d), bf16), pltpu.VMEM((tl, d), bf16), pltpu.VMEM((tl, d), f32),
                        pltpu.VMEM((nf, 8, ft), f32), *h_scr],
        compiler_params=_cparams(3, side_effects=host is not None),
    )(x_mid, f_sv, dxo, g_sv, v_sv, g_sv, gain_pre, gain_post, w_l, w_l, w_l, conv_w,
      conv_b.reshape(conv_b.shape[0], 1, f), *h_ops)


def _ffn_bwd_weights(dg, dv, hdn, h, df, li):
    t, f = dg.shape
    d = h.shape[1]
    ft, tr = min(FFN_FT, f), min(WG_TR, t)
    nf, nr = f // ft, t // tr

    def body(dg_ref, dv_ref, hdn_ref, h_ref, df_ref, o_ref, acc):
        r = pl.program_id(1)

        @pl.when(r == 0)
        def _():
            acc[...] = jnp.zeros_like(acc)

        hb = h_ref[...]
        acc[0] += _dot_tn(dg_ref[...], hb)
        acc[1] += _dot_tn(dv_ref[...], hb)
        acc[2] += _dot_tn(hdn_ref[...], df_ref[...])

        @pl.when(r == nr - 1)
        def _():
            o_ref[...] = acc[...].astype(bf16)

    hid = pl.BlockSpec((tr, ft), lambda j, r: (r, j))
    tok = pl.BlockSpec((tr, d), lambda j, r: (r, 0))
    return pl.pallas_call(
        body, name=f"ffn_bwd_weights_{li}", grid=(nf, nr), out_shape=SDS((3, f, d), bf16),
        in_specs=[hid, hid, hid, tok, tok], out_specs=pl.BlockSpec((3, ft, d), lambda j, r: (0, j, 0)),
        scratch_shapes=[pltpu.VMEM((3, ft, d), f32)], compiler_params=_cparams(2),
    )(dg, dv, hdn, h, df)


def _pool_counts(t0, tl, d):
    gch = d // len(POOL_WINDOWS)
    tpos = (t0 + lax.broadcasted_iota(jnp.int32, (tl, d), 0) + 1).astype(f32)
    lane = lax.broadcasted_iota(jnp.int32, (tl, d), 1)
    win = jnp.full((tl, d), float(POOL_WINDOWS[-1]), f32)
    for gi in range(len(POOL_WINDOWS) - 2, -1, -1):
        win = jnp.where(lane < (gi + 1) * gch, float(POOL_WINDOWS[gi]), win)
    return jnp.minimum(tpos, win)


def _pool_select(parts, tl, d):
    gch = d // len(POOL_WINDOWS)
    lane = lax.broadcasted_iota(jnp.int32, (tl, d), 1)
    out = parts[-1]
    for gi in range(len(parts) - 2, -1, -1):
        out = jnp.where(lane < (gi + 1) * gch, parts[gi], out)
    return out


def _pool_window_sums(u, halo, tl):
    ext = jnp.concatenate([halo, u], axis=0)
    sums, cur = [], ext
    for k in (1, 2, 4, 8):
        cur = cur + pltpu.roll(cur, k, axis=0)
        sums.append(cur[POOL_HALO:POOL_HALO + tl, :])
    return sums


def _pool_mix(u, halo, cnt, pw_ref, scale, tl, d):
    gch = d // len(POOL_WINDOWS)
    pooled = _pool_select(_pool_window_sums(u, halo, tl), tl, d) / cnt
    diff = pooled - u
    outs = [_dot(diff[:, gi * gch:(gi + 1) * gch].astype(bf16), pw_ref[gi]) for gi in range(len(POOL_WINDOWS))]
    return diff, jnp.concatenate(outs, axis=1)


def _pool_fwd(x, gain_pre, gain_post, pw, scale, li):
    bsz, l, d = x.shape
    tl = min(POOL_TL, l)
    nt = l // tl

    def body(x_ref, gpre_ref, gpost_ref, pw_ref, sc_ref, xo_ref, halo):
        t = pl.program_id(1)

        @pl.when(t == 0)
        def _():
            halo[...] = jnp.zeros_like(halo)

        xv = x_ref[...]
        u = _rms_fwd(xv, gpre_ref[...])
        _, out = _pool_mix(u, halo[...], _pool_counts(t * tl, tl, d), pw_ref, sc_ref[...], tl, d)
        halo[...] = u[tl - POOL_HALO:tl, :]
        xo_ref[...] = xv + _rms_fwd(out * sc_ref[...], gpost_ref[...])

    tok = pl.BlockSpec((None, tl, d), lambda b, t: (b, t, 0))
    gain = pl.BlockSpec((1, d), lambda b, t: (0, 0))
    return pl.pallas_call(
        body, name=f"pool_fwd_{li}", grid=(bsz, nt), out_shape=SDS((bsz, l, d), f32),
        in_specs=[tok, gain, gain, pl.BlockSpec(pw.shape, lambda b, t: (0, 0, 0)), gain], out_specs=tok,
        scratch_shapes=[pltpu.VMEM((POOL_HALO, d), f32)], compiler_params=_cparams(2),
    )(x, gain_pre, gain_post, pw, scale)


def _pool_bwd(x, dxo, gain_pre, gain_post, pw, scale, li):
    bsz, l, d = x.shape
    tl = min(POOL_TL, l)
    nt = l // tl
    ng = len(POOL_WINDOWS)
    gch = d // ng
    n_ext = tl + POOL_HALO

    def body(x_ref, xh_ref, dxo_ref, gpre_ref, gpost_ref, pw_ref, sc_ref, dx_ref, dpw_ref, ds_ref, qnext):
        b, t = pl.program_id(0), pl.program_id(1)
        tt = nt - 1 - t

        @pl.when((b == 0) & (t == 0))
        def _():
            dpw_ref[...] = jnp.zeros_like(dpw_ref)
            ds_ref[...] = jnp.zeros_like(ds_ref)

        @pl.when(t == 0)
        def _():
            qnext[...] = jnp.zeros_like(qnext)

        xv = x_ref[...]
        gpre = gpre_ref[...]
        u = _rms_fwd(xv, gpre)
        uh = _rms_fwd(jnp.where(tt > 0, xh_ref[...], 0.0), gpre)
        cnt = _pool_counts(tt * tl, tl, d)
        scale_v = sc_ref[...]
        diff, out = _pool_mix(u, uh, cnt, pw_ref, scale_v, tl, d)
        dxo_v = dxo_ref[...]
        dm, dgpost = _rms_bwd(out * scale_v, gpost_ref[...], dxo_v)
        ds_ref[1:2, :] += dgpost
        ds_ref[2:3, :] += jnp.sum(dm * out, axis=0, keepdims=True)
        dout = (dm * scale_v).astype(bf16)
        ddiffs = []
        for gi in range(ng):
            sl = slice(gi * gch, (gi + 1) * gch)
            dpw_ref[gi] += _dot_tn(diff[:, sl].astype(bf16), dout[:, sl])
            ddiffs.append(_dot_nt(dout[:, sl], pw_ref[gi]))
        ddiff = jnp.concatenate(ddiffs, axis=1)
        q = ddiff / cnt
        ext = jnp.concatenate([q, qnext[...]], axis=0)
        sums, cur = [], ext
        for k in (1, 2, 4, 8):
            cur = cur + pltpu.roll(cur, n_ext - k, axis=0)
            sums.append(cur[0:tl, :])
        du = _pool_select(sums, tl, d) - ddiff
        qnext[...] = q[0:POOL_HALO, :]
        dxp, dgpre = _rms_bwd(xv, gpre, du)
        ds_ref[0:1, :] += dgpre
        dx_ref[...] = dxo_v + dxp

    tok = pl.BlockSpec((None, tl, d), lambda b, t: (b, nt - 1 - t, 0))
    halo = pl.BlockSpec((None, POOL_HALO, d),
                        lambda b, t: (b, jnp.maximum((nt - 1 - t) * (tl // POOL_HALO) - 1, 0), 0))
    gain = pl.BlockSpec((1, d), lambda b, t: (0, 0))
    return pl.pallas_call(
        body, name=f"pool_bwd_{li}", grid=(bsz, nt),
        out_shape=(SDS((bsz, l, d), f32), SDS((ng, gch, gch), f32), SDS((8, d), f32)),
        in_specs=[tok, halo, tok, gain, gain, pl.BlockSpec(pw.shape, lambda b, t: (0, 0, 0)), gain],
        out_specs=(tok, pl.BlockSpec((ng, gch, gch), lambda b, t: (0, 0, 0)), pl.BlockSpec((8, d), lambda b, t: (0, 0))),
        scratch_shapes=[pltpu.VMEM((POOL_HALO, d), f32)], compiler_params=_cparams(2),
    )(x, x, dxo, gain_pre, gain_post, pw, scale)


def _s5_discretise(lam_re, lam_im, log_dt, b_re, b_im):
    def body(lr_ref, li_ref, ld_ref, br_ref, bi_ref, ar_ref, ai_ref, bbr_ref, bbi_ref):
        lr, li = lr_ref[...], li_ref[...]
        dt = jnp.exp(ld_ref[...])
        mag = jnp.exp(lr * dt)
        ar = mag * jnp.cos(li * dt)
        ai = mag * jnp.sin(li * dt)
        den = lr * lr + li * li
        nr, ni = ar - 1.0, ai
        fr = (nr * lr + ni * li) / den
        fi = (ni * lr - nr * li) / den
        br, bi = br_ref[...], bi_ref[...]
        ar_ref[...] = ar
        ai_ref[...] = ai
        bbr_ref[...] = fr * br - fi * bi
        bbi_ref[...] = fr * bi + fi * br

    return pl.pallas_call(
        body, name="s5_discretise",
        out_shape=(SDS(lam_re.shape, f32), SDS(lam_re.shape, f32), SDS(b_re.shape, f32), SDS(b_re.shape, f32)),
        compiler_params=_cparams(),
    )(lam_re, lam_im, log_dt, b_re, b_im)


def _s5_discretise_bwd(lam_re, lam_im, log_dt, b_re, b_im, g_ar, g_ai, g_bbr, g_bbi):
    def body(lr_ref, li_ref, ld_ref, br_ref, bi_ref, gar_ref, gai_ref, gbbr_ref, gbbi_ref,
             dlr_ref, dli_ref, dld_ref, dbr_ref, dbi_ref):
        lr, li = lr_ref[...], li_ref[...]
        dt = jnp.exp(ld_ref[...])
        mag = jnp.exp(lr * dt)
        cs, sn = jnp.cos(li * dt), jnp.sin(li * dt)
        ar, ai = mag * cs, mag * sn
        den = lr * lr + li * li
        nr, ni = ar - 1.0, ai
        fr = (nr * lr + ni * li) / den
        fi = (ni * lr - nr * li) / den
        br, bi = br_ref[...], bi_ref[...]
        gbbr, gbbi = gbbr_ref[...], gbbi_ref[...]
        dbr_ref[...] = fr * gbbr + fi * gbbi
        dbi_ref[...] = fr * gbbi - fi * gbbr
        gfr = jnp.sum(br * gbbr + bi * gbbi, axis=2, keepdims=True)
        gfi = jnp.sum(br * gbbi - bi * gbbr, axis=2, keepdims=True)
        gnr_num, gni_num = gfr / den, gfi / den
        gden = -(gfr * fr + gfi * fi) / den
        g_nr = gnr_num * lr - gni_num * li
        g_ni = gnr_num * li + gni_num * lr
        dlr = gnr_num * nr + gni_num * ni + gden * 2.0 * lr
        dli = gnr_num * ni - gni_num * nr + gden * 2.0 * li
        gar = gar_ref[...] + g_nr
        gai = gai_ref[...] + g_ni
        gq = (gar * cs + gai * sn) * mag
        gth = (gai * cs - gar * sn) * mag
        dlr_ref[...] = dlr + gq * dt
        dli_ref[...] = dli + gth * dt
        dld_ref[...] = jnp.sum(gq * lr + gth * li, axis=3, keepdims=True) * dt

    return pl.pallas_call(
        body, name="s5_discretise_bwd",
        out_shape=(SDS(lam_re.shape, f32), SDS(lam_re.shape, f32), SDS(log_dt.shape, f32),
                   SDS(b_re.shape, f32), SDS(b_re.shape, f32)),
        compiler_params=_cparams(),
    )(lam_re, lam_im, log_dt, b_re, b_im, g_ar, g_ai, g_bbr, g_bbi)


def _pow_2k(ar, ai, n):
    k = n
    while k > 1:
        ar, ai = _cmul(ar, ai, ar, ai)
        k //= 2
    return ar, ai


def _interleaved_scan(xr_sc, xi_sc, ar, ai, carry_r, carry_i, nj, width, reverse, h_sc=None):
    cb = min(S5_CB, width)
    acc_out = []
    for c0 in range(0, width, cb):
        sl = pl.ds(c0, cb)
        a_r, a_i = ar[:, c0:c0 + cb], ai[:, c0:c0 + cb]
        aj_r, aj_i = _pow_2k(a_r, a_i, nj)

        def pos(i):
            return nj - 1 - i if reverse else i

        def local_step(i, st):
            j = pos(i)
            hr, hi = _cmul(a_r, a_i, st[0], st[1])
            hr, hi = hr + xr_sc[j, :, sl], hi + xi_sc[j, :, sl]
            xr_sc[j, :, sl] = hr
            xi_sc[j, :, sl] = hi
            return hr, hi

        zero = jnp.zeros((8, cb), f32)
        fin_r, fin_i = lax.fori_loop(0, nj, local_step, (zero, zero))
        row = lax.broadcasted_iota(jnp.int32, (8, cb), 0)
        c_r, c_i = carry_r[0:1, sl], carry_i[0:1, sl]
        ent_r, ent_i = zero, zero
        order = range(7, -1, -1) if reverse else range(8)
        for s in order:
            ent_r = jnp.where(row == s, c_r, ent_r)
            ent_i = jnp.where(row == s, c_i, ent_i)
            pr, pi_ = _cmul(aj_r[0:1, :], aj_i[0:1, :], c_r, c_i)
            c_r, c_i = fin_r[s:s + 1, :] + pr, fin_i[s:s + 1, :] + pi_
        carry_r[:, sl] = jnp.broadcast_to(c_r, (8, cb))
        carry_i[:, sl] = jnp.broadcast_to(c_i, (8, cb))

        def fix_step(i, st):
            j = pos(i)
            p_r, p_i, nx_r, nx_i, acc_r, acc_i = st
            cr_, ci_ = _cmul(p_r, p_i, ent_r, ent_i)
            hr, hi = xr_sc[j, :, sl] + cr_, xi_sc[j, :, sl] + ci_
            xr_sc[j, :, sl] = hr
            xi_sc[j, :, sl] = hi
            if h_sc is not None:
                sr, si = h_sc[0][j, :, sl], h_sc[1][j, :, sl]
                acc_r = acc_r + nx_r * sr + nx_i * si
                acc_i = acc_i + nx_i * sr - nx_r * si
                nx_r, nx_i = hr, hi
            p_r, p_i = _cmul(p_r, p_i, a_r, a_i)
            return p_r, p_i, nx_r, nx_i, acc_r, acc_i

        st = lax.fori_loop(0, nj, fix_step, (a_r, a_i, ent_r, ent_i, zero, zero))
        acc_out.append((st[4], st[5]))
    return acc_out


def _diag_mask(n_rep, h, ks):
    assert h & (h - 1) == 0 and (ks // n_rep) & (ks // n_rep - 1) == 0
    r = lax.shift_right_logical(lax.broadcasted_iota(jnp.int32, (n_rep * h, ks), 0), h.bit_length() - 1)
    c = lax.shift_right_logical(lax.broadcasted_iota(jnp.int32, (n_rep * h, ks), 1), (ks // n_rep).bit_length() - 1)
    return r == c


def _expand_block_diag(compact, n_rep):
    h, ks = compact.shape
    full = jnp.concatenate([compact] * n_rep, axis=0)
    return jnp.where(_diag_mask(n_rep, h, ks), full, 0.0).astype(bf16)


def _compact_block_diag(full, n_rep):
    rows, ks = full.shape
    h = rows // n_rep
    return jnp.sum(jnp.where(_diag_mask(n_rep, h, ks), full, 0.0).reshape(n_rep, h, ks), axis=0)


def _s5_scan_fwd(x_il, gain_pre, d_skip, ab_r, ab_i, bbc_r, bbc_i, cc_r, cc_in, li, host=None):
    bsz, nblk, tq, d = x_il.shape
    nkb, gch, ks = bbc_r.shape
    kc = d // nkb
    n_rep = kc // gch
    nj = tq // 8
    h_in, h_shape, h_out, h_scr, h_ops = _host_args(host)

    def body(xf_ref, xk_ref, gk_ref, dk_ref, ar_ref, ai_ref, bbr_ref, bbi_ref, cr_ref, ci_ref, *rest):
        o0 = len(h_in)
        y_ref, hr_ref, hi_ref = rest[o0:o0 + 3]
        s0 = o0 + 3 + len(h_out)
        xr_sc, xi_sc, carry_r, carry_i, bbr_sc, bbi_sc, crt_sc, cit_sc = rest[s0:s0 + 8]
        k, b, n = pl.program_id(0), pl.program_id(1), pl.program_id(2)
        finish = _hosted(host, rest[0], rest[o0 + 3], rest[s0 + 8:],
                         (k == 0) & (b == 0) & (n == 0), (k == nkb - 1) & (b == bsz - 1) & (n == nblk - 1))

        @pl.when((b == 0) & (n == 0))
        def _():
            bbr_sc[...] = _expand_block_diag(bbr_ref[...], n_rep)
            bbi_sc[...] = _expand_block_diag(bbi_ref[...], n_rep)
            crt_sc[...] = _expand_block_diag(cr_ref[...], n_rep)
            cit_sc[...] = _expand_block_diag(ci_ref[...], n_rep)

        @pl.when(n == 0)
        def _():
            carry_r[...] = jnp.zeros_like(carry_r)
            carry_i[...] = jnp.zeros_like(carry_i)

        u = xk_ref[...] * _rms_scale(xf_ref[...]) * gk_ref[...]
        ub = u.astype(bf16)
        xr_sc[...] = _dot(ub, bbr_sc[...]).reshape(nj, 8, ks)
        xi_sc[...] = _dot(ub, bbi_sc[...]).reshape(nj, 8, ks)
        ar = jnp.broadcast_to(ar_ref[...], (8, ks))
        ai = jnp.broadcast_to(ai_ref[...], (8, ks))
        _interleaved_scan(xr_sc, xi_sc, ar, ai, carry_r, carry_i, nj, ks, reverse=False)
        hrb = xr_sc[...].reshape(tq, ks).astype(bf16)
        hib = xi_sc[...].reshape(tq, ks).astype(bf16)
        hr_ref[...] = hrb
        hi_ref[...] = hib
        y_ref[...] = _dot_nt(hrb, crt_sc[...]) + _dot_nt(hib, cit_sc[...]) + dk_ref[...] * u
        finish()

    full = pl.BlockSpec((None, None, tq, d), lambda k, b, n: (b, n, 0, 0))
    chan = pl.BlockSpec((None, None, tq, kc), lambda k, b, n: (b, n, 0, k))
    stat = pl.BlockSpec((None, None, tq, ks), lambda k, b, n: (b, n, 0, k))
    vec_c = pl.BlockSpec((1, kc), lambda k, b, n: (0, k))
    vec_s = pl.BlockSpec((1, ks), lambda k, b, n: (0, k))
    cmap = pl.BlockSpec((None, gch, ks), lambda k, b, n: (k, 0, 0))
    s_tot = nkb * ks
    return pl.pallas_call(
        body, name=f"s5_scan_fwd_{li}", grid=(nkb, bsz, nblk),
        out_shape=(SDS((bsz, nblk, tq, d), f32), SDS((bsz, nblk, tq, s_tot), bf16), SDS((bsz, nblk, tq, s_tot), bf16),
                   *h_shape),
        in_specs=[full, chan, vec_c, vec_c, vec_s, vec_s, cmap, cmap, cmap, cmap, *h_in],
        out_specs=(chan, stat, stat, *h_out),
        scratch_shapes=[pltpu.VMEM((nj, 8, ks), f32), pltpu.VMEM((nj, 8, ks), f32),
                        pltpu.VMEM((8, ks), f32), pltpu.VMEM((8, ks), f32)] + [pltpu.VMEM((kc, ks), bf16)] * 4 + h_scr,
        compiler_params=_cparams(3, side_effects=host is not None),
    )(x_il, x_il, gain_pre, d_skip, ab_r, ab_i, bbc_r, bbc_i, cc_r, cc_in, *h_ops)


def _s5_glu_fwd(y, x_il, w_glu, b_glu, gain_post, li):
    t, d = y.shape
    tr = min(TOK_TR, t)

    def body(y_ref, x_ref, w_ref, b_ref, gp_ref, xo_ref):
        z = _gelu(y_ref[...])
        a = _dot(z.astype(bf16), w_ref[...]) + b_ref[...]
        xo_ref[...] = x_ref[...] + _rms_fwd(z * _sigmoid(a), gp_ref[...])

    tok = pl.BlockSpec((tr, d), lambda i: (i, 0))
    vec = pl.BlockSpec((1, d), lambda i: (0, 0))
    return pl.pallas_call(
        body, name=f"s5_glu_fwd_{li}", grid=(t // tr,), out_shape=SDS((t, d), f32),
        in_specs=[tok, tok, pl.BlockSpec((d, d), lambda i: (0, 0)), vec, vec], out_specs=tok,
        compiler_params=_cparams(1),
    )(y, x_il, w_glu, b_glu, gain_post)


def _s5_glu_bwd(y, dxo, w_glu, b_glu, gain_post, li):
    t, d = y.shape
    tr = min(TOK_TR, t)

    def body(y_ref, dxo_ref, w_ref, b_ref, gp_ref, dy_ref, dw_ref, ds_ref):
        @pl.when(pl.program_id(0) == 0)
        def _():
            dw_ref[...] = jnp.zeros_like(dw_ref)
            ds_ref[...] = jnp.zeros_like(ds_ref)

        z, zg = _gelu_and_grad(y_ref[...])
        zb = z.astype(bf16)
        w = w_ref[...]
        s = _sigmoid(_dot(zb, w) + b_ref[...])
        dm, dgpost = _rms_bwd(z * s, gp_ref[...], dxo_ref[...])
        da = dm * z * s * (1.0 - s)
        dab = da.astype(bf16)
        dz = dm * s + _dot_nt(dab, w)
        dw_ref[...] += _dot_tn(zb, dab)
        ds_ref[0:1, :] += dgpost
        ds_ref[1:2, :] += jnp.sum(da, axis=0, keepdims=True)
        dy_ref[...] = dz * zg

    tok = pl.BlockSpec((tr, d), lambda i: (i, 0))
    vec = pl.BlockSpec((1, d), lambda i: (0, 0))
    mat = pl.BlockSpec((d, d), lambda i: (0, 0))
    return pl.pallas_call(
        body, name=f"s5_glu_bwd_{li}", grid=(t // tr,),
        out_shape=(SDS((t, d), f32), SDS((d, d), f32), SDS((8, d), f32)),
        in_specs=[tok, tok, mat, vec, vec], out_specs=(tok, mat, pl.BlockSpec((8, d), lambda i: (0, 0))),
        compiler_params=_cparams(1),
    )(y, dxo, w_glu, b_glu, gain_post)


def _s5_scan_bwd(x_il, dy, h_r, h_i, gain_pre, d_skip, ab_r, ab_i, bbc_r, bbc_i, cc_r, cc_in, li, host=None):
    bsz, nblk, tq, d = x_il.shape
    nkb, gch, ks = bbc_r.shape
    kc = d // nkb
    n_rep = kc // gch
    nj = tq // 8
    h_in, h_shape, h_out, h_scr, h_ops = _host_args(host)

    def body(xf_ref, xk_ref, dy_ref, hr_ref, hi_ref, gk_ref, dk_ref, ar_ref, ai_ref, bbr_ref, bbi_ref, cr_ref, ci_ref,
             *rest):
        o0 = len(h_in)
        du_ref, dbbr_ref, dbbi_ref, dcr_ref, dci_ref, dar_ref, dai_ref, dd_ref = rest[o0:o0 + 8]
        s0 = o0 + 8 + len(h_out)
        (gr_sc, gi_sc, hr_sc, hi_sc, carry_r, carry_i, acc_r, acc_i,
         bbr_sc, bbi_sc, crt_sc, cit_sc, dbbr_acc, dbbi_acc, dcr_acc, dci_acc) = rest[s0:s0 + 16]
        k, b, n = pl.program_id(0), pl.program_id(1), pl.program_id(2)
        finish = _hosted(host, rest[0], rest[o0 + 8], rest[s0 + 16:],
                         (k == 0) & (b == 0) & (n == 0), (k == nkb - 1) & (b == bsz - 1) & (n == nblk - 1))

        @pl.when((b == 0) & (n == 0))
        def _():
            bbr_sc[...] = _expand_block_diag(bbr_ref[...], n_rep)
            bbi_sc[...] = _expand_block_diag(bbi_ref[...], n_rep)
            crt_sc[...] = _expand_block_diag(cr_ref[...], n_rep)
            cit_sc[...] = _expand_block_diag(ci_ref[...], n_rep)
            for ref in (dar_ref, dai_ref, dd_ref, acc_r, acc_i, dbbr_acc, dbbi_acc, dcr_acc, dci_acc):
                ref[...] = jnp.zeros_like(ref)

        @pl.when(n == 0)
        def _():
            carry_r[...] = jnp.zeros_like(carry_r)
            carry_i[...] = jnp.zeros_like(carry_i)

        u = xk_ref[...] * _rms_scale(xf_ref[...]) * gk_ref[...]
        ub = u.astype(bf16)
        dyv = dy_ref[...]
        dyb = dyv.astype(bf16)
        dd_ref[0:1, :] += jnp.sum(dyv * u, axis=0, keepdims=True)
        hrb, hib = hr_ref[...], hi_ref[...]
        dcr_acc[...] += _dot_tn(dyb, hrb)
        dci_acc[...] += _dot_tn(dyb, hib)
        hr_sc[...] = hrb.astype(f32).reshape(nj, 8, ks)
        hi_sc[...] = hib.astype(f32).reshape(nj, 8, ks)
        gr_sc[...] = _dot(dyb, crt_sc[...]).reshape(nj, 8, ks)
        gi_sc[...] = _dot(dyb, cit_sc[...]).reshape(nj, 8, ks)
        ar = jnp.broadcast_to(ar_ref[...], (8, ks))
        ai = jnp.broadcast_to(-ai_ref[...], (8, ks))
        accs = _interleaved_scan(gr_sc, gi_sc, ar, ai, carry_r, carry_i, nj, ks, reverse=True, h_sc=(hr_sc, hi_sc))
        cb = min(S5_CB, ks)
        for q, (a_r, a_i) in enumerate(accs):
            acc_r[:, q * cb:(q + 1) * cb] += a_r
            acc_i[:, q * cb:(q + 1) * cb] += a_i
        grb = gr_sc[...].reshape(tq, ks).astype(bf16)
        gib = gi_sc[...].reshape(tq, ks).astype(bf16)
        dbbr_acc[...] += _dot_tn(ub, grb)
        dbbi_acc[...] += _dot_tn(ub, gib)
        du_ref[...] = dyv * dk_ref[...] + _dot_nt(grb, bbr_sc[...]) + _dot_nt(gib, bbi_sc[...])

        @pl.when((b == bsz - 1) & (n == nblk - 1))
        def _():
            dar_ref[0:1, :] = jnp.sum(acc_r[...], axis=0, keepdims=True)
            dai_ref[0:1, :] = jnp.sum(acc_i[...], axis=0, keepdims=True)
            dbbr_ref[...] = _compact_block_diag(dbbr_acc[...], n_rep)
            dbbi_ref[...] = _compact_block_diag(dbbi_acc[...], n_rep)
            dcr_ref[...] = _compact_block_diag(dcr_acc[...], n_rep)
            dci_ref[...] = _compact_block_diag(dci_acc[...], n_rep)

        finish()

    full = pl.BlockSpec((None, None, tq, d), lambda k, b, n: (b, nblk - 1 - n, 0, 0))
    chan = pl.BlockSpec((None, None, tq, kc), lambda k, b, n: (b, nblk - 1 - n, 0, k))
    stat = pl.BlockSpec((None, None, tq, ks), lambda k, b, n: (b, nblk - 1 - n, 0, k))
    vec_c = pl.BlockSpec((1, kc), lambda k, b, n: (0, k))
    vec_s = pl.BlockSpec((1, ks), lambda k, b, n: (0, k))
    cmap = pl.BlockSpec((None, gch, ks), lambda k, b, n: (k, 0, 0))
    acc_s = pl.BlockSpec((None, 8, ks), lambda k, b, n: (k, 0, 0))
    acc_c = pl.BlockSpec((None, 8, kc), lambda k, b, n: (k, 0, 0))
    cshape = SDS((nkb, gch, ks), f32)
    return pl.pallas_call(
        body, name=f"s5_scan_bwd_{li}", grid=(nkb, bsz, nblk),
        out_shape=(SDS((bsz, nblk, tq, d), f32), cshape, cshape, cshape, cshape, SDS((nkb, 8, ks), f32),
                   SDS((nkb, 8, ks), f32), SDS((nkb, 8, kc), f32), *h_shape),
        in_specs=[full, chan, chan, stat, stat, vec_c, vec_c, vec_s, vec_s, cmap, cmap, cmap, cmap, *h_in],
        out_specs=(chan, cmap, cmap, cmap, cmap, acc_s, acc_s, acc_c, *h_out),
        scratch_shapes=([pltpu.VMEM((nj, 8, ks), f32)] * 4 + [pltpu.VMEM((8, ks), f32)] * 4
                        + [pltpu.VMEM((kc, ks), bf16)] * 4 + [pltpu.VMEM((kc, ks), f32)] * 4 + h_scr),
        compiler_params=_cparams(3, side_effects=host is not None),
    )(x_il, x_il, dy, h_r, h_i, gain_pre, d_skip, ab_r, ab_i, bbc_r, bbc_i, cc_r, cc_in, *h_ops)


def _norm_residual_bwd(x, du, dxo, gain, name):
    t, d = x.shape
    tr = min(TOK_TR, t)

    def body(x_ref, du_ref, dxo_ref, g_ref, dx_ref, dg_ref):
        @pl.when(pl.program_id(0) == 0)
        def _():
            dg_ref[...] = jnp.zeros_like(dg_ref)

        dxp, dgain = _rms_bwd(x_ref[...], g_ref[...], du_ref[...])
        dx_ref[...] = dxo_ref[...] + dxp
        dg_ref[0:1, :] += dgain

    tok = pl.BlockSpec((tr, d), lambda i: (i, 0))
    return pl.pallas_call(
        body, name=name, grid=(t // tr,), out_shape=(SDS((t, d), f32), SDS((8, d), f32)),
        in_specs=[tok, tok, tok, pl.BlockSpec((1, d), lambda i: (0, 0))],
        out_specs=(tok, pl.BlockSpec((8, d), lambda i: (0, 0))), compiler_params=_cparams(1),
    )(x, du, dxo, gain)


def _loss_head(y, target):
    t, d = y.shape
    tr = min(TOK_TR, t)

    def body(y_ref, t_ref, l_ref, dy_ref):
        @pl.when(pl.program_id(0) == 0)
        def _():
            l_ref[...] = jnp.zeros_like(l_ref)

        err = y_ref[...] - t_ref[...]
        dy_ref[...] = err * (1.0 / d)
        l_ref[...] += jnp.sum(jnp.sum(err * err, axis=1, keepdims=True), axis=0, keepdims=True)

    tok = pl.BlockSpec((tr, d), lambda i: (i, 0))
    return pl.pallas_call(
        body, name="loss_head", grid=(t // tr,), out_shape=(SDS((8, 128), f32), SDS((t, d), f32)),
        in_specs=[tok, tok], out_specs=(pl.BlockSpec((8, 128), lambda i: (0, 0)), tok), compiler_params=_cparams(1),
    )(y, target)


def _interleave(a, tq):
    bsz, l, d = a.shape
    return a.reshape(bsz, l // tq, 8, tq // 8, d).transpose(0, 1, 3, 2, 4).reshape(bsz, l // tq, tq, d)


def _deinterleave(a, l):
    bsz, nblk, tq, d = a.shape
    return a.reshape(bsz, nblk, tq // 8, 8, d).transpose(0, 1, 3, 2, 4).reshape(bsz, l, d)


def _compact_maps(a, nkb):
    ns, g, h, p = a.shape
    gl = g // nkb
    return a.reshape(ns, nkb, gl, h, p).transpose(0, 1, 3, 2, 4).reshape(ns, nkb, h, gl * p)


def _uncompact_maps(a, g):
    nkb, h, cols = a.shape
    gl = g // nkb
    return a.reshape(nkb, h, gl, cols // gl).transpose(0, 2, 1, 3).reshape(g, h, cols // gl)


def _pack_rows(arrs, cols, row_mult):
    flat = jnp.concatenate([a.reshape(-1).astype(f32) for a in arrs])
    rows = -(-flat.shape[0] // cols)
    rows = -(-rows // row_mult) * row_mult
    return jnp.pad(flat, (0, rows * cols - flat.shape[0])).reshape(rows, cols)


def _unpack_rows(buf, shapes):
    flat = buf.reshape(-1)
    out, off = [], 0
    for s in shapes:
        n = math.prod(s)
        out.append(flat[off:off + n].reshape(s))
        off += n
    return out


W_NAMES = ['s5_lambda_re', 's5_lambda_im', 's5_log_dt', 's5_b_re', 's5_b_im', 's5_c_re', 's5_c_im', 's5_d', 's5_w_glu',
           's5_b_glu', 'pool_w', 'pool_scale', 'ffn_w_gate', 'ffn_w_val', 'ffn_conv_w', 'ffn_conv_b', 'ffn_w_down',
           'norm_mix_pre', 'norm_mix_post', 'norm_ffn_pre', 'norm_ffn_post']
BIG = ('s5_w_glu', 'ffn_w_gate', 'ffn_w_val', 'ffn_w_down')


def kernel(x, s5_lambda_re, s5_lambda_im, s5_log_dt, s5_b_re, s5_b_im, s5_c_re, s5_c_im, s5_d, s5_w_glu, s5_b_glu, pool_w, pool_scale, ffn_w_gate, ffn_w_val, ffn_conv_w, ffn_conv_b, ffn_w_down, norm_mix_pre, norm_mix_post, norm_ffn_pre, norm_ffn_post, loss_target, m_s5_lambda_re, m_s5_lambda_im, m_s5_log_dt, m_s5_b_re, m_s5_b_im, m_s5_c_re, m_s5_c_im, m_s5_d, m_s5_w_glu, m_s5_b_glu, m_pool_w, m_pool_scale, m_ffn_w_gate, m_ffn_w_val, m_ffn_conv_w, m_ffn_conv_b, m_ffn_w_down, m_norm_mix_pre, m_norm_mix_post, m_norm_ffn_pre, m_norm_ffn_post, v_s5_lambda_re, v_s5_lambda_im, v_s5_log_dt, v_s5_b_re, v_s5_b_im, v_s5_c_re, v_s5_c_im, v_s5_d, v_s5_w_glu, v_s5_b_glu, v_pool_w, v_pool_scale, v_ffn_w_gate, v_ffn_w_val, v_ffn_conv_w, v_ffn_conv_b, v_ffn_w_down, v_norm_mix_pre, v_norm_mix_post, v_norm_ffn_pre, v_norm_ffn_post):
    w_in = dict(zip(W_NAMES, (s5_lambda_re, s5_lambda_im, s5_log_dt, s5_b_re, s5_b_im, s5_c_re, s5_c_im, s5_d, s5_w_glu,
                              s5_b_glu, pool_w, pool_scale, ffn_w_gate, ffn_w_val, ffn_conv_w, ffn_conv_b, ffn_w_down,
                              norm_mix_pre, norm_mix_post, norm_ffn_pre, norm_ffn_post)))
    m_in = dict(zip(W_NAMES, (m_s5_lambda_re, m_s5_lambda_im, m_s5_log_dt, m_s5_b_re, m_s5_b_im, m_s5_c_re, m_s5_c_im,
                              m_s5_d, m_s5_w_glu, m_s5_b_glu, m_pool_w, m_pool_scale, m_ffn_w_gate, m_ffn_w_val,
                              m_ffn_conv_w, m_ffn_conv_b, m_ffn_w_down, m_norm_mix_pre, m_norm_mix_post,
                              m_norm_ffn_pre, m_norm_ffn_post)))
    v_in = dict(zip(W_NAMES, (v_s5_lambda_re, v_s5_lambda_im, v_s5_log_dt, v_s5_b_re, v_s5_b_im, v_s5_c_re, v_s5_c_im,
                              v_s5_d, v_s5_w_glu, v_s5_b_glu, v_pool_w, v_pool_scale, v_ffn_w_gate, v_ffn_w_val,
                              v_ffn_conv_w, v_ffn_conv_b, v_ffn_w_down, v_norm_mix_pre, v_norm_mix_post,
                              v_norm_ffn_pre, v_norm_ffn_post)))

    bsz, seq, d = x.shape
    depth = ffn_w_gate.shape[0]
    n_s5, n_grp, n_state = s5_lambda_re.shape
    n_gch = s5_b_re.shape[3]
    n_pool = pool_w.shape[0]
    fs = ffn_w_gate.shape[2]
    f = N_CHIPS * fs
    gs = s5_w_glu.shape[1]
    nkb = d // S5_KB_CH
    tq = min(S5_TQ, seq)
    chip = 2 * lax.axis_index("x") + lax.axis_index("y")

    s_ffn = jnp.stack([ffn_w_gate.transpose(0, 2, 1), ffn_w_val.transpose(0, 2, 1), ffn_w_down], axis=1).astype(bf16)
    small_shard = [pool_w, pool_scale, ffn_conv_w]
    g_glu, g_small = _chip_exchange([("gather", 1, s5_w_glu.astype(bf16)), ("whole", 0, _pack_rows(small_shard, d, 16))],
                                    "allgather_small")
    parts = [_unpack_rows(g_small[k], [a.shape for a in small_shard]) for k in range(N_CHIPS)]
    pool_w_full = jnp.concatenate([p[0] for p in parts], axis=2).astype(bf16)
    pool_scale_full = jnp.concatenate([p[1] for p in parts], axis=1)
    conv_w_full = jnp.concatenate([p[2] for p in parts], axis=2)

    lam_r4 = s5_lambda_re.reshape(n_s5, n_grp, 1, n_state)
    lam_i4 = s5_lambda_im.reshape(n_s5, n_grp, 1, n_state)
    ldt4 = s5_log_dt.reshape(n_s5, n_grp, 1, 1)
    b_r4 = s5_b_re.transpose(0, 1, 3, 2)
    b_i4 = s5_b_im.transpose(0, 1, 3, 2)
    ab_r4, ab_i4, bb_r4, bb_i4 = _s5_discretise(lam_r4, lam_i4, ldt4, b_r4, b_i4)
    n_st_tot = n_grp * n_state
    ab_r, ab_i = ab_r4.reshape(n_s5, 1, n_st_tot), ab_i4.reshape(n_s5, 1, n_st_tot)
    bbc_r, bbc_i = _compact_maps(bb_r4, nkb), _compact_maps(bb_i4, nkb)
    cc_r, cc_in = _compact_maps(s5_c_re, nkb), _compact_maps(-s5_c_im, nkb)

    def s5_params(j):
        return dict(ab_r=ab_r[j], ab_i=ab_i[j], bb_r=bbc_r[j], bb_i=bbc_i[j], c_r=cc_r[j], c_in=cc_in[j],
                    d_skip=s5_d[j].reshape(1, d), w_glu=g_glu[j], b_glu=s5_b_glu[j].reshape(1, d))

    def row(a, i):
        return a[i].reshape(1, -1)

    saved = []
    xc = x
    w_layer = [None] * depth
    for i in range(depth):
        j = i // 2
        sv = dict(x_in=xc)
        if i % 2 == 0:
            sp = s5_params(j)
            x_il = _interleave(xc, tq)
            res = _s5_scan_fwd(x_il, row(norm_mix_pre, i), sp["d_skip"], sp["ab_r"], sp["ab_i"], sp["bb_r"], sp["bb_i"],
                               sp["c_r"], sp["c_in"], i, host=("gather", 1, s_ffn[0]) if i == 0 else None)
            y_il, h_r, h_i = res[:3]
            if i == 0:
                w_layer[0] = res[3]
            xo_il = _s5_glu_fwd(y_il.reshape(bsz * seq, d), x_il.reshape(bsz * seq, d), sp["w_glu"], sp["b_glu"],
                                row(norm_mix_post, i), i)
            x_mid = _deinterleave(xo_il.reshape(x_il.shape), seq)
            sv.update(sp=sp, x_il=x_il, y_il=y_il, h_r=h_r, h_i=h_i)
        else:
            x_mid = _pool_fwd(xc, row(norm_mix_pre, i), row(norm_mix_post, i), pool_w_full[j], row(pool_scale_full, j), i)
        res = _ffn_fwd(x_mid, row(norm_ffn_pre, i), row(norm_ffn_post, i), w_layer[i], conv_w_full, ffn_conv_b, i,
                       host=("gather", 1, s_ffn[i + 1]) if i + 1 < depth else None)
        xc, g_sv, v_sv, f_sv = res[:4]
        if i + 1 < depth:
            w_layer[i + 1] = res[4]
        sv.update(x_mid=x_mid, g=g_sv, v=v_sv, f=f_sv)
        saved.append(sv)

    sq, dy = _loss_head(xc.reshape(bsz * seq, d), loss_target.reshape(bsz * seq, d))
    loss = lax.psum(sq[0, 0] * (0.5 / d), ("x", "y", "c"))

    dx = dy.reshape(bsz, seq, d)
    dw_layers = [None] * depth
    r_layers = [None] * depth
    g_small_params = {n: [None] * w_in[n].shape[0] for n in W_NAMES if n not in BIG}
    dglu = [None] * n_s5
    for i in range(depth - 1, -1, -1):
        j = i // 2
        sv = saved[i]
        res = _ffn_bwd_act(sv["x_mid"], sv["f"], sv["g"], sv["v"], dx, row(norm_ffn_pre, i), row(norm_ffn_post, i),
                           w_layer[i], conv_w_full, ffn_conv_b, i,
                           host=("slab", 1, dw_layers[i + 1]) if i + 1 < depth else None)
        dx, dg, dv, hdn, hb, dfb, dgain, dconv = res[:8]
        if i + 1 < depth:
            r_layers[i + 1] = res[8]
        dw_layers[i] = _ffn_bwd_weights(dg.reshape(bsz * seq, f), dv.reshape(bsz * seq, f), hdn.reshape(bsz * seq, f),
                                        hb.reshape(bsz * seq, d), dfb.reshape(bsz * seq, d), i)
        g_small_params["norm_ffn_pre"][i] = dgain[0]
        g_small_params["norm_ffn_post"][i] = dgain[1]
        dconv = dconv.transpose(1, 0, 2).reshape(8, f)
        g_small_params["ffn_conv_w"][i] = dconv[0:3]
        g_small_params["ffn_conv_b"][i] = dconv[3]
        if i % 2 == 0:
            sp = sv["sp"]
            dxo_il = _interleave(dx, tq)
            dy_s, dglu[j], ds_glu = _s5_glu_bwd(sv["y_il"].reshape(bsz * seq, d), dxo_il.reshape(bsz * seq, d),
                                                sp["w_glu"], sp["b_glu"], row(norm_mix_post, i), i)
            res = _s5_scan_bwd(sv["x_il"], dy_s.reshape(sv["x_il"].shape), sv["h_r"], sv["h_i"], row(norm_mix_pre, i),
                               sp["d_skip"], sp["ab_r"], sp["ab_i"], sp["bb_r"], sp["bb_i"], sp["c_r"], sp["c_in"], i,
                               host=("slab", 1, dw_layers[0]) if i == 0 else None)
            du, dbb_r, dbb_i, dc_r, dc_in, dab_r, dab_i, dd = res[:8]
            if i == 0:
                r_layers[0] = res[8]
            dx_il, dgpre = _norm_residual_bwd(sv["x_il"].reshape(bsz * seq, d), du.reshape(bsz * seq, d),
                                              dxo_il.reshape(bsz * seq, d), row(norm_mix_pre, i), f"s5_pre_bwd_{i}")
            dx = _deinterleave(dx_il.reshape(sv["x_il"].shape), seq)
            g_small_params["norm_mix_pre"][i] = dgpre[0]
            g_small_params["norm_mix_post"][i] = ds_glu[0]
            g_small_params["s5_b_glu"][j] = ds_glu[1]
            g_small_params["s5_d"][j] = dd[:, 0, :].reshape(d)
            g_small_params["s5_c_re"][j] = _uncompact_maps(dc_r, n_grp)
            g_small_params["s5_c_im"][j] = -_uncompact_maps(dc_in, n_grp)
            sv["g_ab"] = (dab_r[:, 0, :].reshape(n_grp, 1, n_state), dab_i[:, 0, :].reshape(n_grp, 1, n_state),
                          _uncompact_maps(dbb_r, n_grp), _uncompact_maps(dbb_i, n_grp))
        else:
            dx, dpw, dsm = _pool_bwd(sv["x_in"], dx, row(norm_mix_pre, i), row(norm_mix_post, i), pool_w_full[j],
                                     row(pool_scale_full, j), i)
            g_small_params["norm_mix_pre"][i] = dsm[0]
            g_small_params["norm_mix_post"][i] = dsm[1]
            g_small_params["pool_scale"][j] = dsm[2]
            g_small_params["pool_w"][j] = dpw
    grad_x = dx

    g_ab = [saved[2 * j]["g_ab"] for j in range(n_s5)]
    d_lr, d_li, d_ld, d_br, d_bi = _s5_discretise_bwd(
        lam_r4, lam_i4, ldt4, b_r4, b_i4, jnp.stack([g[0] for g in g_ab]), jnp.stack([g[1] for g in g_ab]),
        jnp.stack([g[2] for g in g_ab]), jnp.stack([g[3] for g in g_ab]))
    small_full = {n: (jnp.stack(v) if v[0] is not None else None) for n, v in g_small_params.items()}
    small_full["s5_lambda_re"] = d_lr.reshape(n_s5, n_grp, n_state)
    small_full["s5_lambda_im"] = d_li.reshape(n_s5, n_grp, n_state)
    small_full["s5_log_dt"] = d_ld.reshape(n_s5, n_grp)
    small_full["s5_b_re"] = d_br.transpose(0, 1, 3, 2)
    small_full["s5_b_im"] = d_bi.transpose(0, 1, 3, 2)
    bc_names = ["s5_b_re", "s5_b_im", "s5_c_re", "s5_c_im"]
    small_names = [n for n in W_NAMES if n not in BIG and n not in bc_names and n != "pool_w"]
    q = _pack_rows([small_full[n] for n in small_names], d, 16)
    q_bc = _pack_rows([small_full[n] for n in bc_names], d, 16).astype(bf16)

    r_glu, r_pw, r_q, r_bc = _chip_exchange(
        [("slab", 1, jnp.stack(dglu).astype(bf16)), ("slab", 2, small_full["pool_w"]), ("whole", 0, q), ("whole", 0, q_bc)],
        "exchange_small_grads")
    p_ffn = _sum_slots_layers([r.reshape(N_CHIPS, 3 * fs, d) for r in r_layers], "sum_slots_ffn").reshape(depth * 3 * fs, d)
    p_glu = _sum_slots(r_glu.reshape(N_CHIPS, n_s5 * gs, d), "sum_slots_glu")
    pw_rows, pw_cols = math.prod(pool_w.shape[:3]), pool_w.shape[3]
    p_pw = _sum_slots(r_pw.reshape(N_CHIPS, pw_rows, pw_cols), "sum_slots_pool_w")
    p_q = _sum_slots(r_q, "sum_slots_small")
    p_bc = _sum_slots(r_bc, "sum_slots_bc")
    o_ffn, o_glu, o_pw, o_q, o_bc = _exchange_with_sibling([p_ffn, p_glu, p_pw, p_q, p_bc])

    outs = {}

    def put(name, res, shape):
        outs[name] = tuple(r.reshape(shape) for r in res)

    pf, of = p_ffn.reshape(depth, 3, fs, d), o_ffn.reshape(depth, 3, fs, d)
    for kind, name in ((0, "ffn_w_gate"), (1, "ffn_w_val")):
        parts_g = [a[:, kind].transpose(0, 2, 1).reshape(depth * d, fs) for a in (pf, of)]
        put(name, _adamw(parts_g, w_in[name].reshape(depth * d, fs), m_in[name].reshape(depth * d, fs),
                         v_in[name].reshape(depth * d, fs), f"adamw_{name}"), w_in[name].shape)
    put("ffn_w_down", _adamw([pf[:, 2].reshape(depth * fs, d), of[:, 2].reshape(depth * fs, d)],
                             ffn_w_down.reshape(depth * fs, d), m_ffn_w_down.reshape(depth * fs, d),
                             v_ffn_w_down.reshape(depth * fs, d), "adamw_ffn_w_down"), ffn_w_down.shape)
    put("s5_w_glu", _adamw([p_glu, o_glu], s5_w_glu.reshape(n_s5 * gs, d), m_s5_w_glu.reshape(n_s5 * gs, d),
                           v_s5_w_glu.reshape(n_s5 * gs, d), "adamw_s5_w_glu"), s5_w_glu.shape)

    q_tot, bc_tot, pw_tot = _add_pairs([(p_q, o_q), (p_bc, o_bc), (p_pw, o_pw)], "sum_small")
    g_small = dict(zip(small_names, _unpack_rows(q_tot, [small_full[n].shape for n in small_names])))
    g_small.update(zip(bc_names, _unpack_rows(bc_tot, [small_full[n].shape for n in bc_names])))
    g_small["pool_w"] = pw_tot.reshape(pool_w.shape)
    g_small["pool_scale"] = lax.dynamic_slice_in_dim(g_small["pool_scale"], chip * pool_scale.shape[1],
                                                     pool_scale.shape[1], axis=1)
    g_small["ffn_conv_w"] = lax.dynamic_slice_in_dim(g_small["ffn_conv_w"], chip * fs, fs, axis=2)
    all_small = [n for n in W_NAMES if n not in BIG]
    local_shapes = [w_in[n].shape for n in all_small]
    res = _adamw([_pack_rows([g_small[n] for n in all_small], d, 64)],
                 _pack_rows([w_in[n] for n in all_small], d, 64), _pack_rows([m_in[n] for n in all_small], d, 64),
                 _pack_rows([v_in[n] for n in all_small], d, 64), "adamw_small")
    unpacked = [_unpack_rows(r, local_shapes) for r in res]
    for idx, n in enumerate(all_small):
        outs[n] = (g_small[n], unpacked[1][idx], unpacked[2][idx], unpacked[3][idx])

    return (loss, grad_x, *[outs[n][0] for n in W_NAMES], *[outs[n][1] for n in W_NAMES],
            *[outs[n][2] for n in W_NAMES], *[outs[n][3] for n in W_NAMES])
```

```python
import math

import jax
import jax.numpy as jnp
from jax import lax
from jax.experimental import pallas as pl
from jax.experimental.pallas import tpu as pltpu

f32, bf16 = jnp.float32, jnp.bfloat16
SDS = jax.ShapeDtypeStruct
MESH = pl.DeviceIdType.MESH

RMS_EPS = 1e-6
GELU_C = math.sqrt(2.0 / math.pi)
GELU_K = 0.044715
ADAM_LR, ADAM_B1, ADAM_B2, ADAM_EPS, ADAM_WD, ADAM_STEP = 0.001, 0.9, 0.999, 1e-08, 0.01, 10
POOL_WINDOWS = (2, 4, 8, 16)
POOL_HALO = 16
S5_GROUP_CH = 16
S5_STATE = 64
S5_KB_CH = 256
N_CHIPS = 4

FFN_TL = 512
FFN_FT = 256
FFN_CH = 16
FFN_SPLIT = 2
WG_TR = 2048
POOL_TL = 512
S5_TQ = 512
S5_CB = 512
TOK_TR = 512
VMEM_LIMIT = 56 * 1024 * 1024

HBM = pl.BlockSpec(memory_space=pltpu.HBM)


def _cparams(n_axes=0, side_effects=False):
    kw = dict(vmem_limit_bytes=VMEM_LIMIT)
    if n_axes:
        kw["dimension_semantics"] = ("arbitrary",) * n_axes
    if side_effects:
        kw["has_side_effects"] = True
    return pltpu.CompilerParams(**kw)


def _dot(a, b):
    return jnp.dot(a, b, preferred_element_type=f32)


def _dot_nt(a, b):
    return lax.dot_general(a, b, (((1,), (1,)), ((), ())), preferred_element_type=f32)


def _dot_tn(a, b):
    return lax.dot_general(a, b, (((0,), (0,)), ((), ())), preferred_element_type=f32)


def _rms_scale(x):
    return lax.rsqrt(jnp.mean(x * x, axis=-1, keepdims=True) + RMS_EPS)


def _rms_fwd(x, gain):
    return x * _rms_scale(x) * gain


def _rms_bwd(x, gain, dy):
    r = _rms_scale(x)
    xn = x * r
    dgain = jnp.sum(dy * xn, axis=0, keepdims=True)
    dxn = dy * gain
    dx = r * (dxn - xn * jnp.mean(dxn * xn, axis=-1, keepdims=True))
    return dx, dgain


def _gelu(x):
    t = jnp.tanh(x * (GELU_C + (GELU_C * GELU_K) * (x * x)))
    hx = 0.5 * x
    return hx + hx * t


def _gelu_and_grad(x):
    x2 = x * x
    t = jnp.tanh(x * (GELU_C + (GELU_C * GELU_K) * x2))
    hx = 0.5 * x
    return hx + hx * t, (0.5 + 0.5 * t) + hx * (1.0 - t * t) * (GELU_C + (3.0 * GELU_C * GELU_K) * x2)


def _sigmoid(x):
    return 1.0 / (1.0 + jnp.exp(-x))


def _cmul(ar, ai, br, bi):
    return ar * br - ai * bi, ar * bi + ai * br


def _row_block(n, cap):
    best = None
    for d in range(16, min(n, cap) + 1, 16):
        if n % d == 0:
            best = d
    assert best is not None, n
    return best


def _mesh_pos():
    return lax.axis_index("x"), lax.axis_index("y"), lax.axis_index("c")


def _other_chips(x, y):
    return [(1 - x, y), (x, 1 - y), (1 - x, 1 - y)]


def _slab_index(ndim, axis, start, size):
    return tuple(pl.ds(start, size) if a == axis else slice(None) for a in range(ndim))


class _ChipExchange:
    def __init__(self, kind, axis, src, dst, send_sems, recv_sems, loc_sem):
        x, y, c = _mesh_pos()
        me = 2 * x + y
        nd = len(src.shape)
        size = src.shape[axis] if kind == "gather" else src.shape[axis] // N_CHIPS

        def src_for(kk):
            return src.at[_slab_index(nd, axis, kk * size, size)] if kind == "slab" else src

        def dst_for(kk):
            return dst.at[_slab_index(nd, axis, kk * size, size)] if kind == "gather" else dst.at[kk]

        self.own = pltpu.make_async_copy(src_for(me), dst_for(me), loc_sem)
        self.sends, self.recvs = [], []
        for j, chip in enumerate(_other_chips(x, y)):
            kk = 2 * chip[0] + chip[1]
            peer = dict(send_sem=send_sems.at[j], recv_sem=recv_sems.at[j], device_id=(chip[0], chip[1], c),
                        device_id_type=MESH)
            self.sends.append(pltpu.make_async_remote_copy(src_ref=src_for(kk), dst_ref=dst_for(me), **peer))
            self.recvs.append(pltpu.make_async_remote_copy(src_ref=src_for(me), dst_ref=dst_for(kk), **peer))

    def start(self):
        self.own.start()
        for cp in self.sends:
            cp.start()

    def finish(self):
        for cp in self.recvs:
            cp.wait_recv()
        for cp in self.sends:
            cp.wait_send()
        self.own.wait()


def _exchange_out_shape(kind, axis, src):
    if kind == "gather":
        shape = tuple(N_CHIPS * n if a == axis else n for a, n in enumerate(src.shape))
    elif kind == "slab":
        shape = (N_CHIPS,) + tuple(n // N_CHIPS if a == axis else n for a, n in enumerate(src.shape))
    else:
        shape = (N_CHIPS,) + tuple(src.shape)
    return SDS(shape, src.dtype)


EXCHANGE_SEMS = [pltpu.SemaphoreType.DMA((3,)), pltpu.SemaphoreType.DMA((3,)), pltpu.SemaphoreType.DMA((1,))]


def _chip_exchange(items, name):
    n = len(items)

    def body(*refs):
        ins, outs, sems = refs[:n], refs[n:2 * n], refs[2 * n:]
        exs = [_ChipExchange(kind, axis, ins[i], outs[i], sems[3 * i], sems[3 * i + 1], sems[3 * i + 2].at[0])
               for i, (kind, axis, _) in enumerate(items)]
        for ex in exs:
            ex.start()
        for ex in exs:
            ex.finish()

    return pl.pallas_call(
        body, name=name, out_shape=tuple(_exchange_out_shape(k, a, arr) for k, a, arr in items),
        in_specs=[HBM] * n, out_specs=tuple([HBM] * n), scratch_shapes=EXCHANGE_SEMS * n,
        compiler_params=_cparams(side_effects=True),
    )(*[arr for _, _, arr in items])


def _hosted(host, host_in, host_out, sems, first, last):
    if host is None:
        return lambda: None
    ex = _ChipExchange(host[0], host[1], host_in, host_out, sems[0], sems[1], sems[2].at[0])

    @pl.when(first)
    def _():
        ex.start()

    def finish():
        @pl.when(last)
        def _():
            ex.finish()

    return finish


def _host_args(host):
    if host is None:
        return [], [], [], [], []
    return [HBM], [_exchange_out_shape(*host)], [HBM], EXCHANGE_SEMS, [host[2]]


def _exchange_with_sibling(parts):
    n = len(parts)

    def body(*refs):
        ins, outs = refs[:n], refs[n:2 * n]
        send_sems, recv_sems = refs[2 * n:]
        x, y, c = _mesh_pos()
        cps = [pltpu.make_async_remote_copy(src_ref=s, dst_ref=t, send_sem=send_sems.at[i], recv_sem=recv_sems.at[i],
                                            device_id=(x, y, 1 - c), device_id_type=MESH)
               for i, (s, t) in enumerate(zip(ins, outs))]
        for cp in cps:
            cp.start()
        for cp in cps:
            cp.wait()

    return pl.pallas_call(
        body, name="exchange_with_sibling",
        out_shape=tuple(SDS(p.shape, p.dtype) for p in parts),
        in_specs=[HBM] * n, out_specs=tuple([HBM] * n),
        scratch_shapes=[pltpu.SemaphoreType.DMA((n,)), pltpu.SemaphoreType.DMA((n,))],
        compiler_params=_cparams(side_effects=True),
    )(*parts)


def _sum_slots(r, name):
    _, rows, cols = r.shape
    tr = _row_block(rows, 512)

    def body(r_ref, o_ref):
        o_ref[...] = ((r_ref[0].astype(f32) + r_ref[1].astype(f32)) + r_ref[2].astype(f32)) + r_ref[3].astype(f32)

    return pl.pallas_call(
        body, name=name, grid=(rows // tr,), out_shape=SDS((rows, cols), f32),
        in_specs=[pl.BlockSpec((N_CHIPS, tr, cols), lambda i: (0, i, 0))],
        out_specs=pl.BlockSpec((tr, cols), lambda i: (i, 0)),
        compiler_params=_cparams(1),
    )(r)


def _sum_slots_layers(rs, name):
    _, rows, cols = rs[0].shape
    nl = len(rs)
    tr = _row_block(rows, 256)

    def body(*refs):
        o_ref = refs[nl]
        for l in range(nl):
            r = refs[l]
            o_ref[l] = ((r[0].astype(f32) + r[1].astype(f32)) + r[2].astype(f32)) + r[3].astype(f32)

    return pl.pallas_call(
        body, name=name, grid=(rows // tr,), out_shape=SDS((nl, rows, cols), f32),
        in_specs=[pl.BlockSpec((N_CHIPS, tr, cols), lambda i: (0, i, 0))] * nl,
        out_specs=pl.BlockSpec((nl, tr, cols), lambda i: (0, i, 0)),
        compiler_params=_cparams(1),
    )(*rs)


def _add_pairs(pairs, name):
    n = len(pairs)

    def body(*refs):
        for i in range(n):
            refs[2 * n + i][...] = refs[2 * i][...] + refs[2 * i + 1][...]

    return pl.pallas_call(body, name=name, out_shape=tuple(SDS(a.shape, f32) for a, _ in pairs),
                          compiler_params=_cparams())(*[t for pair in pairs for t in pair])


def _adamw(g_parts, w, m, v, name):
    rows, cols = w.shape
    tr = _row_block(rows, 512)
    n_g = len(g_parts)
    c1 = 1.0 / (1.0 - ADAM_B1 ** ADAM_STEP)
    c2 = 1.0 / (1.0 - ADAM_B2 ** ADAM_STEP)

    def body(*refs):
        g_refs = refs[:n_g]
        w_ref, m_ref, v_ref, go_ref, d_ref, mo_ref, vo_ref = refs[n_g:]
        g = g_refs[0][...]
        for r in g_refs[1:]:
            g = g + r[...]
        mn = ADAM_B1 * m_ref[...] + (1.0 - ADAM_B1) * g
        vn = ADAM_B2 * v_ref[...] + (1.0 - ADAM_B2) * (g * g)
        go_ref[...] = g
        mo_ref[...] = mn
        vo_ref[...] = vn
        d_ref[...] = -ADAM_LR * ((mn * c1) / (jnp.sqrt(vn * c2) + ADAM_EPS) + ADAM_WD * w_ref[...])

    spec = pl.BlockSpec((tr, cols), lambda i: (i, 0))
    return pl.pallas_call(
        body, name=name, grid=(rows // tr,), out_shape=tuple(SDS((rows, cols), f32) for _ in range(4)),
        in_specs=[spec] * (n_g + 3), out_specs=(spec,) * 4, compiler_params=_cparams(1),
    )(*g_parts, w, m, v)


def _shifted_rows(ext):
    return ext[8:, :], pltpu.roll(ext, 1, axis=0)[8:, :], pltpu.roll(ext, 2, axis=0)[8:, :]


def _later_rows(ext):
    n = ext.shape[0]
    return pltpu.roll(ext, n - 1, axis=0)[0:n - 8, :], pltpu.roll(ext, n - 2, axis=0)[0:n - 8, :]


def _ffn_fwd(x_mid, gain_pre, gain_post, w_l, conv_w, conv_b, li, host=None):
    bsz, l, d = x_mid.shape
    f = w_l.shape[1]
    tl, ft = min(FFN_TL, l), min(FFN_FT, f)
    nt, nf = l // tl, f // ft
    n_sub = FFN_SPLIT if tl % (FFN_SPLIT * FFN_CH) == 0 else 1
    sub = tl // n_sub
    h_in, h_shape, h_out, h_scr, h_ops = _host_args(host)

    def body(x_ref, gpre_ref, gpost_ref, wg_ref, wv_ref, wd_ref, cw_ref, cb_ref, *rest):
        xo_ref, g_ref, v_ref, f_ref = rest[len(h_in):len(h_in) + 4]
        s0 = len(h_in) + 4 + len(h_out)
        h_sc, facc, gprev = rest[s0:s0 + 3]
        b, t, j = pl.program_id(0), pl.program_id(1), pl.program_id(2)
        finish = _hosted(host, rest[0], rest[len(h_in) + 4], rest[s0 + 3:],
                         (b == 0) & (t == 0) & (j == 0), (b == bsz - 1) & (t == nt - 1) & (j == nf - 1))

        @pl.when(j == 0)
        def _():
            h_sc[...] = _rms_fwd(x_ref[...], gpre_ref[...]).astype(bf16)
            facc[...] = jnp.zeros_like(facc)

        @pl.when(t == 0)
        def _():
            gprev[j] = jnp.zeros((8, ft), f32)

        wg, wv, wd = wg_ref[...], wv_ref[...], wd_ref[...]
        gs = [_dot_nt(h_sc[s * sub:(s + 1) * sub, :], wg) for s in range(n_sub)]
        vs = [_dot_nt(h_sc[s * sub:(s + 1) * sub, :], wv) for s in range(n_sub)]
        w0, w1, w2, bias = (jnp.broadcast_to(r, (FFN_CH, ft))
                            for r in (cw_ref[0:1, :], cw_ref[1:2, :], cw_ref[2:3, :], cb_ref[...]))
        hist = gprev[j]
        for s in range(n_sub):
            rows = slice(s * sub, (s + 1) * sub)
            g_ref[rows, :] = gs[s].astype(bf16)
            v_ref[rows, :] = vs[s].astype(bf16)
            hdn = []
            for r0 in range(0, sub, FFN_CH):
                g0, g1, g2 = _shifted_rows(jnp.concatenate([hist, gs[s][r0:r0 + FFN_CH, :]], axis=0))
                hist = g0[FFN_CH - 8:, :]
                hdn.append((_gelu(bias + w0 * g2 + w1 * g1 + w2 * g0) * vs[s][r0:r0 + FFN_CH, :]).astype(bf16))
            facc[rows, :] += _dot(jnp.concatenate(hdn, axis=0), wd)
        gprev[j] = hist

        @pl.when(j == nf - 1)
        def _():
            fv = facc[...]
            f_ref[...] = fv
            xo_ref[...] = x_ref[...] + _rms_fwd(fv, gpost_ref[...])

        finish()

    tok = pl.BlockSpec((None, tl, d), lambda b, t, j: (b, t, 0))
    hid = pl.BlockSpec((None, tl, ft), lambda b, t, j: (b, t, j))
    gain = pl.BlockSpec((1, d), lambda b, t, j: (0, 0))

    def wspec(kind):
        return pl.BlockSpec((None, ft, d), lambda b, t, j: (kind, j, 0))

    return pl.pallas_call(
        body, name=f"ffn_fwd_{li}", grid=(bsz, nt, nf),
        out_shape=(SDS((bsz, l, d), f32), SDS((bsz, l, f), bf16), SDS((bsz, l, f), bf16), SDS((bsz, l, d), f32),
                   *h_shape),
        in_specs=[tok, gain, gain, wspec(0), wspec(1), wspec(2),
                  pl.BlockSpec((None, 3, ft), lambda b, t, j: (li, 0, j)),
                  pl.BlockSpec((None, 1, ft), lambda b, t, j: (li, 0, j)), *h_in],
        out_specs=(tok, hid, hid, tok, *h_out),
        scratch_shapes=[pltpu.VMEM((tl, d), bf16), pltpu.VMEM((tl, d), f32), pltpu.VMEM((nf, 8, ft), f32), *h_scr],
        compiler_params=_cparams(3, side_effects=host is not None),
    )(x_mid, gain_pre, gain_post, w_l, w_l, w_l, conv_w, conv_b.reshape(conv_b.shape[0], 1, f), *h_ops)


def _ffn_bwd_act(x_mid, f_sv, g_sv, v_sv, dxo, gain_pre, gain_post, w_l, conv_w, conv_b, li, host=None):
    bsz, l, d = x_mid.shape
    f = w_l.shape[1]
    tl, ft = min(FFN_TL, l), min(FFN_FT, f)
    nt, nf = l // tl, f // ft
    n_sub = FFN_SPLIT if tl % (FFN_SPLIT * FFN_CH) == 0 else 1
    sub = tl // n_sub
    h_in, h_shape, h_out, h_scr, h_ops = _host_args(host)

    def body(x_ref, f_ref, dxo_ref, g_ref, v_ref, gh_ref, gpre_ref, gpost_ref, wg_ref, wv_ref, wd_ref, cw_ref, cb_ref,
             *rest):
        o0 = len(h_in)
        dx_ref, dg_ref, dv_ref, hdn_ref, h_ref, df_ref, dgain_ref, dconv_ref = rest[o0:o0 + 8]
        s0 = o0 + 8 + len(h_out)
        h_sc, df_sc, dh_acc, dgc_next = rest[s0:s0 + 4]
        b, t, j = pl.program_id(0), pl.program_id(1), pl.program_id(2)
        tt = nt - 1 - t
        finish = _hosted(host, rest[0], rest[o0 + 8], rest[s0 + 4:],
                         (b == 0) & (t == 0) & (j == 0), (b == bsz - 1) & (t == nt - 1) & (j == nf - 1))

        @pl.when((b == 0) & (t == 0) & (j == 0))
        def _():
            dgain_ref[...] = jnp.zeros_like(dgain_ref)
            dconv_ref[...] = jnp.zeros_like(dconv_ref)

        @pl.when(j == 0)
        def _():
            hb = _rms_fwd(x_ref[...], gpre_ref[...]).astype(bf16)
            h_sc[...] = hb
            h_ref[...] = hb
            df, dgp = _rms_bwd(f_ref[...], gpost_ref[...], dxo_ref[...])
            dfb = df.astype(bf16)
            df_sc[...] = dfb
            df_ref[...] = dfb
            dgain_ref[1:2, :] += dgp
            dh_acc[...] = jnp.zeros_like(dh_acc)

        @pl.when(t == 0)
        def _():
            dgc_next[j] = jnp.zeros((8, ft), f32)

        wg, wv, wd = wg_ref[...], wv_ref[...], wd_ref[...]
        dhdns = [_dot_nt(df_sc[s * sub:(s + 1) * sub, :], wd) for s in range(n_sub)]
        w0, w1, w2, bias = (jnp.broadcast_to(r, (FFN_CH, ft))
                            for r in (cw_ref[0:1, :], cw_ref[1:2, :], cw_ref[2:3, :], cb_ref[...]))

        def fold(a):
            return a.reshape(FFN_CH // 8, 8, ft).sum(axis=0)

        def conv_bwd(dgc, after):
            d1, d2 = _later_rows(jnp.concatenate([dgc, after], axis=0))
            return (w2 * dgc + w1 * d1 + w0 * d2).astype(bf16)

        def flush(s, dgs, dvs):
            rows = slice(s * sub, (s + 1) * sub)
            dgb, dvb = jnp.concatenate(dgs, axis=0), jnp.concatenate(dvs, axis=0)
            dg_ref[rows, :] = dgb
            dv_ref[rows, :] = dvb
            dh_acc[rows, :] += _dot(dgb, wg) + _dot(dvb, wv)

        hist = jnp.where(tt > 0, gh_ref[...].astype(f32)[8:16, :], 0.0)
        acc = [jnp.zeros((8, ft), f32)] * 4
        first, pending, dgs, dvs_prev = None, None, [], None
        for s in range(n_sub):
            rows = slice(s * sub, (s + 1) * sub)
            g = g_ref[rows, :].astype(f32)
            v = v_ref[rows, :].astype(f32)
            hdn, dvs = [], []
            for r0 in range(0, sub, FFN_CH):
                g0, g1, g2 = _shifted_rows(jnp.concatenate([hist, g[r0:r0 + FFN_CH, :]], axis=0))
                hist = g0[FFN_CH - 8:, :]
                vc, dc = v[r0:r0 + FFN_CH, :], dhdns[s][r0:r0 + FFN_CH, :]
                u, ug = _gelu_and_grad(bias + w0 * g2 + w1 * g1 + w2 * g0)
                hdn.append((u * vc).astype(bf16))
                dvs.append((dc * u).astype(bf16))
                dgc = dc * vc * ug
                acc = [acc[0] + fold(dgc * g2), acc[1] + fold(dgc * g1), acc[2] + fold(dgc * g0), acc[3] + fold(dgc)]
                if pending is None:
                    first = dgc[0:8, :]
                else:
                    dgs.append(conv_bwd(pending, dgc[0:8, :]))
                    if r0 == 0:
                        flush(s - 1, dgs, dvs_prev)
                        dgs = []
                pending = dgc
            hdn_ref[rows, :] = jnp.concatenate(hdn, axis=0)
            dvs_prev = dvs
        dgs.append(conv_bwd(pending, dgc_next[j]))
        flush(n_sub - 1, dgs, dvs_prev)
        dgc_next[j] = first
        for k in range(4):
            dconv_ref[j, k:k + 1, :] += jnp.sum(acc[k], axis=0, keepdims=True)

        @pl.when(j == nf - 1)
        def _():
            dxp, dgp = _rms_bwd(x_ref[...], gpre_ref[...], dh_acc[...])
            dx_ref[...] = dxo_ref[...] + dxp
            dgain_ref[0:1, :] += dgp

        finish()

    tok = pl.BlockSpec((None, tl, d), lambda b, t, j: (b, nt - 1 - t, 0))
    hid = pl.BlockSpec((None, tl, ft), lambda b, t, j: (b, nt - 1 - t, j))
    halo = pl.BlockSpec((None, 16, ft), lambda b, t, j: (b, jnp.maximum((nt - 1 - t) * (tl // 16) - 1, 0), j))
    gain = pl.BlockSpec((1, d), lambda b, t, j: (0, 0))

    def wspec(kind):
        return pl.BlockSpec((None, ft, d), lambda b, t, j: (kind, j, 0))

    return pl.pallas_call(
        body, name=f"ffn_bwd_act_{li}", grid=(bsz, nt, nf),
        out_shape=(SDS((bsz, l, d), f32), SDS((bsz, l, f), bf16), SDS((bsz, l, f), bf16), SDS((bsz, l, f), bf16),
                   SDS((bsz, l, d), bf16), SDS((bsz, l, d), bf16), SDS((8, d), f32), SDS((nf, 8, ft), f32), *h_shape),
        in_specs=[tok, tok, tok, hid, hid, halo, gain, gain, wspec(0), wspec(1), wspec(2),
                  pl.BlockSpec((None, 3, ft), lambda b, t, j: (li, 0, j)),
                  pl.BlockSpec((None, 1, ft), lambda b, t, j: (li, 0, j)), *h_in],
        out_specs=(tok, hid, hid, hid, tok, tok,
                   pl.BlockSpec((8, d), lambda b, t, j: (0, 0)), pl.BlockSpec((nf, 8, ft), lambda b, t, j: (0, 0, 0)),
                   *h_out),
        scratch_shapes=[pltpu.VMEM((tl, d), bf16), pltpu.VMEM((tl, d), bf16), pltpu.VMEM((tl, d), f32),
                        pltpu.VMEM((nf, 8, ft), f32), *h_scr],
        compiler_params=_cparams(3, side_effects=host is not None),
    )(x_mid, f_sv, dxo, g_sv, v_sv, g_sv, gain_pre, gain_post, w_l, w_l, w_l, conv_w,
      conv_b.reshape(conv_b.shape[0], 1, f), *h_ops)


def _ffn_bwd_weights(dg, dv, hdn, h, df, li):
    t, f = dg.shape
    d = h.shape[1]
    ft, tr = min(FFN_FT, f), min(WG_TR, t)
    nf, nr = f // ft, t // tr

    def body(dg_ref, dv_ref, hdn_ref, h_ref, df_ref, o_ref, acc):
        r = pl.program_id(1)

        @pl.when(r == 0)
        def _():
            acc[...] = jnp.zeros_like(acc)

        hb = h_ref[...]
        acc[0] += _dot_tn(dg_ref[...], hb)
        acc[1] += _dot_tn(dv_ref[...], hb)
        acc[2] += _dot_tn(hdn_ref[...], df_ref[...])

        @pl.when(r == nr - 1)
        def _():
            o_ref[...] = acc[...].astype(bf16)

    hid = pl.BlockSpec((tr, ft), lambda j, r: (r, j))
    tok = pl.BlockSpec((tr, d), lambda j, r: (r, 0))
    return pl.pallas_call(
        body, name=f"ffn_bwd_weights_{li}", grid=(nf, nr), out_shape=SDS((3, f, d), bf16),
        in_specs=[hid, hid, hid, tok, tok], out_specs=pl.BlockSpec((3, ft, d), lambda j, r: (0, j, 0)),
        scratch_shapes=[pltpu.VMEM((3, ft, d), f32)], compiler_params=_cparams(2),
    )(dg, dv, hdn, h, df)


def _pool_counts(t0, tl, d):
    gch = d // len(POOL_WINDOWS)
    tpos = (t0 + lax.broadcasted_iota(jnp.int32, (tl, d), 0) + 1).astype(f32)
    lane = lax.broadcasted_iota(jnp.int32, (tl, d), 1)
    win = jnp.full((tl, d), float(POOL_WINDOWS[-1]), f32)
    for gi in range(len(POOL_WINDOWS) - 2, -1, -1):
        win = jnp.where(lane < (gi + 1) * gch, float(POOL_WINDOWS[gi]), win)
    return jnp.minimum(tpos, win)


def _pool_select(parts, tl, d):
    gch = d // len(POOL_WINDOWS)
    lane = lax.broadcasted_iota(jnp.int32, (tl, d), 1)
    out = parts[-1]
    for gi in range(len(parts) - 2, -1, -1):
        out = jnp.where(lane < (gi + 1) * gch, parts[gi], out)
    return out


def _pool_window_sums(u, halo, tl):
    ext = jnp.concatenate([halo, u], axis=0)
    sums, cur = [], ext
    for k in (1, 2, 4, 8):
        cur = cur + pltpu.roll(cur, k, axis=0)
        sums.append(cur[POOL_HALO:POOL_HALO + tl, :])
    return sums


def _pool_mix(u, halo, cnt, pw_ref, scale, tl, d):
    gch = d // len(POOL_WINDOWS)
    pooled = _pool_select(_pool_window_sums(u, halo, tl), tl, d) / cnt
    diff = pooled - u
    outs = [_dot(diff[:, gi * gch:(gi + 1) * gch].astype(bf16), pw_ref[gi]) for gi in range(len(POOL_WINDOWS))]
    return diff, jnp.concatenate(outs, axis=1)


def _pool_fwd(x, gain_pre, gain_post, pw, scale, li):
    bsz, l, d = x.shape
    tl = min(POOL_TL, l)
    nt = l // tl

    def body(x_ref, gpre_ref, gpost_ref, pw_ref, sc_ref, xo_ref, halo):
        t = pl.program_id(1)

        @pl.when(t == 0)
        def _():
            halo[...] = jnp.zeros_like(halo)

        xv = x_ref[...]
        u = _rms_fwd(xv, gpre_ref[...])
        _, out = _pool_mix(u, halo[...], _pool_counts(t * tl, tl, d), pw_ref, sc_ref[...], tl, d)
        halo[...] = u[tl - POOL_HALO:tl, :]
        xo_ref[...] = xv + _rms_fwd(out * sc_ref[...], gpost_ref[...])

    tok = pl.BlockSpec((None, tl, d), lambda b, t: (b, t, 0))
    gain = pl.BlockSpec((1, d), lambda b, t: (0, 0))
    return pl.pallas_call(
        body, name=f"pool_fwd_{li}", grid=(bsz, nt), out_shape=SDS((bsz, l, d), f32),
        in_specs=[tok, gain, gain, pl.BlockSpec(pw.shape, lambda b, t: (0, 0, 0)), gain], out_specs=tok,
        scratch_shapes=[pltpu.VMEM((POOL_HALO, d), f32)], compiler_params=_cparams(2),
    )(x, gain_pre, gain_post, pw, scale)


def _pool_bwd(x, dxo, gain_pre, gain_post, pw, scale, li):
    bsz, l, d = x.shape
    tl = min(POOL_TL, l)
    nt = l // tl
    ng = len(POOL_WINDOWS)
    gch = d // ng
    n_ext = tl + POOL_HALO

    def body(x_ref, xh_ref, dxo_ref, gpre_ref, gpost_ref, pw_ref, sc_ref, dx_ref, dpw_ref, ds_ref, qnext):
        b, t = pl.program_id(0), pl.program_id(1)
        tt = nt - 1 - t

        @pl.when((b == 0) & (t == 0))
        def _():
            dpw_ref[...] = jnp.zeros_like(dpw_ref)
            ds_ref[...] = jnp.zeros_like(ds_ref)

        @pl.when(t == 0)
        def _():
            qnext[...] = jnp.zeros_like(qnext)

        xv = x_ref[...]
        gpre = gpre_ref[...]
        u = _rms_fwd(xv, gpre)
        uh = _rms_fwd(jnp.where(tt > 0, xh_ref[...], 0.0), gpre)
        cnt = _pool_counts(tt * tl, tl, d)
        scale_v = sc_ref[...]
        diff, out = _pool_mix(u, uh, cnt, pw_ref, scale_v, tl, d)
        dxo_v = dxo_ref[...]
        dm, dgpost = _rms_bwd(out * scale_v, gpost_ref[...], dxo_v)
        ds_ref[1:2, :] += dgpost
        ds_ref[2:3, :] += jnp.sum(dm * out, axis=0, keepdims=True)
        dout = (dm * scale_v).astype(bf16)
        ddiffs = []
        for gi in range(ng):
            sl = slice(gi * gch, (gi + 1) * gch)
            dpw_ref[gi] += _dot_tn(diff[:, sl].astype(bf16), dout[:, sl])
            ddiffs.append(_dot_nt(dout[:, sl], pw_ref[gi]))
        ddiff = jnp.concatenate(ddiffs, axis=1)
        q = ddiff / cnt
        ext = jnp.concatenate([q, qnext[...]], axis=0)
        sums, cur = [], ext
        for k in (1, 2, 4, 8):
            cur = cur + pltpu.roll(cur, n_ext - k, axis=0)
            sums.append(cur[0:tl, :])
        du = _pool_select(sums, tl, d) - ddiff
        qnext[...] = q[0:POOL_HALO, :]
        dxp, dgpre = _rms_bwd(xv, gpre, du)
        ds_ref[0:1, :] += dgpre
        dx_ref[...] = dxo_v + dxp

    tok = pl.BlockSpec((None, tl, d), lambda b, t: (b, nt - 1 - t, 0))
    halo = pl.BlockSpec((None, POOL_HALO, d),
                        lambda b, t: (b, jnp.maximum((nt - 1 - t) * (tl // POOL_HALO) - 1, 0), 0))
    gain = pl.BlockSpec((1, d), lambda b, t: (0, 0))
    return pl.pallas_call(
        body, name=f"pool_bwd_{li}", grid=(bsz, nt),
        out_shape=(SDS((bsz, l, d), f32), SDS((ng, gch, gch), f32), SDS((8, d), f32)),
        in_specs=[tok, halo, tok, gain, gain, pl.BlockSpec(pw.shape, lambda b, t: (0, 0, 0)), gain],
        out_specs=(tok, pl.BlockSpec((ng, gch, gch), lambda b, t: (0, 0, 0)), pl.BlockSpec((8, d), lambda b, t: (0, 0))),
        scratch_shapes=[pltpu.VMEM((POOL_HALO, d), f32)], compiler_params=_cparams(2),
    )(x, x, dxo, gain_pre, gain_post, pw, scale)


def _s5_discretise(lam_re, lam_im, log_dt, b_re, b_im):
    def body(lr_ref, li_ref, ld_ref, br_ref, bi_ref, ar_ref, ai_ref, bbr_ref, bbi_ref):
        lr, li = lr_ref[...], li_ref[...]
        dt = jnp.exp(ld_ref[...])
        mag = jnp.exp(lr * dt)
        ar = mag * jnp.cos(li * dt)
        ai = mag * jnp.sin(li * dt)
        den = lr * lr + li * li
        nr, ni = ar - 1.0, ai
        fr = (nr * lr + ni * li) / den
        fi = (ni * lr - nr * li) / den
        br, bi = br_ref[...], bi_ref[...]
        ar_ref[...] = ar
        ai_ref[...] = ai
        bbr_ref[...] = fr * br - fi * bi
        bbi_ref[...] = fr * bi + fi * br

    return pl.pallas_call(
        body, name="s5_discretise",
        out_shape=(SDS(lam_re.shape, f32), SDS(lam_re.shape, f32), SDS(b_re.shape, f32), SDS(b_re.shape, f32)),
        compiler_params=_cparams(),
    )(lam_re, lam_im, log_dt, b_re, b_im)


def _s5_discretise_bwd(lam_re, lam_im, log_dt, b_re, b_im, g_ar, g_ai, g_bbr, g_bbi):
    def body(lr_ref, li_ref, ld_ref, br_ref, bi_ref, gar_ref, gai_ref, gbbr_ref, gbbi_ref,
             dlr_ref, dli_ref, dld_ref, dbr_ref, dbi_ref):
        lr, li = lr_ref[...], li_ref[...]
        dt = jnp.exp(ld_ref[...])
        mag = jnp.exp(lr * dt)
        cs, sn = jnp.cos(li * dt), jnp.sin(li * dt)
        ar, ai = mag * cs, mag * sn
        den = lr * lr + li * li
        nr, ni = ar - 1.0, ai
        fr = (nr * lr + ni * li) / den
        fi = (ni * lr - nr * li) / den
        br, bi = br_ref[...], bi_ref[...]
        gbbr, gbbi = gbbr_ref[...], gbbi_ref[...]
        dbr_ref[...] = fr * gbbr + fi * gbbi
        dbi_ref[...] = fr * gbbi - fi * gbbr
        gfr = jnp.sum(br * gbbr + bi * gbbi, axis=2, keepdims=True)
        gfi = jnp.sum(br * gbbi - bi * gbbr, axis=2, keepdims=True)
        gnr_num, gni_num = gfr / den, gfi / den
        gden = -(gfr * fr + gfi * fi) / den
        g_nr = gnr_num * lr - gni_num * li
        g_ni = gnr_num * li + gni_num * lr
        dlr = gnr_num * nr + gni_num * ni + gden * 2.0 * lr
        dli = gnr_num * ni - gni_num * nr + gden * 2.0 * li
        gar = gar_ref[...] + g_nr
        gai = gai_ref[...] + g_ni
        gq = (gar * cs + gai * sn) * mag
        gth = (gai * cs - gar * sn) * mag
        dlr_ref[...] = dlr + gq * dt
        dli_ref[...] = dli + gth * dt
        dld_ref[...] = jnp.sum(gq * lr + gth * li, axis=3, keepdims=True) * dt

    return pl.pallas_call(
        body, name="s5_discretise_bwd",
        out_shape=(SDS(lam_re.shape, f32), SDS(lam_re.shape, f32), SDS(log_dt.shape, f32),
                   SDS(b_re.shape, f32), SDS(b_re.shape, f32)),
        compiler_params=_cparams(),
    )(lam_re, lam_im, log_dt, b_re, b_im, g_ar, g_ai, g_bbr, g_bbi)


def _fill_powers(pw_r, pw_i, ar, ai, nj, width):
    cb = min(S5_CB, width)
    for c0 in range(0, width, cb):
        sl = pl.ds(c0, cb)
        a_r, a_i = ar[:, c0:c0 + cb], ai[:, c0:c0 + cb]

        def step(i, p):
            pw_r[i, :, sl] = p[0]
            pw_i[i, :, sl] = p[1]
            return _cmul(p[0], p[1], a_r, a_i)

        lax.fori_loop(0, nj, step, (a_r, a_i))


def _interleaved_scan(xr_sc, xi_sc, ar, ai, pw_r, pw_i, carry_r, carry_i, nj, width, reverse, h_sc=None):
    cb = min(S5_CB, width)
    acc_out = []
    for c0 in range(0, width, cb):
        sl = pl.ds(c0, cb)
        a_r, a_i = ar[:, c0:c0 + cb], ai[:, c0:c0 + cb]
        aj_r, aj_i = pw_r[nj - 1, :, sl], pw_i[nj - 1, :, sl]

        def pos(i):
            return nj - 1 - i if reverse else i

        def local_step(i, st):
            j = pos(i)
            hr, hi = _cmul(a_r, a_i, st[0], st[1])
            hr, hi = hr + xr_sc[j, :, sl], hi + xi_sc[j, :, sl]
            xr_sc[j, :, sl] = hr
            xi_sc[j, :, sl] = hi
            return hr, hi

        zero = jnp.zeros((8, cb), f32)
        fin_r, fin_i = lax.fori_loop(0, nj, local_step, (zero, zero))
        row = lax.broadcasted_iota(jnp.int32, (8, cb), 0)
        c_r, c_i = carry_r[0:1, sl], carry_i[0:1, sl]
        ent_r, ent_i = zero, zero
        order = range(7, -1, -1) if reverse else range(8)
        for s in order:
            ent_r = jnp.where(row == s, c_r, ent_r)
            ent_i = jnp.where(row == s, c_i, ent_i)
            pr, pi_ = _cmul(aj_r[0:1, :], aj_i[0:1, :], c_r, c_i)
            c_r, c_i = fin_r[s:s + 1, :] + pr, fin_i[s:s + 1, :] + pi_
        carry_r[:, sl] = jnp.broadcast_to(c_r, (8, cb))
        carry_i[:, sl] = jnp.broadcast_to(c_i, (8, cb))

        def fix_step(i, st):
            j = pos(i)
            nx_r, nx_i, acc_r, acc_i = st
            cr_, ci_ = _cmul(pw_r[i, :, sl], pw_i[i, :, sl], ent_r, ent_i)
            hr, hi = xr_sc[j, :, sl] + cr_, xi_sc[j, :, sl] + ci_
            xr_sc[j, :, sl] = hr
            xi_sc[j, :, sl] = hi
            if h_sc is not None:
                sr, si = h_sc[0][j, :, sl], h_sc[1][j, :, sl]
                acc_r = acc_r + nx_r * sr + nx_i * si
                acc_i = acc_i + nx_i * sr - nx_r * si
                nx_r, nx_i = hr, hi
            return nx_r, nx_i, acc_r, acc_i

        st = lax.fori_loop(0, nj, fix_step, (ent_r, ent_i, zero, zero))
        acc_out.append((st[2], st[3]))
    return acc_out


def _diag_mask(n_rep, h, ks):
    assert h & (h - 1) == 0 and (ks // n_rep) & (ks // n_rep - 1) == 0
    r = lax.shift_right_logical(lax.broadcasted_iota(jnp.int32, (n_rep * h, ks), 0), h.bit_length() - 1)
    c = lax.shift_right_logical(lax.broadcasted_iota(jnp.int32, (n_rep * h, ks), 1), (ks // n_rep).bit_length() - 1)
    return r == c


def _expand_block_diag(compact, n_rep):
    h, ks = compact.shape
    full = jnp.concatenate([compact] * n_rep, axis=0)
    return jnp.where(_diag_mask(n_rep, h, ks), full, 0.0).astype(bf16)


def _compact_block_diag(full, n_rep):
    rows, ks = full.shape
    h = rows // n_rep
    return jnp.sum(jnp.where(_diag_mask(n_rep, h, ks), full, 0.0).reshape(n_rep, h, ks), axis=0)


def _s5_scan_fwd(x_il, gain_pre, d_skip, ab_r, ab_i, bbc_r, bbc_i, cc_r, cc_in, li, host=None):
    bsz, nblk, tq, d = x_il.shape
    nkb, gch, ks = bbc_r.shape
    kc = d // nkb
    n_rep = kc // gch
    nj = tq // 8
    h_in, h_shape, h_out, h_scr, h_ops = _host_args(host)

    def body(xf_ref, xk_ref, gk_ref, dk_ref, ar_ref, ai_ref, bbr_ref, bbi_ref, cr_ref, ci_ref, *rest):
        o0 = len(h_in)
        y_ref, hr_ref, hi_ref = rest[o0:o0 + 3]
        s0 = o0 + 3 + len(h_out)
        xr_sc, xi_sc, pw_r, pw_i, carry_r, carry_i, bbr_sc, bbi_sc, crt_sc, cit_sc = rest[s0:s0 + 10]
        k, b, n = pl.program_id(0), pl.program_id(1), pl.program_id(2)
        finish = _hosted(host, rest[0], rest[o0 + 3], rest[s0 + 10:],
                         (k == 0) & (b == 0) & (n == 0), (k == nkb - 1) & (b == bsz - 1) & (n == nblk - 1))
        ar = jnp.broadcast_to(ar_ref[...], (8, ks))
        ai = jnp.broadcast_to(ai_ref[...], (8, ks))

        @pl.when((b == 0) & (n == 0))
        def _():
            bbr_sc[...] = _expand_block_diag(bbr_ref[...], n_rep)
            bbi_sc[...] = _expand_block_diag(bbi_ref[...], n_rep)
            crt_sc[...] = _expand_block_diag(cr_ref[...], n_rep)
            cit_sc[...] = _expand_block_diag(ci_ref[...], n_rep)
            _fill_powers(pw_r, pw_i, ar, ai, nj, ks)

        @pl.when(n == 0)
        def _():
            carry_r[...] = jnp.zeros_like(carry_r)
            carry_i[...] = jnp.zeros_like(carry_i)

        u = xk_ref[...] * _rms_scale(xf_ref[...]) * gk_ref[...]
        ub = u.astype(bf16)
        xr_sc[...] = _dot(ub, bbr_sc[...]).reshape(nj, 8, ks)
        xi_sc[...] = _dot(ub, bbi_sc[...]).reshape(nj, 8, ks)
        _interleaved_scan(xr_sc, xi_sc, ar, ai, pw_r, pw_i, carry_r, carry_i, nj, ks, reverse=False)
        hrb = xr_sc[...].reshape(tq, ks).astype(bf16)
        hib = xi_sc[...].reshape(tq, ks).astype(bf16)
        hr_ref[...] = hrb
        hi_ref[...] = hib
        y_ref[...] = _dot_nt(hrb, crt_sc[...]) + _dot_nt(hib, cit_sc[...]) + dk_ref[...] * u
        finish()

    full = pl.BlockSpec((None, None, tq, d), lambda k, b, n: (b, n, 0, 0))
    chan = pl.BlockSpec((None, None, tq, kc), lambda k, b, n: (b, n, 0, k))
    stat = pl.BlockSpec((None, None, tq, ks), lambda k, b, n: (b, n, 0, k))
    vec_c = pl.BlockSpec((1, kc), lambda k, b, n: (0, k))
    vec_s = pl.BlockSpec((1, ks), lambda k, b, n: (0, k))
    cmap = pl.BlockSpec((None, gch, ks), lambda k, b, n: (k, 0, 0))
    s_tot = nkb * ks
    return pl.pallas_call(
        body, name=f"s5_scan_fwd_{li}", grid=(nkb, bsz, nblk),
        out_shape=(SDS((bsz, nblk, tq, d), f32), SDS((bsz, nblk, tq, s_tot), bf16), SDS((bsz, nblk, tq, s_tot), bf16),
                   *h_shape),
        in_specs=[full, chan, vec_c, vec_c, vec_s, vec_s, cmap, cmap, cmap, cmap, *h_in],
        out_specs=(chan, stat, stat, *h_out),
        scratch_shapes=[pltpu.VMEM((nj, 8, ks), f32)] * 4 + [pltpu.VMEM((8, ks), f32)] * 2
        + [pltpu.VMEM((kc, ks), bf16)] * 4 + h_scr,
        compiler_params=_cparams(3, side_effects=host is not None),
    )(x_il, x_il, gain_pre, d_skip, ab_r, ab_i, bbc_r, bbc_i, cc_r, cc_in, *h_ops)


def _s5_glu_fwd(y, x_il, w_glu, b_glu, gain_post, li):
    t, d = y.shape
    tr = min(TOK_TR, t)

    def body(y_ref, x_ref, w_ref, b_ref, gp_ref, xo_ref):
        z = _gelu(y_ref[...])
        a = _dot(z.astype(bf16), w_ref[...]) + b_ref[...]
        xo_ref[...] = x_ref[...] + _rms_fwd(z * _sigmoid(a), gp_ref[...])

    tok = pl.BlockSpec((tr, d), lambda i: (i, 0))
    vec = pl.BlockSpec((1, d), lambda i: (0, 0))
    return pl.pallas_call(
        body, name=f"s5_glu_fwd_{li}", grid=(t // tr,), out_shape=SDS((t, d), f32),
        in_specs=[tok, tok, pl.BlockSpec((d, d), lambda i: (0, 0)), vec, vec], out_specs=tok,
        compiler_params=_cparams(1),
    )(y, x_il, w_glu, b_glu, gain_post)


def _s5_glu_bwd(y, dxo, w_glu, b_glu, gain_post, li):
    t, d = y.shape
    tr = min(TOK_TR, t)

    def body(y_ref, dxo_ref, w_ref, b_ref, gp_ref, dy_ref, dw_ref, ds_ref):
        @pl.when(pl.program_id(0) == 0)
        def _():
            dw_ref[...] = jnp.zeros_like(dw_ref)
            ds_ref[...] = jnp.zeros_like(ds_ref)

        z, zg = _gelu_and_grad(y_ref[...])
        zb = z.astype(bf16)
        w = w_ref[...]
        s = _sigmoid(_dot(zb, w) + b_ref[...])
        dm, dgpost = _rms_bwd(z * s, gp_ref[...], dxo_ref[...])
        da = dm * z * s * (1.0 - s)
        dab = da.astype(bf16)
        dz = dm * s + _dot_nt(dab, w)
        dw_ref[...] += _dot_tn(zb, dab)
        ds_ref[0:1, :] += dgpost
        ds_ref[1:2, :] += jnp.sum(da, axis=0, keepdims=True)
        dy_ref[...] = dz * zg

    tok = pl.BlockSpec((tr, d), lambda i: (i, 0))
    vec = pl.BlockSpec((1, d), lambda i: (0, 0))
    mat = pl.BlockSpec((d, d), lambda i: (0, 0))
    return pl.pallas_call(
        body, name=f"s5_glu_bwd_{li}", grid=(t // tr,),
        out_shape=(SDS((t, d), f32), SDS((d, d), f32), SDS((8, d), f32)),
        in_specs=[tok, tok, mat, vec, vec], out_specs=(tok, mat, pl.BlockSpec((8, d), lambda i: (0, 0))),
        compiler_params=_cparams(1),
    )(y, dxo, w_glu, b_glu, gain_post)


def _s5_scan_bwd(x_il, dy, h_r, h_i, gain_pre, d_skip, ab_r, ab_i, bbc_r, bbc_i, cc_r, cc_in, li, host=None):
    bsz, nblk, tq, d = x_il.shape
    nkb, gch, ks = bbc_r.shape
    kc = d // nkb
    n_rep = kc // gch
    nj = tq // 8
    h_in, h_shape, h_out, h_scr, h_ops = _host_args(host)

    def body(xf_ref, xk_ref, dy_ref, hr_ref, hi_ref, gk_ref, dk_ref, ar_ref, ai_ref, bbr_ref, bbi_ref, cr_ref, ci_ref,
             *rest):
        o0 = len(h_in)
        du_ref, dbbr_ref, dbbi_ref, dcr_ref, dci_ref, dar_ref, dai_ref, dd_ref = rest[o0:o0 + 8]
        s0 = o0 + 8 + len(h_out)
        (gr_sc, gi_sc, hr_sc, hi_sc, pw_r, pw_i, carry_r, carry_i, acc_r, acc_i,
         bbr_sc, bbi_sc, crt_sc, cit_sc, dbbr_acc, dbbi_acc, dcr_acc, dci_acc) = rest[s0:s0 + 18]
        k, b, n = pl.program_id(0), pl.program_id(1), pl.program_id(2)
        finish = _hosted(host, rest[0], rest[o0 + 8], rest[s0 + 18:],
                         (k == 0) & (b == 0) & (n == 0), (k == nkb - 1) & (b == bsz - 1) & (n == nblk - 1))
        ar = jnp.broadcast_to(ar_ref[...], (8, ks))
        ai = jnp.broadcast_to(-ai_ref[...], (8, ks))

        @pl.when((b == 0) & (n == 0))
        def _():
            bbr_sc[...] = _expand_block_diag(bbr_ref[...], n_rep)
            bbi_sc[...] = _expand_block_diag(bbi_ref[...], n_rep)
            crt_sc[...] = _expand_block_diag(cr_ref[...], n_rep)
            cit_sc[...] = _expand_block_diag(ci_ref[...], n_rep)
            _fill_powers(pw_r, pw_i, ar, ai, nj, ks)
            for ref in (dar_ref, dai_ref, dd_ref, acc_r, acc_i, dbbr_acc, dbbi_acc, dcr_acc, dci_acc):
                ref[...] = jnp.zeros_like(ref)

        @pl.when(n == 0)
        def _():
            carry_r[...] = jnp.zeros_like(carry_r)
            carry_i[...] = jnp.zeros_like(carry_i)

        u = xk_ref[...] * _rms_scale(xf_ref[...]) * gk_ref[...]
        ub = u.astype(bf16)
        dyv = dy_ref[...]
        dyb = dyv.astype(bf16)
        dd_ref[0:1, :] += jnp.sum(dyv * u, axis=0, keepdims=True)
        hrb, hib = hr_ref[...], hi_ref[...]
        dcr_acc[...] += _dot_tn(dyb, hrb)
        dci_acc[...] += _dot_tn(dyb, hib)
        hr_sc[...] = hrb.astype(f32).reshape(nj, 8, ks)
        hi_sc[...] = hib.astype(f32).reshape(nj, 8, ks)
        gr_sc[...] = _dot(dyb, crt_sc[...]).reshape(nj, 8, ks)
        gi_sc[...] = _dot(dyb, cit_sc[...]).reshape(nj, 8, ks)
        accs = _interleaved_scan(gr_sc, gi_sc, ar, ai, pw_r, pw_i, carry_r, carry_i, nj, ks, reverse=True,
                                 h_sc=(hr_sc, hi_sc))
        cb = min(S5_CB, ks)
        for q, (a_r, a_i) in enumerate(accs):
            acc_r[:, q * cb:(q + 1) * cb] += a_r
            acc_i[:, q * cb:(q + 1) * cb] += a_i
        grb = gr_sc[...].reshape(tq, ks).astype(bf16)
        gib = gi_sc[...].reshape(tq, ks).astype(bf16)
        dbbr_acc[...] += _dot_tn(ub, grb)
        dbbi_acc[...] += _dot_tn(ub, gib)
        du_ref[...] = dyv * dk_ref[...] + _dot_nt(grb, bbr_sc[...]) + _dot_nt(gib, bbi_sc[...])

        @pl.when((b == bsz - 1) & (n == nblk - 1))
        def _():
            dar_ref[0:1, :] = jnp.sum(acc_r[...], axis=0, keepdims=True)
            dai_ref[0:1, :] = jnp.sum(acc_i[...], axis=0, keepdims=True)
            dbbr_ref[...] = _compact_block_diag(dbbr_acc[...], n_rep)
            dbbi_ref[...] = _compact_block_diag(dbbi_acc[...], n_rep)
            dcr_ref[...] = _compact_block_diag(dcr_acc[...], n_rep)
            dci_ref[...] = _compact_block_diag(dci_acc[...], n_rep)

        finish()

    full = pl.BlockSpec((None, None, tq, d), lambda k, b, n: (b, nblk - 1 - n, 0, 0))
    chan = pl.BlockSpec((None, None, tq, kc), lambda k, b, n: (b, nblk - 1 - n, 0, k))
    stat = pl.BlockSpec((None, None, tq, ks), lambda k, b, n: (b, nblk - 1 - n, 0, k))
    vec_c = pl.BlockSpec((1, kc), lambda k, b, n: (0, k))
    vec_s = pl.BlockSpec((1, ks), lambda k, b, n: (0, k))
    cmap = pl.BlockSpec((None, gch, ks), lambda k, b, n: (k, 0, 0))
    acc_s = pl.BlockSpec((None, 8, ks), lambda k, b, n: (k, 0, 0))
    acc_c = pl.BlockSpec((None, 8, kc), lambda k, b, n: (k, 0, 0))
    cshape = SDS((nkb, gch, ks), f32)
    return pl.pallas_call(
        body, name=f"s5_scan_bwd_{li}", grid=(nkb, bsz, nblk),
        out_shape=(SDS((bsz, nblk, tq, d), f32), cshape, cshape, cshape, cshape, SDS((nkb, 8, ks), f32),
                   SDS((nkb, 8, ks), f32), SDS((nkb, 8, kc), f32), *h_shape),
        in_specs=[full, chan, chan, stat, stat, vec_c, vec_c, vec_s, vec_s, cmap, cmap, cmap, cmap, *h_in],
        out_specs=(chan, cmap, cmap, cmap, cmap, acc_s, acc_s, acc_c, *h_out),
        scratch_shapes=([pltpu.VMEM((nj, 8, ks), f32)] * 6 + [pltpu.VMEM((8, ks), f32)] * 4
                        + [pltpu.VMEM((kc, ks), bf16)] * 4 + [pltpu.VMEM((kc, ks), f32)] * 4 + h_scr),
        compiler_params=_cparams(3, side_effects=host is not None),
    )(x_il, x_il, dy, h_r, h_i, gain_pre, d_skip, ab_r, ab_i, bbc_r, bbc_i, cc_r, cc_in, *h_ops)


def _norm_residual_bwd(x, du, dxo, gain, name):
    t, d = x.shape
    tr = min(TOK_TR, t)

    def body(x_ref, du_ref, dxo_ref, g_ref, dx_ref, dg_ref):
        @pl.when(pl.program_id(0) == 0)
        def _():
            dg_ref[...] = jnp.zeros_like(dg_ref)

        dxp, dgain = _rms_bwd(x_ref[...], g_ref[...], du_ref[...])
        dx_ref[...] = dxo_ref[...] + dxp
        dg_ref[0:1, :] += dgain

    tok = pl.BlockSpec((tr, d), lambda i: (i, 0))
    return pl.pallas_call(
        body, name=name, grid=(t // tr,), out_shape=(SDS((t, d), f32), SDS((8, d), f32)),
        in_specs=[tok, tok, tok, pl.BlockSpec((1, d), lambda i: (0, 0))],
        out_specs=(tok, pl.BlockSpec((8, d), lambda i: (0, 0))), compiler_params=_cparams(1),
    )(x, du, dxo, gain)


def _loss_head(y, target):
    t, d = y.shape
    tr = min(TOK_TR, t)

    def body(y_ref, t_ref, l_ref, dy_ref):
        @pl.when(pl.program_id(0) == 0)
        def _():
            l_ref[...] = jnp.zeros_like(l_ref)

        err = y_ref[...] - t_ref[...]
        dy_ref[...] = err * (1.0 / d)
        l_ref[...] += jnp.sum(jnp.sum(err * err, axis=1, keepdims=True), axis=0, keepdims=True)

    tok = pl.BlockSpec((tr, d), lambda i: (i, 0))
    return pl.pallas_call(
        body, name="loss_head", grid=(t // tr,), out_shape=(SDS((8, 128), f32), SDS((t, d), f32)),
        in_specs=[tok, tok], out_specs=(pl.BlockSpec((8, 128), lambda i: (0, 0)), tok), compiler_params=_cparams(1),
    )(y, target)


def _interleave(a, tq):
    bsz, l, d = a.shape
    return a.reshape(bsz, l // tq, 8, tq // 8, d).transpose(0, 1, 3, 2, 4).reshape(bsz, l // tq, tq, d)


def _deinterleave(a, l):
    bsz, nblk, tq, d = a.shape
    return a.reshape(bsz, nblk, tq // 8, 8, d).transpose(0, 1, 3, 2, 4).reshape(bsz, l, d)


def _compact_maps(a, nkb):
    ns, g, h, p = a.shape
    gl = g // nkb
    return a.reshape(ns, nkb, gl, h, p).transpose(0, 1, 3, 2, 4).reshape(ns, nkb, h, gl * p)


def _uncompact_maps(a, g):
    nkb, h, cols = a.shape
    gl = g // nkb
    return a.reshape(nkb, h, gl, cols // gl).transpose(0, 2, 1, 3).reshape(g, h, cols // gl)


def _pack_rows(arrs, cols, row_mult):
    flat = jnp.concatenate([a.reshape(-1).astype(f32) for a in arrs])
    rows = -(-flat.shape[0] // cols)
    rows = -(-rows // row_mult) * row_mult
    return jnp.pad(flat, (0, rows * cols - flat.shape[0])).reshape(rows, cols)


def _unpack_rows(buf, shapes):
    flat = buf.reshape(-1)
    out, off = [], 0
    for s in shapes:
        n = math.prod(s)
        out.append(flat[off:off + n].reshape(s))
        off += n
    return out


W_NAMES = ['s5_lambda_re', 's5_lambda_im', 's5_log_dt', 's5_b_re', 's5_b_im', 's5_c_re', 's5_c_im', 's5_d', 's5_w_glu',
           's5_b_glu', 'pool_w', 'pool_scale', 'ffn_w_gate', 'ffn_w_val', 'ffn_conv_w', 'ffn_conv_b', 'ffn_w_down',
           'norm_mix_pre', 'norm_mix_post', 'norm_ffn_pre', 'norm_ffn_post']
BIG = ('s5_w_glu', 'ffn_w_gate', 'ffn_w_val', 'ffn_w_down')


def kernel(x, s5_lambda_re, s5_lambda_im, s5_log_dt, s5_b_re, s5_b_im, s5_c_re, s5_c_im, s5_d, s5_w_glu, s5_b_glu, pool_w, pool_scale, ffn_w_gate, ffn_w_val, ffn_conv_w, ffn_conv_b, ffn_w_down, norm_mix_pre, norm_mix_post, norm_ffn_pre, norm_ffn_post, loss_target, m_s5_lambda_re, m_s5_lambda_im, m_s5_log_dt, m_s5_b_re, m_s5_b_im, m_s5_c_re, m_s5_c_im, m_s5_d, m_s5_w_glu, m_s5_b_glu, m_pool_w, m_pool_scale, m_ffn_w_gate, m_ffn_w_val, m_ffn_conv_w, m_ffn_conv_b, m_ffn_w_down, m_norm_mix_pre, m_norm_mix_post, m_norm_ffn_pre, m_norm_ffn_post, v_s5_lambda_re, v_s5_lambda_im, v_s5_log_dt, v_s5_b_re, v_s5_b_im, v_s5_c_re, v_s5_c_im, v_s5_d, v_s5_w_glu, v_s5_b_glu, v_pool_w, v_pool_scale, v_ffn_w_gate, v_ffn_w_val, v_ffn_conv_w, v_ffn_conv_b, v_ffn_w_down, v_norm_mix_pre, v_norm_mix_post, v_norm_ffn_pre, v_norm_ffn_post):
    w_in = dict(zip(W_NAMES, (s5_lambda_re, s5_lambda_im, s5_log_dt, s5_b_re, s5_b_im, s5_c_re, s5_c_im, s5_d, s5_w_glu,
                              s5_b_glu, pool_w, pool_scale, ffn_w_gate, ffn_w_val, ffn_conv_w, ffn_conv_b, ffn_w_down,
                              norm_mix_pre, norm_mix_post, norm_ffn_pre, norm_ffn_post)))
    m_in = dict(zip(W_NAMES, (m_s5_lambda_re, m_s5_lambda_im, m_s5_log_dt, m_s5_b_re, m_s5_b_im, m_s5_c_re, m_s5_c_im,
                              m_s5_d, m_s5_w_glu, m_s5_b_glu, m_pool_w, m_pool_scale, m_ffn_w_gate, m_ffn_w_val,
                              m_ffn_conv_w, m_ffn_conv_b, m_ffn_w_down, m_norm_mix_pre, m_norm_mix_post,
                              m_norm_ffn_pre, m_norm_ffn_post)))
    v_in = dict(zip(W_NAMES, (v_s5_lambda_re, v_s5_lambda_im, v_s5_log_dt, v_s5_b_re, v_s5_b_im, v_s5_c_re, v_s5_c_im,
                              v_s5_d, v_s5_w_glu, v_s5_b_glu, v_pool_w, v_pool_scale, v_ffn_w_gate, v_ffn_w_val,
                              v_ffn_conv_w, v_ffn_conv_b, v_ffn_w_down, v_norm_mix_pre, v_norm_mix_post,
                              v_norm_ffn_pre, v_norm_ffn_post)))

    bsz, seq, d = x.shape
    depth = ffn_w_gate.shape[0]
    n_s5, n_grp, n_state = s5_lambda_re.shape
    n_gch = s5_b_re.shape[3]
    n_pool = pool_w.shape[0]
    fs = ffn_w_gate.shape[2]
    f = N_CHIPS * fs
    gs = s5_w_glu.shape[1]
    nkb = d // S5_KB_CH
    tq = min(S5_TQ, seq)
    chip = 2 * lax.axis_index("x") + lax.axis_index("y")

    s_ffn = jnp.stack([ffn_w_gate.transpose(0, 2, 1), ffn_w_val.transpose(0, 2, 1), ffn_w_down], axis=1).astype(bf16)
    small_shard = [pool_w, pool_scale, ffn_conv_w]
    g_glu, g_small = _chip_exchange([("gather", 1, s5_w_glu.astype(bf16)), ("whole", 0, _pack_rows(small_shard, d, 16))],
                                    "allgather_small")
    parts = [_unpack_rows(g_small[k], [a.shape for a in small_shard]) for k in range(N_CHIPS)]
    pool_w_full = jnp.concatenate([p[0] for p in parts], axis=2).astype(bf16)
    pool_scale_full = jnp.concatenate([p[1] for p in parts], axis=1)
    conv_w_full = jnp.concatenate([p[2] for p in parts], axis=2)

    lam_r4 = s5_lambda_re.reshape(n_s5, n_grp, 1, n_state)
    lam_i4 = s5_lambda_im.reshape(n_s5, n_grp, 1, n_state)
    ldt4 = s5_log_dt.reshape(n_s5, n_grp, 1, 1)
    b_r4 = s5_b_re.transpose(0, 1, 3, 2)
    b_i4 = s5_b_im.transpose(0, 1, 3, 2)
    ab_r4, ab_i4, bb_r4, bb_i4 = _s5_discretise(lam_r4, lam_i4, ldt4, b_r4, b_i4)
    n_st_tot = n_grp * n_state
    ab_r, ab_i = ab_r4.reshape(n_s5, 1, n_st_tot), ab_i4.reshape(n_s5, 1, n_st_tot)
    bbc_r, bbc_i = _compact_maps(bb_r4, nkb), _compact_maps(bb_i4, nkb)
    cc_r, cc_in = _compact_maps(s5_c_re, nkb), _compact_maps(-s5_c_im, nkb)

    def s5_params(j):
        return dict(ab_r=ab_r[j], ab_i=ab_i[j], bb_r=bbc_r[j], bb_i=bbc_i[j], c_r=cc_r[j], c_in=cc_in[j],
                    d_skip=s5_d[j].reshape(1, d), w_glu=g_glu[j], b_glu=s5_b_glu[j].reshape(1, d))

    def row(a, i):
        return a[i].reshape(1, -1)

    saved = []
    xc = x
    w_layer = [None] * depth
    for i in range(depth):
        j = i // 2
        sv = dict(x_in=xc)
        if i % 2 == 0:
            sp = s5_params(j)
            x_il = _interleave(xc, tq)
            res = _s5_scan_fwd(x_il, row(norm_mix_pre, i), sp["d_skip"], sp["ab_r"], sp["ab_i"], sp["bb_r"], sp["bb_i"],
                               sp["c_r"], sp["c_in"], i, host=("gather", 1, s_ffn[0]) if i == 0 else None)
            y_il, h_r, h_i = res[:3]
            if i == 0:
                w_layer[0] = res[3]
            xo_il = _s5_glu_fwd(y_il.reshape(bsz * seq, d), x_il.reshape(bsz * seq, d), sp["w_glu"], sp["b_glu"],
                                row(norm_mix_post, i), i)
            x_mid = _deinterleave(xo_il.reshape(x_il.shape), seq)
            sv.update(sp=sp, x_il=x_il, y_il=y_il, h_r=h_r, h_i=h_i)
        else:
            x_mid = _pool_fwd(xc, row(norm_mix_pre, i), row(norm_mix_post, i), pool_w_full[j], row(pool_scale_full, j), i)
        res = _ffn_fwd(x_mid, row(norm_ffn_pre, i), row(norm_ffn_post, i), w_layer[i], conv_w_full, ffn_conv_b, i,
                       host=("gather", 1, s_ffn[i + 1]) if i + 1 < depth else None)
        xc, g_sv, v_sv, f_sv = res[:4]
        if i + 1 < depth:
            w_layer[i + 1] = res[4]
        sv.update(x_mid=x_mid, g=g_sv, v=v_sv, f=f_sv)
        saved.append(sv)

    sq, dy = _loss_head(xc.reshape(bsz * seq, d), loss_target.reshape(bsz * seq, d))
    loss = lax.psum(sq[0, 0] * (0.5 / d), ("x", "y", "c"))

    dx = dy.reshape(bsz, seq, d)
    dw_layers = [None] * depth
    r_layers = [None] * depth
    g_small_params = {n: [None] * w_in[n].shape[0] for n in W_NAMES if n not in BIG}
    dglu = [None] * n_s5
    for i in range(depth - 1, -1, -1):
        j = i // 2
        sv = saved[i]
        res = _ffn_bwd_act(sv["x_mid"], sv["f"], sv["g"], sv["v"], dx, row(norm_ffn_pre, i), row(norm_ffn_post, i),
                           w_layer[i], conv_w_full, ffn_conv_b, i,
                           host=("slab", 1, dw_layers[i + 1]) if i + 1 < depth else None)
        dx, dg, dv, hdn, hb, dfb, dgain, dconv = res[:8]
        if i + 1 < depth:
            r_layers[i + 1] = res[8]
        dw_layers[i] = _ffn_bwd_weights(dg.reshape(bsz * seq, f), dv.reshape(bsz * seq, f), hdn.reshape(bsz * seq, f),
                                        hb.reshape(bsz * seq, d), dfb.reshape(bsz * seq, d), i)
        g_small_params["norm_ffn_pre"][i] = dgain[0]
        g_small_params["norm_ffn_post"][i] = dgain[1]
        dconv = dconv.transpose(1, 0, 2).reshape(8, f)
        g_small_params["ffn_conv_w"][i] = dconv[0:3]
        g_small_params["ffn_conv_b"][i] = dconv[3]
        if i % 2 == 0:
            sp = sv["sp"]
            dxo_il = _interleave(dx, tq)
            dy_s, dglu[j], ds_glu = _s5_glu_bwd(sv["y_il"].reshape(bsz * seq, d), dxo_il.reshape(bsz * seq, d),
                                                sp["w_glu"], sp["b_glu"], row(norm_mix_post, i), i)
            res = _s5_scan_bwd(sv["x_il"], dy_s.reshape(sv["x_il"].shape), sv["h_r"], sv["h_i"], row(norm_mix_pre, i),
                               sp["d_skip"], sp["ab_r"], sp["ab_i"], sp["bb_r"], sp["bb_i"], sp["c_r"], sp["c_in"], i,
                               host=("slab", 1, dw_layers[0]) if i == 0 else None)
            du, dbb_r, dbb_i, dc_r, dc_in, dab_r, dab_i, dd = res[:8]
            if i == 0:
                r_layers[0] = res[8]
            dx_il, dgpre = _norm_residual_bwd(sv["x_il"].reshape(bsz * seq, d), du.reshape(bsz * seq, d),
                                              dxo_il.reshape(bsz * seq, d), row(norm_mix_pre, i), f"s5_pre_bwd_{i}")
            dx = _deinterleave(dx_il.reshape(sv["x_il"].shape), seq)
            g_small_params["norm_mix_pre"][i] = dgpre[0]
            g_small_params["norm_mix_post"][i] = ds_glu[0]
            g_small_params["s5_b_glu"][j] = ds_glu[1]
            g_small_params["s5_d"][j] = dd[:, 0, :].reshape(d)
            g_small_params["s5_c_re"][j] = _uncompact_maps(dc_r, n_grp)
            g_small_params["s5_c_im"][j] = -_uncompact_maps(dc_in, n_grp)
            sv["g_ab"] = (dab_r[:, 0, :].reshape(n_grp, 1, n_state), dab_i[:, 0, :].reshape(n_grp, 1, n_state),
                          _uncompact_maps(dbb_r, n_grp), _uncompact_maps(dbb_i, n_grp))
        else:
            dx, dpw, dsm = _pool_bwd(sv["x_in"], dx, row(norm_mix_pre, i), row(norm_mix_post, i), pool_w_full[j],
                                     row(pool_scale_full, j), i)
            g_small_params["norm_mix_pre"][i] = dsm[0]
            g_small_params["norm_mix_post"][i] = dsm[1]
            g_small_params["pool_scale"][j] = dsm[2]
            g_small_params["pool_w"][j] = dpw
    grad_x = dx

    g_ab = [saved[2 * j]["g_ab"] for j in range(n_s5)]
    d_lr, d_li, d_ld, d_br, d_bi = _s5_discretise_bwd(
        lam_r4, lam_i4, ldt4, b_r4, b_i4, jnp.stack([g[0] for g in g_ab]), jnp.stack([g[1] for g in g_ab]),
        jnp.stack([g[2] for g in g_ab]), jnp.stack([g[3] for g in g_ab]))
    small_full = {n: (jnp.stack(v) if v[0] is not None else None) for n, v in g_small_params.items()}
    small_full["s5_lambda_re"] = d_lr.reshape(n_s5, n_grp, n_state)
    small_full["s5_lambda_im"] = d_li.reshape(n_s5, n_grp, n_state)
    small_full["s5_log_dt"] = d_ld.reshape(n_s5, n_grp)
    small_full["s5_b_re"] = d_br.transpose(0, 1, 3, 2)
    small_full["s5_b_im"] = d_bi.transpose(0, 1, 3, 2)
    bc_names = ["s5_b_re", "s5_b_im", "s5_c_re", "s5_c_im"]
    small_names = [n for n in W_NAMES if n not in BIG and n not in bc_names and n != "pool_w"]
    q = _pack_rows([small_full[n] for n in small_names], d, 16)
    q_bc = _pack_rows([small_full[n] for n in bc_names], d, 16).astype(bf16)

    r_glu, r_pw, r_q, r_bc = _chip_exchange(
        [("slab", 1, jnp.stack(dglu).astype(bf16)), ("slab", 2, small_full["pool_w"]), ("whole", 0, q), ("whole", 0, q_bc)],
        "exchange_small_grads")
    p_ffn = _sum_slots_layers([r.reshape(N_CHIPS, 3 * fs, d) for r in r_layers], "sum_slots_ffn").reshape(depth * 3 * fs, d)
    p_glu = _sum_slots(r_glu.reshape(N_CHIPS, n_s5 * gs, d), "sum_slots_glu")
    pw_rows, pw_cols = math.prod(pool_w.shape[:3]), pool_w.shape[3]
    p_pw = _sum_slots(r_pw.reshape(N_CHIPS, pw_rows, pw_cols), "sum_slots_pool_w")
    p_q = _sum_slots(r_q, "sum_slots_small")
    p_bc = _sum_slots(r_bc, "sum_slots_bc")
    o_ffn, o_glu, o_pw, o_q, o_bc = _exchange_with_sibling([p_ffn, p_glu, p_pw, p_q, p_bc])

    outs = {}

    def put(name, res, shape):
        outs[name] = tuple(r.reshape(shape) for r in res)

    pf, of = p_ffn.reshape(depth, 3, fs, d), o_ffn.reshape(depth, 3, fs, d)
    for kind, name in ((0, "ffn_w_gate"), (1, "ffn_w_val")):
        parts_g = [a[:, kind].transpose(0, 2, 1).reshape(depth * d, fs) for a in (pf, of)]
        put(name, _adamw(parts_g, w_in[name].reshape(depth * d, fs), m_in[name].reshape(depth * d, fs),
                         v_in[name].reshape(depth * d, fs), f"adamw_{name}"), w_in[name].shape)
    put("ffn_w_down", _adamw([pf[:, 2].reshape(depth * fs, d), of[:, 2].reshape(depth * fs, d)],
                             ffn_w_down.reshape(depth * fs, d), m_ffn_w_down.reshape(depth * fs, d),
                             v_ffn_w_down.reshape(depth * fs, d), "adamw_ffn_w_down"), ffn_w_down.shape)
    put("s5_w_glu", _adamw([p_glu, o_glu], s5_w_glu.reshape(n_s5 * gs, d), m_s5_w_glu.reshape(n_s5 * gs, d),
                           v_s5_w_glu.reshape(n_s5 * gs, d), "adamw_s5_w_glu"), s5_w_glu.shape)

    q_tot, bc_tot, pw_tot = _add_pairs([(p_q, o_q), (p_bc, o_bc), (p_pw, o_pw)], "sum_small")
    g_small = dict(zip(small_names, _unpack_rows(q_tot, [small_full[n].shape for n in small_names])))
    g_small.update(zip(bc_names, _unpack_rows(bc_tot, [small_full[n].shape for n in bc_names])))
    g_small["pool_w"] = pw_tot.reshape(pool_w.shape)
    g_small["pool_scale"] = lax.dynamic_slice_in_dim(g_small["pool_scale"], chip * pool_scale.shape[1],
                                                     pool_scale.shape[1], axis=1)
    g_small["ffn_conv_w"] = lax.dynamic_slice_in_dim(g_small["ffn_conv_w"], chip * fs, fs, axis=2)
    all_small = [n for n in W_NAMES if n not in BIG]
    local_shapes = [w_in[n].shape for n in all_small]
    res = _adamw([_pack_rows([g_small[n] for n in all_small], d, 64)],
                 _pack_rows([w_in[n] for n in all_small], d, 64), _pack_rows([m_in[n] for n in all_small], d, 64),
                 _pack_rows([v_in[n] for n in all_small], d, 64), "adamw_small")
    unpacked = [_unpack_rows(r, local_shapes) for r in res]
    for idx, n in enumerate(all_small):
        outs[n] = (g_small[n], unpacked[1][idx], unpacked[2][idx], unpacked[3][idx])

    return (loss, grad_x, *[outs[n][0] for n in W_NAMES], *[outs[n][1] for n in W_NAMES],
            *[outs[n][2] for n in W_NAMES], *[outs[n][3] for n in W_NAMES])
```

```python
import math

import jax
import jax.numpy as jnp
from jax import lax
from jax.experimental import pallas as pl
from jax.experimental.pallas import tpu as pltpu

f32, bf16 = jnp.float32, jnp.bfloat16
SDS = jax.ShapeDtypeStruct
MESH = pl.DeviceIdType.MESH

RMS_EPS = 1e-6
GELU_C = math.sqrt(2.0 / math.pi)
GELU_K = 0.044715
ADAM_LR, ADAM_B1, ADAM_B2, ADAM_EPS, ADAM_WD, ADAM_STEP = 0.001, 0.9, 0.999, 1e-08, 0.01, 10
POOL_WINDOWS = (2, 4, 8, 16)
POOL_HALO = 16
S5_GROUP_CH = 16
S5_STATE = 64
S5_KB_CH = 256
N_CHIPS = 4

FFN_TL = 512
FFN_FT = 256
FFN_CH = 16
FFN_SPLIT = 2
WG_TR = 2048
POOL_TL = 512
S5_TQ = 512
S5_CB = 512
TOK_TR = 512
VMEM_LIMIT = 56 * 1024 * 1024

HBM = pl.BlockSpec(memory_space=pltpu.HBM)


def _cparams(n_axes=0, side_effects=False):
    kw = dict(vmem_limit_bytes=VMEM_LIMIT)
    if n_axes:
        kw["dimension_semantics"] = ("arbitrary",) * n_axes
    if side_effects:
        kw["has_side_effects"] = True
    return pltpu.CompilerParams(**kw)


def _dot(a, b):
    return jnp.dot(a, b, preferred_element_type=f32)


def _dot_nt(a, b):
    return lax.dot_general(a, b, (((1,), (1,)), ((), ())), preferred_element_type=f32)


def _dot_tn(a, b):
    return lax.dot_general(a, b, (((0,), (0,)), ((), ())), preferred_element_type=f32)


def _rms_scale(x):
    return lax.rsqrt(jnp.mean(x * x, axis=-1, keepdims=True) + RMS_EPS)


def _rms_fwd(x, gain):
    return x * _rms_scale(x) * gain


def _rms_bwd(x, gain, dy):
    r = _rms_scale(x)
    xn = x * r
    dgain = jnp.sum(dy * xn, axis=0, keepdims=True)
    dxn = dy * gain
    dx = r * (dxn - xn * jnp.mean(dxn * xn, axis=-1, keepdims=True))
    return dx, dgain


def _gelu(x):
    t = jnp.tanh(x * (GELU_C + (GELU_C * GELU_K) * (x * x)))
    hx = 0.5 * x
    return hx + hx * t


def _gelu_and_grad(x):
    x2 = x * x
    t = jnp.tanh(x * (GELU_C + (GELU_C * GELU_K) * x2))
    hx = 0.5 * x
    return hx + hx * t, (0.5 + 0.5 * t) + hx * (1.0 - t * t) * (GELU_C + (3.0 * GELU_C * GELU_K) * x2)


def _sigmoid(x):
    return 1.0 / (1.0 + jnp.exp(-x))


def _cmul(ar, ai, br, bi):
    return ar * br - ai * bi, ar * bi + ai * br


def _row_block(n, cap):
    best = None
    for d in range(16, min(n, cap) + 1, 16):
        if n % d == 0:
            best = d
    assert best is not None, n
    return best


def _mesh_pos():
    return lax.axis_index("x"), lax.axis_index("y"), lax.axis_index("c")


def _other_chips(x, y):
    return [(1 - x, y), (x, 1 - y), (1 - x, 1 - y)]


def _slab_index(ndim, axis, start, size):
    return tuple(pl.ds(start, size) if a == axis else slice(None) for a in range(ndim))


class _ChipExchange:
    def __init__(self, kind, axis, src, dst, send_sems, recv_sems, loc_sem):
        x, y, c = _mesh_pos()
        me = 2 * x + y
        nd = len(src.shape)
        size = src.shape[axis] if kind == "gather" else src.shape[axis] // N_CHIPS

        def src_for(kk):
            return src.at[_slab_index(nd, axis, kk * size, size)] if kind == "slab" else src

        def dst_for(kk):
            return dst.at[_slab_index(nd, axis, kk * size, size)] if kind == "gather" else dst.at[kk]

        self.own = pltpu.make_async_copy(src_for(me), dst_for(me), loc_sem)
        self.sends, self.recvs = [], []
        for j, chip in enumerate(_other_chips(x, y)):
            kk = 2 * chip[0] + chip[1]
            peer = dict(send_sem=send_sems.at[j], recv_sem=recv_sems.at[j], device_id=(chip[0], chip[1], c),
                        device_id_type=MESH)
            self.sends.append(pltpu.make_async_remote_copy(src_ref=src_for(kk), dst_ref=dst_for(me), **peer))
            self.recvs.append(pltpu.make_async_remote_copy(src_ref=src_for(me), dst_ref=dst_for(kk), **peer))

    def start(self):
        self.own.start()
        for cp in self.sends:
            cp.start()

    def finish(self):
        for cp in self.recvs:
            cp.wait_recv()
        for cp in self.sends:
            cp.wait_send()
        self.own.wait()


def _exchange_out_shape(kind, axis, src):
    if kind == "gather":
        shape = tuple(N_CHIPS * n if a == axis else n for a, n in enumerate(src.shape))
    elif kind == "slab":
        shape = (N_CHIPS,) + tuple(n // N_CHIPS if a == axis else n for a, n in enumerate(src.shape))
    else:
        shape = (N_CHIPS,) + tuple(src.shape)
    return SDS(shape, src.dtype)


EXCHANGE_SEMS = [pltpu.SemaphoreType.DMA((3,)), pltpu.SemaphoreType.DMA((3,)), pltpu.SemaphoreType.DMA((1,))]


def _chip_exchange(items, name):
    n = len(items)

    def body(*refs):
        ins, outs, sems = refs[:n], refs[n:2 * n], refs[2 * n:]
        exs = [_ChipExchange(kind, axis, ins[i], outs[i], sems[3 * i], sems[3 * i + 1], sems[3 * i + 2].at[0])
               for i, (kind, axis, _) in enumerate(items)]
        for ex in exs:
            ex.start()
        for ex in exs:
            ex.finish()

    return pl.pallas_call(
        body, name=name, out_shape=tuple(_exchange_out_shape(k, a, arr) for k, a, arr in items),
        in_specs=[HBM] * n, out_specs=tuple([HBM] * n), scratch_shapes=EXCHANGE_SEMS * n,
        compiler_params=_cparams(side_effects=True),
    )(*[arr for _, _, arr in items])


def _hosted(host, host_in, host_out, sems, first, last):
    if host is None:
        return lambda: None
    ex = _ChipExchange(host[0], host[1], host_in, host_out, sems[0], sems[1], sems[2].at[0])

    @pl.when(first)
    def _():
        ex.start()

    def finish():
        @pl.when(last)
        def _():
            ex.finish()

    return finish


def _host_args(host):
    if host is None:
        return [], [], [], [], []
    return [HBM], [_exchange_out_shape(*host)], [HBM], EXCHANGE_SEMS, [host[2]]


def _exchange_with_sibling(parts):
    n = len(parts)

    def body(*refs):
        ins, outs = refs[:n], refs[n:2 * n]
        send_sems, recv_sems = refs[2 * n:]
        x, y, c = _mesh_pos()
        cps = [pltpu.make_async_remote_copy(src_ref=s, dst_ref=t, send_sem=send_sems.at[i], recv_sem=recv_sems.at[i],
                                            device_id=(x, y, 1 - c), device_id_type=MESH)
               for i, (s, t) in enumerate(zip(ins, outs))]
        for cp in cps:
            cp.start()
        for cp in cps:
            cp.wait()

    return pl.pallas_call(
        body, name="exchange_with_sibling",
        out_shape=tuple(SDS(p.shape, p.dtype) for p in parts),
        in_specs=[HBM] * n, out_specs=tuple([HBM] * n),
        scratch_shapes=[pltpu.SemaphoreType.DMA((n,)), pltpu.SemaphoreType.DMA((n,))],
        compiler_params=_cparams(side_effects=True),
    )(*parts)


def _sum_slots(r, name):
    _, rows, cols = r.shape
    tr = _row_block(rows, 512)

    def body(r_ref, o_ref):
        o_ref[...] = ((r_ref[0].astype(f32) + r_ref[1].astype(f32)) + r_ref[2].astype(f32)) + r_ref[3].astype(f32)

    return pl.pallas_call(
        body, name=name, grid=(rows // tr,), out_shape=SDS((rows, cols), f32),
        in_specs=[pl.BlockSpec((N_CHIPS, tr, cols), lambda i: (0, i, 0))],
        out_specs=pl.BlockSpec((tr, cols), lambda i: (i, 0)),
        compiler_params=_cparams(1),
    )(r)


def _sum_slots_layers(rs, name):
    _, rows, cols = rs[0].shape
    nl = len(rs)
    tr = _row_block(rows, 256)

    def body(*refs):
        o_ref = refs[nl]
        for l in range(nl):
            r = refs[l]
            o_ref[l] = ((r[0].astype(f32) + r[1].astype(f32)) + r[2].astype(f32)) + r[3].astype(f32)

    return pl.pallas_call(
        body, name=name, grid=(rows // tr,), out_shape=SDS((nl, rows, cols), f32),
        in_specs=[pl.BlockSpec((N_CHIPS, tr, cols), lambda i: (0, i, 0))] * nl,
        out_specs=pl.BlockSpec((nl, tr, cols), lambda i: (0, i, 0)),
        compiler_params=_cparams(1),
    )(*rs)


def _add_pairs(pairs, name):
    n = len(pairs)

    def body(*refs):
        for i in range(n):
            refs[2 * n + i][...] = refs[2 * i][...] + refs[2 * i + 1][...]

    return pl.pallas_call(body, name=name, out_shape=tuple(SDS(a.shape, f32) for a, _ in pairs),
                          compiler_params=_cparams())(*[t for pair in pairs for t in pair])


def _adamw(g_parts, w, m, v, name):
    rows, cols = w.shape
    tr = _row_block(rows, 512)
    n_g = len(g_parts)
    c1 = 1.0 / (1.0 - ADAM_B1 ** ADAM_STEP)
    c2 = 1.0 / (1.0 - ADAM_B2 ** ADAM_STEP)

    def body(*refs):
        g_refs = refs[:n_g]
        w_ref, m_ref, v_ref, go_ref, d_ref, mo_ref, vo_ref = refs[n_g:]
        g = g_refs[0][...]
        for r in g_refs[1:]:
            g = g + r[...]
        mn = ADAM_B1 * m_ref[...] + (1.0 - ADAM_B1) * g
        vn = ADAM_B2 * v_ref[...] + (1.0 - ADAM_B2) * (g * g)
        go_ref[...] = g
        mo_ref[...] = mn
        vo_ref[...] = vn
        d_ref[...] = -ADAM_LR * ((mn * c1) / (jnp.sqrt(vn * c2) + ADAM_EPS) + ADAM_WD * w_ref[...])

    spec = pl.BlockSpec((tr, cols), lambda i: (i, 0))
    return pl.pallas_call(
        body, name=name, grid=(rows // tr,), out_shape=tuple(SDS((rows, cols), f32) for _ in range(4)),
        in_specs=[spec] * (n_g + 3), out_specs=(spec,) * 4, compiler_params=_cparams(1),
    )(*g_parts, w, m, v)


def _resident_spec(shape, n_grid):
    zeros = (0,) * len(shape)
    return pl.BlockSpec(tuple(shape), {2: lambda a, b: zeros, 3: lambda a, b, c: zeros}[n_grid],
                        pipeline_mode=pl.Buffered(1))


def _weight_tiles(w_ref, j, ft):
    rows = pl.ds(pl.multiple_of(j * ft, ft), ft)
    return w_ref[0, rows, :], w_ref[1, rows, :], w_ref[2, rows, :]


def _shifted_rows(ext):
    return ext[8:, :], pltpu.roll(ext, 1, axis=0)[8:, :], pltpu.roll(ext, 2, axis=0)[8:, :]


def _later_rows(ext):
    n = ext.shape[0]
    return pltpu.roll(ext, n - 1, axis=0)[0:n - 8, :], pltpu.roll(ext, n - 2, axis=0)[0:n - 8, :]


def _ffn_fwd(x_mid, gain_pre, gain_post, w_l, conv_w, conv_b, li, host=None):
    bsz, l, d = x_mid.shape
    f = w_l.shape[1]
    tl, ft = min(FFN_TL, l), min(FFN_FT, f)
    nt, nf = l // tl, f // ft
    n_sub = FFN_SPLIT if tl % (FFN_SPLIT * FFN_CH) == 0 else 1
    sub = tl // n_sub
    h_in, h_shape, h_out, h_scr, h_ops = _host_args(host)

    def body(x_ref, gpre_ref, gpost_ref, w_ref, cw_ref, cb_ref, *rest):
        xo_ref, g_ref, v_ref, f_ref = rest[len(h_in):len(h_in) + 4]
        s0 = len(h_in) + 4 + len(h_out)
        h_sc, facc, gprev = rest[s0:s0 + 3]
        b, t, j = pl.program_id(0), pl.program_id(1), pl.program_id(2)
        finish = _hosted(host, rest[0], rest[len(h_in) + 4], rest[s0 + 3:],
                         (b == 0) & (t == 0) & (j == 0), (b == bsz - 1) & (t == nt - 1) & (j == nf - 1))

        @pl.when(j == 0)
        def _():
            h_sc[...] = _rms_fwd(x_ref[...], gpre_ref[...]).astype(bf16)
            facc[...] = jnp.zeros_like(facc)

        @pl.when(t == 0)
        def _():
            gprev[j] = jnp.zeros((8, ft), f32)

        wg, wv, wd = _weight_tiles(w_ref, j, ft)
        gs = [_dot_nt(h_sc[s * sub:(s + 1) * sub, :], wg) for s in range(n_sub)]
        vs = [_dot_nt(h_sc[s * sub:(s + 1) * sub, :], wv) for s in range(n_sub)]
        w0, w1, w2, bias = (jnp.broadcast_to(r, (FFN_CH, ft))
                            for r in (cw_ref[0:1, :], cw_ref[1:2, :], cw_ref[2:3, :], cb_ref[...]))
        hist = gprev[j]
        for s in range(n_sub):
            rows = slice(s * sub, (s + 1) * sub)
            g_ref[rows, :] = gs[s].astype(bf16)
            v_ref[rows, :] = vs[s].astype(bf16)
            hdn = []
            for r0 in range(0, sub, FFN_CH):
                g0, g1, g2 = _shifted_rows(jnp.concatenate([hist, gs[s][r0:r0 + FFN_CH, :]], axis=0))
                hist = g0[FFN_CH - 8:, :]
                hdn.append((_gelu(bias + w0 * g2 + w1 * g1 + w2 * g0) * vs[s][r0:r0 + FFN_CH, :]).astype(bf16))
            facc[rows, :] += _dot(jnp.concatenate(hdn, axis=0), wd)
        gprev[j] = hist

        @pl.when(j == nf - 1)
        def _():
            fv = facc[...]
            f_ref[...] = fv
            xo_ref[...] = x_ref[...] + _rms_fwd(fv, gpost_ref[...])

        finish()

    tok = pl.BlockSpec((None, tl, d), lambda b, t, j: (b, t, 0))
    hid = pl.BlockSpec((None, tl, ft), lambda b, t, j: (j, b * nt + t, 0))
    gain = pl.BlockSpec((1, d), lambda b, t, j: (0, 0))
    hid_shape = SDS((nf, bsz * l, ft), bf16)
    return pl.pallas_call(
        body, name=f"ffn_fwd_{li}", grid=(bsz, nt, nf),
        out_shape=(SDS((bsz, l, d), f32), hid_shape, hid_shape, SDS((bsz, l, d), f32), *h_shape),
        in_specs=[tok, gain, gain, _resident_spec(w_l.shape, 3),
                  pl.BlockSpec((None, 3, ft), lambda b, t, j: (li, 0, j)),
                  pl.BlockSpec((None, 1, ft), lambda b, t, j: (li, 0, j)), *h_in],
        out_specs=(tok, hid, hid, tok, *h_out),
        scratch_shapes=[pltpu.VMEM((tl, d), bf16), pltpu.VMEM((tl, d), f32), pltpu.VMEM((nf, 8, ft), f32), *h_scr],
        compiler_params=_cparams(3, side_effects=host is not None),
    )(x_mid, gain_pre, gain_post, w_l, conv_w, conv_b.reshape(conv_b.shape[0], 1, f), *h_ops)


def _ffn_bwd_act(x_mid, f_sv, g_sv, v_sv, dxo, gain_pre, gain_post, w_l, conv_w, conv_b, li, host=None):
    bsz, l, d = x_mid.shape
    f = w_l.shape[1]
    tl, ft = min(FFN_TL, l), min(FFN_FT, f)
    nt, nf = l // tl, f // ft
    n_sub = FFN_SPLIT if tl % (FFN_SPLIT * FFN_CH) == 0 else 1
    sub = tl // n_sub
    h_in, h_shape, h_out, h_scr, h_ops = _host_args(host)

    def body(x_ref, f_ref, dxo_ref, g_ref, v_ref, gh_ref, gpre_ref, gpost_ref, w_ref, cw_ref, cb_ref, *rest):
        o0 = len(h_in)
        dx_ref, dg_ref, dv_ref, hdn_ref, h_ref, df_ref, dgain_ref, dconv_ref = rest[o0:o0 + 8]
        s0 = o0 + 8 + len(h_out)
        h_sc, df_sc, dh_acc, dgc_next = rest[s0:s0 + 4]
        b, t, j = pl.program_id(0), pl.program_id(1), pl.program_id(2)
        tt = nt - 1 - t
        finish = _hosted(host, rest[0], rest[o0 + 8], rest[s0 + 4:],
                         (b == 0) & (t == 0) & (j == 0), (b == bsz - 1) & (t == nt - 1) & (j == nf - 1))

        @pl.when((b == 0) & (t == 0) & (j == 0))
        def _():
            dgain_ref[...] = jnp.zeros_like(dgain_ref)
            dconv_ref[...] = jnp.zeros_like(dconv_ref)

        @pl.when(j == 0)
        def _():
            hb = _rms_fwd(x_ref[...], gpre_ref[...]).astype(bf16)
            h_sc[...] = hb
            h_ref[...] = hb
            df, dgp = _rms_bwd(f_ref[...], gpost_ref[...], dxo_ref[...])
            dfb = df.astype(bf16)
            df_sc[...] = dfb
            df_ref[...] = dfb
            dgain_ref[1:2, :] += dgp
            dh_acc[...] = jnp.zeros_like(dh_acc)

        @pl.when(t == 0)
        def _():
            dgc_next[j] = jnp.zeros((8, ft), f32)

        wg, wv, wd = _weight_tiles(w_ref, j, ft)
        dhdns = [_dot_nt(df_sc[s * sub:(s + 1) * sub, :], wd) for s in range(n_sub)]
        w0, w1, w2, bias = (jnp.broadcast_to(r, (FFN_CH, ft))
                            for r in (cw_ref[0:1, :], cw_ref[1:2, :], cw_ref[2:3, :], cb_ref[...]))

        def fold(a):
            return a.reshape(FFN_CH // 8, 8, ft).sum(axis=0)

        def conv_bwd(dgc, after):
            d1, d2 = _later_rows(jnp.concatenate([dgc, after], axis=0))
            return (w2 * dgc + w1 * d1 + w0 * d2).astype(bf16)

        def flush(s, dgs, dvs):
            rows = slice(s * sub, (s + 1) * sub)
            dgb, dvb = jnp.concatenate(dgs, axis=0), jnp.concatenate(dvs, axis=0)
            dg_ref[rows, :] = dgb
            dv_ref[rows, :] = dvb
            dh_acc[rows, :] += _dot(dgb, wg) + _dot(dvb, wv)

        hist = jnp.where(tt > 0, gh_ref[...].astype(f32)[8:16, :], 0.0)
        acc = [jnp.zeros((8, ft), f32)] * 4
        first, pending, dgs, dvs_prev = None, None, [], None
        for s in range(n_sub):
            rows = slice(s * sub, (s + 1) * sub)
            g = g_ref[rows, :].astype(f32)
            v = v_ref[rows, :].astype(f32)
            hdn, dvs = [], []
            for r0 in range(0, sub, FFN_CH):
                g0, g1, g2 = _shifted_rows(jnp.concatenate([hist, g[r0:r0 + FFN_CH, :]], axis=0))
                hist = g0[FFN_CH - 8:, :]
                vc, dc = v[r0:r0 + FFN_CH, :], dhdns[s][r0:r0 + FFN_CH, :]
                u, ug = _gelu_and_grad(bias + w0 * g2 + w1 * g1 + w2 * g0)
                hdn.append((u * vc).astype(bf16))
                dvs.append((dc * u).astype(bf16))
                dgc = dc * vc * ug
                acc = [acc[0] + fold(dgc * g2), acc[1] + fold(dgc * g1), acc[2] + fold(dgc * g0), acc[3] + fold(dgc)]
                if pending is None:
                    first = dgc[0:8, :]
                else:
                    dgs.append(conv_bwd(pending, dgc[0:8, :]))
                    if r0 == 0:
                        flush(s - 1, dgs, dvs_prev)
                        dgs = []
                pending = dgc
            hdn_ref[rows, :] = jnp.concatenate(hdn, axis=0)
            dvs_prev = dvs
        dgs.append(conv_bwd(pending, dgc_next[j]))
        flush(n_sub - 1, dgs, dvs_prev)
        dgc_next[j] = first
        for k in range(4):
            dconv_ref[j, k:k + 1, :] += jnp.sum(acc[k], axis=0, keepdims=True)

        @pl.when(j == nf - 1)
        def _():
            dxp, dgp = _rms_bwd(x_ref[...], gpre_ref[...], dh_acc[...])
            dx_ref[...] = dxo_ref[...] + dxp
            dgain_ref[0:1, :] += dgp

        finish()

    tok = pl.BlockSpec((None, tl, d), lambda b, t, j: (b, nt - 1 - t, 0))
    hid = pl.BlockSpec((None, tl, ft), lambda b, t, j: (j, b * nt + nt - 1 - t, 0))
    halo = pl.BlockSpec((None, 16, ft),
                        lambda b, t, j: (j, jnp.maximum((b * nt + nt - 1 - t) * (tl // 16) - 1, 0), 0))
    gain = pl.BlockSpec((1, d), lambda b, t, j: (0, 0))
    hid_shape = SDS((nf, bsz * l, ft), bf16)
    return pl.pallas_call(
        body, name=f"ffn_bwd_act_{li}", grid=(bsz, nt, nf),
        out_shape=(SDS((bsz, l, d), f32), hid_shape, hid_shape, hid_shape,
                   SDS((bsz, l, d), bf16), SDS((bsz, l, d), bf16), SDS((8, d), f32), SDS((nf, 8, ft), f32), *h_shape),
        in_specs=[tok, tok, tok, hid, hid, halo, gain, gain, _resident_spec(w_l.shape, 3),
                  pl.BlockSpec((None, 3, ft), lambda b, t, j: (li, 0, j)),
                  pl.BlockSpec((None, 1, ft), lambda b, t, j: (li, 0, j)), *h_in],
        out_specs=(tok, hid, hid, hid, tok, tok,
                   pl.BlockSpec((8, d), lambda b, t, j: (0, 0)), pl.BlockSpec((nf, 8, ft), lambda b, t, j: (0, 0, 0)),
                   *h_out),
        scratch_shapes=[pltpu.VMEM((tl, d), bf16), pltpu.VMEM((tl, d), bf16), pltpu.VMEM((tl, d), f32),
                        pltpu.VMEM((nf, 8, ft), f32), *h_scr],
        compiler_params=_cparams(3, side_effects=host is not None),
    )(x_mid, f_sv, dxo, g_sv, v_sv, g_sv, gain_pre, gain_post, w_l, conv_w,
      conv_b.reshape(conv_b.shape[0], 1, f), *h_ops)


def _ffn_bwd_weights(dg, dv, hdn, h, df, li):
    nf, t, ft = dg.shape
    f = nf * ft
    d = h.shape[1]
    tr = min(WG_TR, t)
    nr = t // tr

    def body(dg_ref, dv_ref, hdn_ref, h_ref, df_ref, o_ref, acc):
        r = pl.program_id(1)

        @pl.when(r == 0)
        def _():
            acc[...] = jnp.zeros_like(acc)

        rows = pl.ds(pl.multiple_of(r * tr, tr), tr)
        hb = h_ref[rows, :]
        acc[0] += _dot_tn(dg_ref[...], hb)
        acc[1] += _dot_tn(dv_ref[...], hb)
        acc[2] += _dot_tn(hdn_ref[...], df_ref[rows, :])

        @pl.when(r == nr - 1)
        def _():
            o_ref[...] = acc[...].astype(bf16)

    hid = pl.BlockSpec((None, tr, ft), lambda j, r: (j, r, 0))
    tok = _resident_spec((t, d), 2)
    return pl.pallas_call(
        body, name=f"ffn_bwd_weights_{li}", grid=(nf, nr), out_shape=SDS((3, f, d), bf16),
        in_specs=[hid, hid, hid, tok, tok], out_specs=pl.BlockSpec((3, ft, d), lambda j, r: (0, j, 0)),
        scratch_shapes=[pltpu.VMEM((3, ft, d), f32)], compiler_params=_cparams(2),
    )(dg, dv, hdn, h, df)


def _pool_counts(t0, tl, d):
    gch = d // len(POOL_WINDOWS)
    tpos = (t0 + lax.broadcasted_iota(jnp.int32, (tl, d), 0) + 1).astype(f32)
    lane = lax.broadcasted_iota(jnp.int32, (tl, d), 1)
    win = jnp.full((tl, d), float(POOL_WINDOWS[-1]), f32)
    for gi in range(len(POOL_WINDOWS) - 2, -1, -1):
        win = jnp.where(lane < (gi + 1) * gch, float(POOL_WINDOWS[gi]), win)
    return jnp.minimum(tpos, win)


def _pool_select(parts, tl, d):
    gch = d // len(POOL_WINDOWS)
    lane = lax.broadcasted_iota(jnp.int32, (tl, d), 1)
    out = parts[-1]
    for gi in range(len(parts) - 2, -1, -1):
        out = jnp.where(lane < (gi + 1) * gch, parts[gi], out)
    return out


def _pool_window_sums(u, halo, tl):
    ext = jnp.concatenate([halo, u], axis=0)
    sums, cur = [], ext
    for k in (1, 2, 4, 8):
        cur = cur + pltpu.roll(cur, k, axis=0)
        sums.append(cur[POOL_HALO:POOL_HALO + tl, :])
    return sums


def _pool_mix(u, halo, cnt, pw_ref, scale, tl, d):
    gch = d // len(POOL_WINDOWS)
    pooled = _pool_select(_pool_window_sums(u, halo, tl), tl, d) / cnt
    diff = pooled - u
    outs = [_dot(diff[:, gi * gch:(gi + 1) * gch].astype(bf16), pw_ref[gi]) for gi in range(len(POOL_WINDOWS))]
    return diff, jnp.concatenate(outs, axis=1)


def _pool_fwd(x, gain_pre, gain_post, pw, scale, li):
    bsz, l, d = x.shape
    tl = min(POOL_TL, l)
    nt = l // tl

    def body(x_ref, gpre_ref, gpost_ref, pw_ref, sc_ref, xo_ref, halo):
        t = pl.program_id(1)

        @pl.when(t == 0)
        def _():
            halo[...] = jnp.zeros_like(halo)

        xv = x_ref[...]
        u = _rms_fwd(xv, gpre_ref[...])
        _, out = _pool_mix(u, halo[...], _pool_counts(t * tl, tl, d), pw_ref, sc_ref[...], tl, d)
        halo[...] = u[tl - POOL_HALO:tl, :]
        xo_ref[...] = xv + _rms_fwd(out * sc_ref[...], gpost_ref[...])

    tok = pl.BlockSpec((None, tl, d), lambda b, t: (b, t, 0))
    gain = pl.BlockSpec((1, d), lambda b, t: (0, 0))
    return pl.pallas_call(
        body, name=f"pool_fwd_{li}", grid=(bsz, nt), out_shape=SDS((bsz, l, d), f32),
        in_specs=[tok, gain, gain, pl.BlockSpec(pw.shape, lambda b, t: (0, 0, 0)), gain], out_specs=tok,
        scratch_shapes=[pltpu.VMEM((POOL_HALO, d), f32)], compiler_params=_cparams(2),
    )(x, gain_pre, gain_post, pw, scale)


def _pool_bwd(x, dxo, gain_pre, gain_post, pw, scale, li):
    bsz, l, d = x.shape
    tl = min(POOL_TL, l)
    nt = l // tl
    ng = len(POOL_WINDOWS)
    gch = d // ng
    n_ext = tl + POOL_HALO

    def body(x_ref, xh_ref, dxo_ref, gpre_ref, gpost_ref, pw_ref, sc_ref, dx_ref, dpw_ref, ds_ref, qnext):
        b, t = pl.program_id(0), pl.program_id(1)
        tt = nt - 1 - t

        @pl.when((b == 0) & (t == 0))
        def _():
            dpw_ref[...] = jnp.zeros_like(dpw_ref)
            ds_ref[...] = jnp.zeros_like(ds_ref)

        @pl.when(t == 0)
        def _():
            qnext[...] = jnp.zeros_like(qnext)

        xv = x_ref[...]
        gpre = gpre_ref[...]
        u = _rms_fwd(xv, gpre)
        uh = _rms_fwd(jnp.where(tt > 0, xh_ref[...], 0.0), gpre)
        cnt = _pool_counts(tt * tl, tl, d)
        scale_v = sc_ref[...]
        diff, out = _pool_mix(u, uh, cnt, pw_ref, scale_v, tl, d)
        dxo_v = dxo_ref[...]
        dm, dgpost = _rms_bwd(out * scale_v, gpost_ref[...], dxo_v)
        ds_ref[1:2, :] += dgpost
        ds_ref[2:3, :] += jnp.sum(dm * out, axis=0, keepdims=True)
        dout = (dm * scale_v).astype(bf16)
        ddiffs = []
        for gi in range(ng):
            sl = slice(gi * gch, (gi + 1) * gch)
            dpw_ref[gi] += _dot_tn(diff[:, sl].astype(bf16), dout[:, sl])
            ddiffs.append(_dot_nt(dout[:, sl], pw_ref[gi]))
        ddiff = jnp.concatenate(ddiffs, axis=1)
        q = ddiff / cnt
        ext = jnp.concatenate([q, qnext[...]], axis=0)
        sums, cur = [], ext
        for k in (1, 2, 4, 8):
            cur = cur + pltpu.roll(cur, n_ext - k, axis=0)
            sums.append(cur[0:tl, :])
        du = _pool_select(sums, tl, d) - ddiff
        qnext[...] = q[0:POOL_HALO, :]
        dxp, dgpre = _rms_bwd(xv, gpre, du)
        ds_ref[0:1, :] += dgpre
        dx_ref[...] = dxo_v + dxp

    tok = pl.BlockSpec((None, tl, d), lambda b, t: (b, nt - 1 - t, 0))
    halo = pl.BlockSpec((None, POOL_HALO, d),
                        lambda b, t: (b, jnp.maximum((nt - 1 - t) * (tl // POOL_HALO) - 1, 0), 0))
    gain = pl.BlockSpec((1, d), lambda b, t: (0, 0))
    return pl.pallas_call(
        body, name=f"pool_bwd_{li}", grid=(bsz, nt),
        out_shape=(SDS((bsz, l, d), f32), SDS((ng, gch, gch), f32), SDS((8, d), f32)),
        in_specs=[tok, halo, tok, gain, gain, pl.BlockSpec(pw.shape, lambda b, t: (0, 0, 0)), gain],
        out_specs=(tok, pl.BlockSpec((ng, gch, gch), lambda b, t: (0, 0, 0)), pl.BlockSpec((8, d), lambda b, t: (0, 0))),
        scratch_shapes=[pltpu.VMEM((POOL_HALO, d), f32)], compiler_params=_cparams(2),
    )(x, x, dxo, gain_pre, gain_post, pw, scale)


def _s5_discretise(lam_re, lam_im, log_dt, b_re, b_im):
    def body(lr_ref, li_ref, ld_ref, br_ref, bi_ref, ar_ref, ai_ref, bbr_ref, bbi_ref):
        lr, li = lr_ref[...], li_ref[...]
        dt = jnp.exp(ld_ref[...])
        mag = jnp.exp(lr * dt)
        ar = mag * jnp.cos(li * dt)
        ai = mag * jnp.sin(li * dt)
        den = lr * lr + li * li
        nr, ni = ar - 1.0, ai
        fr = (nr * lr + ni * li) / den
        fi = (ni * lr - nr * li) / den
        br, bi = br_ref[...], bi_ref[...]
        ar_ref[...] = ar
        ai_ref[...] = ai
        bbr_ref[...] = fr * br - fi * bi
        bbi_ref[...] = fr * bi + fi * br

    return pl.pallas_call(
        body, name="s5_discretise",
        out_shape=(SDS(lam_re.shape, f32), SDS(lam_re.shape, f32), SDS(b_re.shape, f32), SDS(b_re.shape, f32)),
        compiler_params=_cparams(),
    )(lam_re, lam_im, log_dt, b_re, b_im)


def _s5_discretise_bwd(lam_re, lam_im, log_dt, b_re, b_im, g_ar, g_ai, g_bbr, g_bbi):
    def body(lr_ref, li_ref, ld_ref, br_ref, bi_ref, gar_ref, gai_ref, gbbr_ref, gbbi_ref,
             dlr_ref, dli_ref, dld_ref, dbr_ref, dbi_ref):
        lr, li = lr_ref[...], li_ref[...]
        dt = jnp.exp(ld_ref[...])
        mag = jnp.exp(lr * dt)
        cs, sn = jnp.cos(li * dt), jnp.sin(li * dt)
        ar, ai = mag * cs, mag * sn
        den = lr * lr + li * li
        nr, ni = ar - 1.0, ai
        fr = (nr * lr + ni * li) / den
        fi = (ni * lr - nr * li) / den
        br, bi = br_ref[...], bi_ref[...]
        gbbr, gbbi = gbbr_ref[...], gbbi_ref[...]
        dbr_ref[...] = fr * gbbr + fi * gbbi
        dbi_ref[...] = fr * gbbi - fi * gbbr
        gfr = jnp.sum(br * gbbr + bi * gbbi, axis=2, keepdims=True)
        gfi = jnp.sum(br * gbbi - bi * gbbr, axis=2, keepdims=True)
        gnr_num, gni_num = gfr / den, gfi / den
        gden = -(gfr * fr + gfi * fi) / den
        g_nr = gnr_num * lr - gni_num * li
        g_ni = gnr_num * li + gni_num * lr
        dlr = gnr_num * nr + gni_num * ni + gden * 2.0 * lr
        dli = gnr_num * ni - gni_num * nr + gden * 2.0 * li
        gar = gar_ref[...] + g_nr
        gai = gai_ref[...] + g_ni
        gq = (gar * cs + gai * sn) * mag
        gth = (gai * cs - gar * sn) * mag
        dlr_ref[...] = dlr + gq * dt
        dli_ref[...] = dli + gth * dt
        dld_ref[...] = jnp.sum(gq * lr + gth * li, axis=3, keepdims=True) * dt

    return pl.pallas_call(
        body, name="s5_discretise_bwd",
        out_shape=(SDS(lam_re.shape, f32), SDS(lam_re.shape, f32), SDS(log_dt.shape, f32),
                   SDS(b_re.shape, f32), SDS(b_re.shape, f32)),
        compiler_params=_cparams(),
    )(lam_re, lam_im, log_dt, b_re, b_im, g_ar, g_ai, g_bbr, g_bbi)


def _fill_powers(pw_r, pw_i, ar, ai, nj, width):
    cb = min(S5_CB, width)
    for c0 in range(0, width, cb):
        sl = pl.ds(c0, cb)
        a_r, a_i = ar[:, c0:c0 + cb], ai[:, c0:c0 + cb]

        def step(i, p):
            pw_r[i, :, sl] = p[0]
            pw_i[i, :, sl] = p[1]
            return _cmul(p[0], p[1], a_r, a_i)

        lax.fori_loop(0, nj, step, (a_r, a_i))


def _interleaved_scan(xr_sc, xi_sc, ar, ai, pw_r, pw_i, carry_r, carry_i, nj, width, reverse, h_sc=None):
    cb = min(S5_CB, width)
    acc_out = []
    for c0 in range(0, width, cb):
        sl = pl.ds(c0, cb)
        a_r, a_i = ar[:, c0:c0 + cb], ai[:, c0:c0 + cb]
        aj_r, aj_i = pw_r[nj - 1, :, sl], pw_i[nj - 1, :, sl]

        def pos(i):
            return nj - 1 - i if reverse else i

        def local_step(i, st):
            j = pos(i)
            hr, hi = _cmul(a_r, a_i, st[0], st[1])
            hr, hi = hr + xr_sc[j, :, sl], hi + xi_sc[j, :, sl]
            xr_sc[j, :, sl] = hr
            xi_sc[j, :, sl] = hi
            return hr, hi

        zero = jnp.zeros((8, cb), f32)
        fin_r, fin_i = lax.fori_loop(0, nj, local_step, (zero, zero))
        row = lax.broadcasted_iota(jnp.int32, (8, cb), 0)
        c_r, c_i = carry_r[0:1, sl], carry_i[0:1, sl]
        ent_r, ent_i = zero, zero
        order = range(7, -1, -1) if reverse else range(8)
        for s in order:
            ent_r = jnp.where(row == s, c_r, ent_r)
            ent_i = jnp.where(row == s, c_i, ent_i)
            pr, pi_ = _cmul(aj_r[0:1, :], aj_i[0:1, :], c_r, c_i)
            c_r, c_i = fin_r[s:s + 1, :] + pr, fin_i[s:s + 1, :] + pi_
        carry_r[:, sl] = jnp.broadcast_to(c_r, (8, cb))
        carry_i[:, sl] = jnp.broadcast_to(c_i, (8, cb))

        def fix_step(i, st):
            j = pos(i)
            nx_r, nx_i, acc_r, acc_i = st
            cr_, ci_ = _cmul(pw_r[i, :, sl], pw_i[i, :, sl], ent_r, ent_i)
            hr, hi = xr_sc[j, :, sl] + cr_, xi_sc[j, :, sl] + ci_
            xr_sc[j, :, sl] = hr
            xi_sc[j, :, sl] = hi
            if h_sc is not None:
                sr, si = h_sc[0][j, :, sl], h_sc[1][j, :, sl]
                acc_r = acc_r + nx_r * sr + nx_i * si
                acc_i = acc_i + nx_i * sr - nx_r * si
                nx_r, nx_i = hr, hi
            return nx_r, nx_i, acc_r, acc_i

        st = lax.fori_loop(0, nj, fix_step, (ent_r, ent_i, zero, zero))
        acc_out.append((st[2], st[3]))
    return acc_out


def _diag_mask(n_rep, h, ks):
    assert h & (h - 1) == 0 and (ks // n_rep) & (ks // n_rep - 1) == 0
    r = lax.shift_right_logical(lax.broadcasted_iota(jnp.int32, (n_rep * h, ks), 0), h.bit_length() - 1)
    c = lax.shift_right_logical(lax.broadcasted_iota(jnp.int32, (n_rep * h, ks), 1), (ks // n_rep).bit_length() - 1)
    return r == c


def _expand_block_diag(compact, n_rep):
    h, ks = compact.shape
    full = jnp.concatenate([compact] * n_rep, axis=0)
    return jnp.where(_diag_mask(n_rep, h, ks), full, 0.0).astype(bf16)


def _compact_block_diag(full, n_rep):
    rows, ks = full.shape
    h = rows // n_rep
    return jnp.sum(jnp.where(_diag_mask(n_rep, h, ks), full, 0.0).reshape(n_rep, h, ks), axis=0)


def _s5_scan_fwd(x_il, gain_pre, d_skip, ab_r, ab_i, bbc_r, bbc_i, cc_r, cc_in, li, host=None):
    bsz, nblk, tq, d = x_il.shape
    nkb, gch, ks = bbc_r.shape
    kc = d // nkb
    n_rep = kc // gch
    nj = tq // 8
    h_in, h_shape, h_out, h_scr, h_ops = _host_args(host)

    def body(xf_ref, xk_ref, gk_ref, dk_ref, ar_ref, ai_ref, bbr_ref, bbi_ref, cr_ref, ci_ref, *rest):
        o0 = len(h_in)
        y_ref, hr_ref, hi_ref = rest[o0:o0 + 3]
        s0 = o0 + 3 + len(h_out)
        xr_sc, xi_sc, pw_r, pw_i, carry_r, carry_i, bbr_sc, bbi_sc, crt_sc, cit_sc = rest[s0:s0 + 10]
        k, b, n = pl.program_id(0), pl.program_id(1), pl.program_id(2)
        finish = _hosted(host, rest[0], rest[o0 + 3], rest[s0 + 10:],
                         (k == 0) & (b == 0) & (n == 0), (k == nkb - 1) & (b == bsz - 1) & (n == nblk - 1))
        ar = jnp.broadcast_to(ar_ref[...], (8, ks))
        ai = jnp.broadcast_to(ai_ref[...], (8, ks))

        @pl.when((b == 0) & (n == 0))
        def _():
            bbr_sc[...] = _expand_block_diag(bbr_ref[...], n_rep)
            bbi_sc[...] = _expand_block_diag(bbi_ref[...], n_rep)
            crt_sc[...] = _expand_block_diag(cr_ref[...], n_rep)
            cit_sc[...] = _expand_block_diag(ci_ref[...], n_rep)
            _fill_powers(pw_r, pw_i, ar, ai, nj, ks)

        @pl.when(n == 0)
        def _():
            carry_r[...] = jnp.zeros_like(carry_r)
            carry_i[...] = jnp.zeros_like(carry_i)

        u = xk_ref[...] * _rms_scale(xf_ref[...]) * gk_ref[...]
        ub = u.astype(bf16)
        xr_sc[...] = _dot(ub, bbr_sc[...]).reshape(nj, 8, ks)
        xi_sc[...] = _dot(ub, bbi_sc[...]).reshape(nj, 8, ks)
        _interleaved_scan(xr_sc, xi_sc, ar, ai, pw_r, pw_i, carry_r, carry_i, nj, ks, reverse=False)
        hrb = xr_sc[...].reshape(tq, ks).astype(bf16)
        hib = xi_sc[...].reshape(tq, ks).astype(bf16)
        hr_ref[...] = hrb
        hi_ref[...] = hib
        y_ref[...] = _dot_nt(hrb, crt_sc[...]) + _dot_nt(hib, cit_sc[...]) + dk_ref[...] * u
        finish()

    full = pl.BlockSpec((None, None, tq, d), lambda k, b, n: (b, n, 0, 0))
    chan = pl.BlockSpec((None, None, tq, kc), lambda k, b, n: (b, n, 0, k))
    stat = pl.BlockSpec((None, None, tq, ks), lambda k, b, n: (b, n, 0, k))
    vec_c = pl.BlockSpec((1, kc), lambda k, b, n: (0, k))
    vec_s = pl.BlockSpec((1, ks), lambda k, b, n: (0, k))
    cmap = pl.BlockSpec((None, gch, ks), lambda k, b, n: (k, 0, 0))
    s_tot = nkb * ks
    return pl.pallas_call(
        body, name=f"s5_scan_fwd_{li}", grid=(nkb, bsz, nblk),
        out_shape=(SDS((bsz, nblk, tq, d), f32), SDS((bsz, nblk, tq, s_tot), bf16), SDS((bsz, nblk, tq, s_tot), bf16),
                   *h_shape),
        in_specs=[full, chan, vec_c, vec_c, vec_s, vec_s, cmap, cmap, cmap, cmap, *h_in],
        out_specs=(chan, stat, stat, *h_out),
        scratch_shapes=[pltpu.VMEM((nj, 8, ks), f32)] * 4 + [pltpu.VMEM((8, ks), f32)] * 2
        + [pltpu.VMEM((kc, ks), bf16)] * 4 + h_scr,
        compiler_params=_cparams(3, side_effects=host is not None),
    )(x_il, x_il, gain_pre, d_skip, ab_r, ab_i, bbc_r, bbc_i, cc_r, cc_in, *h_ops)


def _s5_glu_fwd(y, x_il, w_glu, b_glu, gain_post, li):
    t, d = y.shape
    tr = min(TOK_TR, t)

    def body(y_ref, x_ref, w_ref, b_ref, gp_ref, xo_ref):
        z = _gelu(y_ref[...])
        a = _dot(z.astype(bf16), w_ref[...]) + b_ref[...]
        xo_ref[...] = x_ref[...] + _rms_fwd(z * _sigmoid(a), gp_ref[...])

    tok = pl.BlockSpec((tr, d), lambda i: (i, 0))
    vec = pl.BlockSpec((1, d), lambda i: (0, 0))
    return pl.pallas_call(
        body, name=f"s5_glu_fwd_{li}", grid=(t // tr,), out_shape=SDS((t, d), f32),
        in_specs=[tok, tok, pl.BlockSpec((d, d), lambda i: (0, 0)), vec, vec], out_specs=tok,
        compiler_params=_cparams(1),
    )(y, x_il, w_glu, b_glu, gain_post)


def _s5_glu_bwd(y, dxo, w_glu, b_glu, gain_post, li):
    t, d = y.shape
    tr = min(TOK_TR, t)

    def body(y_ref, dxo_ref, w_ref, b_ref, gp_ref, dy_ref, dw_ref, ds_ref):
        @pl.when(pl.program_id(0) == 0)
        def _():
            dw_ref[...] = jnp.zeros_like(dw_ref)
            ds_ref[...] = jnp.zeros_like(ds_ref)

        z, zg = _gelu_and_grad(y_ref[...])
        zb = z.astype(bf16)
        w = w_ref[...]
        s = _sigmoid(_dot(zb, w) + b_ref[...])
        dm, dgpost = _rms_bwd(z * s, gp_ref[...], dxo_ref[...])
        da = dm * z * s * (1.0 - s)
        dab = da.astype(bf16)
        dz = dm * s + _dot_nt(dab, w)
        dw_ref[...] += _dot_tn(zb, dab)
        ds_ref[0:1, :] += dgpost
        ds_ref[1:2, :] += jnp.sum(da, axis=0, keepdims=True)
        dy_ref[...] = dz * zg

    tok = pl.BlockSpec((tr, d), lambda i: (i, 0))
    vec = pl.BlockSpec((1, d), lambda i: (0, 0))
    mat = pl.BlockSpec((d, d), lambda i: (0, 0))
    return pl.pallas_call(
        body, name=f"s5_glu_bwd_{li}", grid=(t // tr,),
        out_shape=(SDS((t, d), f32), SDS((d, d), f32), SDS((8, d), f32)),
        in_specs=[tok, tok, mat, vec, vec], out_specs=(tok, mat, pl.BlockSpec((8, d), lambda i: (0, 0))),
        compiler_params=_cparams(1),
    )(y, dxo, w_glu, b_glu, gain_post)


def _s5_scan_bwd(x_il, dy, h_r, h_i, gain_pre, d_skip, ab_r, ab_i, bbc_r, bbc_i, cc_r, cc_in, li, host=None):
    bsz, nblk, tq, d = x_il.shape
    nkb, gch, ks = bbc_r.shape
    kc = d // nkb
    n_rep = kc // gch
    nj = tq // 8
    h_in, h_shape, h_out, h_scr, h_ops = _host_args(host)

    def body(xf_ref, xk_ref, dy_ref, hr_ref, hi_ref, gk_ref, dk_ref, ar_ref, ai_ref, bbr_ref, bbi_ref, cr_ref, ci_ref,
             *rest):
        o0 = len(h_in)
        du_ref, dbbr_ref, dbbi_ref, dcr_ref, dci_ref, dar_ref, dai_ref, dd_ref = rest[o0:o0 + 8]
        s0 = o0 + 8 + len(h_out)
        (gr_sc, gi_sc, hr_sc, hi_sc, pw_r, pw_i, carry_r, carry_i, acc_r, acc_i,
         bbr_sc, bbi_sc, crt_sc, cit_sc, dbbr_acc, dbbi_acc, dcr_acc, dci_acc) = rest[s0:s0 + 18]
        k, b, n = pl.program_id(0), pl.program_id(1), pl.program_id(2)
        finish = _hosted(host, rest[0], rest[o0 + 8], rest[s0 + 18:],
                         (k == 0) & (b == 0) & (n == 0), (k == nkb - 1) & (b == bsz - 1) & (n == nblk - 1))
        ar = jnp.broadcast_to(ar_ref[...], (8, ks))
        ai = jnp.broadcast_to(-ai_ref[...], (8, ks))

        @pl.when((b == 0) & (n == 0))
        def _():
            bbr_sc[...] = _expand_block_diag(bbr_ref[...], n_rep)
            bbi_sc[...] = _expand_block_diag(bbi_ref[...], n_rep)
            crt_sc[...] = _expand_block_diag(cr_ref[...], n_rep)
            cit_sc[...] = _expand_block_diag(ci_ref[...], n_rep)
            _fill_powers(pw_r, pw_i, ar, ai, nj, ks)
            for ref in (dar_ref, dai_ref, dd_ref, acc_r, acc_i, dbbr_acc, dbbi_acc, dcr_acc, dci_acc):
                ref[...] = jnp.zeros_like(ref)

        @pl.when(n == 0)
        def _():
            carry_r[...] = jnp.zeros_like(carry_r)
            carry_i[...] = jnp.zeros_like(carry_i)

        u = xk_ref[...] * _rms_scale(xf_ref[...]) * gk_ref[...]
        ub = u.astype(bf16)
        dyv = dy_ref[...]
        dyb = dyv.astype(bf16)
        dd_ref[0:1, :] += jnp.sum(dyv * u, axis=0, keepdims=True)
        hrb, hib = hr_ref[...], hi_ref[...]
        dcr_acc[...] += _dot_tn(dyb, hrb)
        dci_acc[...] += _dot_tn(dyb, hib)
        hr_sc[...] = hrb.astype(f32).reshape(nj, 8, ks)
        hi_sc[...] = hib.astype(f32).reshape(nj, 8, ks)
        gr_sc[...] = _dot(dyb, crt_sc[...]).reshape(nj, 8, ks)
        gi_sc[...] = _dot(dyb, cit_sc[...]).reshape(nj, 8, ks)
        accs = _interleaved_scan(gr_sc, gi_sc, ar, ai, pw_r, pw_i, carry_r, carry_i, nj, ks, reverse=True,
                                 h_sc=(hr_sc, hi_sc))
        cb = min(S5_CB, ks)
        for q, (a_r, a_i) in enumerate(accs):
            acc_r[:, q * cb:(q + 1) * cb] += a_r
            acc_i[:, q * cb:(q + 1) * cb] += a_i
        grb = gr_sc[...].reshape(tq, ks).astype(bf16)
        gib = gi_sc[...].reshape(tq, ks).astype(bf16)
        dbbr_acc[...] += _dot_tn(ub, grb)
        dbbi_acc[...] += _dot_tn(ub, gib)
        du_ref[...] = dyv * dk_ref[...] + _dot_nt(grb, bbr_sc[...]) + _dot_nt(gib, bbi_sc[...])

        @pl.when((b == bsz - 1) & (n == nblk - 1))
        def _():
            dar_ref[0:1, :] = jnp.sum(acc_r[...], axis=0, keepdims=True)
            dai_ref[0:1, :] = jnp.sum(acc_i[...], axis=0, keepdims=True)
            dbbr_ref[...] = _compact_block_diag(dbbr_acc[...], n_rep)
            dbbi_ref[...] = _compact_block_diag(dbbi_acc[...], n_rep)
            dcr_ref[...] = _compact_block_diag(dcr_acc[...], n_rep)
            dci_ref[...] = _compact_block_diag(dci_acc[...], n_rep)

        finish()

    full = pl.BlockSpec((None, None, tq, d), lambda k, b, n: (b, nblk - 1 - n, 0, 0))
    chan = pl.BlockSpec((None, None, tq, kc), lambda k, b, n: (b, nblk - 1 - n, 0, k))
    stat = pl.BlockSpec((None, None, tq, ks), lambda k, b, n: (b, nblk - 1 - n, 0, k))
    vec_c = pl.BlockSpec((1, kc), lambda k, b, n: (0, k))
    vec_s = pl.BlockSpec((1, ks), lambda k, b, n: (0, k))
    cmap = pl.BlockSpec((None, gch, ks), lambda k, b, n: (k, 0, 0))
    acc_s = pl.BlockSpec((None, 8, ks), lambda k, b, n: (k, 0, 0))
    acc_c = pl.BlockSpec((None, 8, kc), lambda k, b, n: (k, 0, 0))
    cshape = SDS((nkb, gch, ks), f32)
    return pl.pallas_call(
        body, name=f"s5_scan_bwd_{li}", grid=(nkb, bsz, nblk),
        out_shape=(SDS((bsz, nblk, tq, d), f32), cshape, cshape, cshape, cshape, SDS((nkb, 8, ks), f32),
                   SDS((nkb, 8, ks), f32), SDS((nkb, 8, kc), f32), *h_shape),
        in_specs=[full, chan, chan, stat, stat, vec_c, vec_c, vec_s, vec_s, cmap, cmap, cmap, cmap, *h_in],
        out_specs=(chan, cmap, cmap, cmap, cmap, acc_s, acc_s, acc_c, *h_out),
        scratch_shapes=([pltpu.VMEM((nj, 8, ks), f32)] * 6 + [pltpu.VMEM((8, ks), f32)] * 4
                        + [pltpu.VMEM((kc, ks), bf16)] * 4 + [pltpu.VMEM((kc, ks), f32)] * 4 + h_scr),
        compiler_params=_cparams(3, side_effects=host is not None),
    )(x_il, x_il, dy, h_r, h_i, gain_pre, d_skip, ab_r, ab_i, bbc_r, bbc_i, cc_r, cc_in, *h_ops)


def _norm_residual_bwd(x, du, dxo, gain, name):
    t, d = x.shape
    tr = min(TOK_TR, t)

    def body(x_ref, du_ref, dxo_ref, g_ref, dx_ref, dg_ref):
        @pl.when(pl.program_id(0) == 0)
        def _():
            dg_ref[...] = jnp.zeros_like(dg_ref)

        dxp, dgain = _rms_bwd(x_ref[...], g_ref[...], du_ref[...])
        dx_ref[...] = dxo_ref[...] + dxp
        dg_ref[0:1, :] += dgain

    tok = pl.BlockSpec((tr, d), lambda i: (i, 0))
    return pl.pallas_call(
        body, name=name, grid=(t // tr,), out_shape=(SDS((t, d), f32), SDS((8, d), f32)),
        in_specs=[tok, tok, tok, pl.BlockSpec((1, d), lambda i: (0, 0))],
        out_specs=(tok, pl.BlockSpec((8, d), lambda i: (0, 0))), compiler_params=_cparams(1),
    )(x, du, dxo, gain)


def _loss_head(y, target):
    t, d = y.shape
    tr = min(TOK_TR, t)

    def body(y_ref, t_ref, l_ref, dy_ref):
        @pl.when(pl.program_id(0) == 0)
        def _():
            l_ref[...] = jnp.zeros_like(l_ref)

        err = y_ref[...] - t_ref[...]
        dy_ref[...] = err * (1.0 / d)
        l_ref[...] += jnp.sum(jnp.sum(err * err, axis=1, keepdims=True), axis=0, keepdims=True)

    tok = pl.BlockSpec((tr, d), lambda i: (i, 0))
    return pl.pallas_call(
        body, name="loss_head", grid=(t // tr,), out_shape=(SDS((8, 128), f32), SDS((t, d), f32)),
        in_specs=[tok, tok], out_specs=(pl.BlockSpec((8, 128), lambda i: (0, 0)), tok), compiler_params=_cparams(1),
    )(y, target)


def _interleave(a, tq):
    bsz, l, d = a.shape
    return a.reshape(bsz, l // tq, 8, tq // 8, d).transpose(0, 1, 3, 2, 4).reshape(bsz, l // tq, tq, d)


def _deinterleave(a, l):
    bsz, nblk, tq, d = a.shape
    return a.reshape(bsz, nblk, tq // 8, 8, d).transpose(0, 1, 3, 2, 4).reshape(bsz, l, d)


def _compact_maps(a, nkb):
    ns, g, h, p = a.shape
    gl = g // nkb
    return a.reshape(ns, nkb, gl, h, p).transpose(0, 1, 3, 2, 4).reshape(ns, nkb, h, gl * p)


def _uncompact_maps(a, g):
    nkb, h, cols = a.shape
    gl = g // nkb
    return a.reshape(nkb, h, gl, cols // gl).transpose(0, 2, 1, 3).reshape(g, h, cols // gl)


def _pack_rows(arrs, cols, row_mult):
    flat = jnp.concatenate([a.reshape(-1).astype(f32) for a in arrs])
    rows = -(-flat.shape[0] // cols)
    rows = -(-rows // row_mult) * row_mult
    return jnp.pad(flat, (0, rows * cols - flat.shape[0])).reshape(rows, cols)


def _unpack_rows(buf, shapes):
    flat = buf.reshape(-1)
    out, off = [], 0
    for s in shapes:
        n = math.prod(s)
        out.append(flat[off:off + n].reshape(s))
        off += n
    return out


W_NAMES = ['s5_lambda_re', 's5_lambda_im', 's5_log_dt', 's5_b_re', 's5_b_im', 's5_c_re', 's5_c_im', 's5_d', 's5_w_glu',
           's5_b_glu', 'pool_w', 'pool_scale', 'ffn_w_gate', 'ffn_w_val', 'ffn_conv_w', 'ffn_conv_b', 'ffn_w_down',
           'norm_mix_pre', 'norm_mix_post', 'norm_ffn_pre', 'norm_ffn_post']
BIG = ('s5_w_glu', 'ffn_w_gate', 'ffn_w_val', 'ffn_w_down')


def kernel(x, s5_lambda_re, s5_lambda_im, s5_log_dt, s5_b_re, s5_b_im, s5_c_re, s5_c_im, s5_d, s5_w_glu, s5_b_glu, pool_w, pool_scale, ffn_w_gate, ffn_w_val, ffn_conv_w, ffn_conv_b, ffn_w_down, norm_mix_pre, norm_mix_post, norm_ffn_pre, norm_ffn_post, loss_target, m_s5_lambda_re, m_s5_lambda_im, m_s5_log_dt, m_s5_b_re, m_s5_b_im, m_s5_c_re, m_s5_c_im, m_s5_d, m_s5_w_glu, m_s5_b_glu, m_pool_w, m_pool_scale, m_ffn_w_gate, m_ffn_w_val, m_ffn_conv_w, m_ffn_conv_b, m_ffn_w_down, m_norm_mix_pre, m_norm_mix_post, m_norm_ffn_pre, m_norm_ffn_post, v_s5_lambda_re, v_s5_lambda_im, v_s5_log_dt, v_s5_b_re, v_s5_b_im, v_s5_c_re, v_s5_c_im, v_s5_d, v_s5_w_glu, v_s5_b_glu, v_pool_w, v_pool_scale, v_ffn_w_gate, v_ffn_w_val, v_ffn_conv_w, v_ffn_conv_b, v_ffn_w_down, v_norm_mix_pre, v_norm_mix_post, v_norm_ffn_pre, v_norm_ffn_post):
    w_in = dict(zip(W_NAMES, (s5_lambda_re, s5_lambda_im, s5_log_dt, s5_b_re, s5_b_im, s5_c_re, s5_c_im, s5_d, s5_w_glu,
                              s5_b_glu, pool_w, pool_scale, ffn_w_gate, ffn_w_val, ffn_conv_w, ffn_conv_b, ffn_w_down,
                              norm_mix_pre, norm_mix_post, norm_ffn_pre, norm_ffn_post)))
    m_in = dict(zip(W_NAMES, (m_s5_lambda_re, m_s5_lambda_im, m_s5_log_dt, m_s5_b_re, m_s5_b_im, m_s5_c_re, m_s5_c_im,
                              m_s5_d, m_s5_w_glu, m_s5_b_glu, m_pool_w, m_pool_scale, m_ffn_w_gate, m_ffn_w_val,
                              m_ffn_conv_w, m_ffn_conv_b, m_ffn_w_down, m_norm_mix_pre, m_norm_mix_post,
                              m_norm_ffn_pre, m_norm_ffn_post)))
    v_in = dict(zip(W_NAMES, (v_s5_lambda_re, v_s5_lambda_im, v_s5_log_dt, v_s5_b_re, v_s5_b_im, v_s5_c_re, v_s5_c_im,
                              v_s5_d, v_s5_w_glu, v_s5_b_glu, v_pool_w, v_pool_scale, v_ffn_w_gate, v_ffn_w_val,
                              v_ffn_conv_w, v_ffn_conv_b, v_ffn_w_down, v_norm_mix_pre, v_norm_mix_post,
                              v_norm_ffn_pre, v_norm_ffn_post)))

    bsz, seq, d = x.shape
    depth = ffn_w_gate.shape[0]
    n_s5, n_grp, n_state = s5_lambda_re.shape
    n_gch = s5_b_re.shape[3]
    n_pool = pool_w.shape[0]
    fs = ffn_w_gate.shape[2]
    f = N_CHIPS * fs
    gs = s5_w_glu.shape[1]
    nkb = d // S5_KB_CH
    tq = min(S5_TQ, seq)
    chip = 2 * lax.axis_index("x") + lax.axis_index("y")

    s_ffn = jnp.stack([ffn_w_gate.transpose(0, 2, 1), ffn_w_val.transpose(0, 2, 1), ffn_w_down], axis=1).astype(bf16)
    small_shard = [pool_w, pool_scale, ffn_conv_w]
    g_glu, g_small = _chip_exchange([("gather", 1, s5_w_glu.astype(bf16)), ("whole", 0, _pack_rows(small_shard, d, 16))],
                                    "allgather_small")
    parts = [_unpack_rows(g_small[k], [a.shape for a in small_shard]) for k in range(N_CHIPS)]
    pool_w_full = jnp.concatenate([p[0] for p in parts], axis=2).astype(bf16)
    pool_scale_full = jnp.concatenate([p[1] for p in parts], axis=1)
    conv_w_full = jnp.concatenate([p[2] for p in parts], axis=2)

    lam_r4 = s5_lambda_re.reshape(n_s5, n_grp, 1, n_state)
    lam_i4 = s5_lambda_im.reshape(n_s5, n_grp, 1, n_state)
    ldt4 = s5_log_dt.reshape(n_s5, n_grp, 1, 1)
    b_r4 = s5_b_re.transpose(0, 1, 3, 2)
    b_i4 = s5_b_im.transpose(0, 1, 3, 2)
    ab_r4, ab_i4, bb_r4, bb_i4 = _s5_discretise(lam_r4, lam_i4, ldt4, b_r4, b_i4)
    n_st_tot = n_grp * n_state
    ab_r, ab_i = ab_r4.reshape(n_s5, 1, n_st_tot), ab_i4.reshape(n_s5, 1, n_st_tot)
    bbc_r, bbc_i = _compact_maps(bb_r4, nkb), _compact_maps(bb_i4, nkb)
    cc_r, cc_in = _compact_maps(s5_c_re, nkb), _compact_maps(-s5_c_im, nkb)

    def s5_params(j):
        return dict(ab_r=ab_r[j], ab_i=ab_i[j], bb_r=bbc_r[j], bb_i=bbc_i[j], c_r=cc_r[j], c_in=cc_in[j],
                    d_skip=s5_d[j].reshape(1, d), w_glu=g_glu[j], b_glu=s5_b_glu[j].reshape(1, d))

    def row(a, i):
        return a[i].reshape(1, -1)

    saved = []
    xc = x
    w_layer = [None] * depth
    for i in range(depth):
        j = i // 2
        sv = dict(x_in=xc)
        if i % 2 == 0:
            sp = s5_params(j)
            x_il = _interleave(xc, tq)
            res = _s5_scan_fwd(x_il, row(norm_mix_pre, i), sp["d_skip"], sp["ab_r"], sp["ab_i"], sp["bb_r"], sp["bb_i"],
                               sp["c_r"], sp["c_in"], i, host=("gather", 1, s_ffn[0]) if i == 0 else None)
            y_il, h_r, h_i = res[:3]
            if i == 0:
                w_layer[0] = res[3]
            xo_il = _s5_glu_fwd(y_il.reshape(bsz * seq, d), x_il.reshape(bsz * seq, d), sp["w_glu"], sp["b_glu"],
                                row(norm_mix_post, i), i)
            x_mid = _deinterleave(xo_il.reshape(x_il.shape), seq)
            sv.update(sp=sp, x_il=x_il, y_il=y_il, h_r=h_r, h_i=h_i)
        else:
            x_mid = _pool_fwd(xc, row(norm_mix_pre, i), row(norm_mix_post, i), pool_w_full[j], row(pool_scale_full, j), i)
        res = _ffn_fwd(x_mid, row(norm_ffn_pre, i), row(norm_ffn_post, i), w_layer[i], conv_w_full, ffn_conv_b, i,
                       host=("gather", 1, s_ffn[i + 1]) if i + 1 < depth else None)
        xc, g_sv, v_sv, f_sv = res[:4]
        if i + 1 < depth:
            w_layer[i + 1] = res[4]
        sv.update(x_mid=x_mid, g=g_sv, v=v_sv, f=f_sv)
        saved.append(sv)

    sq, dy = _loss_head(xc.reshape(bsz * seq, d), loss_target.reshape(bsz * seq, d))
    loss = lax.psum(sq[0, 0] * (0.5 / d), ("x", "y", "c"))

    dx = dy.reshape(bsz, seq, d)
    dw_layers = [None] * depth
    r_layers = [None] * depth
    g_small_params = {n: [None] * w_in[n].shape[0] for n in W_NAMES if n not in BIG}
    dglu = [None] * n_s5
    for i in range(depth - 1, -1, -1):
        j = i // 2
        sv = saved[i]
        res = _ffn_bwd_act(sv["x_mid"], sv["f"], sv["g"], sv["v"], dx, row(norm_ffn_pre, i), row(norm_ffn_post, i),
                           w_layer[i], conv_w_full, ffn_conv_b, i,
                           host=("slab", 1, dw_layers[i + 1]) if i + 1 < depth else None)
        dx, dg, dv, hdn, hb, dfb, dgain, dconv = res[:8]
        if i + 1 < depth:
            r_layers[i + 1] = res[8]
        dw_layers[i] = _ffn_bwd_weights(dg, dv, hdn, hb.reshape(bsz * seq, d), dfb.reshape(bsz * seq, d), i)
        g_small_params["norm_ffn_pre"][i] = dgain[0]
        g_small_params["norm_ffn_post"][i] = dgain[1]
        dconv = dconv.transpose(1, 0, 2).reshape(8, f)
        g_small_params["ffn_conv_w"][i] = dconv[0:3]
        g_small_params["ffn_conv_b"][i] = dconv[3]
        if i % 2 == 0:
            sp = sv["sp"]
            dxo_il = _interleave(dx, tq)
            dy_s, dglu[j], ds_glu = _s5_glu_bwd(sv["y_il"].reshape(bsz * seq, d), dxo_il.reshape(bsz * seq, d),
                                                sp["w_glu"], sp["b_glu"], row(norm_mix_post, i), i)
            res = _s5_scan_bwd(sv["x_il"], dy_s.reshape(sv["x_il"].shape), sv["h_r"], sv["h_i"], row(norm_mix_pre, i),
                               sp["d_skip"], sp["ab_r"], sp["ab_i"], sp["bb_r"], sp["bb_i"], sp["c_r"], sp["c_in"], i,
                               host=("slab", 1, dw_layers[0]) if i == 0 else None)
            du, dbb_r, dbb_i, dc_r, dc_in, dab_r, dab_i, dd = res[:8]
            if i == 0:
                r_layers[0] = res[8]
            dx_il, dgpre = _norm_residual_bwd(sv["x_il"].reshape(bsz * seq, d), du.reshape(bsz * seq, d),
                                              dxo_il.reshape(bsz * seq, d), row(norm_mix_pre, i), f"s5_pre_bwd_{i}")
            dx = _deinterleave(dx_il.reshape(sv["x_il"].shape), seq)
            g_small_params["norm_mix_pre"][i] = dgpre[0]
            g_small_params["norm_mix_post"][i] = ds_glu[0]
            g_small_params["s5_b_glu"][j] = ds_glu[1]
            g_small_params["s5_d"][j] = dd[:, 0, :].reshape(d)
            g_small_params["s5_c_re"][j] = _uncompact_maps(dc_r, n_grp)
            g_small_params["s5_c_im"][j] = -_uncompact_maps(dc_in, n_grp)
            sv["g_ab"] = (dab_r[:, 0, :].reshape(n_grp, 1, n_state), dab_i[:, 0, :].reshape(n_grp, 1, n_state),
                          _uncompact_maps(dbb_r, n_grp), _uncompact_maps(dbb_i, n_grp))
        else:
            dx, dpw, dsm = _pool_bwd(sv["x_in"], dx, row(norm_mix_pre, i), row(norm_mix_post, i), pool_w_full[j],
                                     row(pool_scale_full, j), i)
            g_small_params["norm_mix_pre"][i] = dsm[0]
            g_small_params["norm_mix_post"][i] = dsm[1]
            g_small_params["pool_scale"][j] = dsm[2]
            g_small_params["pool_w"][j] = dpw
    grad_x = dx

    g_ab = [saved[2 * j]["g_ab"] for j in range(n_s5)]
    d_lr, d_li, d_ld, d_br, d_bi = _s5_discretise_bwd(
        lam_r4, lam_i4, ldt4, b_r4, b_i4, jnp.stack([g[0] for g in g_ab]), jnp.stack([g[1] for g in g_ab]),
        jnp.stack([g[2] for g in g_ab]), jnp.stack([g[3] for g in g_ab]))
    small_full = {n: (jnp.stack(v) if v[0] is not None else None) for n, v in g_small_params.items()}
    small_full["s5_lambda_re"] = d_lr.reshape(n_s5, n_grp, n_state)
    small_full["s5_lambda_im"] = d_li.reshape(n_s5, n_grp, n_state)
    small_full["s5_log_dt"] = d_ld.reshape(n_s5, n_grp)
    small_full["s5_b_re"] = d_br.transpose(0, 1, 3, 2)
    small_full["s5_b_im"] = d_bi.transpose(0, 1, 3, 2)
    bc_names = ["s5_b_re", "s5_b_im", "s5_c_re", "s5_c_im"]
    small_names = [n for n in W_NAMES if n not in BIG and n not in bc_names and n != "pool_w"]
    q = _pack_rows([small_full[n] for n in small_names], d, 16)
    q_bc = _pack_rows([small_full[n] for n in bc_names], d, 16).astype(bf16)

    r_glu, r_pw, r_q, r_bc = _chip_exchange(
        [("slab", 1, jnp.stack(dglu).astype(bf16)), ("slab", 2, small_full["pool_w"]), ("whole", 0, q), ("whole", 0, q_bc)],
        "exchange_small_grads")
    p_ffn = _sum_slots_layers([r.reshape(N_CHIPS, 3 * fs, d) for r in r_layers], "sum_slots_ffn").reshape(depth * 3 * fs, d)
    p_glu = _sum_slots(r_glu.reshape(N_CHIPS, n_s5 * gs, d), "sum_slots_glu")
    pw_rows, pw_cols = math.prod(pool_w.shape[:3]), pool_w.shape[3]
    p_pw = _sum_slots(r_pw.reshape(N_CHIPS, pw_rows, pw_cols), "sum_slots_pool_w")
    p_q = _sum_slots(r_q, "sum_slots_small")
    p_bc = _sum_slots(r_bc, "sum_slots_bc")
    o_ffn, o_glu, o_pw, o_q, o_bc = _exchange_with_sibling([p_ffn, p_glu, p_pw, p_q, p_bc])

    outs = {}

    def put(name, res, shape):
        outs[name] = tuple(r.reshape(shape) for r in res)

    pf, of = p_ffn.reshape(depth, 3, fs, d), o_ffn.reshape(depth, 3, fs, d)
    for kind, name in ((0, "ffn_w_gate"), (1, "ffn_w_val")):
        parts_g = [a[:, kind].transpose(0, 2, 1).reshape(depth * d, fs) for a in (pf, of)]
        put(name, _adamw(parts_g, w_in[name].reshape(depth * d, fs), m_in[name].reshape(depth * d, fs),
                         v_in[name].reshape(depth * d, fs), f"adamw_{name}"), w_in[name].shape)
    put("ffn_w_down", _adamw([pf[:, 2].reshape(depth * fs, d), of[:, 2].reshape(depth * fs, d)],
                             ffn_w_down.reshape(depth * fs, d), m_ffn_w_down.reshape(depth * fs, d),
                             v_ffn_w_down.reshape(depth * fs, d), "adamw_ffn_w_down"), ffn_w_down.shape)
    put("s5_w_glu", _adamw([p_glu, o_glu], s5_w_glu.reshape(n_s5 * gs, d), m_s5_w_glu.reshape(n_s5 * gs, d),
                           v_s5_w_glu.reshape(n_s5 * gs, d), "adamw_s5_w_glu"), s5_w_glu.shape)

    q_tot, bc_tot, pw_tot = _add_pairs([(p_q, o_q), (p_bc, o_bc), (p_pw, o_pw)], "sum_small")
    g_small = dict(zip(small_names, _unpack_rows(q_tot, [small_full[n].shape for n in small_names])))
    g_small.update(zip(bc_names, _unpack_rows(bc_tot, [small_full[n].shape for n in bc_names])))
    g_small["pool_w"] = pw_tot.reshape(pool_w.shape)
    g_small["pool_scale"] = lax.dynamic_slice_in_dim(g_small["pool_scale"], chip * pool_scale.shape[1],
                                                     pool_scale.shape[1], axis=1)
    g_small["ffn_conv_w"] = lax.dynamic_slice_in_dim(g_small["ffn_conv_w"], chip * fs, fs, axis=2)
    all_small = [n for n in W_NAMES if n not in BIG]
    local_shapes = [w_in[n].shape for n in all_small]
    res = _adamw([_pack_rows([g_small[n] for n in all_small], d, 64)],
                 _pack_rows([w_in[n] for n in all_small], d, 64), _pack_rows([m_in[n] for n in all_small], d, 64),
                 _pack_rows([v_in[n] for n in all_small], d, 64), "adamw_small")
    unpacked = [_unpack_rows(r, local_shapes) for r in res]
    for idx, n in enumerate(all_small):
        outs[n] = (g_small[n], unpacked[1][idx], unpacked[2][idx], unpacked[3][idx])

    return (loss, grad_x, *[outs[n][0] for n in W_NAMES], *[outs[n][1] for n in W_NAMES],
            *[outs[n][2] for n in W_NAMES], *[outs[n][3] for n in W_NAMES])
```

```python
import math

import jax
import jax.numpy as jnp
from jax import lax
from jax.experimental import pallas as pl
from jax.experimental.pallas import tpu as pltpu

f32, bf16 = jnp.float32, jnp.bfloat16
SDS = jax.ShapeDtypeStruct
MESH = pl.DeviceIdType.MESH

RMS_EPS = 1e-6
GELU_C = math.sqrt(2.0 / math.pi)
GELU_K = 0.044715
ADAM_LR, ADAM_B1, ADAM_B2, ADAM_EPS, ADAM_WD, ADAM_STEP = 0.001, 0.9, 0.999, 1e-08, 0.01, 10
POOL_WINDOWS = (2, 4, 8, 16)
POOL_HALO = 16
S5_GROUP_CH = 16
S5_STATE = 64
S5_KB_CH = 256
N_CHIPS = 4

FFN_TL = 512
FFN_FT = 256
FFN_CH = 16
FFN_SPLIT = 2
WG_TR = 2048
POOL_TL = 512
S5_TQ = 512
S5_CB = 512
TOK_TR = 512
VMEM_LIMIT = 56 * 1024 * 1024

HBM = pl.BlockSpec(memory_space=pltpu.HBM)


def _cparams(n_axes=0, side_effects=False):
    kw = dict(vmem_limit_bytes=VMEM_LIMIT)
    if n_axes:
        kw["dimension_semantics"] = ("arbitrary",) * n_axes
    if side_effects:
        kw["has_side_effects"] = True
    return pltpu.CompilerParams(**kw)


def _dot(a, b):
    return jnp.dot(a, b, preferred_element_type=f32)


def _dot_nt(a, b):
    return lax.dot_general(a, b, (((1,), (1,)), ((), ())), preferred_element_type=f32)


def _dot_tn(a, b):
    return lax.dot_general(a, b, (((0,), (0,)), ((), ())), preferred_element_type=f32)


def _rms_scale(x):
    return lax.rsqrt(jnp.mean(x * x, axis=-1, keepdims=True) + RMS_EPS)


def _rms_fwd(x, gain):
    return x * _rms_scale(x) * gain


def _rms_bwd(x, gain, dy):
    r = _rms_scale(x)
    xn = x * r
    dgain = jnp.sum(dy * xn, axis=0, keepdims=True)
    dxn = dy * gain
    dx = r * (dxn - xn * jnp.mean(dxn * xn, axis=-1, keepdims=True))
    return dx, dgain


def _gelu(x):
    t = jnp.tanh(x * (GELU_C + (GELU_C * GELU_K) * (x * x)))
    hx = 0.5 * x
    return hx + hx * t


def _gelu_and_grad(x):
    x2 = x * x
    t = jnp.tanh(x * (GELU_C + (GELU_C * GELU_K) * x2))
    hx = 0.5 * x
    return hx + hx * t, (0.5 + 0.5 * t) + hx * (1.0 - t * t) * (GELU_C + (3.0 * GELU_C * GELU_K) * x2)


def _sigmoid(x):
    return 1.0 / (1.0 + jnp.exp(-x))


def _cmul(ar, ai, br, bi):
    return ar * br - ai * bi, ar * bi + ai * br


def _row_block(n, cap):
    best = None
    for d in range(16, min(n, cap) + 1, 16):
        if n % d == 0:
            best = d
    assert best is not None, n
    return best


def _mesh_pos():
    return lax.axis_index("x"), lax.axis_index("y"), lax.axis_index("c")


def _other_chips(x, y):
    return [(1 - x, y), (x, 1 - y), (1 - x, 1 - y)]


def _slab_index(ndim, axis, start, size):
    return tuple(pl.ds(start, size) if a == axis else slice(None) for a in range(ndim))


class _ChipExchange:
    def __init__(self, kind, axis, src, dst, send_sems, recv_sems, loc_sem):
        x, y, c = _mesh_pos()
        me = 2 * x + y
        nd = len(src.shape)
        size = src.shape[axis] if kind == "gather" else src.shape[axis] // N_CHIPS

        def src_for(kk):
            return src.at[_slab_index(nd, axis, kk * size, size)] if kind == "slab" else src

        def dst_for(kk):
            return dst.at[_slab_index(nd, axis, kk * size, size)] if kind == "gather" else dst.at[kk]

        self.own = pltpu.make_async_copy(src_for(me), dst_for(me), loc_sem)
        self.sends, self.recvs = [], []
        for j, chip in enumerate(_other_chips(x, y)):
            kk = 2 * chip[0] + chip[1]
            peer = dict(send_sem=send_sems.at[j], recv_sem=recv_sems.at[j], device_id=(chip[0], chip[1], c),
                        device_id_type=MESH)
            self.sends.append(pltpu.make_async_remote_copy(src_ref=src_for(kk), dst_ref=dst_for(me), **peer))
            self.recvs.append(pltpu.make_async_remote_copy(src_ref=src_for(me), dst_ref=dst_for(kk), **peer))

    def start(self):
        self.own.start()
        for cp in self.sends:
            cp.start()

    def finish(self):
        for cp in self.recvs:
            cp.wait_recv()
        for cp in self.sends:
            cp.wait_send()
        self.own.wait()


def _exchange_out_shape(kind, axis, src):
    if kind == "gather":
        shape = tuple(N_CHIPS * n if a == axis else n for a, n in enumerate(src.shape))
    elif kind == "slab":
        shape = (N_CHIPS,) + tuple(n // N_CHIPS if a == axis else n for a, n in enumerate(src.shape))
    else:
        shape = (N_CHIPS,) + tuple(src.shape)
    return SDS(shape, src.dtype)


EXCHANGE_SEMS = [pltpu.SemaphoreType.DMA((3,)), pltpu.SemaphoreType.DMA((3,)), pltpu.SemaphoreType.DMA((1,))]


def _chip_exchange(items, name):
    n = len(items)

    def body(*refs):
        ins, outs, sems = refs[:n], refs[n:2 * n], refs[2 * n:]
        exs = [_ChipExchange(kind, axis, ins[i], outs[i], sems[3 * i], sems[3 * i + 1], sems[3 * i + 2].at[0])
               for i, (kind, axis, _) in enumerate(items)]
        for ex in exs:
            ex.start()
        for ex in exs:
            ex.finish()

    return pl.pallas_call(
        body, name=name, out_shape=tuple(_exchange_out_shape(k, a, arr) for k, a, arr in items),
        in_specs=[HBM] * n, out_specs=tuple([HBM] * n), scratch_shapes=EXCHANGE_SEMS * n,
        compiler_params=_cparams(side_effects=True),
    )(*[arr for _, _, arr in items])


def _hosted(hosts, host_ins, host_outs, sems, first, last):
    if not hosts:
        return lambda: None
    exs = [_ChipExchange(h[0], h[1], host_ins[i], host_outs[i], sems[3 * i], sems[3 * i + 1], sems[3 * i + 2].at[0])
           for i, h in enumerate(hosts)]

    @pl.when(first)
    def _():
        for ex in exs:
            ex.start()

    def finish():
        @pl.when(last)
        def _():
            for ex in exs:
                ex.finish()

    return finish


def _host_args(hosts):
    hosts = hosts or []
    n = len(hosts)
    return [HBM] * n, [_exchange_out_shape(*h) for h in hosts], [HBM] * n, EXCHANGE_SEMS * n, [h[2] for h in hosts]


def _exchange_with_sibling(parts):
    n = len(parts)

    def body(*refs):
        ins, outs = refs[:n], refs[n:2 * n]
        send_sems, recv_sems = refs[2 * n:]
        x, y, c = _mesh_pos()
        cps = [pltpu.make_async_remote_copy(src_ref=s, dst_ref=t, send_sem=send_sems.at[i], recv_sem=recv_sems.at[i],
                                            device_id=(x, y, 1 - c), device_id_type=MESH)
               for i, (s, t) in enumerate(zip(ins, outs))]
        for cp in cps:
            cp.start()
        for cp in cps:
            cp.wait()

    return pl.pallas_call(
        body, name="exchange_with_sibling",
        out_shape=tuple(SDS(p.shape, p.dtype) for p in parts),
        in_specs=[HBM] * n, out_specs=tuple([HBM] * n),
        scratch_shapes=[pltpu.SemaphoreType.DMA((n,)), pltpu.SemaphoreType.DMA((n,))],
        compiler_params=_cparams(side_effects=True),
    )(*parts)


def _sum_slots(r, name):
    _, rows, cols = r.shape
    tr = _row_block(rows, 512)

    def body(r_ref, o_ref):
        o_ref[...] = ((r_ref[0].astype(f32) + r_ref[1].astype(f32)) + r_ref[2].astype(f32)) + r_ref[3].astype(f32)

    return pl.pallas_call(
        body, name=name, grid=(rows // tr,), out_shape=SDS((rows, cols), f32),
        in_specs=[pl.BlockSpec((N_CHIPS, tr, cols), lambda i: (0, i, 0))],
        out_specs=pl.BlockSpec((tr, cols), lambda i: (i, 0)),
        compiler_params=_cparams(1),
    )(r)


def _sum_slots_layers(rs, name):
    _, rows, cols = rs[0].shape
    nl = len(rs)
    tr = _row_block(rows, 256)

    def body(*refs):
        o_ref = refs[nl]
        for l in range(nl):
            r = refs[l]
            o_ref[l] = ((r[0].astype(f32) + r[1].astype(f32)) + r[2].astype(f32)) + r[3].astype(f32)

    return pl.pallas_call(
        body, name=name, grid=(rows // tr,), out_shape=SDS((nl, rows, cols), f32),
        in_specs=[pl.BlockSpec((N_CHIPS, tr, cols), lambda i: (0, i, 0))] * nl,
        out_specs=pl.BlockSpec((nl, tr, cols), lambda i: (0, i, 0)),
        compiler_params=_cparams(1),
    )(*rs)


def _add_pairs(pairs, name):
    n = len(pairs)

    def body(*refs):
        for i in range(n):
            refs[2 * n + i][...] = refs[2 * i][...] + refs[2 * i + 1][...]

    return pl.pallas_call(body, name=name, out_shape=tuple(SDS(a.shape, f32) for a, _ in pairs),
                          compiler_params=_cparams())(*[t for pair in pairs for t in pair])


def _adamw(g_parts, w, m, v, name):
    rows, cols = w.shape[-2:]
    tr = _row_block(rows, 512)
    n_g = len(g_parts)
    c1 = 1.0 / (1.0 - ADAM_B1 ** ADAM_STEP)
    c2 = 1.0 / (1.0 - ADAM_B2 ** ADAM_STEP)

    def body(*refs):
        g_refs = refs[:n_g]
        w_ref, m_ref, v_ref, go_ref, d_ref, mo_ref, vo_ref = refs[n_g:]
        g = g_refs[0][...]
        for r in g_refs[1:]:
            g = g + r[...]
        mn = ADAM_B1 * m_ref[...] + (1.0 - ADAM_B1) * g
        vn = ADAM_B2 * v_ref[...] + (1.0 - ADAM_B2) * (g * g)
        go_ref[...] = g
        mo_ref[...] = mn
        vo_ref[...] = vn
        d_ref[...] = -ADAM_LR * ((mn * c1) / (jnp.sqrt(vn * c2) + ADAM_EPS) + ADAM_WD * w_ref[...])

    if w.ndim == 2:
        grid, spec = (rows // tr,), pl.BlockSpec((tr, cols), lambda i: (i, 0))
    else:
        grid, spec = (w.shape[0], rows // tr), pl.BlockSpec((None, tr, cols), lambda a, i: (a, i, 0))
    return pl.pallas_call(
        body, name=name, grid=grid, out_shape=tuple(SDS(w.shape, f32) for _ in range(4)),
        in_specs=[spec] * (n_g + 3), out_specs=(spec,) * 4, compiler_params=_cparams(len(grid)),
    )(*g_parts, w, m, v)


def _resident_spec(shape, n_grid):
    zeros = (0,) * len(shape)
    return pl.BlockSpec(tuple(shape), {2: lambda a, b: zeros, 3: lambda a, b, c: zeros}[n_grid],
                        pipeline_mode=pl.Buffered(1))


def _weight_tiles(w_ref, j, ft):
    rows = pl.ds(j * ft if isinstance(j, int) else pl.multiple_of(j * ft, ft), ft)
    return w_ref[0, rows, :], w_ref[1, rows, :], w_ref[2, rows, :]


def _shifted_rows(ext):
    return ext[8:, :], pltpu.roll(ext, 1, axis=0)[8:, :], pltpu.roll(ext, 2, axis=0)[8:, :]


def _later_rows(ext):
    n = ext.shape[0]
    return pltpu.roll(ext, n - 1, axis=0)[0:n - 8, :], pltpu.roll(ext, n - 2, axis=0)[0:n - 8, :]


def _ffn_fwd(x_mid, gain_pre, gain_post, w_l, conv_w, conv_b, li, host=None):
    bsz, l, d = x_mid.shape
    f = w_l.shape[1]
    tl, ft = min(FFN_TL, l), min(FFN_FT, f)
    nt, nf = l // tl, f // ft
    n_sub = FFN_SPLIT if tl % (FFN_SPLIT * FFN_CH) == 0 else 1
    sub = tl // n_sub
    h_in, h_shape, h_out, h_scr, h_ops = _host_args(host)

    def body(x_ref, gpre_ref, gpost_ref, w_ref, cw_ref, cb_ref, *rest):
        xo_ref, g_ref, v_ref, f_ref = rest[len(h_in):len(h_in) + 4]
        s0 = len(h_in) + 4 + len(h_out)
        h_sc, facc, gprev = rest[s0:s0 + 3]
        b, t, j = pl.program_id(0), pl.program_id(1), pl.program_id(2)
        finish = _hosted(host, rest[:len(h_in)], rest[len(h_in) + 4:s0], rest[s0 + 3:],
                         (b == 0) & (t == 0) & (j == 0), (b == bsz - 1) & (t == nt - 1) & (j == nf - 1))

        @pl.when(j == 0)
        def _():
            h_sc[...] = _rms_fwd(x_ref[...], gpre_ref[...]).astype(bf16)
            facc[...] = jnp.zeros_like(facc)

        @pl.when(t == 0)
        def _():
            gprev[j] = jnp.zeros((8, ft), f32)

        wg, wv, wd = _weight_tiles(w_ref, j, ft)
        gs = [_dot_nt(h_sc[s * sub:(s + 1) * sub, :], wg) for s in range(n_sub)]
        vs = [_dot_nt(h_sc[s * sub:(s + 1) * sub, :], wv) for s in range(n_sub)]
        w0, w1, w2, bias = (jnp.broadcast_to(r, (FFN_CH, ft))
                            for r in (cw_ref[0:1, :], cw_ref[1:2, :], cw_ref[2:3, :], cb_ref[...]))
        hist = gprev[j]
        for s in range(n_sub):
            rows = slice(s * sub, (s + 1) * sub)
            g_ref[rows, :] = gs[s].astype(bf16)
            v_ref[rows, :] = vs[s].astype(bf16)
            hdn = []
            for r0 in range(0, sub, FFN_CH):
                g0, g1, g2 = _shifted_rows(jnp.concatenate([hist, gs[s][r0:r0 + FFN_CH, :]], axis=0))
                hist = g0[FFN_CH - 8:, :]
                hdn.append((_gelu(bias + w0 * g2 + w1 * g1 + w2 * g0) * vs[s][r0:r0 + FFN_CH, :]).astype(bf16))
            facc[rows, :] += _dot(jnp.concatenate(hdn, axis=0), wd)
        gprev[j] = hist

        @pl.when(j == nf - 1)
        def _():
            fv = facc[...]
            f_ref[...] = fv
            xo_ref[...] = x_ref[...] + _rms_fwd(fv, gpost_ref[...])

        finish()

    tok = pl.BlockSpec((None, tl, d), lambda b, t, j: (b, t, 0))
    hid = pl.BlockSpec((None, tl, ft), lambda b, t, j: (j, b * nt + t, 0))
    gain = pl.BlockSpec((1, d), lambda b, t, j: (0, 0))
    hid_shape = SDS((nf, bsz * l, ft), bf16)
    return pl.pallas_call(
        body, name=f"ffn_fwd_{li}", grid=(bsz, nt, nf),
        out_shape=(SDS((bsz, l, d), f32), hid_shape, hid_shape, SDS((bsz, l, d), f32), *h_shape),
        in_specs=[tok, gain, gain, _resident_spec(w_l.shape, 3),
                  pl.BlockSpec((None, 3, ft), lambda b, t, j: (li, 0, j)),
                  pl.BlockSpec((None, 1, ft), lambda b, t, j: (li, 0, j)), *h_in],
        out_specs=(tok, hid, hid, tok, *h_out),
        scratch_shapes=[pltpu.VMEM((tl, d), bf16), pltpu.VMEM((tl, d), f32), pltpu.VMEM((nf, 8, ft), f32), *h_scr],
        compiler_params=_cparams(3, side_effects=bool(host)),
    )(x_mid, gain_pre, gain_post, w_l, conv_w, conv_b.reshape(conv_b.shape[0], 1, f), *h_ops)


def _ffn_bwd_act(x_mid, f_sv, g_sv, v_sv, dxo, gain_pre, gain_post, w_l, conv_w, conv_b, li, host=None):
    bsz, l, d = x_mid.shape
    f = w_l.shape[1]
    tl, ft = min(FFN_TL, l), min(FFN_FT, f)
    nt, nf = l // tl, f // ft
    n_sub = FFN_SPLIT if tl % (FFN_SPLIT * FFN_CH) == 0 else 1
    sub = tl // n_sub
    h_in, h_shape, h_out, h_scr, h_ops = _host_args(host)

    def body(x_ref, f_ref, dxo_ref, g_ref, v_ref, gh_ref, gpre_ref, gpost_ref, w_ref, cw_ref, cb_ref, *rest):
        o0 = len(h_in)
        dx_ref, dg_ref, dv_ref, hdn_ref, h_ref, df_ref, dgain_ref, dconv_ref = rest[o0:o0 + 8]
        s0 = o0 + 8 + len(h_out)
        h_sc, df_sc, dh_acc, dgc_next = rest[s0:s0 + 4]
        b, t, j = pl.program_id(0), pl.program_id(1), pl.program_id(2)
        tt = nt - 1 - t
        finish = _hosted(host, rest[:o0], rest[o0 + 8:s0], rest[s0 + 4:],
                         (b == 0) & (t == 0) & (j == 0), (b == bsz - 1) & (t == nt - 1) & (j == nf - 1))

        @pl.when((b == 0) & (t == 0) & (j == 0))
        def _():
            dgain_ref[...] = jnp.zeros_like(dgain_ref)
            dconv_ref[...] = jnp.zeros_like(dconv_ref)

        @pl.when(j == 0)
        def _():
            hb = _rms_fwd(x_ref[...], gpre_ref[...]).astype(bf16)
            h_sc[...] = hb
            h_ref[...] = hb
            df, dgp = _rms_bwd(f_ref[...], gpost_ref[...], dxo_ref[...])
            dfb = df.astype(bf16)
            df_sc[...] = dfb
            df_ref[...] = dfb
            dgain_ref[1:2, :] += dgp
            dh_acc[...] = jnp.zeros_like(dh_acc)

        @pl.when(t == 0)
        def _():
            dgc_next[j] = jnp.zeros((8, ft), f32)

        wg, wv, wd = _weight_tiles(w_ref, j, ft)
        dhdns = [_dot_nt(df_sc[s * sub:(s + 1) * sub, :], wd) for s in range(n_sub)]
        w0, w1, w2, bias = (jnp.broadcast_to(r, (FFN_CH, ft))
                            for r in (cw_ref[0:1, :], cw_ref[1:2, :], cw_ref[2:3, :], cb_ref[...]))

        def fold(a):
            return a.reshape(FFN_CH // 8, 8, ft).sum(axis=0)

        def conv_bwd(dgc, after):
            d1, d2 = _later_rows(jnp.concatenate([dgc, after], axis=0))
            return (w2 * dgc + w1 * d1 + w0 * d2).astype(bf16)

        def flush(s, dgs, dvs):
            rows = slice(s * sub, (s + 1) * sub)
            dgb, dvb = jnp.concatenate(dgs, axis=0), jnp.concatenate(dvs, axis=0)
            dg_ref[rows, :] = dgb
            dv_ref[rows, :] = dvb
            dh_acc[rows, :] += _dot(dgb, wg) + _dot(dvb, wv)

        hist = jnp.where(tt > 0, gh_ref[...].astype(f32)[8:16, :], 0.0)
        acc = [jnp.zeros((8, ft), f32)] * 4
        first, pending, dgs, dvs_prev = None, None, [], None
        for s in range(n_sub):
            rows = slice(s * sub, (s + 1) * sub)
            g = g_ref[rows, :].astype(f32)
            v = v_ref[rows, :].astype(f32)
            hdn, dvs = [], []
            for r0 in range(0, sub, FFN_CH):
                g0, g1, g2 = _shifted_rows(jnp.concatenate([hist, g[r0:r0 + FFN_CH, :]], axis=0))
                hist = g0[FFN_CH - 8:, :]
                vc, dc = v[r0:r0 + FFN_CH, :], dhdns[s][r0:r0 + FFN_CH, :]
                u, ug = _gelu_and_grad(bias + w0 * g2 + w1 * g1 + w2 * g0)
                hdn.append((u * vc).astype(bf16))
                dvs.append((dc * u).astype(bf16))
                dgc = dc * vc * ug
                acc = [acc[0] + fold(dgc * g2), acc[1] + fold(dgc * g1), acc[2] + fold(dgc * g0), acc[3] + fold(dgc)]
                if pending is None:
                    first = dgc[0:8, :]
                else:
                    dgs.append(conv_bwd(pending, dgc[0:8, :]))
                    if r0 == 0:
                        flush(s - 1, dgs, dvs_prev)
                        dgs = []
                pending = dgc
            hdn_ref[rows, :] = jnp.concatenate(hdn, axis=0)
            dvs_prev = dvs
        dgs.append(conv_bwd(pending, dgc_next[j]))
        flush(n_sub - 1, dgs, dvs_prev)
        dgc_next[j] = first
        for k in range(4):
            dconv_ref[j, k:k + 1, :] += jnp.sum(acc[k], axis=0, keepdims=True)

        @pl.when(j == nf - 1)
        def _():
            dxp, dgp = _rms_bwd(x_ref[...], gpre_ref[...], dh_acc[...])
            dx_ref[...] = dxo_ref[...] + dxp
            dgain_ref[0:1, :] += dgp

        finish()

    tok = pl.BlockSpec((None, tl, d), lambda b, t, j: (b, nt - 1 - t, 0))
    hid = pl.BlockSpec((None, tl, ft), lambda b, t, j: (j, b * nt + nt - 1 - t, 0))
    halo = pl.BlockSpec((None, 16, ft),
                        lambda b, t, j: (j, jnp.maximum((b * nt + nt - 1 - t) * (tl // 16) - 1, 0), 0))
    gain = pl.BlockSpec((1, d), lambda b, t, j: (0, 0))
    hid_shape = SDS((nf, bsz * l, ft), bf16)
    return pl.pallas_call(
        body, name=f"ffn_bwd_act_{li}", grid=(bsz, nt, nf),
        out_shape=(SDS((bsz, l, d), f32), hid_shape, hid_shape, hid_shape,
                   SDS((bsz, l, d), bf16), SDS((bsz, l, d), bf16), SDS((8, d), f32), SDS((nf, 8, ft), f32), *h_shape),
        in_specs=[tok, tok, tok, hid, hid, halo, gain, gain, _resident_spec(w_l.shape, 3),
                  pl.BlockSpec((None, 3, ft), lambda b, t, j: (li, 0, j)),
                  pl.BlockSpec((None, 1, ft), lambda b, t, j: (li, 0, j)), *h_in],
        out_specs=(tok, hid, hid, hid, tok, tok,
                   pl.BlockSpec((8, d), lambda b, t, j: (0, 0)), pl.BlockSpec((nf, 8, ft), lambda b, t, j: (0, 0, 0)),
                   *h_out),
        scratch_shapes=[pltpu.VMEM((tl, d), bf16), pltpu.VMEM((tl, d), bf16), pltpu.VMEM((tl, d), f32),
                        pltpu.VMEM((nf, 8, ft), f32), *h_scr],
        compiler_params=_cparams(3, side_effects=bool(host)),
    )(x_mid, f_sv, dxo, g_sv, v_sv, g_sv, gain_pre, gain_post, w_l, conv_w,
      conv_b.reshape(conv_b.shape[0], 1, f), *h_ops)


def _ffn_bwd_weights(dg, dv, hdn, h, df, li):
    nf, t, ft = dg.shape
    f = nf * ft
    d = h.shape[1]
    tr = min(WG_TR, t)
    nr = t // tr

    def body(dg_ref, dv_ref, hdn_ref, h_ref, df_ref, o_ref, acc):
        r = pl.program_id(1)

        @pl.when(r == 0)
        def _():
            acc[...] = jnp.zeros_like(acc)

        rows = pl.ds(pl.multiple_of(r * tr, tr), tr)
        hb = h_ref[rows, :]
        acc[0] += _dot_tn(dg_ref[...], hb)
        acc[1] += _dot_tn(dv_ref[...], hb)
        acc[2] += _dot_tn(hdn_ref[...], df_ref[rows, :])

        @pl.when(r == nr - 1)
        def _():
            o_ref[...] = acc[...].astype(bf16)

    hid = pl.BlockSpec((None, tr, ft), lambda j, r: (j, r, 0))
    tok = _resident_spec((t, d), 2)
    return pl.pallas_call(
        body, name=f"ffn_bwd_weights_{li}", grid=(nf, nr), out_shape=SDS((3, f, d), bf16),
        in_specs=[hid, hid, hid, tok, tok], out_specs=pl.BlockSpec((3, ft, d), lambda j, r: (0, j, 0)),
        scratch_shapes=[pltpu.VMEM((3, ft, d), f32)], compiler_params=_cparams(2),
    )(dg, dv, hdn, h, df)


def _pool_counts(t0, tl, d):
    gch = d // len(POOL_WINDOWS)
    tpos = (t0 + lax.broadcasted_iota(jnp.int32, (tl, d), 0) + 1).astype(f32)
    lane = lax.broadcasted_iota(jnp.int32, (tl, d), 1)
    win = jnp.full((tl, d), float(POOL_WINDOWS[-1]), f32)
    for gi in range(len(POOL_WINDOWS) - 2, -1, -1):
        win = jnp.where(lane < (gi + 1) * gch, float(POOL_WINDOWS[gi]), win)
    return jnp.minimum(tpos, win)


def _pool_select(parts, tl, d):
    gch = d // len(POOL_WINDOWS)
    lane = lax.broadcasted_iota(jnp.int32, (tl, d), 1)
    out = parts[-1]
    for gi in range(len(parts) - 2, -1, -1):
        out = jnp.where(lane < (gi + 1) * gch, parts[gi], out)
    return out


def _pool_window_sums(u, halo, tl):
    ext = jnp.concatenate([halo, u], axis=0)
    sums, cur = [], ext
    for k in (1, 2, 4, 8):
        cur = cur + pltpu.roll(cur, k, axis=0)
        sums.append(cur[POOL_HALO:POOL_HALO + tl, :])
    return sums


def _pool_mix(u, halo, cnt, pw_ref, scale, tl, d):
    gch = d // len(POOL_WINDOWS)
    pooled = _pool_select(_pool_window_sums(u, halo, tl), tl, d) / cnt
    diff = pooled - u
    outs = [_dot(diff[:, gi * gch:(gi + 1) * gch].astype(bf16), pw_ref[gi]) for gi in range(len(POOL_WINDOWS))]
    return diff, jnp.concatenate(outs, axis=1)


def _pool_fwd(x, gain_pre, gain_post, pw, scale, li):
    bsz, l, d = x.shape
    tl = min(POOL_TL, l)
    nt = l // tl

    def body(x_ref, gpre_ref, gpost_ref, pw_ref, sc_ref, xo_ref, halo):
        t = pl.program_id(1)

        @pl.when(t == 0)
        def _():
            halo[...] = jnp.zeros_like(halo)

        xv = x_ref[...]
        u = _rms_fwd(xv, gpre_ref[...])
        _, out = _pool_mix(u, halo[...], _pool_counts(t * tl, tl, d), pw_ref, sc_ref[...], tl, d)
        halo[...] = u[tl - POOL_HALO:tl, :]
        xo_ref[...] = xv + _rms_fwd(out * sc_ref[...], gpost_ref[...])

    tok = pl.BlockSpec((None, tl, d), lambda b, t: (b, t, 0))
    gain = pl.BlockSpec((1, d), lambda b, t: (0, 0))
    return pl.pallas_call(
        body, name=f"pool_fwd_{li}", grid=(bsz, nt), out_shape=SDS((bsz, l, d), f32),
        in_specs=[tok, gain, gain, pl.BlockSpec(pw.shape, lambda b, t: (0, 0, 0)), gain], out_specs=tok,
        scratch_shapes=[pltpu.VMEM((POOL_HALO, d), f32)], compiler_params=_cparams(2),
    )(x, gain_pre, gain_post, pw, scale)


def _pool_bwd(x, dxo, gain_pre, gain_post, pw, scale, li):
    bsz, l, d = x.shape
    tl = min(POOL_TL, l)
    nt = l // tl
    ng = len(POOL_WINDOWS)
    gch = d // ng
    n_ext = tl + POOL_HALO

    def body(x_ref, xh_ref, dxo_ref, gpre_ref, gpost_ref, pw_ref, sc_ref, dx_ref, dpw_ref, ds_ref, qnext):
        b, t = pl.program_id(0), pl.program_id(1)
        tt = nt - 1 - t

        @pl.when((b == 0) & (t == 0))
        def _():
            dpw_ref[...] = jnp.zeros_like(dpw_ref)
            ds_ref[...] = jnp.zeros_like(ds_ref)

        @pl.when(t == 0)
        def _():
            qnext[...] = jnp.zeros_like(qnext)

        xv = x_ref[...]
        gpre = gpre_ref[...]
        u = _rms_fwd(xv, gpre)
        uh = _rms_fwd(jnp.where(tt > 0, xh_ref[...], 0.0), gpre)
        cnt = _pool_counts(tt * tl, tl, d)
        scale_v = sc_ref[...]
        diff, out = _pool_mix(u, uh, cnt, pw_ref, scale_v, tl, d)
        dxo_v = dxo_ref[...]
        dm, dgpost = _rms_bwd(out * scale_v, gpost_ref[...], dxo_v)
        ds_ref[1:2, :] += dgpost
        ds_ref[2:3, :] += jnp.sum(dm * out, axis=0, keepdims=True)
        dout = (dm * scale_v).astype(bf16)
        ddiffs = []
        for gi in range(ng):
            sl = slice(gi * gch, (gi + 1) * gch)
            dpw_ref[gi] += _dot_tn(diff[:, sl].astype(bf16), dout[:, sl])
            ddiffs.append(_dot_nt(dout[:, sl], pw_ref[gi]))
        ddiff = jnp.concatenate(ddiffs, axis=1)
        q = ddiff / cnt
        ext = jnp.concatenate([q, qnext[...]], axis=0)
        sums, cur = [], ext
        for k in (1, 2, 4, 8):
            cur = cur + pltpu.roll(cur, n_ext - k, axis=0)
            sums.append(cur[0:tl, :])
        du = _pool_select(sums, tl, d) - ddiff
        qnext[...] = q[0:POOL_HALO, :]
        dxp, dgpre = _rms_bwd(xv, gpre, du)
        ds_ref[0:1, :] += dgpre
        dx_ref[...] = dxo_v + dxp

    tok = pl.BlockSpec((None, tl, d), lambda b, t: (b, nt - 1 - t, 0))
    halo = pl.BlockSpec((None, POOL_HALO, d),
                        lambda b, t: (b, jnp.maximum((nt - 1 - t) * (tl // POOL_HALO) - 1, 0), 0))
    gain = pl.BlockSpec((1, d), lambda b, t: (0, 0))
    return pl.pallas_call(
        body, name=f"pool_bwd_{li}", grid=(bsz, nt),
        out_shape=(SDS((bsz, l, d), f32), SDS((ng, gch, gch), f32), SDS((8, d), f32)),
        in_specs=[tok, halo, tok, gain, gain, pl.BlockSpec(pw.shape, lambda b, t: (0, 0, 0)), gain],
        out_specs=(tok, pl.BlockSpec((ng, gch, gch), lambda b, t: (0, 0, 0)), pl.BlockSpec((8, d), lambda b, t: (0, 0))),
        scratch_shapes=[pltpu.VMEM((POOL_HALO, d), f32)], compiler_params=_cparams(2),
    )(x, x, dxo, gain_pre, gain_post, pw, scale)


def _s5_discretise(lam_re, lam_im, log_dt, b_re, b_im):
    def body(lr_ref, li_ref, ld_ref, br_ref, bi_ref, ar_ref, ai_ref, bbr_ref, bbi_ref):
        lr, li = lr_ref[...], li_ref[...]
        dt = jnp.exp(ld_ref[...])
        mag = jnp.exp(lr * dt)
        ar = mag * jnp.cos(li * dt)
        ai = mag * jnp.sin(li * dt)
        den = lr * lr + li * li
        nr, ni = ar - 1.0, ai
        fr = (nr * lr + ni * li) / den
        fi = (ni * lr - nr * li) / den
        br, bi = br_ref[...], bi_ref[...]
        ar_ref[...] = ar
        ai_ref[...] = ai
        bbr_ref[...] = fr * br - fi * bi
        bbi_ref[...] = fr * bi + fi * br

    return pl.pallas_call(
        body, name="s5_discretise",
        out_shape=(SDS(lam_re.shape, f32), SDS(lam_re.shape, f32), SDS(b_re.shape, f32), SDS(b_re.shape, f32)),
        compiler_params=_cparams(),
    )(lam_re, lam_im, log_dt, b_re, b_im)


def _s5_discretise_bwd(lam_re, lam_im, log_dt, b_re, b_im, g_ar, g_ai, g_bbr, g_bbi):
    def body(lr_ref, li_ref, ld_ref, br_ref, bi_ref, gar_ref, gai_ref, gbbr_ref, gbbi_ref,
             dlr_ref, dli_ref, dld_ref, dbr_ref, dbi_ref):
        lr, li = lr_ref[...], li_ref[...]
        dt = jnp.exp(ld_ref[...])
        mag = jnp.exp(lr * dt)
        cs, sn = jnp.cos(li * dt), jnp.sin(li * dt)
        ar, ai = mag * cs, mag * sn
        den = lr * lr + li * li
        nr, ni = ar - 1.0, ai
        fr = (nr * lr + ni * li) / den
        fi = (ni * lr - nr * li) / den
        br, bi = br_ref[...], bi_ref[...]
        gbbr, gbbi = gbbr_ref[...], gbbi_ref[...]
        dbr_ref[...] = fr * gbbr + fi * gbbi
        dbi_ref[...] = fr * gbbi - fi * gbbr
        gfr = jnp.sum(br * gbbr + bi * gbbi, axis=2, keepdims=True)
        gfi = jnp.sum(br * gbbi - bi * gbbr, axis=2, keepdims=True)
        gnr_num, gni_num = gfr / den, gfi / den
        gden = -(gfr * fr + gfi * fi) / den
        g_nr = gnr_num * lr - gni_num * li
        g_ni = gnr_num * li + gni_num * lr
        dlr = gnr_num * nr + gni_num * ni + gden * 2.0 * lr
        dli = gnr_num * ni - gni_num * nr + gden * 2.0 * li
        gar = gar_ref[...] + g_nr
        gai = gai_ref[...] + g_ni
        gq = (gar * cs + gai * sn) * mag
        gth = (gai * cs - gar * sn) * mag
        dlr_ref[...] = dlr + gq * dt
        dli_ref[...] = dli + gth * dt
        dld_ref[...] = jnp.sum(gq * lr + gth * li, axis=3, keepdims=True) * dt

    return pl.pallas_call(
        body, name="s5_discretise_bwd",
        out_shape=(SDS(lam_re.shape, f32), SDS(lam_re.shape, f32), SDS(log_dt.shape, f32),
                   SDS(b_re.shape, f32), SDS(b_re.shape, f32)),
        compiler_params=_cparams(),
    )(lam_re, lam_im, log_dt, b_re, b_im, g_ar, g_ai, g_bbr, g_bbi)


def _fill_powers(pw_r, pw_i, ar, ai, nj, width):
    cb = min(S5_CB, width)
    for c0 in range(0, width, cb):
        sl = pl.ds(c0, cb)
        a_r, a_i = ar[:, c0:c0 + cb], ai[:, c0:c0 + cb]

        def step(i, p):
            pw_r[i, :, sl] = p[0]
            pw_i[i, :, sl] = p[1]
            return _cmul(p[0], p[1], a_r, a_i)

        lax.fori_loop(0, nj, step, (a_r, a_i))


def _interleaved_scan(xr_sc, xi_sc, ar, ai, pw_r, pw_i, carry_r, carry_i, nj, width, reverse, h_sc=None):
    cb = min(S5_CB, width)
    acc_out = []
    for c0 in range(0, width, cb):
        sl = pl.ds(c0, cb)
        a_r, a_i = ar[:, c0:c0 + cb], ai[:, c0:c0 + cb]
        aj_r, aj_i = pw_r[nj - 1, :, sl], pw_i[nj - 1, :, sl]

        def pos(i):
            return nj - 1 - i if reverse else i

        def local_step(i, st):
            j = pos(i)
            hr, hi = _cmul(a_r, a_i, st[0], st[1])
            hr, hi = hr + xr_sc[j, :, sl], hi + xi_sc[j, :, sl]
            xr_sc[j, :, sl] = hr
            xi_sc[j, :, sl] = hi
            return hr, hi

        zero = jnp.zeros((8, cb), f32)
        fin_r, fin_i = lax.fori_loop(0, nj, local_step, (zero, zero))
        row = lax.broadcasted_iota(jnp.int32, (8, cb), 0)
        c_r, c_i = carry_r[0:1, sl], carry_i[0:1, sl]
        ent_r, ent_i = zero, zero
        order = range(7, -1, -1) if reverse else range(8)
        for s in order:
            ent_r = jnp.where(row == s, c_r, ent_r)
            ent_i = jnp.where(row == s, c_i, ent_i)
            pr, pi_ = _cmul(aj_r[0:1, :], aj_i[0:1, :], c_r, c_i)
            c_r, c_i = fin_r[s:s + 1, :] + pr, fin_i[s:s + 1, :] + pi_
        carry_r[:, sl] = jnp.broadcast_to(c_r, (8, cb))
        carry_i[:, sl] = jnp.broadcast_to(c_i, (8, cb))

        def fix_step(i, st):
            j = pos(i)
            nx_r, nx_i, acc_r, acc_i = st
            cr_, ci_ = _cmul(pw_r[i, :, sl], pw_i[i, :, sl], ent_r, ent_i)
            hr, hi = xr_sc[j, :, sl] + cr_, xi_sc[j, :, sl] + ci_
            xr_sc[j, :, sl] = hr
            xi_sc[j, :, sl] = hi
            if h_sc is not None:
                sr, si = h_sc[0][j, :, sl], h_sc[1][j, :, sl]
                acc_r = acc_r + nx_r * sr + nx_i * si
                acc_i = acc_i + nx_i * sr - nx_r * si
                nx_r, nx_i = hr, hi
            return nx_r, nx_i, acc_r, acc_i

        st = lax.fori_loop(0, nj, fix_step, (ent_r, ent_i, zero, zero))
        acc_out.append((st[2], st[3]))
    return acc_out


def _diag_mask(n_rep, h, ks):
    assert h & (h - 1) == 0 and (ks // n_rep) & (ks // n_rep - 1) == 0
    r = lax.shift_right_logical(lax.broadcasted_iota(jnp.int32, (n_rep * h, ks), 0), h.bit_length() - 1)
    c = lax.shift_right_logical(lax.broadcasted_iota(jnp.int32, (n_rep * h, ks), 1), (ks // n_rep).bit_length() - 1)
    return r == c


def _expand_block_diag(compact, n_rep):
    h, ks = compact.shape
    full = jnp.concatenate([compact] * n_rep, axis=0)
    return jnp.where(_diag_mask(n_rep, h, ks), full, 0.0).astype(bf16)


def _compact_block_diag(full, n_rep):
    rows, ks = full.shape
    h = rows // n_rep
    return jnp.sum(jnp.where(_diag_mask(n_rep, h, ks), full, 0.0).reshape(n_rep, h, ks), axis=0)


def _s5_scan_fwd(x_il, gain_pre, d_skip, ab_r, ab_i, bbc_r, bbc_i, cc_r, cc_in, li, host=None):
    bsz, nblk, tq, d = x_il.shape
    nkb, gch, ks = bbc_r.shape
    kc = d // nkb
    n_rep = kc // gch
    nj = tq // 8
    h_in, h_shape, h_out, h_scr, h_ops = _host_args(host)

    def body(xf_ref, xk_ref, gk_ref, dk_ref, ar_ref, ai_ref, bbr_ref, bbi_ref, cr_ref, ci_ref, *rest):
        o0 = len(h_in)
        y_ref, hr_ref, hi_ref = rest[o0:o0 + 3]
        s0 = o0 + 3 + len(h_out)
        xr_sc, xi_sc, pw_r, pw_i, carry_r, carry_i, bbr_sc, bbi_sc, crt_sc, cit_sc = rest[s0:s0 + 10]
        k, b, n = pl.program_id(0), pl.program_id(1), pl.program_id(2)
        finish = _hosted(host, rest[:o0], rest[o0 + 3:s0], rest[s0 + 10:],
                         (k == 0) & (b == 0) & (n == 0), (k == nkb - 1) & (b == bsz - 1) & (n == nblk - 1))
        ar = jnp.broadcast_to(ar_ref[...], (8, ks))
        ai = jnp.broadcast_to(ai_ref[...], (8, ks))

        @pl.when((b == 0) & (n == 0))
        def _():
            bbr_sc[...] = _expand_block_diag(bbr_ref[...], n_rep)
            bbi_sc[...] = _expand_block_diag(bbi_ref[...], n_rep)
            crt_sc[...] = _expand_block_diag(cr_ref[...], n_rep)
            cit_sc[...] = _expand_block_diag(ci_ref[...], n_rep)
            _fill_powers(pw_r, pw_i, ar, ai, nj, ks)

        @pl.when(n == 0)
        def _():
            carry_r[...] = jnp.zeros_like(carry_r)
            carry_i[...] = jnp.zeros_like(carry_i)

        u = xk_ref[...] * _rms_scale(xf_ref[...]) * gk_ref[...]
        ub = u.astype(bf16)
        xr_sc[...] = _dot(ub, bbr_sc[...]).reshape(nj, 8, ks)
        xi_sc[...] = _dot(ub, bbi_sc[...]).reshape(nj, 8, ks)
        _interleaved_scan(xr_sc, xi_sc, ar, ai, pw_r, pw_i, carry_r, carry_i, nj, ks, reverse=False)
        hrb = xr_sc[...].reshape(tq, ks).astype(bf16)
        hib = xi_sc[...].reshape(tq, ks).astype(bf16)
        hr_ref[...] = hrb
        hi_ref[...] = hib
        crt, cit, dk = crt_sc[...], cit_sc[...], dk_ref[...]
        for r0 in range(0, tq, tq // 2):
            rows = slice(r0, r0 + tq // 2)
            y_ref[rows, :] = _dot_nt(hrb[rows, :], crt) + _dot_nt(hib[rows, :], cit) + dk * u[rows, :]
        finish()

    full = pl.BlockSpec((None, None, tq, d), lambda k, b, n: (b, n, 0, 0))
    chan = pl.BlockSpec((None, None, tq, kc), lambda k, b, n: (b, n, 0, k))
    stat = pl.BlockSpec((None, None, tq, ks), lambda k, b, n: (b, n, 0, k))
    vec_c = pl.BlockSpec((1, kc), lambda k, b, n: (0, k))
    vec_s = pl.BlockSpec((1, ks), lambda k, b, n: (0, k))
    cmap = pl.BlockSpec((None, gch, ks), lambda k, b, n: (k, 0, 0))
    s_tot = nkb * ks
    return pl.pallas_call(
        body, name=f"s5_scan_fwd_{li}", grid=(nkb, bsz, nblk),
        out_shape=(SDS((bsz, nblk, tq, d), f32), SDS((bsz, nblk, tq, s_tot), bf16), SDS((bsz, nblk, tq, s_tot), bf16),
                   *h_shape),
        in_specs=[full, chan, vec_c, vec_c, vec_s, vec_s, cmap, cmap, cmap, cmap, *h_in],
        out_specs=(chan, stat, stat, *h_out),
        scratch_shapes=[pltpu.VMEM((nj, 8, ks), f32)] * 4 + [pltpu.VMEM((8, ks), f32)] * 2
        + [pltpu.VMEM((kc, ks), bf16)] * 4 + h_scr,
        compiler_params=_cparams(3, side_effects=bool(host)),
    )(x_il, x_il, gain_pre, d_skip, ab_r, ab_i, bbc_r, bbc_i, cc_r, cc_in, *h_ops)


def _s5_glu_fwd(y, x_il, w_glu, b_glu, gain_post, li):
    t, d = y.shape
    tr = min(TOK_TR, t)

    def body(y_ref, x_ref, w_ref, b_ref, gp_ref, xo_ref):
        z = _gelu(y_ref[...])
        a = _dot(z.astype(bf16), w_ref[...]) + b_ref[...]
        xo_ref[...] = x_ref[...] + _rms_fwd(z * _sigmoid(a), gp_ref[...])

    tok = pl.BlockSpec((tr, d), lambda i: (i, 0))
    vec = pl.BlockSpec((1, d), lambda i: (0, 0))
    return pl.pallas_call(
        body, name=f"s5_glu_fwd_{li}", grid=(t // tr,), out_shape=SDS((t, d), f32),
        in_specs=[tok, tok, pl.BlockSpec((d, d), lambda i: (0, 0)), vec, vec], out_specs=tok,
        compiler_params=_cparams(1),
    )(y, x_il, w_glu, b_glu, gain_post)


def _s5_glu_bwd(y, dxo, w_glu, b_glu, gain_post, li):
    t, d = y.shape
    tr = min(TOK_TR, t)

    def body(y_ref, dxo_ref, w_ref, b_ref, gp_ref, dy_ref, dw_ref, ds_ref):
        @pl.when(pl.program_id(0) == 0)
        def _():
            dw_ref[...] = jnp.zeros_like(dw_ref)
            ds_ref[...] = jnp.zeros_like(ds_ref)

        z, zg = _gelu_and_grad(y_ref[...])
        zb = z.astype(bf16)
        w = w_ref[...]
        s = _sigmoid(_dot(zb, w) + b_ref[...])
        dm, dgpost = _rms_bwd(z * s, gp_ref[...], dxo_ref[...])
        da = dm * z * s * (1.0 - s)
        dab = da.astype(bf16)
        dz = dm * s + _dot_nt(dab, w)
        dw_ref[...] += _dot_tn(zb, dab)
        ds_ref[0:1, :] += dgpost
        ds_ref[1:2, :] += jnp.sum(da, axis=0, keepdims=True)
        dy_ref[...] = dz * zg

    tok = pl.BlockSpec((tr, d), lambda i: (i, 0))
    vec = pl.BlockSpec((1, d), lambda i: (0, 0))
    mat = pl.BlockSpec((d, d), lambda i: (0, 0))
    return pl.pallas_call(
        body, name=f"s5_glu_bwd_{li}", grid=(t // tr,),
        out_shape=(SDS((t, d), f32), SDS((d, d), f32), SDS((8, d), f32)),
        in_specs=[tok, tok, mat, vec, vec], out_specs=(tok, mat, pl.BlockSpec((8, d), lambda i: (0, 0))),
        compiler_params=_cparams(1),
    )(y, dxo, w_glu, b_glu, gain_post)


def _s5_scan_bwd(x_il, dy, h_r, h_i, gain_pre, d_skip, ab_r, ab_i, bbc_r, bbc_i, cc_r, cc_in, li, host=None):
    bsz, nblk, tq, d = x_il.shape
    nkb, gch, ks = bbc_r.shape
    kc = d // nkb
    n_rep = kc // gch
    nj = tq // 8
    h_in, h_shape, h_out, h_scr, h_ops = _host_args(host)

    def body(xf_ref, xk_ref, dy_ref, hr_ref, hi_ref, gk_ref, dk_ref, ar_ref, ai_ref, bbr_ref, bbi_ref, cr_ref, ci_ref,
             *rest):
        o0 = len(h_in)
        du_ref, dbbr_ref, dbbi_ref, dcr_ref, dci_ref, dar_ref, dai_ref, dd_ref = rest[o0:o0 + 8]
        s0 = o0 + 8 + len(h_out)
        (gr_sc, gi_sc, hr_sc, hi_sc, pw_r, pw_i, carry_r, carry_i, acc_r, acc_i,
         bbr_sc, bbi_sc, crt_sc, cit_sc, dbbr_acc, dbbi_acc, dcr_acc, dci_acc) = rest[s0:s0 + 18]
        k, b, n = pl.program_id(0), pl.program_id(1), pl.program_id(2)
        finish = _hosted(host, rest[:o0], rest[o0 + 8:s0], rest[s0 + 18:],
                         (k == 0) & (b == 0) & (n == 0), (k == nkb - 1) & (b == bsz - 1) & (n == nblk - 1))
        ar = jnp.broadcast_to(ar_ref[...], (8, ks))
        ai = jnp.broadcast_to(-ai_ref[...], (8, ks))

        @pl.when((b == 0) & (n == 0))
        def _():
            bbr_sc[...] = _expand_block_diag(bbr_ref[...], n_rep)
            bbi_sc[...] = _expand_block_diag(bbi_ref[...], n_rep)
            crt_sc[...] = _expand_block_diag(cr_ref[...], n_rep)
            cit_sc[...] = _expand_block_diag(ci_ref[...], n_rep)
            _fill_powers(pw_r, pw_i, ar, ai, nj, ks)
            for ref in (dar_ref, dai_ref, dd_ref, acc_r, acc_i, dbbr_acc, dbbi_acc, dcr_acc, dci_acc):
                ref[...] = jnp.zeros_like(ref)

        @pl.when(n == 0)
        def _():
            carry_r[...] = jnp.zeros_like(carry_r)
            carry_i[...] = jnp.zeros_like(carry_i)

        u = xk_ref[...] * _rms_scale(xf_ref[...]) * gk_ref[...]
        ub = u.astype(bf16)
        dyv = dy_ref[...]
        dyb = dyv.astype(bf16)
        dd_ref[0:1, :] += jnp.sum(dyv * u, axis=0, keepdims=True)
        hrb, hib = hr_ref[...], hi_ref[...]
        dcr_acc[...] += _dot_tn(dyb, hrb)
        dci_acc[...] += _dot_tn(dyb, hib)
        hr_sc[...] = hrb.astype(f32).reshape(nj, 8, ks)
        hi_sc[...] = hib.astype(f32).reshape(nj, 8, ks)
        gr_sc[...] = _dot(dyb, crt_sc[...]).reshape(nj, 8, ks)
        gi_sc[...] = _dot(dyb, cit_sc[...]).reshape(nj, 8, ks)
        accs = _interleaved_scan(gr_sc, gi_sc, ar, ai, pw_r, pw_i, carry_r, carry_i, nj, ks, reverse=True,
                                 h_sc=(hr_sc, hi_sc))
        cb = min(S5_CB, ks)
        for q, (a_r, a_i) in enumerate(accs):
            acc_r[:, q * cb:(q + 1) * cb] += a_r
            acc_i[:, q * cb:(q + 1) * cb] += a_i
        grb = gr_sc[...].reshape(tq, ks).astype(bf16)
        gib = gi_sc[...].reshape(tq, ks).astype(bf16)
        dbbr_acc[...] += _dot_tn(ub, grb)
        dbbi_acc[...] += _dot_tn(ub, gib)
        bbr, bbi, dk = bbr_sc[...], bbi_sc[...], dk_ref[...]
        for r0 in range(0, tq, tq // 2):
            rows = slice(r0, r0 + tq // 2)
            du_ref[rows, :] = dyv[rows, :] * dk + _dot_nt(grb[rows, :], bbr) + _dot_nt(gib[rows, :], bbi)

        @pl.when((b == bsz - 1) & (n == nblk - 1))
        def _():
            dar_ref[0:1, :] = jnp.sum(acc_r[...], axis=0, keepdims=True)
            dai_ref[0:1, :] = jnp.sum(acc_i[...], axis=0, keepdims=True)
            dbbr_ref[...] = _compact_block_diag(dbbr_acc[...], n_rep)
            dbbi_ref[...] = _compact_block_diag(dbbi_acc[...], n_rep)
            dcr_ref[...] = _compact_block_diag(dcr_acc[...], n_rep)
            dci_ref[...] = _compact_block_diag(dci_acc[...], n_rep)

        finish()

    full = pl.BlockSpec((None, None, tq, d), lambda k, b, n: (b, nblk - 1 - n, 0, 0))
    chan = pl.BlockSpec((None, None, tq, kc), lambda k, b, n: (b, nblk - 1 - n, 0, k))
    stat = pl.BlockSpec((None, None, tq, ks), lambda k, b, n: (b, nblk - 1 - n, 0, k))
    vec_c = pl.BlockSpec((1, kc), lambda k, b, n: (0, k))
    vec_s = pl.BlockSpec((1, ks), lambda k, b, n: (0, k))
    cmap = pl.BlockSpec((None, gch, ks), lambda k, b, n: (k, 0, 0))
    acc_s = pl.BlockSpec((None, 8, ks), lambda k, b, n: (k, 0, 0))
    acc_c = pl.BlockSpec((None, 8, kc), lambda k, b, n: (k, 0, 0))
    cshape = SDS((nkb, gch, ks), f32)
    return pl.pallas_call(
        body, name=f"s5_scan_bwd_{li}", grid=(nkb, bsz, nblk),
        out_shape=(SDS((bsz, nblk, tq, d), f32), cshape, cshape, cshape, cshape, SDS((nkb, 8, ks), f32),
                   SDS((nkb, 8, ks), f32), SDS((nkb, 8, kc), f32), *h_shape),
        in_specs=[full, chan, chan, stat, stat, vec_c, vec_c, vec_s, vec_s, cmap, cmap, cmap, cmap, *h_in],
        out_specs=(chan, cmap, cmap, cmap, cmap, acc_s, acc_s, acc_c, *h_out),
        scratch_shapes=([pltpu.VMEM((nj, 8, ks), f32)] * 6 + [pltpu.VMEM((8, ks), f32)] * 4
                        + [pltpu.VMEM((kc, ks), bf16)] * 4 + [pltpu.VMEM((kc, ks), f32)] * 4 + h_scr),
        compiler_params=_cparams(3, side_effects=bool(host)),
    )(x_il, x_il, dy, h_r, h_i, gain_pre, d_skip, ab_r, ab_i, bbc_r, bbc_i, cc_r, cc_in, *h_ops)


def _norm_residual_bwd(x, du, dxo, gain, name):
    t, d = x.shape
    tr = min(TOK_TR, t)

    def body(x_ref, du_ref, dxo_ref, g_ref, dx_ref, dg_ref):
        @pl.when(pl.program_id(0) == 0)
        def _():
            dg_ref[...] = jnp.zeros_like(dg_ref)

        dxp, dgain = _rms_bwd(x_ref[...], g_ref[...], du_ref[...])
        dx_ref[...] = dxo_ref[...] + dxp
        dg_ref[0:1, :] += dgain

    tok = pl.BlockSpec((tr, d), lambda i: (i, 0))
    return pl.pallas_call(
        body, name=name, grid=(t // tr,), out_shape=(SDS((t, d), f32), SDS((8, d), f32)),
        in_specs=[tok, tok, tok, pl.BlockSpec((1, d), lambda i: (0, 0))],
        out_specs=(tok, pl.BlockSpec((8, d), lambda i: (0, 0))), compiler_params=_cparams(1),
    )(x, du, dxo, gain)


def _loss_head(y, target):
    t, d = y.shape
    tr = min(TOK_TR, t)

    def body(y_ref, t_ref, l_ref, dy_ref):
        @pl.when(pl.program_id(0) == 0)
        def _():
            l_ref[...] = jnp.zeros_like(l_ref)

        err = y_ref[...] - t_ref[...]
        dy_ref[...] = err * (1.0 / d)
        l_ref[...] += jnp.sum(jnp.sum(err * err, axis=1, keepdims=True), axis=0, keepdims=True)

    tok = pl.BlockSpec((tr, d), lambda i: (i, 0))
    return pl.pallas_call(
        body, name="loss_head", grid=(t // tr,), out_shape=(SDS((8, 128), f32), SDS((t, d), f32)),
        in_specs=[tok, tok], out_specs=(pl.BlockSpec((8, 128), lambda i: (0, 0)), tok), compiler_params=_cparams(1),
    )(y, target)


def _interleave(a, tq):
    bsz, l, d = a.shape
    return a.reshape(bsz, l // tq, 8, tq // 8, d).transpose(0, 1, 3, 2, 4).reshape(bsz, l // tq, tq, d)


def _deinterleave(a, l):
    bsz, nblk, tq, d = a.shape
    return a.reshape(bsz, nblk, tq // 8, 8, d).transpose(0, 1, 3, 2, 4).reshape(bsz, l, d)


def _compact_maps(a, nkb):
    ns, g, h, p = a.shape
    gl = g // nkb
    return a.reshape(ns, nkb, gl, h, p).transpose(0, 1, 3, 2, 4).reshape(ns, nkb, h, gl * p)


def _uncompact_maps(a, g):
    nkb, h, cols = a.shape
    gl = g // nkb
    return a.reshape(nkb, h, gl, cols // gl).transpose(0, 2, 1, 3).reshape(g, h, cols // gl)


def _pack_rows(arrs, cols, row_mult):
    flat = jnp.concatenate([a.reshape(-1).astype(f32) for a in arrs])
    rows = -(-flat.shape[0] // cols)
    rows = -(-rows // row_mult) * row_mult
    return jnp.pad(flat, (0, rows * cols - flat.shape[0])).reshape(rows, cols)


def _unpack_rows(buf, shapes):
    flat = buf.reshape(-1)
    out, off = [], 0
    for s in shapes:
        n = math.prod(s)
        out.append(flat[off:off + n].reshape(s))
        off += n
    return out


W_NAMES = ['s5_lambda_re', 's5_lambda_im', 's5_log_dt', 's5_b_re', 's5_b_im', 's5_c_re', 's5_c_im', 's5_d', 's5_w_glu',
           's5_b_glu', 'pool_w', 'pool_scale', 'ffn_w_gate', 'ffn_w_val', 'ffn_conv_w', 'ffn_conv_b', 'ffn_w_down',
           'norm_mix_pre', 'norm_mix_post', 'norm_ffn_pre', 'norm_ffn_post']
BIG = ('s5_w_glu', 'ffn_w_gate', 'ffn_w_val', 'ffn_w_down')


def kernel(x, s5_lambda_re, s5_lambda_im, s5_log_dt, s5_b_re, s5_b_im, s5_c_re, s5_c_im, s5_d, s5_w_glu, s5_b_glu, pool_w, pool_scale, ffn_w_gate, ffn_w_val, ffn_conv_w, ffn_conv_b, ffn_w_down, norm_mix_pre, norm_mix_post, norm_ffn_pre, norm_ffn_post, loss_target, m_s5_lambda_re, m_s5_lambda_im, m_s5_log_dt, m_s5_b_re, m_s5_b_im, m_s5_c_re, m_s5_c_im, m_s5_d, m_s5_w_glu, m_s5_b_glu, m_pool_w, m_pool_scale, m_ffn_w_gate, m_ffn_w_val, m_ffn_conv_w, m_ffn_conv_b, m_ffn_w_down, m_norm_mix_pre, m_norm_mix_post, m_norm_ffn_pre, m_norm_ffn_post, v_s5_lambda_re, v_s5_lambda_im, v_s5_log_dt, v_s5_b_re, v_s5_b_im, v_s5_c_re, v_s5_c_im, v_s5_d, v_s5_w_glu, v_s5_b_glu, v_pool_w, v_pool_scale, v_ffn_w_gate, v_ffn_w_val, v_ffn_conv_w, v_ffn_conv_b, v_ffn_w_down, v_norm_mix_pre, v_norm_mix_post, v_norm_ffn_pre, v_norm_ffn_post):
    w_in = dict(zip(W_NAMES, (s5_lambda_re, s5_lambda_im, s5_log_dt, s5_b_re, s5_b_im, s5_c_re, s5_c_im, s5_d, s5_w_glu,
                              s5_b_glu, pool_w, pool_scale, ffn_w_gate, ffn_w_val, ffn_conv_w, ffn_conv_b, ffn_w_down,
                              norm_mix_pre, norm_mix_post, norm_ffn_pre, norm_ffn_post)))
    m_in = dict(zip(W_NAMES, (m_s5_lambda_re, m_s5_lambda_im, m_s5_log_dt, m_s5_b_re, m_s5_b_im, m_s5_c_re, m_s5_c_im,
                              m_s5_d, m_s5_w_glu, m_s5_b_glu, m_pool_w, m_pool_scale, m_ffn_w_gate, m_ffn_w_val,
                              m_ffn_conv_w, m_ffn_conv_b, m_ffn_w_down, m_norm_mix_pre, m_norm_mix_post,
                              m_norm_ffn_pre, m_norm_ffn_post)))
    v_in = dict(zip(W_NAMES, (v_s5_lambda_re, v_s5_lambda_im, v_s5_log_dt, v_s5_b_re, v_s5_b_im, v_s5_c_re, v_s5_c_im,
                              v_s5_d, v_s5_w_glu, v_s5_b_glu, v_pool_w, v_pool_scale, v_ffn_w_gate, v_ffn_w_val,
                              v_ffn_conv_w, v_ffn_conv_b, v_ffn_w_down, v_norm_mix_pre, v_norm_mix_post,
                              v_norm_ffn_pre, v_norm_ffn_post)))

    bsz, seq, d = x.shape
    depth = ffn_w_gate.shape[0]
    n_s5, n_grp, n_state = s5_lambda_re.shape
    n_gch = s5_b_re.shape[3]
    n_pool = pool_w.shape[0]
    fs = ffn_w_gate.shape[2]
    f = N_CHIPS * fs
    gs = s5_w_glu.shape[1]
    nkb = d // S5_KB_CH
    tq = min(S5_TQ, seq)
    chip = 2 * lax.axis_index("x") + lax.axis_index("y")

    s_ffn = jnp.stack([ffn_w_gate.transpose(0, 2, 1), ffn_w_val.transpose(0, 2, 1), ffn_w_down], axis=1).astype(bf16)
    small_shard = [pool_w, pool_scale, ffn_conv_w]
    first_hosts = [("gather", 1, s_ffn[0]), ("gather", 1, s5_w_glu.astype(bf16)),
                   ("whole", 0, _pack_rows(small_shard, d, 16))]

    lam_r4 = s5_lambda_re.reshape(n_s5, n_grp, 1, n_state)
    lam_i4 = s5_lambda_im.reshape(n_s5, n_grp, 1, n_state)
    ldt4 = s5_log_dt.reshape(n_s5, n_grp, 1, 1)
    b_r4 = s5_b_re.transpose(0, 1, 3, 2)
    b_i4 = s5_b_im.transpose(0, 1, 3, 2)
    ab_r4, ab_i4, bb_r4, bb_i4 = _s5_discretise(lam_r4, lam_i4, ldt4, b_r4, b_i4)
    n_st_tot = n_grp * n_state
    ab_r, ab_i = ab_r4.reshape(n_s5, 1, n_st_tot), ab_i4.reshape(n_s5, 1, n_st_tot)
    bbc_r, bbc_i = _compact_maps(bb_r4, nkb), _compact_maps(bb_i4, nkb)
    cc_r, cc_in = _compact_maps(s5_c_re, nkb), _compact_maps(-s5_c_im, nkb)

    def s5_params(j):
        return dict(ab_r=ab_r[j], ab_i=ab_i[j], bb_r=bbc_r[j], bb_i=bbc_i[j], c_r=cc_r[j], c_in=cc_in[j],
                    d_skip=s5_d[j].reshape(1, d), b_glu=s5_b_glu[j].reshape(1, d))

    def row(a, i):
        return a[i].reshape(1, -1)

    assert depth >= 1 and N_CHIPS == 4
    saved = []
    xc = x
    w_layer = [None] * depth
    for i in range(depth):
        j = i // 2
        sv = dict(x_in=xc)
        if i % 2 == 0:
            sp = s5_params(j)
            x_il = _interleave(xc, tq)
            res = _s5_scan_fwd(x_il, row(norm_mix_pre, i), sp["d_skip"], sp["ab_r"], sp["ab_i"], sp["bb_r"], sp["bb_i"],
                               sp["c_r"], sp["c_in"], i, host=first_hosts if i == 0 else None)
            y_il, h_r, h_i = res[:3]
            if i == 0:
                w_layer[0], g_glu, g_small = res[3:6]
                parts = [_unpack_rows(g_small[k], [a.shape for a in small_shard]) for k in range(N_CHIPS)]
                pool_w_full = jnp.concatenate([p[0] for p in parts], axis=2).astype(bf16)
                pool_scale_full = jnp.concatenate([p[1] for p in parts], axis=1)
                conv_w_full = jnp.concatenate([p[2] for p in parts], axis=2)
            sp["w_glu"] = g_glu[j]
            xo_il = _s5_glu_fwd(y_il.reshape(bsz * seq, d), x_il.reshape(bsz * seq, d), sp["w_glu"], sp["b_glu"],
                                row(norm_mix_post, i), i)
            x_mid = _deinterleave(xo_il.reshape(x_il.shape), seq)
            sv.update(sp=sp, x_il=x_il, y_il=y_il, h_r=h_r, h_i=h_i)
        else:
            x_mid = _pool_fwd(xc, row(norm_mix_pre, i), row(norm_mix_post, i), pool_w_full[j], row(pool_scale_full, j), i)
        res = _ffn_fwd(x_mid, row(norm_ffn_pre, i), row(norm_ffn_post, i), w_layer[i], conv_w_full, ffn_conv_b, i,
                       host=[("gather", 1, s_ffn[i + 1])] if i + 1 < depth else None)
        xc, g_sv, v_sv, f_sv = res[:4]
        if i + 1 < depth:
            w_layer[i + 1] = res[4]
        sv.update(x_mid=x_mid, g=g_sv, v=v_sv, f=f_sv)
        saved.append(sv)

    sq, dy = _loss_head(xc.reshape(bsz * seq, d), loss_target.reshape(bsz * seq, d))
    loss = lax.psum(sq[0, 0] * (0.5 / d), ("x", "y", "c"))

    dx = dy.reshape(bsz, seq, d)
    dw_layers = [None] * depth
    r_layers = [None] * depth
    g_small_params = {n: [None] * w_in[n].shape[0] for n in W_NAMES if n not in BIG}
    dglu = [None] * n_s5
    for i in range(depth - 1, -1, -1):
        j = i // 2
        sv = saved[i]
        res = _ffn_bwd_act(sv["x_mid"], sv["f"], sv["g"], sv["v"], dx, row(norm_ffn_pre, i), row(norm_ffn_post, i),
                           w_layer[i], conv_w_full, ffn_conv_b, i,
                           host=[("slab", 1, dw_layers[i + 1])] if i + 1 < depth else None)
        dx, dg, dv, hdn, hb, dfb, dgain, dconv = res[:8]
        if i + 1 < depth:
            r_layers[i + 1] = res[8]
        dw_layers[i] = _ffn_bwd_weights(dg, dv, hdn, hb.reshape(bsz * seq, d), dfb.reshape(bsz * seq, d), i)
        g_small_params["norm_ffn_pre"][i] = dgain[0]
        g_small_params["norm_ffn_post"][i] = dgain[1]
        dconv = dconv.transpose(1, 0, 2).reshape(8, f)
        g_small_params["ffn_conv_w"][i] = dconv[0:3]
        g_small_params["ffn_conv_b"][i] = dconv[3]
        if i % 2 == 0:
            sp = sv["sp"]
            dxo_il = _interleave(dx, tq)
            dy_s, dglu[j], ds_glu = _s5_glu_bwd(sv["y_il"].reshape(bsz * seq, d), dxo_il.reshape(bsz * seq, d),
                                                sp["w_glu"], sp["b_glu"], row(norm_mix_post, i), i)
            res = _s5_scan_bwd(sv["x_il"], dy_s.reshape(sv["x_il"].shape), sv["h_r"], sv["h_i"], row(norm_mix_pre, i),
                               sp["d_skip"], sp["ab_r"], sp["ab_i"], sp["bb_r"], sp["bb_i"], sp["c_r"], sp["c_in"], i,
                               host=[("slab", 1, dw_layers[0])] if i == 0 else None)
            du, dbb_r, dbb_i, dc_r, dc_in, dab_r, dab_i, dd = res[:8]
            if i == 0:
                r_layers[0] = res[8]
            dx_il, dgpre = _norm_residual_bwd(sv["x_il"].reshape(bsz * seq, d), du.reshape(bsz * seq, d),
                                              dxo_il.reshape(bsz * seq, d), row(norm_mix_pre, i), f"s5_pre_bwd_{i}")
            dx = _deinterleave(dx_il.reshape(sv["x_il"].shape), seq)
            g_small_params["norm_mix_pre"][i] = dgpre[0]
            g_small_params["norm_mix_post"][i] = ds_glu[0]
            g_small_params["s5_b_glu"][j] = ds_glu[1]
            g_small_params["s5_d"][j] = dd[:, 0, :].reshape(d)
            g_small_params["s5_c_re"][j] = _uncompact_maps(dc_r, n_grp)
            g_small_params["s5_c_im"][j] = -_uncompact_maps(dc_in, n_grp)
            sv["g_ab"] = (dab_r[:, 0, :].reshape(n_grp, 1, n_state), dab_i[:, 0, :].reshape(n_grp, 1, n_state),
                          _uncompact_maps(dbb_r, n_grp), _uncompact_maps(dbb_i, n_grp))
        else:
            dx, dpw, dsm = _pool_bwd(sv["x_in"], dx, row(norm_mix_pre, i), row(norm_mix_post, i), pool_w_full[j],
                                     row(pool_scale_full, j), i)
            g_small_params["norm_mix_pre"][i] = dsm[0]
            g_small_params["norm_mix_post"][i] = dsm[1]
            g_small_params["pool_scale"][j] = dsm[2]
            g_small_params["pool_w"][j] = dpw
    grad_x = dx

    g_ab = [saved[2 * j]["g_ab"] for j in range(n_s5)]
    d_lr, d_li, d_ld, d_br, d_bi = _s5_discretise_bwd(
        lam_r4, lam_i4, ldt4, b_r4, b_i4, jnp.stack([g[0] for g in g_ab]), jnp.stack([g[1] for g in g_ab]),
        jnp.stack([g[2] for g in g_ab]), jnp.stack([g[3] for g in g_ab]))
    small_full = {n: (jnp.stack(v) if v[0] is not None else None) for n, v in g_small_params.items()}
    small_full["s5_lambda_re"] = d_lr.reshape(n_s5, n_grp, n_state)
    small_full["s5_lambda_im"] = d_li.reshape(n_s5, n_grp, n_state)
    small_full["s5_log_dt"] = d_ld.reshape(n_s5, n_grp)
    small_full["s5_b_re"] = d_br.transpose(0, 1, 3, 2)
    small_full["s5_b_im"] = d_bi.transpose(0, 1, 3, 2)
    bc_names = ["s5_b_re", "s5_b_im", "s5_c_re", "s5_c_im"]
    small_names = [n for n in W_NAMES if n not in BIG and n not in bc_names and n != "pool_w"]
    q = _pack_rows([small_full[n] for n in small_names], d, 16)
    q_bc = _pack_rows([small_full[n] for n in bc_names], d, 16).astype(bf16)

    r_glu, r_pw, r_q, r_bc = _chip_exchange(
        [("slab", 1, jnp.stack(dglu).astype(bf16)), ("slab", 2, small_full["pool_w"]), ("whole", 0, q), ("whole", 0, q_bc)],
        "exchange_small_grads")
    p_ffn = _sum_slots_layers([r.reshape(N_CHIPS, 3 * fs, d) for r in r_layers], "sum_slots_ffn").reshape(depth * 3 * fs, d)
    p_glu = _sum_slots(r_glu.reshape(N_CHIPS, n_s5 * gs, d), "sum_slots_glu")
    pw_rows, pw_cols = math.prod(pool_w.shape[:3]), pool_w.shape[3]
    p_pw = _sum_slots(r_pw.reshape(N_CHIPS, pw_rows, pw_cols), "sum_slots_pool_w")
    p_q = _sum_slots(r_q, "sum_slots_small")
    p_bc = _sum_slots(r_bc, "sum_slots_bc")
    o_ffn, o_glu, o_pw, o_q, o_bc = _exchange_with_sibling([p_ffn, p_glu, p_pw, p_q, p_bc])

    outs = {}

    def put(name, res, shape):
        outs[name] = tuple(r.reshape(shape) for r in res)

    pf, of = p_ffn.reshape(depth, 3, fs, d), o_ffn.reshape(depth, 3, fs, d)
    for kind, name in ((0, "ffn_w_gate"), (1, "ffn_w_val")):
        parts_g = [a[:, kind].transpose(0, 2, 1) for a in (pf, of)]
        put(name, _adamw(parts_g, w_in[name], m_in[name], v_in[name], f"adamw_{name}"), w_in[name].shape)
    put("ffn_w_down", _adamw([pf[:, 2].reshape(depth * fs, d), of[:, 2].reshape(depth * fs, d)],
                             ffn_w_down.reshape(depth * fs, d), m_ffn_w_down.reshape(depth * fs, d),
                             v_ffn_w_down.reshape(depth * fs, d), "adamw_ffn_w_down"), ffn_w_down.shape)
    put("s5_w_glu", _adamw([p_glu, o_glu], s5_w_glu.reshape(n_s5 * gs, d), m_s5_w_glu.reshape(n_s5 * gs, d),
                           v_s5_w_glu.reshape(n_s5 * gs, d), "adamw_s5_w_glu"), s5_w_glu.shape)

    q_tot, bc_tot, pw_tot = _add_pairs([(p_q, o_q), (p_bc, o_bc), (p_pw, o_pw)], "sum_small")
    g_small = dict(zip(small_names, _unpack_rows(q_tot, [small_full[n].shape for n in small_names])))
    g_small.update(zip(bc_names, _unpack_rows(bc_tot, [small_full[n].shape for n in bc_names])))
    g_small["pool_w"] = pw_tot.reshape(pool_w.shape)
    g_small["pool_scale"] = lax.dynamic_slice_in_dim(g_small["pool_scale"], chip * pool_scale.shape[1],
                                                     pool_scale.shape[1], axis=1)
    g_small["ffn_conv_w"] = lax.dynamic_slice_in_dim(g_small["ffn_conv_w"], chip * fs, fs, axis=2)
    all_small = [n for n in W_NAMES if n not in BIG]
    local_shapes = [w_in[n].shape for n in all_small]
    res = _adamw([_pack_rows([g_small[n] for n in all_small], d, 64)],
                 _pack_rows([w_in[n] for n in all_small], d, 64), _pack_rows([m_in[n] for n in all_small], d, 64),
                 _pack_rows([v_in[n] for n in all_small], d, 64), "adamw_small")
    unpacked = [_unpack_rows(r, local_shapes) for r in res]
    for idx, n in enumerate(all_small):
        outs[n] = (g_small[n], unpacked[1][idx], unpacked[2][idx], unpacked[3][idx])

    return (loss, grad_x, *[outs[n][0] for n in W_NAMES], *[outs[n][1] for n in W_NAMES],
            *[outs[n][2] for n in W_NAMES], *[outs[n][3] for n in W_NAMES])
```

```python
import math

import jax
import jax.numpy as jnp
from jax import lax
from jax.experimental import pallas as pl
from jax.experimental.pallas import tpu as pltpu

f32, bf16 = jnp.float32, jnp.bfloat16
SDS = jax.ShapeDtypeStruct
MESH = pl.DeviceIdType.MESH

RMS_EPS = 1e-6
GELU_C = math.sqrt(2.0 / math.pi)
GELU_K = 0.044715
ADAM_LR, ADAM_B1, ADAM_B2, ADAM_EPS, ADAM_WD, ADAM_STEP = 0.001, 0.9, 0.999, 1e-08, 0.01, 10
POOL_WINDOWS = (2, 4, 8, 16)
POOL_HALO = 16
S5_GROUP_CH = 16
S5_STATE = 64
S5_KB_CH = 256
N_CHIPS = 4

FFN_TL = 512
FFN_FT = 256
FFN_CH = 16
FFN_SPLIT = 2
WG_TR = 2048
POOL_TL = 512
S5_TQ = 512
S5_CB = 1024
TOK_TR = 512
VMEM_LIMIT = 56 * 1024 * 1024

HBM = pl.BlockSpec(memory_space=pltpu.HBM)


def _cparams(n_axes=0, side_effects=False):
    kw = dict(vmem_limit_bytes=VMEM_LIMIT)
    if n_axes:
        kw["dimension_semantics"] = ("arbitrary",) * n_axes
    if side_effects:
        kw["has_side_effects"] = True
    return pltpu.CompilerParams(**kw)


def _dot(a, b):
    return jnp.dot(a, b, preferred_element_type=f32)


def _dot_nt(a, b):
    return lax.dot_general(a, b, (((1,), (1,)), ((), ())), preferred_element_type=f32)


def _dot_tn(a, b):
    return lax.dot_general(a, b, (((0,), (0,)), ((), ())), preferred_element_type=f32)


def _rms_scale(x):
    return lax.rsqrt(jnp.mean(x * x, axis=-1, keepdims=True) + RMS_EPS)


def _rms_fwd(x, gain):
    return x * _rms_scale(x) * gain


def _rms_bwd(x, gain, dy):
    r = _rms_scale(x)
    xn = x * r
    dgain = jnp.sum(dy * xn, axis=0, keepdims=True)
    dxn = dy * gain
    dx = r * (dxn - xn * jnp.mean(dxn * xn, axis=-1, keepdims=True))
    return dx, dgain


def _gelu(x):
    t = jnp.tanh(x * (GELU_C + (GELU_C * GELU_K) * (x * x)))
    hx = 0.5 * x
    return hx + hx * t


def _gelu_and_grad(x):
    x2 = x * x
    t = jnp.tanh(x * (GELU_C + (GELU_C * GELU_K) * x2))
    hx = 0.5 * x
    return hx + hx * t, (0.5 + 0.5 * t) + hx * (1.0 - t * t) * (GELU_C + (3.0 * GELU_C * GELU_K) * x2)


def _sigmoid(x):
    return 1.0 / (1.0 + jnp.exp(-x))


def _cmul(ar, ai, br, bi):
    return ar * br - ai * bi, ar * bi + ai * br


def _row_block(n, cap):
    best = None
    for d in range(16, min(n, cap) + 1, 16):
        if n % d == 0:
            best = d
    assert best is not None, n
    return best


def _mesh_pos():
    return lax.axis_index("x"), lax.axis_index("y"), lax.axis_index("c")


def _other_chips(x, y):
    return [(1 - x, y), (x, 1 - y), (1 - x, 1 - y)]


def _slab_index(ndim, axis, start, size):
    return tuple(pl.ds(start, size) if a == axis else slice(None) for a in range(ndim))


class _ChipExchange:
    def __init__(self, kind, axis, src, dst, send_sems, recv_sems, loc_sem):
        x, y, c = _mesh_pos()
        me = 2 * x + y
        nd = len(src.shape)
        size = src.shape[axis] if kind == "gather" else src.shape[axis] // N_CHIPS

        def src_for(kk):
            return src.at[_slab_index(nd, axis, kk * size, size)] if kind == "slab" else src

        def dst_for(kk):
            return dst.at[_slab_index(nd, axis, kk * size, size)] if kind == "gather" else dst.at[kk]

        self.own = pltpu.make_async_copy(src_for(me), dst_for(me), loc_sem)
        self.sends, self.recvs = [], []
        for j, chip in enumerate(_other_chips(x, y)):
            kk = 2 * chip[0] + chip[1]
            peer = dict(send_sem=send_sems.at[j], recv_sem=recv_sems.at[j], device_id=(chip[0], chip[1], c),
                        device_id_type=MESH)
            self.sends.append(pltpu.make_async_remote_copy(src_ref=src_for(kk), dst_ref=dst_for(me), **peer))
            self.recvs.append(pltpu.make_async_remote_copy(src_ref=src_for(me), dst_ref=dst_for(kk), **peer))

    def start(self):
        self.own.start()
        for cp in self.sends:
            cp.start()

    def finish(self):
        for cp in self.recvs:
            cp.wait_recv()
        for cp in self.sends:
            cp.wait_send()
        self.own.wait()


def _exchange_out_shape(kind, axis, src):
    if kind == "gather":
        shape = tuple(N_CHIPS * n if a == axis else n for a, n in enumerate(src.shape))
    elif kind == "slab":
        shape = (N_CHIPS,) + tuple(n // N_CHIPS if a == axis else n for a, n in enumerate(src.shape))
    else:
        shape = (N_CHIPS,) + tuple(src.shape)
    return SDS(shape, src.dtype)


EXCHANGE_SEMS = [pltpu.SemaphoreType.DMA((3,)), pltpu.SemaphoreType.DMA((3,)), pltpu.SemaphoreType.DMA((1,))]


def _chip_exchange(items, name):
    n = len(items)

    def body(*refs):
        ins, outs, sems = refs[:n], refs[n:2 * n], refs[2 * n:]
        exs = [_ChipExchange(kind, axis, ins[i], outs[i], sems[3 * i], sems[3 * i + 1], sems[3 * i + 2].at[0])
               for i, (kind, axis, _) in enumerate(items)]
        for ex in exs:
            ex.start()
        for ex in exs:
            ex.finish()

    return pl.pallas_call(
        body, name=name, out_shape=tuple(_exchange_out_shape(k, a, arr) for k, a, arr in items),
        in_specs=[HBM] * n, out_specs=tuple([HBM] * n), scratch_shapes=EXCHANGE_SEMS * n,
        compiler_params=_cparams(side_effects=True),
    )(*[arr for _, _, arr in items])


def _hosted(hosts, host_ins, host_outs, sems, first, last):
    if not hosts:
        return lambda: None
    exs = [_ChipExchange(h[0], h[1], host_ins[i], host_outs[i], sems[3 * i], sems[3 * i + 1], sems[3 * i + 2].at[0])
           for i, h in enumerate(hosts)]

    @pl.when(first)
    def _():
        for ex in exs:
            ex.start()

    def finish():
        @pl.when(last)
        def _():
            for ex in exs:
                ex.finish()

    return finish


def _host_args(hosts):
    hosts = hosts or []
    n = len(hosts)
    return [HBM] * n, [_exchange_out_shape(*h) for h in hosts], [HBM] * n, EXCHANGE_SEMS * n, [h[2] for h in hosts]


def _exchange_with_sibling(parts):
    n = len(parts)

    def body(*refs):
        ins, outs = refs[:n], refs[n:2 * n]
        send_sems, recv_sems = refs[2 * n:]
        x, y, c = _mesh_pos()
        cps = [pltpu.make_async_remote_copy(src_ref=s, dst_ref=t, send_sem=send_sems.at[i], recv_sem=recv_sems.at[i],
                                            device_id=(x, y, 1 - c), device_id_type=MESH)
               for i, (s, t) in enumerate(zip(ins, outs))]
        for cp in cps:
            cp.start()
        for cp in cps:
            cp.wait()

    return pl.pallas_call(
        body, name="exchange_with_sibling",
        out_shape=tuple(SDS(p.shape, p.dtype) for p in parts),
        in_specs=[HBM] * n, out_specs=tuple([HBM] * n),
        scratch_shapes=[pltpu.SemaphoreType.DMA((n,)), pltpu.SemaphoreType.DMA((n,))],
        compiler_params=_cparams(side_effects=True),
    )(*parts)


def _sum_slots(r, name):
    _, rows, cols = r.shape
    tr = _row_block(rows, 512)

    def body(r_ref, o_ref):
        o_ref[...] = ((r_ref[0].astype(f32) + r_ref[1].astype(f32)) + r_ref[2].astype(f32)) + r_ref[3].astype(f32)

    return pl.pallas_call(
        body, name=name, grid=(rows // tr,), out_shape=SDS((rows, cols), f32),
        in_specs=[pl.BlockSpec((N_CHIPS, tr, cols), lambda i: (0, i, 0))],
        out_specs=pl.BlockSpec((tr, cols), lambda i: (i, 0)),
        compiler_params=_cparams(1),
    )(r)


def _sum_slots_layers(rs, name):
    _, rows, cols = rs[0].shape
    nl = len(rs)
    tr = _row_block(rows, 256)

    def body(*refs):
        o_ref = refs[nl]
        for l in range(nl):
            r = refs[l]
            o_ref[l] = ((r[0].astype(f32) + r[1].astype(f32)) + r[2].astype(f32)) + r[3].astype(f32)

    return pl.pallas_call(
        body, name=name, grid=(rows // tr,), out_shape=SDS((nl, rows, cols), f32),
        in_specs=[pl.BlockSpec((N_CHIPS, tr, cols), lambda i: (0, i, 0))] * nl,
        out_specs=pl.BlockSpec((nl, tr, cols), lambda i: (0, i, 0)),
        compiler_params=_cparams(1),
    )(*rs)


def _add_pairs(pairs, name):
    n = len(pairs)

    def body(*refs):
        for i in range(n):
            refs[2 * n + i][...] = refs[2 * i][...] + refs[2 * i + 1][...]

    return pl.pallas_call(body, name=name, out_shape=tuple(SDS(a.shape, f32) for a, _ in pairs),
                          compiler_params=_cparams())(*[t for pair in pairs for t in pair])


def _adamw(g_parts, w, m, v, name):
    rows, cols = w.shape[-2:]
    tr = _row_block(rows, 512)
    n_g = len(g_parts)
    c1 = 1.0 / (1.0 - ADAM_B1 ** ADAM_STEP)
    c2 = 1.0 / (1.0 - ADAM_B2 ** ADAM_STEP)

    def body(*refs):
        g_refs = refs[:n_g]
        w_ref, m_ref, v_ref, go_ref, d_ref, mo_ref, vo_ref = refs[n_g:]
        g = g_refs[0][...]
        for r in g_refs[1:]:
            g = g + r[...]
        mn = ADAM_B1 * m_ref[...] + (1.0 - ADAM_B1) * g
        vn = ADAM_B2 * v_ref[...] + (1.0 - ADAM_B2) * (g * g)
        go_ref[...] = g
        mo_ref[...] = mn
        vo_ref[...] = vn
        d_ref[...] = -ADAM_LR * ((mn * c1) / (jnp.sqrt(vn * c2) + ADAM_EPS) + ADAM_WD * w_ref[...])

    if w.ndim == 2:
        grid, spec = (rows // tr,), pl.BlockSpec((tr, cols), lambda i: (i, 0))
    else:
        grid, spec = (w.shape[0], rows // tr), pl.BlockSpec((None, tr, cols), lambda a, i: (a, i, 0))
    return pl.pallas_call(
        body, name=name, grid=grid, out_shape=tuple(SDS(w.shape, f32) for _ in range(4)),
        in_specs=[spec] * (n_g + 3), out_specs=(spec,) * 4, compiler_params=_cparams(len(grid)),
    )(*g_parts, w, m, v)


def _resident_spec(shape, n_grid):
    zeros = (0,) * len(shape)
    return pl.BlockSpec(tuple(shape), {2: lambda a, b: zeros, 3: lambda a, b, c: zeros}[n_grid],
                        pipeline_mode=pl.Buffered(1))


def _weight_tiles(w_ref, j, ft):
    rows = pl.ds(j * ft if isinstance(j, int) else pl.multiple_of(j * ft, ft), ft)
    return w_ref[0, rows, :], w_ref[1, rows, :], w_ref[2, rows, :]


def _shifted_rows(ext):
    return ext[8:, :], pltpu.roll(ext, 1, axis=0)[8:, :], pltpu.roll(ext, 2, axis=0)[8:, :]


def _later_rows(ext):
    n = ext.shape[0]
    return pltpu.roll(ext, n - 1, axis=0)[0:n - 8, :], pltpu.roll(ext, n - 2, axis=0)[0:n - 8, :]


def _ffn_fwd(x_mid, gain_pre, gain_post, w_l, conv_w, conv_b, li, host=None):
    bsz, l, d = x_mid.shape
    f = w_l.shape[1]
    tl, ft = min(FFN_TL, l), min(FFN_FT, f)
    nt, nf = l // tl, f // ft
    n_sub = FFN_SPLIT if tl % (FFN_SPLIT * FFN_CH) == 0 else 1
    sub = tl // n_sub
    h_in, h_shape, h_out, h_scr, h_ops = _host_args(host)

    def body(x_ref, gpre_ref, gpost_ref, w_ref, cw_ref, cb_ref, *rest):
        xo_ref, g_ref, v_ref, f_ref = rest[len(h_in):len(h_in) + 4]
        s0 = len(h_in) + 4 + len(h_out)
        h_sc, facc, gprev = rest[s0:s0 + 3]
        b, t, j = pl.program_id(0), pl.program_id(1), pl.program_id(2)
        finish = _hosted(host, rest[:len(h_in)], rest[len(h_in) + 4:s0], rest[s0 + 3:],
                         (b == 0) & (t == 0) & (j == 0), (b == bsz - 1) & (t == nt - 1) & (j == nf - 1))

        @pl.when(j == 0)
        def _():
            h_sc[...] = _rms_fwd(x_ref[...], gpre_ref[...]).astype(bf16)
            facc[...] = jnp.zeros_like(facc)

        @pl.when(t == 0)
        def _():
            gprev[j] = jnp.zeros((8, ft), f32)

        wg, wv, wd = _weight_tiles(w_ref, j, ft)
        gs = [_dot_nt(h_sc[s * sub:(s + 1) * sub, :], wg) for s in range(n_sub)]
        vs = [_dot_nt(h_sc[s * sub:(s + 1) * sub, :], wv) for s in range(n_sub)]
        w0, w1, w2, bias = (jnp.broadcast_to(r, (FFN_CH, ft))
                            for r in (cw_ref[0:1, :], cw_ref[1:2, :], cw_ref[2:3, :], cb_ref[...]))
        hist = gprev[j]
        for s in range(n_sub):
            rows = slice(s * sub, (s + 1) * sub)
            g_ref[rows, :] = gs[s].astype(bf16)
            v_ref[rows, :] = vs[s].astype(bf16)
            hdn = []
            for r0 in range(0, sub, FFN_CH):
                g0, g1, g2 = _shifted_rows(jnp.concatenate([hist, gs[s][r0:r0 + FFN_CH, :]], axis=0))
                hist = g0[FFN_CH - 8:, :]
                hdn.append((_gelu(bias + w0 * g2 + w1 * g1 + w2 * g0) * vs[s][r0:r0 + FFN_CH, :]).astype(bf16))
            facc[rows, :] += _dot(jnp.concatenate(hdn, axis=0), wd)
        gprev[j] = hist

        @pl.when(j == nf - 1)
        def _():
            fv = facc[...]
            f_ref[...] = fv
            xo_ref[...] = x_ref[...] + _rms_fwd(fv, gpost_ref[...])

        finish()

    tok = pl.BlockSpec((None, tl, d), lambda b, t, j: (b, t, 0))
    hid = pl.BlockSpec((None, tl, ft), lambda b, t, j: (j, b * nt + t, 0))
    gain = pl.BlockSpec((1, d), lambda b, t, j: (0, 0))
    hid_shape = SDS((nf, bsz * l, ft), bf16)
    return pl.pallas_call(
        body, name=f"ffn_fwd_{li}", grid=(bsz, nt, nf),
        out_shape=(SDS((bsz, l, d), f32), hid_shape, hid_shape, SDS((bsz, l, d), f32), *h_shape),
        in_specs=[tok, gain, gain, _resident_spec(w_l.shape, 3),
                  pl.BlockSpec((None, 3, ft), lambda b, t, j: (li, 0, j)),
                  pl.BlockSpec((None, 1, ft), lambda b, t, j: (li, 0, j)), *h_in],
        out_specs=(tok, hid, hid, tok, *h_out),
        scratch_shapes=[pltpu.VMEM((tl, d), bf16), pltpu.VMEM((tl, d), f32), pltpu.VMEM((nf, 8, ft), f32), *h_scr],
        compiler_params=_cparams(3, side_effects=bool(host)),
    )(x_mid, gain_pre, gain_post, w_l, conv_w, conv_b.reshape(conv_b.shape[0], 1, f), *h_ops)


def _ffn_bwd_act(x_mid, f_sv, g_sv, v_sv, dxo, gain_pre, gain_post, w_l, conv_w, conv_b, li, host=None):
    bsz, l, d = x_mid.shape
    f = w_l.shape[1]
    tl, ft = min(FFN_TL, l), min(FFN_FT, f)
    nt, nf = l // tl, f // ft
    n_sub = FFN_SPLIT if tl % (FFN_SPLIT * FFN_CH) == 0 else 1
    sub = tl // n_sub
    h_in, h_shape, h_out, h_scr, h_ops = _host_args(host)

    def body(x_ref, f_ref, dxo_ref, g_ref, v_ref, gh_ref, gpre_ref, gpost_ref, w_ref, cw_ref, cb_ref, *rest):
        o0 = len(h_in)
        dx_ref, dg_ref, dv_ref, hdn_ref, h_ref, df_ref, dgain_ref, dconv_ref = rest[o0:o0 + 8]
        s0 = o0 + 8 + len(h_out)
        h_sc, df_sc, dh_acc, dgc_next = rest[s0:s0 + 4]
        b, t, j = pl.program_id(0), pl.program_id(1), pl.program_id(2)
        tt = nt - 1 - t
        finish = _hosted(host, rest[:o0], rest[o0 + 8:s0], rest[s0 + 4:],
                         (b == 0) & (t == 0) & (j == 0), (b == bsz - 1) & (t == nt - 1) & (j == nf - 1))

        @pl.when((b == 0) & (t == 0) & (j == 0))
        def _():
            dgain_ref[...] = jnp.zeros_like(dgain_ref)
            dconv_ref[...] = jnp.zeros_like(dconv_ref)

        @pl.when(j == 0)
        def _():
            hb = _rms_fwd(x_ref[...], gpre_ref[...]).astype(bf16)
            h_sc[...] = hb
            h_ref[...] = hb
            df, dgp = _rms_bwd(f_ref[...], gpost_ref[...], dxo_ref[...])
            dfb = df.astype(bf16)
            df_sc[...] = dfb
            df_ref[...] = dfb
            dgain_ref[1:2, :] += dgp
            dh_acc[...] = jnp.zeros_like(dh_acc)

        @pl.when(t == 0)
        def _():
            dgc_next[j] = jnp.zeros((8, ft), f32)

        wg, wv, wd = _weight_tiles(w_ref, j, ft)
        dhdns = [_dot_nt(df_sc[s * sub:(s + 1) * sub, :], wd) for s in range(n_sub)]
        w0, w1, w2, bias = (jnp.broadcast_to(r, (FFN_CH, ft))
                            for r in (cw_ref[0:1, :], cw_ref[1:2, :], cw_ref[2:3, :], cb_ref[...]))

        def fold(a):
            return a.reshape(FFN_CH // 8, 8, ft).sum(axis=0)

        def conv_bwd(dgc, after):
            d1, d2 = _later_rows(jnp.concatenate([dgc, after], axis=0))
            return (w2 * dgc + w1 * d1 + w0 * d2).astype(bf16)

        def flush(s, dgs, dvs):
            rows = slice(s * sub, (s + 1) * sub)
            dgb, dvb = jnp.concatenate(dgs, axis=0), jnp.concatenate(dvs, axis=0)
            dg_ref[rows, :] = dgb
            dv_ref[rows, :] = dvb
            dh_acc[rows, :] += _dot(dgb, wg) + _dot(dvb, wv)

        hist = jnp.where(tt > 0, gh_ref[...].astype(f32)[8:16, :], 0.0)
        acc = [jnp.zeros((8, ft), f32)] * 4
        first, pending, dgs, dvs_prev = None, None, [], None
        for s in range(n_sub):
            rows = slice(s * sub, (s + 1) * sub)
            g = g_ref[rows, :].astype(f32)
            v = v_ref[rows, :].astype(f32)
            hdn, dvs = [], []
            for r0 in range(0, sub, FFN_CH):
                g0, g1, g2 = _shifted_rows(jnp.concatenate([hist, g[r0:r0 + FFN_CH, :]], axis=0))
                hist = g0[FFN_CH - 8:, :]
                vc, dc = v[r0:r0 + FFN_CH, :], dhdns[s][r0:r0 + FFN_CH, :]
                u, ug = _gelu_and_grad(bias + w0 * g2 + w1 * g1 + w2 * g0)
                hdn.append((u * vc).astype(bf16))
                dvs.append((dc * u).astype(bf16))
                dgc = dc * vc * ug
                acc = [acc[0] + fold(dgc * g2), acc[1] + fold(dgc * g1), acc[2] + fold(dgc * g0), acc[3] + fold(dgc)]
                if pending is None:
                    first = dgc[0:8, :]
                else:
                    dgs.append(conv_bwd(pending, dgc[0:8, :]))
                    if r0 == 0:
                        flush(s - 1, dgs, dvs_prev)
                        dgs = []
                pending = dgc
            hdn_ref[rows, :] = jnp.concatenate(hdn, axis=0)
            dvs_prev = dvs
        dgs.append(conv_bwd(pending, dgc_next[j]))
        flush(n_sub - 1, dgs, dvs_prev)
        dgc_next[j] = first
        for k in range(4):
            dconv_ref[j, k:k + 1, :] += jnp.sum(acc[k], axis=0, keepdims=True)

        @pl.when(j == nf - 1)
        def _():
            dxp, dgp = _rms_bwd(x_ref[...], gpre_ref[...], dh_acc[...])
            dx_ref[...] = dxo_ref[...] + dxp
            dgain_ref[0:1, :] += dgp

        finish()

    tok = pl.BlockSpec((None, tl, d), lambda b, t, j: (b, nt - 1 - t, 0))
    hid = pl.BlockSpec((None, tl, ft), lambda b, t, j: (j, b * nt + nt - 1 - t, 0))
    halo = pl.BlockSpec((None, 16, ft),
                        lambda b, t, j: (j, jnp.maximum((b * nt + nt - 1 - t) * (tl // 16) - 1, 0), 0))
    gain = pl.BlockSpec((1, d), lambda b, t, j: (0, 0))
    hid_shape = SDS((nf, bsz * l, ft), bf16)
    return pl.pallas_call(
        body, name=f"ffn_bwd_act_{li}", grid=(bsz, nt, nf),
        out_shape=(SDS((bsz, l, d), f32), hid_shape, hid_shape, hid_shape,
                   SDS((bsz, l, d), bf16), SDS((bsz, l, d), bf16), SDS((8, d), f32), SDS((nf, 8, ft), f32), *h_shape),
        in_specs=[tok, tok, tok, hid, hid, halo, gain, gain, _resident_spec(w_l.shape, 3),
                  pl.BlockSpec((None, 3, ft), lambda b, t, j: (li, 0, j)),
                  pl.BlockSpec((None, 1, ft), lambda b, t, j: (li, 0, j)), *h_in],
        out_specs=(tok, hid, hid, hid, tok, tok,
                   pl.BlockSpec((8, d), lambda b, t, j: (0, 0)), pl.BlockSpec((nf, 8, ft), lambda b, t, j: (0, 0, 0)),
                   *h_out),
        scratch_shapes=[pltpu.VMEM((tl, d), bf16), pltpu.VMEM((tl, d), bf16), pltpu.VMEM((tl, d), f32),
                        pltpu.VMEM((nf, 8, ft), f32), *h_scr],
        compiler_params=_cparams(3, side_effects=bool(host)),
    )(x_mid, f_sv, dxo, g_sv, v_sv, g_sv, gain_pre, gain_post, w_l, conv_w,
      conv_b.reshape(conv_b.shape[0], 1, f), *h_ops)


def _ffn_bwd_weights(dg, dv, hdn, h, df, li):
    nf, t, ft = dg.shape
    f = nf * ft
    d = h.shape[1]
    tr = min(WG_TR, t)
    nr = t // tr

    def body(dg_ref, dv_ref, hdn_ref, h_ref, df_ref, o_ref, acc):
        r = pl.program_id(1)

        @pl.when(r == 0)
        def _():
            acc[...] = jnp.zeros_like(acc)

        rows = pl.ds(pl.multiple_of(r * tr, tr), tr)
        hb = h_ref[rows, :]
        acc[0] += _dot_tn(dg_ref[...], hb)
        acc[1] += _dot_tn(dv_ref[...], hb)
        acc[2] += _dot_tn(hdn_ref[...], df_ref[rows, :])

        @pl.when(r == nr - 1)
        def _():
            o_ref[...] = acc[...].astype(bf16)

    hid = pl.BlockSpec((None, tr, ft), lambda j, r: (j, r, 0))
    tok = _resident_spec((t, d), 2)
    return pl.pallas_call(
        body, name=f"ffn_bwd_weights_{li}", grid=(nf, nr), out_shape=SDS((3, f, d), bf16),
        in_specs=[hid, hid, hid, tok, tok], out_specs=pl.BlockSpec((3, ft, d), lambda j, r: (0, j, 0)),
        scratch_shapes=[pltpu.VMEM((3, ft, d), f32)], compiler_params=_cparams(2),
    )(dg, dv, hdn, h, df)


def _pool_counts(t0, tl, d):
    gch = d // len(POOL_WINDOWS)
    tpos = (t0 + lax.broadcasted_iota(jnp.int32, (tl, d), 0) + 1).astype(f32)
    lane = lax.broadcasted_iota(jnp.int32, (tl, d), 1)
    win = jnp.full((tl, d), float(POOL_WINDOWS[-1]), f32)
    for gi in range(len(POOL_WINDOWS) - 2, -1, -1):
        win = jnp.where(lane < (gi + 1) * gch, float(POOL_WINDOWS[gi]), win)
    return jnp.minimum(tpos, win)


def _pool_select(parts, tl, d):
    gch = d // len(POOL_WINDOWS)
    lane = lax.broadcasted_iota(jnp.int32, (tl, d), 1)
    out = parts[-1]
    for gi in range(len(parts) - 2, -1, -1):
        out = jnp.where(lane < (gi + 1) * gch, parts[gi], out)
    return out


def _pool_window_sums(u, halo, tl):
    ext = jnp.concatenate([halo, u], axis=0)
    sums, cur = [], ext
    for k in (1, 2, 4, 8):
        cur = cur + pltpu.roll(cur, k, axis=0)
        sums.append(cur[POOL_HALO:POOL_HALO + tl, :])
    return sums


def _pool_mix(u, halo, cnt, pw_ref, scale, tl, d):
    gch = d // len(POOL_WINDOWS)
    pooled = _pool_select(_pool_window_sums(u, halo, tl), tl, d) / cnt
    diff = pooled - u
    outs = [_dot(diff[:, gi * gch:(gi + 1) * gch].astype(bf16), pw_ref[gi]) for gi in range(len(POOL_WINDOWS))]
    return diff, jnp.concatenate(outs, axis=1)


def _pool_fwd(x, gain_pre, gain_post, pw, scale, li):
    bsz, l, d = x.shape
    tl = min(POOL_TL, l)
    nt = l // tl

    def body(x_ref, gpre_ref, gpost_ref, pw_ref, sc_ref, xo_ref, halo):
        t = pl.program_id(1)

        @pl.when(t == 0)
        def _():
            halo[...] = jnp.zeros_like(halo)

        xv = x_ref[...]
        u = _rms_fwd(xv, gpre_ref[...])
        _, out = _pool_mix(u, halo[...], _pool_counts(t * tl, tl, d), pw_ref, sc_ref[...], tl, d)
        halo[...] = u[tl - POOL_HALO:tl, :]
        xo_ref[...] = xv + _rms_fwd(out * sc_ref[...], gpost_ref[...])

    tok = pl.BlockSpec((None, tl, d), lambda b, t: (b, t, 0))
    gain = pl.BlockSpec((1, d), lambda b, t: (0, 0))
    return pl.pallas_call(
        body, name=f"pool_fwd_{li}", grid=(bsz, nt), out_shape=SDS((bsz, l, d), f32),
        in_specs=[tok, gain, gain, pl.BlockSpec(pw.shape, lambda b, t: (0, 0, 0)), gain], out_specs=tok,
        scratch_shapes=[pltpu.VMEM((POOL_HALO, d), f32)], compiler_params=_cparams(2),
    )(x, gain_pre, gain_post, pw, scale)


def _pool_bwd(x, dxo, gain_pre, gain_post, pw, scale, li):
    bsz, l, d = x.shape
    tl = min(POOL_TL, l)
    nt = l // tl
    ng = len(POOL_WINDOWS)
    gch = d // ng
    n_ext = tl + POOL_HALO

    def body(x_ref, xh_ref, dxo_ref, gpre_ref, gpost_ref, pw_ref, sc_ref, dx_ref, dpw_ref, ds_ref, qnext):
        b, t = pl.program_id(0), pl.program_id(1)
        tt = nt - 1 - t

        @pl.when((b == 0) & (t == 0))
        def _():
            dpw_ref[...] = jnp.zeros_like(dpw_ref)
            ds_ref[...] = jnp.zeros_like(ds_ref)

        @pl.when(t == 0)
        def _():
            qnext[...] = jnp.zeros_like(qnext)

        xv = x_ref[...]
        gpre = gpre_ref[...]
        u = _rms_fwd(xv, gpre)
        uh = _rms_fwd(jnp.where(tt > 0, xh_ref[...], 0.0), gpre)
        cnt = _pool_counts(tt * tl, tl, d)
        scale_v = sc_ref[...]
        diff, out = _pool_mix(u, uh, cnt, pw_ref, scale_v, tl, d)
        dxo_v = dxo_ref[...]
        dm, dgpost = _rms_bwd(out * scale_v, gpost_ref[...], dxo_v)
        ds_ref[1:2, :] += dgpost
        ds_ref[2:3, :] += jnp.sum(dm * out, axis=0, keepdims=True)
        dout = (dm * scale_v).astype(bf16)
        ddiffs = []
        for gi in range(ng):
            sl = slice(gi * gch, (gi + 1) * gch)
            dpw_ref[gi] += _dot_tn(diff[:, sl].astype(bf16), dout[:, sl])
            ddiffs.append(_dot_nt(dout[:, sl], pw_ref[gi]))
        ddiff = jnp.concatenate(ddiffs, axis=1)
        q = ddiff / cnt
        ext = jnp.concatenate([q, qnext[...]], axis=0)
        sums, cur = [], ext
        for k in (1, 2, 4, 8):
            cur = cur + pltpu.roll(cur, n_ext - k, axis=0)
            sums.append(cur[0:tl, :])
        du = _pool_select(sums, tl, d) - ddiff
        qnext[...] = q[0:POOL_HALO, :]
        dxp, dgpre = _rms_bwd(xv, gpre, du)
        ds_ref[0:1, :] += dgpre
        dx_ref[...] = dxo_v + dxp

    tok = pl.BlockSpec((None, tl, d), lambda b, t: (b, nt - 1 - t, 0))
    halo = pl.BlockSpec((None, POOL_HALO, d),
                        lambda b, t: (b, jnp.maximum((nt - 1 - t) * (tl // POOL_HALO) - 1, 0), 0))
    gain = pl.BlockSpec((1, d), lambda b, t: (0, 0))
    return pl.pallas_call(
        body, name=f"pool_bwd_{li}", grid=(bsz, nt),
        out_shape=(SDS((bsz, l, d), f32), SDS((ng, gch, gch), f32), SDS((8, d), f32)),
        in_specs=[tok, halo, tok, gain, gain, pl.BlockSpec(pw.shape, lambda b, t: (0, 0, 0)), gain],
        out_specs=(tok, pl.BlockSpec((ng, gch, gch), lambda b, t: (0, 0, 0)), pl.BlockSpec((8, d), lambda b, t: (0, 0))),
        scratch_shapes=[pltpu.VMEM((POOL_HALO, d), f32)], compiler_params=_cparams(2),
    )(x, x, dxo, gain_pre, gain_post, pw, scale)


def _s5_discretise(lam_re, lam_im, log_dt, b_re, b_im):
    def body(lr_ref, li_ref, ld_ref, br_ref, bi_ref, ar_ref, ai_ref, bbr_ref, bbi_ref):
        lr, li = lr_ref[...], li_ref[...]
        dt = jnp.exp(ld_ref[...])
        mag = jnp.exp(lr * dt)
        ar = mag * jnp.cos(li * dt)
        ai = mag * jnp.sin(li * dt)
        den = lr * lr + li * li
        nr, ni = ar - 1.0, ai
        fr = (nr * lr + ni * li) / den
        fi = (ni * lr - nr * li) / den
        br, bi = br_ref[...], bi_ref[...]
        ar_ref[...] = ar
        ai_ref[...] = ai
        bbr_ref[...] = fr * br - fi * bi
        bbi_ref[...] = fr * bi + fi * br

    return pl.pallas_call(
        body, name="s5_discretise",
        out_shape=(SDS(lam_re.shape, f32), SDS(lam_re.shape, f32), SDS(b_re.shape, f32), SDS(b_re.shape, f32)),
        compiler_params=_cparams(),
    )(lam_re, lam_im, log_dt, b_re, b_im)


def _s5_discretise_bwd(lam_re, lam_im, log_dt, b_re, b_im, g_ar, g_ai, g_bbr, g_bbi):
    def body(lr_ref, li_ref, ld_ref, br_ref, bi_ref, gar_ref, gai_ref, gbbr_ref, gbbi_ref,
             dlr_ref, dli_ref, dld_ref, dbr_ref, dbi_ref):
        lr, li = lr_ref[...], li_ref[...]
        dt = jnp.exp(ld_ref[...])
        mag = jnp.exp(lr * dt)
        cs, sn = jnp.cos(li * dt), jnp.sin(li * dt)
        ar, ai = mag * cs, mag * sn
        den = lr * lr + li * li
        nr, ni = ar - 1.0, ai
        fr = (nr * lr + ni * li) / den
        fi = (ni * lr - nr * li) / den
        br, bi = br_ref[...], bi_ref[...]
        gbbr, gbbi = gbbr_ref[...], gbbi_ref[...]
        dbr_ref[...] = fr * gbbr + fi * gbbi
        dbi_ref[...] = fr * gbbi - fi * gbbr
        gfr = jnp.sum(br * gbbr + bi * gbbi, axis=2, keepdims=True)
        gfi = jnp.sum(br * gbbi - bi * gbbr, axis=2, keepdims=True)
        gnr_num, gni_num = gfr / den, gfi / den
        gden = -(gfr * fr + gfi * fi) / den
        g_nr = gnr_num * lr - gni_num * li
        g_ni = gnr_num * li + gni_num * lr
        dlr = gnr_num * nr + gni_num * ni + gden * 2.0 * lr
        dli = gnr_num * ni - gni_num * nr + gden * 2.0 * li
        gar = gar_ref[...] + g_nr
        gai = gai_ref[...] + g_ni
        gq = (gar * cs + gai * sn) * mag
        gth = (gai * cs - gar * sn) * mag
        dlr_ref[...] = dlr + gq * dt
        dli_ref[...] = dli + gth * dt
        dld_ref[...] = jnp.sum(gq * lr + gth * li, axis=3, keepdims=True) * dt

    return pl.pallas_call(
        body, name="s5_discretise_bwd",
        out_shape=(SDS(lam_re.shape, f32), SDS(lam_re.shape, f32), SDS(log_dt.shape, f32),
                   SDS(b_re.shape, f32), SDS(b_re.shape, f32)),
        compiler_params=_cparams(),
    )(lam_re, lam_im, log_dt, b_re, b_im, g_ar, g_ai, g_bbr, g_bbi)


def _fill_powers(pw_r, pw_i, ar, ai, nj, width):
    cb = min(S5_CB, width)
    for c0 in range(0, width, cb):
        sl = pl.ds(c0, cb)
        a_r, a_i = ar[:, c0:c0 + cb], ai[:, c0:c0 + cb]

        def step(i, p):
            pw_r[i, :, sl] = p[0]
            pw_i[i, :, sl] = p[1]
            return _cmul(p[0], p[1], a_r, a_i)

        lax.fori_loop(0, nj, step, (a_r, a_i))


def _interleaved_scan(xr_sc, xi_sc, ar, ai, pw_r, pw_i, carry_r, carry_i, nj, width, reverse, h_sc=None):
    cb = min(S5_CB, width)
    acc_out = []
    for c0 in range(0, width, cb):
        sl = pl.ds(c0, cb)
        a_r, a_i = ar[:, c0:c0 + cb], ai[:, c0:c0 + cb]
        aj_r, aj_i = pw_r[nj - 1, :, sl], pw_i[nj - 1, :, sl]

        def pos(i):
            return nj - 1 - i if reverse else i

        def local_step(i, st):
            j = pos(i)
            hr, hi = _cmul(a_r, a_i, st[0], st[1])
            hr, hi = hr + xr_sc[j, :, sl], hi + xi_sc[j, :, sl]
            xr_sc[j, :, sl] = hr
            xi_sc[j, :, sl] = hi
            return hr, hi

        zero = jnp.zeros((8, cb), f32)
        fin_r, fin_i = lax.fori_loop(0, nj, local_step, (zero, zero))
        row = lax.broadcasted_iota(jnp.int32, (8, cb), 0)
        c_r, c_i = carry_r[0:1, sl], carry_i[0:1, sl]
        ent_r, ent_i = zero, zero
        order = range(7, -1, -1) if reverse else range(8)
        for s in order:
            ent_r = jnp.where(row == s, c_r, ent_r)
            ent_i = jnp.where(row == s, c_i, ent_i)
            pr, pi_ = _cmul(aj_r[0:1, :], aj_i[0:1, :], c_r, c_i)
            c_r, c_i = fin_r[s:s + 1, :] + pr, fin_i[s:s + 1, :] + pi_
        carry_r[:, sl] = jnp.broadcast_to(c_r, (8, cb))
        carry_i[:, sl] = jnp.broadcast_to(c_i, (8, cb))

        def fix_step(i, st):
            j = pos(i)
            nx_r, nx_i, acc_r, acc_i = st
            cr_, ci_ = _cmul(pw_r[i, :, sl], pw_i[i, :, sl], ent_r, ent_i)
            hr, hi = xr_sc[j, :, sl] + cr_, xi_sc[j, :, sl] + ci_
            xr_sc[j, :, sl] = hr
            xi_sc[j, :, sl] = hi
            if h_sc is not None:
                sr, si = h_sc[0][j, :, sl], h_sc[1][j, :, sl]
                acc_r = acc_r + nx_r * sr + nx_i * si
                acc_i = acc_i + nx_i * sr - nx_r * si
                nx_r, nx_i = hr, hi
            return nx_r, nx_i, acc_r, acc_i

        st = lax.fori_loop(0, nj, fix_step, (ent_r, ent_i, zero, zero))
        acc_out.append((st[2], st[3]))
    return acc_out


def _diag_mask(n_rep, h, ks):
    assert h & (h - 1) == 0 and (ks // n_rep) & (ks // n_rep - 1) == 0
    r = lax.shift_right_logical(lax.broadcasted_iota(jnp.int32, (n_rep * h, ks), 0), h.bit_length() - 1)
    c = lax.shift_right_logical(lax.broadcasted_iota(jnp.int32, (n_rep * h, ks), 1), (ks // n_rep).bit_length() - 1)
    return r == c


def _expand_block_diag(compact, n_rep):
    h, ks = compact.shape
    full = jnp.concatenate([compact] * n_rep, axis=0)
    return jnp.where(_diag_mask(n_rep, h, ks), full, 0.0).astype(bf16)


def _compact_block_diag(full, n_rep):
    rows, ks = full.shape
    h = rows // n_rep
    return jnp.sum(jnp.where(_diag_mask(n_rep, h, ks), full, 0.0).reshape(n_rep, h, ks), axis=0)


def _s5_scan_fwd(x_il, gain_pre, d_skip, ab_r, ab_i, bbc_r, bbc_i, cc_r, cc_in, li, host=None):
    bsz, nblk, tq, d = x_il.shape
    nkb, gch, ks = bbc_r.shape
    kc = d // nkb
    n_rep = kc // gch
    nj = tq // 8
    h_in, h_shape, h_out, h_scr, h_ops = _host_args(host)

    def body(xf_ref, xk_ref, gk_ref, dk_ref, ar_ref, ai_ref, bbr_ref, bbi_ref, cr_ref, ci_ref, *rest):
        o0 = len(h_in)
        y_ref, hr_ref, hi_ref = rest[o0:o0 + 3]
        s0 = o0 + 3 + len(h_out)
        xr_sc, xi_sc, pw_r, pw_i, carry_r, carry_i, bbr_sc, bbi_sc, crt_sc, cit_sc = rest[s0:s0 + 10]
        k, b, n = pl.program_id(0), pl.program_id(1), pl.program_id(2)
        finish = _hosted(host, rest[:o0], rest[o0 + 3:s0], rest[s0 + 10:],
                         (k == 0) & (b == 0) & (n == 0), (k == nkb - 1) & (b == bsz - 1) & (n == nblk - 1))
        ar = jnp.broadcast_to(ar_ref[...], (8, ks))
        ai = jnp.broadcast_to(ai_ref[...], (8, ks))

        @pl.when((b == 0) & (n == 0))
        def _():
            bbr_sc[...] = _expand_block_diag(bbr_ref[...], n_rep)
            bbi_sc[...] = _expand_block_diag(bbi_ref[...], n_rep)
            crt_sc[...] = _expand_block_diag(cr_ref[...], n_rep)
            cit_sc[...] = _expand_block_diag(ci_ref[...], n_rep)
            _fill_powers(pw_r, pw_i, ar, ai, nj, ks)

        @pl.when(n == 0)
        def _():
            carry_r[...] = jnp.zeros_like(carry_r)
            carry_i[...] = jnp.zeros_like(carry_i)

        u = xk_ref[...] * _rms_scale(xf_ref[...]) * gk_ref[...]
        ub = u.astype(bf16)
        xr_sc[...] = _dot(ub, bbr_sc[...]).reshape(nj, 8, ks)
        xi_sc[...] = _dot(ub, bbi_sc[...]).reshape(nj, 8, ks)
        _interleaved_scan(xr_sc, xi_sc, ar, ai, pw_r, pw_i, carry_r, carry_i, nj, ks, reverse=False)
        hrb = xr_sc[...].reshape(tq, ks).astype(bf16)
        hib = xi_sc[...].reshape(tq, ks).astype(bf16)
        hr_ref[...] = hrb
        hi_ref[...] = hib
        crt, cit, dk = crt_sc[...], cit_sc[...], dk_ref[...]
        for r0 in range(0, tq, tq // 2):
            rows = slice(r0, r0 + tq // 2)
            y_ref[rows, :] = _dot_nt(hrb[rows, :], crt) + _dot_nt(hib[rows, :], cit) + dk * u[rows, :]
        finish()

    full = pl.BlockSpec((None, None, tq, d), lambda k, b, n: (b, n, 0, 0))
    chan = pl.BlockSpec((None, None, tq, kc), lambda k, b, n: (b, n, 0, k))
    stat = pl.BlockSpec((None, None, tq, ks), lambda k, b, n: (b, n, 0, k))
    vec_c = pl.BlockSpec((1, kc), lambda k, b, n: (0, k))
    vec_s = pl.BlockSpec((1, ks), lambda k, b, n: (0, k))
    cmap = pl.BlockSpec((None, gch, ks), lambda k, b, n: (k, 0, 0))
    s_tot = nkb * ks
    return pl.pallas_call(
        body, name=f"s5_scan_fwd_{li}", grid=(nkb, bsz, nblk),
        out_shape=(SDS((bsz, nblk, tq, d), f32), SDS((bsz, nblk, tq, s_tot), bf16), SDS((bsz, nblk, tq, s_tot), bf16),
                   *h_shape),
        in_specs=[full, chan, vec_c, vec_c, vec_s, vec_s, cmap, cmap, cmap, cmap, *h_in],
        out_specs=(chan, stat, stat, *h_out),
        scratch_shapes=[pltpu.VMEM((nj, 8, ks), f32)] * 4 + [pltpu.VMEM((8, ks), f32)] * 2
        + [pltpu.VMEM((kc, ks), bf16)] * 4 + h_scr,
        compiler_params=_cparams(3, side_effects=bool(host)),
    )(x_il, x_il, gain_pre, d_skip, ab_r, ab_i, bbc_r, bbc_i, cc_r, cc_in, *h_ops)


def _s5_glu_fwd(y, x_il, w_glu, b_glu, gain_post, li):
    t, d = y.shape
    tr = min(TOK_TR, t)

    def body(y_ref, x_ref, w_ref, b_ref, gp_ref, xo_ref):
        z = _gelu(y_ref[...])
        a = _dot(z.astype(bf16), w_ref[...]) + b_ref[...]
        xo_ref[...] = x_ref[...] + _rms_fwd(z * _sigmoid(a), gp_ref[...])

    tok = pl.BlockSpec((tr, d), lambda i: (i, 0))
    vec = pl.BlockSpec((1, d), lambda i: (0, 0))
    return pl.pallas_call(
        body, name=f"s5_glu_fwd_{li}", grid=(t // tr,), out_shape=SDS((t, d), f32),
        in_specs=[tok, tok, pl.BlockSpec((d, d), lambda i: (0, 0)), vec, vec], out_specs=tok,
        compiler_params=_cparams(1),
    )(y, x_il, w_glu, b_glu, gain_post)


def _s5_glu_bwd(y, dxo, w_glu, b_glu, gain_post, li):
    t, d = y.shape
    tr = min(TOK_TR, t)

    def body(y_ref, dxo_ref, w_ref, b_ref, gp_ref, dy_ref, dw_ref, ds_ref):
        @pl.when(pl.program_id(0) == 0)
        def _():
            dw_ref[...] = jnp.zeros_like(dw_ref)
            ds_ref[...] = jnp.zeros_like(ds_ref)

        z, zg = _gelu_and_grad(y_ref[...])
        zb = z.astype(bf16)
        w = w_ref[...]
        s = _sigmoid(_dot(zb, w) + b_ref[...])
        dm, dgpost = _rms_bwd(z * s, gp_ref[...], dxo_ref[...])
        da = dm * z * s * (1.0 - s)
        dab = da.astype(bf16)
        dz = dm * s + _dot_nt(dab, w)
        dw_ref[...] += _dot_tn(zb, dab)
        ds_ref[0:1, :] += dgpost
        ds_ref[1:2, :] += jnp.sum(da, axis=0, keepdims=True)
        dy_ref[...] = dz * zg

    tok = pl.BlockSpec((tr, d), lambda i: (i, 0))
    vec = pl.BlockSpec((1, d), lambda i: (0, 0))
    mat = pl.BlockSpec((d, d), lambda i: (0, 0))
    return pl.pallas_call(
        body, name=f"s5_glu_bwd_{li}", grid=(t // tr,),
        out_shape=(SDS((t, d), f32), SDS((d, d), f32), SDS((8, d), f32)),
        in_specs=[tok, tok, mat, vec, vec], out_specs=(tok, mat, pl.BlockSpec((8, d), lambda i: (0, 0))),
        compiler_params=_cparams(1),
    )(y, dxo, w_glu, b_glu, gain_post)


def _s5_scan_bwd(x_il, dy, h_r, h_i, gain_pre, d_skip, ab_r, ab_i, bbc_r, bbc_i, cc_r, cc_in, li, host=None):
    bsz, nblk, tq, d = x_il.shape
    nkb, gch, ks = bbc_r.shape
    kc = d // nkb
    n_rep = kc // gch
    nj = tq // 8
    h_in, h_shape, h_out, h_scr, h_ops = _host_args(host)

    def body(xf_ref, xk_ref, dy_ref, hr_ref, hi_ref, gk_ref, dk_ref, ar_ref, ai_ref, bbr_ref, bbi_ref, cr_ref, ci_ref,
             *rest):
        o0 = len(h_in)
        du_ref, dbbr_ref, dbbi_ref, dcr_ref, dci_ref, dar_ref, dai_ref, dd_ref = rest[o0:o0 + 8]
        s0 = o0 + 8 + len(h_out)
        (gr_sc, gi_sc, hr_sc, hi_sc, pw_r, pw_i, carry_r, carry_i, acc_r, acc_i,
         bbr_sc, bbi_sc, crt_sc, cit_sc, dbbr_acc, dbbi_acc, dcr_acc, dci_acc) = rest[s0:s0 + 18]
        k, b, n = pl.program_id(0), pl.program_id(1), pl.program_id(2)
        finish = _hosted(host, rest[:o0], rest[o0 + 8:s0], rest[s0 + 18:],
                         (k == 0) & (b == 0) & (n == 0), (k == nkb - 1) & (b == bsz - 1) & (n == nblk - 1))
        ar = jnp.broadcast_to(ar_ref[...], (8, ks))
        ai = jnp.broadcast_to(-ai_ref[...], (8, ks))

        @pl.when((b == 0) & (n == 0))
        def _():
            bbr_sc[...] = _expand_block_diag(bbr_ref[...], n_rep)
            bbi_sc[...] = _expand_block_diag(bbi_ref[...], n_rep)
            crt_sc[...] = _expand_block_diag(cr_ref[...], n_rep)
            cit_sc[...] = _expand_block_diag(ci_ref[...], n_rep)
            _fill_powers(pw_r, pw_i, ar, ai, nj, ks)
            for ref in (dar_ref, dai_ref, dd_ref, acc_r, acc_i, dbbr_acc, dbbi_acc, dcr_acc, dci_acc):
                ref[...] = jnp.zeros_like(ref)

        @pl.when(n == 0)
        def _():
            carry_r[...] = jnp.zeros_like(carry_r)
            carry_i[...] = jnp.zeros_like(carry_i)

        u = xk_ref[...] * _rms_scale(xf_ref[...]) * gk_ref[...]
        ub = u.astype(bf16)
        dyv = dy_ref[...]
        dyb = dyv.astype(bf16)
        dd_ref[0:1, :] += jnp.sum(dyv * u, axis=0, keepdims=True)
        hrb, hib = hr_ref[...], hi_ref[...]
        dcr_acc[...] += _dot_tn(dyb, hrb)
        dci_acc[...] += _dot_tn(dyb, hib)
        hr_sc[...] = hrb.astype(f32).reshape(nj, 8, ks)
        hi_sc[...] = hib.astype(f32).reshape(nj, 8, ks)
        gr_sc[...] = _dot(dyb, crt_sc[...]).reshape(nj, 8, ks)
        gi_sc[...] = _dot(dyb, cit_sc[...]).reshape(nj, 8, ks)
        accs = _interleaved_scan(gr_sc, gi_sc, ar, ai, pw_r, pw_i, carry_r, carry_i, nj, ks, reverse=True,
                                 h_sc=(hr_sc, hi_sc))
        cb = min(S5_CB, ks)
        for q, (a_r, a_i) in enumerate(accs):
            acc_r[:, q * cb:(q + 1) * cb] += a_r
            acc_i[:, q * cb:(q + 1) * cb] += a_i
        grb = gr_sc[...].reshape(tq, ks).astype(bf16)
        gib = gi_sc[...].reshape(tq, ks).astype(bf16)
        dbbr_acc[...] += _dot_tn(ub, grb)
        dbbi_acc[...] += _dot_tn(ub, gib)
        bbr, bbi, dk = bbr_sc[...], bbi_sc[...], dk_ref[...]
        for r0 in range(0, tq, tq // 2):
            rows = slice(r0, r0 + tq // 2)
            du_ref[rows, :] = dyv[rows, :] * dk + _dot_nt(grb[rows, :], bbr) + _dot_nt(gib[rows, :], bbi)

        @pl.when((b == bsz - 1) & (n == nblk - 1))
        def _():
            dar_ref[0:1, :] = jnp.sum(acc_r[...], axis=0, keepdims=True)
            dai_ref[0:1, :] = jnp.sum(acc_i[...], axis=0, keepdims=True)
            dbbr_ref[...] = _compact_block_diag(dbbr_acc[...], n_rep)
            dbbi_ref[...] = _compact_block_diag(dbbi_acc[...], n_rep)
            dcr_ref[...] = _compact_block_diag(dcr_acc[...], n_rep)
            dci_ref[...] = _compact_block_diag(dci_acc[...], n_rep)

        finish()

    full = pl.BlockSpec((None, None, tq, d), lambda k, b, n: (b, nblk - 1 - n, 0, 0))
    chan = pl.BlockSpec((None, None, tq, kc), lambda k, b, n: (b, nblk - 1 - n, 0, k))
    stat = pl.BlockSpec((None, None, tq, ks), lambda k, b, n: (b, nblk - 1 - n, 0, k))
    vec_c = pl.BlockSpec((1, kc), lambda k, b, n: (0, k))
    vec_s = pl.BlockSpec((1, ks), lambda k, b, n: (0, k))
    cmap = pl.BlockSpec((None, gch, ks), lambda k, b, n: (k, 0, 0))
    acc_s = pl.BlockSpec((None, 8, ks), lambda k, b, n: (k, 0, 0))
    acc_c = pl.BlockSpec((None, 8, kc), lambda k, b, n: (k, 0, 0))
    cshape = SDS((nkb, gch, ks), f32)
    return pl.pallas_call(
        body, name=f"s5_scan_bwd_{li}", grid=(nkb, bsz, nblk),
        out_shape=(SDS((bsz, nblk, tq, d), f32), cshape, cshape, cshape, cshape, SDS((nkb, 8, ks), f32),
                   SDS((nkb, 8, ks), f32), SDS((nkb, 8, kc), f32), *h_shape),
        in_specs=[full, chan, chan, stat, stat, vec_c, vec_c, vec_s, vec_s, cmap, cmap, cmap, cmap, *h_in],
        out_specs=(chan, cmap, cmap, cmap, cmap, acc_s, acc_s, acc_c, *h_out),
        scratch_shapes=([pltpu.VMEM((nj, 8, ks), f32)] * 6 + [pltpu.VMEM((8, ks), f32)] * 4
                        + [pltpu.VMEM((kc, ks), bf16)] * 4 + [pltpu.VMEM((kc, ks), f32)] * 4 + h_scr),
        compiler_params=_cparams(3, side_effects=bool(host)),
    )(x_il, x_il, dy, h_r, h_i, gain_pre, d_skip, ab_r, ab_i, bbc_r, bbc_i, cc_r, cc_in, *h_ops)


def _norm_residual_bwd(x, du, dxo, gain, name):
    t, d = x.shape
    tr = min(TOK_TR, t)

    def body(x_ref, du_ref, dxo_ref, g_ref, dx_ref, dg_ref):
        @pl.when(pl.program_id(0) == 0)
        def _():
            dg_ref[...] = jnp.zeros_like(dg_ref)

        dxp, dgain = _rms_bwd(x_ref[...], g_ref[...], du_ref[...])
        dx_ref[...] = dxo_ref[...] + dxp
        dg_ref[0:1, :] += dgain

    tok = pl.BlockSpec((tr, d), lambda i: (i, 0))
    return pl.pallas_call(
        body, name=name, grid=(t // tr,), out_shape=(SDS((t, d), f32), SDS((8, d), f32)),
        in_specs=[tok, tok, tok, pl.BlockSpec((1, d), lambda i: (0, 0))],
        out_specs=(tok, pl.BlockSpec((8, d), lambda i: (0, 0))), compiler_params=_cparams(1),
    )(x, du, dxo, gain)


def _loss_head(y, target):
    t, d = y.shape
    tr = min(TOK_TR, t)

    def body(y_ref, t_ref, l_ref, dy_ref):
        @pl.when(pl.program_id(0) == 0)
        def _():
            l_ref[...] = jnp.zeros_like(l_ref)

        err = y_ref[...] - t_ref[...]
        dy_ref[...] = err * (1.0 / d)
        l_ref[...] += jnp.sum(jnp.sum(err * err, axis=1, keepdims=True), axis=0, keepdims=True)

    tok = pl.BlockSpec((tr, d), lambda i: (i, 0))
    return pl.pallas_call(
        body, name="loss_head", grid=(t // tr,), out_shape=(SDS((8, 128), f32), SDS((t, d), f32)),
        in_specs=[tok, tok], out_specs=(pl.BlockSpec((8, 128), lambda i: (0, 0)), tok), compiler_params=_cparams(1),
    )(y, target)


def _interleave(a, tq):
    bsz, l, d = a.shape
    return a.reshape(bsz, l // tq, 8, tq // 8, d).transpose(0, 1, 3, 2, 4).reshape(bsz, l // tq, tq, d)


def _deinterleave(a, l):
    bsz, nblk, tq, d = a.shape
    return a.reshape(bsz, nblk, tq // 8, 8, d).transpose(0, 1, 3, 2, 4).reshape(bsz, l, d)


def _compact_maps(a, nkb):
    ns, g, h, p = a.shape
    gl = g // nkb
    return a.reshape(ns, nkb, gl, h, p).transpose(0, 1, 3, 2, 4).reshape(ns, nkb, h, gl * p)


def _uncompact_maps(a, g):
    nkb, h, cols = a.shape
    gl = g // nkb
    return a.reshape(nkb, h, gl, cols // gl).transpose(0, 2, 1, 3).reshape(g, h, cols // gl)


def _pack_rows(arrs, cols, row_mult):
    flat = jnp.concatenate([a.reshape(-1).astype(f32) for a in arrs])
    rows = -(-flat.shape[0] // cols)
    rows = -(-rows // row_mult) * row_mult
    return jnp.pad(flat, (0, rows * cols - flat.shape[0])).reshape(rows, cols)


def _unpack_rows(buf, shapes):
    flat = buf.reshape(-1)
    out, off = [], 0
    for s in shapes:
        n = math.prod(s)
        out.append(flat[off:off + n].reshape(s))
        off += n
    return out


W_NAMES = ['s5_lambda_re', 's5_lambda_im', 's5_log_dt', 's5_b_re', 's5_b_im', 's5_c_re', 's5_c_im', 's5_d', 's5_w_glu',
           's5_b_glu', 'pool_w', 'pool_scale', 'ffn_w_gate', 'ffn_w_val', 'ffn_conv_w', 'ffn_conv_b', 'ffn_w_down',
           'norm_mix_pre', 'norm_mix_post', 'norm_ffn_pre', 'norm_ffn_post']
BIG = ('s5_w_glu', 'ffn_w_gate', 'ffn_w_val', 'ffn_w_down')


def kernel(x, s5_lambda_re, s5_lambda_im, s5_log_dt, s5_b_re, s5_b_im, s5_c_re, s5_c_im, s5_d, s5_w_glu, s5_b_glu, pool_w, pool_scale, ffn_w_gate, ffn_w_val, ffn_conv_w, ffn_conv_b, ffn_w_down, norm_mix_pre, norm_mix_post, norm_ffn_pre, norm_ffn_post, loss_target, m_s5_lambda_re, m_s5_lambda_im, m_s5_log_dt, m_s5_b_re, m_s5_b_im, m_s5_c_re, m_s5_c_im, m_s5_d, m_s5_w_glu, m_s5_b_glu, m_pool_w, m_pool_scale, m_ffn_w_gate, m_ffn_w_val, m_ffn_conv_w, m_ffn_conv_b, m_ffn_w_down, m_norm_mix_pre, m_norm_mix_post, m_norm_ffn_pre, m_norm_ffn_post, v_s5_lambda_re, v_s5_lambda_im, v_s5_log_dt, v_s5_b_re, v_s5_b_im, v_s5_c_re, v_s5_c_im, v_s5_d, v_s5_w_glu, v_s5_b_glu, v_pool_w, v_pool_scale, v_ffn_w_gate, v_ffn_w_val, v_ffn_conv_w, v_ffn_conv_b, v_ffn_w_down, v_norm_mix_pre, v_norm_mix_post, v_norm_ffn_pre, v_norm_ffn_post):
    w_in = dict(zip(W_NAMES, (s5_lambda_re, s5_lambda_im, s5_log_dt, s5_b_re, s5_b_im, s5_c_re, s5_c_im, s5_d, s5_w_glu,
                              s5_b_glu, pool_w, pool_scale, ffn_w_gate, ffn_w_val, ffn_conv_w, ffn_conv_b, ffn_w_down,
                              norm_mix_pre, norm_mix_post, norm_ffn_pre, norm_ffn_post)))
    m_in = dict(zip(W_NAMES, (m_s5_lambda_re, m_s5_lambda_im, m_s5_log_dt, m_s5_b_re, m_s5_b_im, m_s5_c_re, m_s5_c_im,
                              m_s5_d, m_s5_w_glu, m_s5_b_glu, m_pool_w, m_pool_scale, m_ffn_w_gate, m_ffn_w_val,
                              m_ffn_conv_w, m_ffn_conv_b, m_ffn_w_down, m_norm_mix_pre, m_norm_mix_post,
                              m_norm_ffn_pre, m_norm_ffn_post)))
    v_in = dict(zip(W_NAMES, (v_s5_lambda_re, v_s5_lambda_im, v_s5_log_dt, v_s5_b_re, v_s5_b_im, v_s5_c_re, v_s5_c_im,
                              v_s5_d, v_s5_w_glu, v_s5_b_glu, v_pool_w, v_pool_scale, v_ffn_w_gate, v_ffn_w_val,
                              v_ffn_conv_w, v_ffn_conv_b, v_ffn_w_down, v_norm_mix_pre, v_norm_mix_post,
                              v_norm_ffn_pre, v_norm_ffn_post)))

    bsz, seq, d = x.shape
    depth = ffn_w_gate.shape[0]
    n_s5, n_grp, n_state = s5_lambda_re.shape
    n_gch = s5_b_re.shape[3]
    n_pool = pool_w.shape[0]
    fs = ffn_w_gate.shape[2]
    f = N_CHIPS * fs
    gs = s5_w_glu.shape[1]
    nkb = d // S5_KB_CH
    tq = min(S5_TQ, seq)
    chip = 2 * lax.axis_index("x") + lax.axis_index("y")

    s_ffn = jnp.stack([ffn_w_gate.transpose(0, 2, 1), ffn_w_val.transpose(0, 2, 1), ffn_w_down], axis=1).astype(bf16)
    small_shard = [pool_w, pool_scale, ffn_conv_w]
    first_hosts = [("gather", 1, s_ffn[0]), ("gather", 1, s5_w_glu.astype(bf16)),
                   ("whole", 0, _pack_rows(small_shard, d, 16))]

    lam_r4 = s5_lambda_re.reshape(n_s5, n_grp, 1, n_state)
    lam_i4 = s5_lambda_im.reshape(n_s5, n_grp, 1, n_state)
    ldt4 = s5_log_dt.reshape(n_s5, n_grp, 1, 1)
    b_r4 = s5_b_re.transpose(0, 1, 3, 2)
    b_i4 = s5_b_im.transpose(0, 1, 3, 2)
    ab_r4, ab_i4, bb_r4, bb_i4 = _s5_discretise(lam_r4, lam_i4, ldt4, b_r4, b_i4)
    n_st_tot = n_grp * n_state
    ab_r, ab_i = ab_r4.reshape(n_s5, 1, n_st_tot), ab_i4.reshape(n_s5, 1, n_st_tot)
    bbc_r, bbc_i = _compact_maps(bb_r4, nkb), _compact_maps(bb_i4, nkb)
    cc_r, cc_in = _compact_maps(s5_c_re, nkb), _compact_maps(-s5_c_im, nkb)

    def s5_params(j):
        return dict(ab_r=ab_r[j], ab_i=ab_i[j], bb_r=bbc_r[j], bb_i=bbc_i[j], c_r=cc_r[j], c_in=cc_in[j],
                    d_skip=s5_d[j].reshape(1, d), b_glu=s5_b_glu[j].reshape(1, d))

    def row(a, i):
        return a[i].reshape(1, -1)

    assert depth >= 1 and N_CHIPS == 4
    saved = []
    xc = x
    w_layer = [None] * depth
    for i in range(depth):
        j = i // 2
        sv = dict(x_in=xc)
        if i % 2 == 0:
            sp = s5_params(j)
            x_il = _interleave(xc, tq)
            res = _s5_scan_fwd(x_il, row(norm_mix_pre, i), sp["d_skip"], sp["ab_r"], sp["ab_i"], sp["bb_r"], sp["bb_i"],
                               sp["c_r"], sp["c_in"], i, host=first_hosts if i == 0 else None)
            y_il, h_r, h_i = res[:3]
            if i == 0:
                w_layer[0], g_glu, g_small = res[3:6]
                parts = [_unpack_rows(g_small[k], [a.shape for a in small_shard]) for k in range(N_CHIPS)]
                pool_w_full = jnp.concatenate([p[0] for p in parts], axis=2).astype(bf16)
                pool_scale_full = jnp.concatenate([p[1] for p in parts], axis=1)
                conv_w_full = jnp.concatenate([p[2] for p in parts], axis=2)
            sp["w_glu"] = g_glu[j]
            xo_il = _s5_glu_fwd(y_il.reshape(bsz * seq, d), x_il.reshape(bsz * seq, d), sp["w_glu"], sp["b_glu"],
                                row(norm_mix_post, i), i)
            x_mid = _deinterleave(xo_il.reshape(x_il.shape), seq)
            sv.update(sp=sp, x_il=x_il, y_il=y_il, h_r=h_r, h_i=h_i)
        else:
            x_mid = _pool_fwd(xc, row(norm_mix_pre, i), row(norm_mix_post, i), pool_w_full[j], row(pool_scale_full, j), i)
        res = _ffn_fwd(x_mid, row(norm_ffn_pre, i), row(norm_ffn_post, i), w_layer[i], conv_w_full, ffn_conv_b, i,
                       host=[("gather", 1, s_ffn[i + 1])] if i + 1 < depth else None)
        xc, g_sv, v_sv, f_sv = res[:4]
        if i + 1 < depth:
            w_layer[i + 1] = res[4]
        sv.update(x_mid=x_mid, g=g_sv, v=v_sv, f=f_sv)
        saved.append(sv)

    sq, dy = _loss_head(xc.reshape(bsz * seq, d), loss_target.reshape(bsz * seq, d))
    loss = lax.psum(sq[0, 0] * (0.5 / d), ("x", "y", "c"))

    dx = dy.reshape(bsz, seq, d)
    dw_layers = [None] * depth
    r_layers = [None] * depth
    g_small_params = {n: [None] * w_in[n].shape[0] for n in W_NAMES if n not in BIG}
    dglu = [None] * n_s5
    for i in range(depth - 1, -1, -1):
        j = i // 2
        sv = saved[i]
        res = _ffn_bwd_act(sv["x_mid"], sv["f"], sv["g"], sv["v"], dx, row(norm_ffn_pre, i), row(norm_ffn_post, i),
                           w_layer[i], conv_w_full, ffn_conv_b, i,
                           host=[("slab", 1, dw_layers[i + 1])] if i + 1 < depth else None)
        dx, dg, dv, hdn, hb, dfb, dgain, dconv = res[:8]
        if i + 1 < depth:
            r_layers[i + 1] = res[8]
        dw_layers[i] = _ffn_bwd_weights(dg, dv, hdn, hb.reshape(bsz * seq, d), dfb.reshape(bsz * seq, d), i)
        g_small_params["norm_ffn_pre"][i] = dgain[0]
        g_small_params["norm_ffn_post"][i] = dgain[1]
        dconv = dconv.transpose(1, 0, 2).reshape(8, f)
        g_small_params["ffn_conv_w"][i] = dconv[0:3]
        g_small_params["ffn_conv_b"][i] = dconv[3]
        if i % 2 == 0:
            sp = sv["sp"]
            dxo_il = _interleave(dx, tq)
            dy_s, dglu[j], ds_glu = _s5_glu_bwd(sv["y_il"].reshape(bsz * seq, d), dxo_il.reshape(bsz * seq, d),
                                                sp["w_glu"], sp["b_glu"], row(norm_mix_post, i), i)
            res = _s5_scan_bwd(sv["x_il"], dy_s.reshape(sv["x_il"].shape), sv["h_r"], sv["h_i"], row(norm_mix_pre, i),
                               sp["d_skip"], sp["ab_r"], sp["ab_i"], sp["bb_r"], sp["bb_i"], sp["c_r"], sp["c_in"], i,
                               host=[("slab", 1, dw_layers[0])] if i == 0 else None)
            du, dbb_r, dbb_i, dc_r, dc_in, dab_r, dab_i, dd = res[:8]
            if i == 0:
                r_layers[0] = res[8]
            dx_il, dgpre = _norm_residual_bwd(sv["x_il"].reshape(bsz * seq, d), du.reshape(bsz * seq, d),
                                              dxo_il.reshape(bsz * seq, d), row(norm_mix_pre, i), f"s5_pre_bwd_{i}")
            dx = _deinterleave(dx_il.reshape(sv["x_il"].shape), seq)
            g_small_params["norm_mix_pre"][i] = dgpre[0]
            g_small_params["norm_mix_post"][i] = ds_glu[0]
            g_small_params["s5_b_glu"][j] = ds_glu[1]
            g_small_params["s5_d"][j] = dd[:, 0, :].reshape(d)
            g_small_params["s5_c_re"][j] = _uncompact_maps(dc_r, n_grp)
            g_small_params["s5_c_im"][j] = -_uncompact_maps(dc_in, n_grp)
            sv["g_ab"] = (dab_r[:, 0, :].reshape(n_grp, 1, n_state), dab_i[:, 0, :].reshape(n_grp, 1, n_state),
                          _uncompact_maps(dbb_r, n_grp), _uncompact_maps(dbb_i, n_grp))
        else:
            dx, dpw, dsm = _pool_bwd(sv["x_in"], dx, row(norm_mix_pre, i), row(norm_mix_post, i), pool_w_full[j],
                                     row(pool_scale_full, j), i)
            g_small_params["norm_mix_pre"][i] = dsm[0]
            g_small_params["norm_mix_post"][i] = dsm[1]
            g_small_params["pool_scale"][j] = dsm[2]
            g_small_params["pool_w"][j] = dpw
    grad_x = dx

    g_ab = [saved[2 * j]["g_ab"] for j in range(n_s5)]
    d_lr, d_li, d_ld, d_br, d_bi = _s5_discretise_bwd(
        lam_r4, lam_i4, ldt4, b_r4, b_i4, jnp.stack([g[0] for g in g_ab]), jnp.stack([g[1] for g in g_ab]),
        jnp.stack([g[2] for g in g_ab]), jnp.stack([g[3] for g in g_ab]))
    small_full = {n: (jnp.stack(v) if v[0] is not None else None) for n, v in g_small_params.items()}
    small_full["s5_lambda_re"] = d_lr.reshape(n_s5, n_grp, n_state)
    small_full["s5_lambda_im"] = d_li.reshape(n_s5, n_grp, n_state)
    small_full["s5_log_dt"] = d_ld.reshape(n_s5, n_grp)
    small_full["s5_b_re"] = d_br.transpose(0, 1, 3, 2)
    small_full["s5_b_im"] = d_bi.transpose(0, 1, 3, 2)
    bc_names = ["s5_b_re", "s5_b_im", "s5_c_re", "s5_c_im"]
    small_names = [n for n in W_NAMES if n not in BIG and n not in bc_names and n != "pool_w"]
    q = _pack_rows([small_full[n] for n in small_names], d, 16)
    q_bc = _pack_rows([small_full[n] for n in bc_names], d, 16).astype(bf16)

    r_glu, r_pw, r_q, r_bc = _chip_exchange(
        [("slab", 1, jnp.stack(dglu).astype(bf16)), ("slab", 2, small_full["pool_w"]), ("whole", 0, q), ("whole", 0, q_bc)],
        "exchange_small_grads")
    p_ffn = _sum_slots_layers([r.reshape(N_CHIPS, 3 * fs, d) for r in r_layers], "sum_slots_ffn").reshape(depth * 3 * fs, d)
    p_glu = _sum_slots(r_glu.reshape(N_CHIPS, n_s5 * gs, d), "sum_slots_glu")
    pw_rows, pw_cols = math.prod(pool_w.shape[:3]), pool_w.shape[3]
    p_pw = _sum_slots(r_pw.reshape(N_CHIPS, pw_rows, pw_cols), "sum_slots_pool_w")
    p_q = _sum_slots(r_q, "sum_slots_small")
    p_bc = _sum_slots(r_bc, "sum_slots_bc")
    o_ffn, o_glu, o_pw, o_q, o_bc = _exchange_with_sibling([p_ffn, p_glu, p_pw, p_q, p_bc])

    outs = {}

    def put(name, res, shape):
        outs[name] = tuple(r.reshape(shape) for r in res)

    pf, of = p_ffn.reshape(depth, 3, fs, d), o_ffn.reshape(depth, 3, fs, d)
    for kind, name in ((0, "ffn_w_gate"), (1, "ffn_w_val")):
        parts_g = [a[:, kind].transpose(0, 2, 1).reshape(depth * d, fs) for a in (pf, of)]
        put(name, _adamw(parts_g, w_in[name].reshape(depth * d, fs), m_in[name].reshape(depth * d, fs),
                         v_in[name].reshape(depth * d, fs), f"adamw_{name}"), w_in[name].shape)
    put("ffn_w_down", _adamw([pf[:, 2].reshape(depth * fs, d), of[:, 2].reshape(depth * fs, d)],
                             ffn_w_down.reshape(depth * fs, d), m_ffn_w_down.reshape(depth * fs, d),
                             v_ffn_w_down.reshape(depth * fs, d), "adamw_ffn_w_down"), ffn_w_down.shape)
    put("s5_w_glu", _adamw([p_glu, o_glu], s5_w_glu.reshape(n_s5 * gs, d), m_s5_w_glu.reshape(n_s5 * gs, d),
                           v_s5_w_glu.reshape(n_s5 * gs, d), "adamw_s5_w_glu"), s5_w_glu.shape)

    q_tot, bc_tot, pw_tot = _add_pairs([(p_q, o_q), (p_bc, o_bc), (p_pw, o_pw)], "sum_small")
    g_small = dict(zip(small_names, _unpack_rows(q_tot, [small_full[n].shape for n in small_names])))
    g_small.update(zip(bc_names, _unpack_rows(bc_tot, [small_full[n].shape for n in bc_names])))
    g_small["pool_w"] = pw_tot.reshape(pool_w.shape)
    g_small["pool_scale"] = lax.dynamic_slice_in_dim(g_small["pool_scale"], chip * pool_scale.shape[1],
                                                     pool_scale.shape[1], axis=1)
    g_small["ffn_conv_w"] = lax.dynamic_slice_in_dim(g_small["ffn_conv_w"], chip * fs, fs, axis=2)
    all_small = [n for n in W_NAMES if n not in BIG]
    local_shapes = [w_in[n].shape for n in all_small]
    res = _adamw([_pack_rows([g_small[n] for n in all_small], d, 64)],
                 _pack_rows([w_in[n] for n in all_small], d, 64), _pack_rows([m_in[n] for n in all_small], d, 64),
                 _pack_rows([v_in[n] for n in all_small], d, 64), "adamw_small")
    unpacked = [_unpack_rows(r, local_shapes) for r in res]
    for idx, n in enumerate(all_small):
        outs[n] = (g_small[n], unpacked[1][idx], unpacked[2][idx], unpacked[3][idx])

    return (loss, grad_x, *[outs[n][0] for n in W_NAMES], *[outs[n][1] for n in W_NAMES],
            *[outs[n][2] for n in W_NAMES], *[outs[n][3] for n in W_NAMES])
```

```python
import math

import jax
import jax.numpy as jnp
from jax import lax
from jax.experimental import pallas as pl
from jax.experimental.pallas import tpu as pltpu

f32, bf16 = jnp.float32, jnp.bfloat16
SDS = jax.ShapeDtypeStruct
MESH = pl.DeviceIdType.MESH

RMS_EPS = 1e-6
GELU_C = math.sqrt(2.0 / math.pi)
GELU_K = 0.044715
ADAM_LR, ADAM_B1, ADAM_B2, ADAM_EPS, ADAM_WD, ADAM_STEP = 0.001, 0.9, 0.999, 1e-08, 0.01, 10
POOL_WINDOWS = (2, 4, 8, 16)
POOL_HALO = 16
S5_GROUP_CH = 16
S5_STATE = 64
S5_KB_CH = 256
N_CHIPS = 4

FFN_TL = 512
FFN_TL_BWD = 256
FFN_FT = 256
FFN_CH = 16
FFN_SPLIT = 2
WG_TR = 2048
POOL_TL = 512
S5_TQ = 512
S5_CB = 1024
S5_UNROLL = 2
TOK_TR = 512
VMEM_LIMIT = 56 * 1024 * 1024

HBM = pl.BlockSpec(memory_space=pltpu.HBM)


def _cparams(n_axes=0, side_effects=False):
    kw = dict(vmem_limit_bytes=VMEM_LIMIT)
    if n_axes:
        kw["dimension_semantics"] = ("arbitrary",) * n_axes
    if side_effects:
        kw["has_side_effects"] = True
    return pltpu.CompilerParams(**kw)


def _dot(a, b):
    return jnp.dot(a, b, preferred_element_type=f32)


def _dot_nt(a, b):
    return lax.dot_general(a, b, (((1,), (1,)), ((), ())), preferred_element_type=f32)


def _dot_tn(a, b):
    return lax.dot_general(a, b, (((0,), (0,)), ((), ())), preferred_element_type=f32)


def _rms_scale(x):
    return lax.rsqrt(jnp.mean(x * x, axis=-1, keepdims=True) + RMS_EPS)


def _rms_fwd(x, gain):
    return x * _rms_scale(x) * gain


def _rms_bwd(x, gain, dy):
    r = _rms_scale(x)
    xn = x * r
    dgain = jnp.sum(dy * xn, axis=0, keepdims=True)
    dxn = dy * gain
    dx = r * (dxn - xn * jnp.mean(dxn * xn, axis=-1, keepdims=True))
    return dx, dgain


def _gelu(x):
    t = jnp.tanh(x * (GELU_C + (GELU_C * GELU_K) * (x * x)))
    hx = 0.5 * x
    return hx + hx * t


def _gelu_and_grad(x):
    x2 = x * x
    t = jnp.tanh(x * (GELU_C + (GELU_C * GELU_K) * x2))
    hx = 0.5 * x
    return hx + hx * t, (0.5 + 0.5 * t) + hx * (1.0 - t * t) * (GELU_C + (3.0 * GELU_C * GELU_K) * x2)


def _sigmoid(x):
    return 1.0 / (1.0 + jnp.exp(-x))


def _cmul(ar, ai, br, bi):
    return ar * br - ai * bi, ar * bi + ai * br


def _row_block(n, cap):
    best = None
    for d in range(16, min(n, cap) + 1, 16):
        if n % d == 0:
            best = d
    assert best is not None, n
    return best


def _mesh_pos():
    return lax.axis_index("x"), lax.axis_index("y"), lax.axis_index("c")


def _other_chips(x, y):
    return [(1 - x, y), (x, 1 - y), (1 - x, 1 - y)]


def _slab_index(ndim, axis, start, size):
    return tuple(pl.ds(start, size) if a == axis else slice(None) for a in range(ndim))


class _ChipExchange:
    def __init__(self, kind, axis, src, dst, send_sems, recv_sems, loc_sem):
        x, y, c = _mesh_pos()
        me = 2 * x + y
        nd = len(src.shape)
        size = src.shape[axis] if kind == "gather" else src.shape[axis] // N_CHIPS

        def src_for(kk):
            return src.at[_slab_index(nd, axis, kk * size, size)] if kind == "slab" else src

        def dst_for(kk):
            return dst.at[_slab_index(nd, axis, kk * size, size)] if kind == "gather" else dst.at[kk]

        self.own = pltpu.make_async_copy(src_for(me), dst_for(me), loc_sem)
        self.sends, self.recvs = [], []
        for j, chip in enumerate(_other_chips(x, y)):
            kk = 2 * chip[0] + chip[1]
            peer = dict(send_sem=send_sems.at[j], recv_sem=recv_sems.at[j], device_id=(chip[0], chip[1], c),
                        device_id_type=MESH)
            self.sends.append(pltpu.make_async_remote_copy(src_ref=src_for(kk), dst_ref=dst_for(me), **peer))
            self.recvs.append(pltpu.make_async_remote_copy(src_ref=src_for(me), dst_ref=dst_for(kk), **peer))

    def start(self):
        self.own.start()
        for cp in self.sends:
            cp.start()

    def finish(self):
        for cp in self.recvs:
            cp.wait_recv()
        for cp in self.sends:
            cp.wait_send()
        self.own.wait()


def _exchange_out_shape(kind, axis, src):
    if kind == "gather":
        shape = tuple(N_CHIPS * n if a == axis else n for a, n in enumerate(src.shape))
    elif kind == "slab":
        shape = (N_CHIPS,) + tuple(n // N_CHIPS if a == axis else n for a, n in enumerate(src.shape))
    else:
        shape = (N_CHIPS,) + tuple(src.shape)
    return SDS(shape, src.dtype)


EXCHANGE_SEMS = [pltpu.SemaphoreType.DMA((3,)), pltpu.SemaphoreType.DMA((3,)), pltpu.SemaphoreType.DMA((1,))]


def _chip_exchange(items, name):
    n = len(items)

    def body(*refs):
        ins, outs, sems = refs[:n], refs[n:2 * n], refs[2 * n:]
        exs = [_ChipExchange(kind, axis, ins[i], outs[i], sems[3 * i], sems[3 * i + 1], sems[3 * i + 2].at[0])
               for i, (kind, axis, _) in enumerate(items)]
        for ex in exs:
            ex.start()
        for ex in exs:
            ex.finish()

    return pl.pallas_call(
        body, name=name, out_shape=tuple(_exchange_out_shape(k, a, arr) for k, a, arr in items),
        in_specs=[HBM] * n, out_specs=tuple([HBM] * n), scratch_shapes=EXCHANGE_SEMS * n,
        compiler_params=_cparams(side_effects=True),
    )(*[arr for _, _, arr in items])


def _hosted(hosts, host_ins, host_outs, sems, first, last):
    if not hosts:
        return lambda: None
    exs = [_ChipExchange(h[0], h[1], host_ins[i], host_outs[i], sems[3 * i], sems[3 * i + 1], sems[3 * i + 2].at[0])
           for i, h in enumerate(hosts)]

    @pl.when(first)
    def _():
        for ex in exs:
            ex.start()

    def finish():
        @pl.when(last)
        def _():
            for ex in exs:
                ex.finish()

    return finish


def _host_args(hosts):
    hosts = hosts or []
    n = len(hosts)
    return [HBM] * n, [_exchange_out_shape(*h) for h in hosts], [HBM] * n, EXCHANGE_SEMS * n, [h[2] for h in hosts]


def _exchange_with_sibling(parts):
    n = len(parts)

    def body(*refs):
        ins, outs = refs[:n], refs[n:2 * n]
        send_sems, recv_sems = refs[2 * n:]
        x, y, c = _mesh_pos()
        cps = [pltpu.make_async_remote_copy(src_ref=s, dst_ref=t, send_sem=send_sems.at[i], recv_sem=recv_sems.at[i],
                                            device_id=(x, y, 1 - c), device_id_type=MESH)
               for i, (s, t) in enumerate(zip(ins, outs))]
        for cp in cps:
            cp.start()
        for cp in cps:
            cp.wait()

    return pl.pallas_call(
        body, name="exchange_with_sibling",
        out_shape=tuple(SDS(p.shape, p.dtype) for p in parts),
        in_specs=[HBM] * n, out_specs=tuple([HBM] * n),
        scratch_shapes=[pltpu.SemaphoreType.DMA((n,)), pltpu.SemaphoreType.DMA((n,))],
        compiler_params=_cparams(side_effects=True),
    )(*parts)


def _sum_slots(r, name):
    _, rows, cols = r.shape
    tr = _row_block(rows, 512)

    def body(r_ref, o_ref):
        o_ref[...] = ((r_ref[0].astype(f32) + r_ref[1].astype(f32)) + r_ref[2].astype(f32)) + r_ref[3].astype(f32)

    return pl.pallas_call(
        body, name=name, grid=(rows // tr,), out_shape=SDS((rows, cols), f32),
        in_specs=[pl.BlockSpec((N_CHIPS, tr, cols), lambda i: (0, i, 0))],
        out_specs=pl.BlockSpec((tr, cols), lambda i: (i, 0)),
        compiler_params=_cparams(1),
    )(r)


def _sum_slots_layers(rs, name):
    _, rows, cols = rs[0].shape
    nl = len(rs)
    tr = _row_block(rows, 256)

    def body(*refs):
        o_ref = refs[nl]
        for l in range(nl):
            r = refs[l]
            o_ref[l] = ((r[0].astype(f32) + r[1].astype(f32)) + r[2].astype(f32)) + r[3].astype(f32)

    return pl.pallas_call(
        body, name=name, grid=(rows // tr,), out_shape=SDS((nl, rows, cols), f32),
        in_specs=[pl.BlockSpec((N_CHIPS, tr, cols), lambda i: (0, i, 0))] * nl,
        out_specs=pl.BlockSpec((nl, tr, cols), lambda i: (0, i, 0)),
        compiler_params=_cparams(1),
    )(*rs)


def _add_pairs(pairs, name):
    n = len(pairs)

    def body(*refs):
        for i in range(n):
            refs[2 * n + i][...] = refs[2 * i][...] + refs[2 * i + 1][...]

    return pl.pallas_call(body, name=name, out_shape=tuple(SDS(a.shape, f32) for a, _ in pairs),
                          compiler_params=_cparams())(*[t for pair in pairs for t in pair])


def _adamw(g_parts, w, m, v, name):
    rows, cols = w.shape[-2:]
    tr = _row_block(rows, 512)
    n_g = len(g_parts)
    c1 = 1.0 / (1.0 - ADAM_B1 ** ADAM_STEP)
    c2 = 1.0 / (1.0 - ADAM_B2 ** ADAM_STEP)

    def body(*refs):
        g_refs = refs[:n_g]
        w_ref, m_ref, v_ref, go_ref, d_ref, mo_ref, vo_ref = refs[n_g:]
        g = g_refs[0][...]
        for r in g_refs[1:]:
            g = g + r[...]
        mn = ADAM_B1 * m_ref[...] + (1.0 - ADAM_B1) * g
        vn = ADAM_B2 * v_ref[...] + (1.0 - ADAM_B2) * (g * g)
        go_ref[...] = g
        mo_ref[...] = mn
        vo_ref[...] = vn
        d_ref[...] = -ADAM_LR * ((mn * c1) / (jnp.sqrt(vn * c2) + ADAM_EPS) + ADAM_WD * w_ref[...])

    if w.ndim == 2:
        grid, spec = (rows // tr,), pl.BlockSpec((tr, cols), lambda i: (i, 0))
    else:
        grid, spec = (w.shape[0], rows // tr), pl.BlockSpec((None, tr, cols), lambda a, i: (a, i, 0))
    return pl.pallas_call(
        body, name=name, grid=grid, out_shape=tuple(SDS(w.shape, f32) for _ in range(4)),
        in_specs=[spec] * (n_g + 3), out_specs=(spec,) * 4, compiler_params=_cparams(len(grid)),
    )(*g_parts, w, m, v)


def _resident_spec(shape, n_grid):
    zeros = (0,) * len(shape)
    return pl.BlockSpec(tuple(shape), {2: lambda a, b: zeros, 3: lambda a, b, c: zeros}[n_grid],
                        pipeline_mode=pl.Buffered(1))


def _weight_tiles(w_ref, j, ft):
    rows = pl.ds(j * ft if isinstance(j, int) else pl.multiple_of(j * ft, ft), ft)
    return w_ref[0, rows, :], w_ref[1, rows, :], w_ref[2, rows, :]


def _shifted_rows(ext):
    return ext[8:, :], pltpu.roll(ext, 1, axis=0)[8:, :], pltpu.roll(ext, 2, axis=0)[8:, :]


def _later_rows(ext):
    n = ext.shape[0]
    return pltpu.roll(ext, n - 1, axis=0)[0:n - 8, :], pltpu.roll(ext, n - 2, axis=0)[0:n - 8, :]


def _ffn_fwd(x_mid, gain_pre, gain_post, w_l, conv_w, conv_b, li, host=None):
    bsz, l, d = x_mid.shape
    f = w_l.shape[1]
    tl, ft = min(FFN_TL, l), min(FFN_FT, f)
    nt, nf = l // tl, f // ft
    n_sub = FFN_SPLIT if tl % (FFN_SPLIT * FFN_CH) == 0 else 1
    sub = tl // n_sub
    h_in, h_shape, h_out, h_scr, h_ops = _host_args(host)

    def body(x_ref, gpre_ref, gpost_ref, w_ref, cw_ref, cb_ref, *rest):
        xo_ref, g_ref, v_ref, f_ref = rest[len(h_in):len(h_in) + 4]
        s0 = len(h_in) + 4 + len(h_out)
        h_sc, facc, gprev = rest[s0:s0 + 3]
        b, t = pl.program_id(0), pl.program_id(1)
        finish = _hosted(host, rest[:len(h_in)], rest[len(h_in) + 4:s0], rest[s0 + 3:],
                         (b == 0) & (t == 0), (b == bsz - 1) & (t == nt - 1))
        h_sc[...] = _rms_fwd(x_ref[...], gpre_ref[...]).astype(bf16)

        @pl.when(t == 0)
        def _():
            gprev[...] = jnp.zeros_like(gprev)

        def gate_val(j):
            wg, wv = w_ref[0, j * ft:(j + 1) * ft, :], w_ref[1, j * ft:(j + 1) * ft, :]
            return ([_dot_nt(h_sc[s * sub:(s + 1) * sub, :], wg) for s in range(n_sub)],
                    [_dot_nt(h_sc[s * sub:(s + 1) * sub, :], wv) for s in range(n_sub)])

        ahead = gate_val(0)
        for j in range(nf):
            gs, vs = ahead
            if j + 1 < nf:
                ahead = gate_val(j + 1)
            cols = slice(j * ft, (j + 1) * ft)
            wd = w_ref[2, cols, :]
            w0, w1, w2, bias = (jnp.broadcast_to(r, (FFN_CH, ft))
                                for r in (cw_ref[0:1, cols], cw_ref[1:2, cols], cw_ref[2:3, cols], cb_ref[:, cols]))
            hist = gprev[j]
            for s in range(n_sub):
                rows = slice(s * sub, (s + 1) * sub)
                g_ref[j, rows, :] = gs[s].astype(bf16)
                v_ref[j, rows, :] = vs[s].astype(bf16)
                hdn = []
                for r0 in range(0, sub, FFN_CH):
                    g0, g1, g2 = _shifted_rows(jnp.concatenate([hist, gs[s][r0:r0 + FFN_CH, :]], axis=0))
                    hist = g0[FFN_CH - 8:, :]
                    hdn.append((_gelu(bias + w0 * g2 + w1 * g1 + w2 * g0) * vs[s][r0:r0 + FFN_CH, :]).astype(bf16))
                part = _dot(jnp.concatenate(hdn, axis=0), wd)
                if j == 0:
                    facc[rows, :] = part
                else:
                    facc[rows, :] += part
            gprev[j] = hist

        fv = facc[...]
        f_ref[...] = fv
        xo_ref[...] = x_ref[...] + _rms_fwd(fv, gpost_ref[...])
        finish()

    tok = pl.BlockSpec((None, tl, d), lambda b, t: (b, t, 0))
    hid = pl.BlockSpec((nf, tl, ft), lambda b, t: (0, b * nt + t, 0))
    gain = pl.BlockSpec((1, d), lambda b, t: (0, 0))
    hid_shape = SDS((nf, bsz * l, ft), bf16)
    return pl.pallas_call(
        body, name=f"ffn_fwd_{li}", grid=(bsz, nt),
        out_shape=(SDS((bsz, l, d), f32), hid_shape, hid_shape, SDS((bsz, l, d), f32), *h_shape),
        in_specs=[tok, gain, gain, _resident_spec(w_l.shape, 2),
                  pl.BlockSpec((None, 3, f), lambda b, t: (li, 0, 0)),
                  pl.BlockSpec((None, 1, f), lambda b, t: (li, 0, 0)), *h_in],
        out_specs=(tok, hid, hid, tok, *h_out),
        scratch_shapes=[pltpu.VMEM((tl, d), bf16), pltpu.VMEM((tl, d), f32), pltpu.VMEM((nf, 8, ft), f32), *h_scr],
        compiler_params=_cparams(2, side_effects=bool(host)),
    )(x_mid, gain_pre, gain_post, w_l, conv_w, conv_b.reshape(conv_b.shape[0], 1, f), *h_ops)


def _ffn_bwd_act(x_mid, f_sv, g_sv, v_sv, dxo, gain_pre, gain_post, w_l, conv_w, conv_b, li, host=None):
    bsz, l, d = x_mid.shape
    f = w_l.shape[1]
    tl, ft = min(FFN_TL_BWD, l), min(FFN_FT, f)
    nt, nf = l // tl, f // ft
    h_in, h_shape, h_out, h_scr, h_ops = _host_args(host)

    def body(x_ref, f_ref, dxo_ref, g_ref, v_ref, gh_ref, gpre_ref, gpost_ref, w_ref, cw_ref, cb_ref, *rest):
        o0 = len(h_in)
        dx_ref, dg_ref, dv_ref, hdn_ref, h_ref, df_ref, dgain_ref, dconv_ref = rest[o0:o0 + 8]
        s0 = o0 + 8 + len(h_out)
        h_sc, df_sc, dh_acc, dgc_next = rest[s0:s0 + 4]
        b, t = pl.program_id(0), pl.program_id(1)
        tt = nt - 1 - t
        finish = _hosted(host, rest[:o0], rest[o0 + 8:s0], rest[s0 + 4:],
                         (b == 0) & (t == 0), (b == bsz - 1) & (t == nt - 1))

        @pl.when((b == 0) & (t == 0))
        def _():
            dgain_ref[...] = jnp.zeros_like(dgain_ref)
            dconv_ref[...] = jnp.zeros_like(dconv_ref)

        @pl.when(t == 0)
        def _():
            dgc_next[...] = jnp.zeros_like(dgc_next)

        hb = _rms_fwd(x_ref[...], gpre_ref[...]).astype(bf16)
        h_sc[...] = hb
        h_ref[...] = hb
        df, dgp = _rms_bwd(f_ref[...], gpost_ref[...], dxo_ref[...])
        dfb = df.astype(bf16)
        df_sc[...] = dfb
        df_ref[...] = dfb
        dgain_ref[1:2, :] += dgp

        def fold(a):
            return a.reshape(FFN_CH // 8, 8, ft).sum(axis=0)

        ahead = _dot_nt(df_sc[...], w_ref[2, 0:ft, :])
        for j in range(nf):
            dhdn = ahead
            if j + 1 < nf:
                ahead = _dot_nt(df_sc[...], w_ref[2, (j + 1) * ft:(j + 2) * ft, :])
            cols = slice(j * ft, (j + 1) * ft)
            w0, w1, w2, bias = (jnp.broadcast_to(r, (FFN_CH, ft))
                                for r in (cw_ref[0:1, cols], cw_ref[1:2, cols], cw_ref[2:3, cols], cb_ref[:, cols]))

            def conv_bwd(dgc, after):
                d1, d2 = _later_rows(jnp.concatenate([dgc, after], axis=0))
                return (w2 * dgc + w1 * d1 + w0 * d2).astype(bf16)

            g = g_ref[j].astype(f32)
            v = v_ref[j].astype(f32)
            hist = jnp.where(tt > 0, gh_ref[j].astype(f32)[8:16, :], 0.0)
            acc = [jnp.zeros((8, ft), f32)] * 4
            first, pending, hdn, dvs, dgs = None, None, [], [], []
            for r0 in range(0, tl, FFN_CH):
                g0, g1, g2 = _shifted_rows(jnp.concatenate([hist, g[r0:r0 + FFN_CH, :]], axis=0))
                hist = g0[FFN_CH - 8:, :]
                vc, dc = v[r0:r0 + FFN_CH, :], dhdn[r0:r0 + FFN_CH, :]
                u, ug = _gelu_and_grad(bias + w0 * g2 + w1 * g1 + w2 * g0)
                hdn.append((u * vc).astype(bf16))
                dvs.append((dc * u).astype(bf16))
                dgc = dc * vc * ug
                acc = [acc[0] + fold(dgc * g2), acc[1] + fold(dgc * g1), acc[2] + fold(dgc * g0), acc[3] + fold(dgc)]
                if pending is None:
                    first = dgc[0:8, :]
                else:
                    dgs.append(conv_bwd(pending, dgc[0:8, :]))
                pending = dgc
            dgs.append(conv_bwd(pending, dgc_next[j]))
            dgc_next[j] = first
            dgb, dvb = jnp.concatenate(dgs, axis=0), jnp.concatenate(dvs, axis=0)
            hdn_ref[j] = jnp.concatenate(hdn, axis=0)
            dg_ref[j] = dgb
            dv_ref[j] = dvb
            part = _dot(dgb, w_ref[0, cols, :]) + _dot(dvb, w_ref[1, cols, :])
            if j == 0:
                dh_acc[...] = part
            else:
                dh_acc[...] += part
            for k in range(4):
                dconv_ref[j, k:k + 1, :] += jnp.sum(acc[k], axis=0, keepdims=True)

        dxp, dgp = _rms_bwd(x_ref[...], gpre_ref[...], dh_acc[...])
        dx_ref[...] = dxo_ref[...] + dxp
        dgain_ref[0:1, :] += dgp
        finish()

    tok = pl.BlockSpec((None, tl, d), lambda b, t: (b, nt - 1 - t, 0))
    hid = pl.BlockSpec((nf, tl, ft), lambda b, t: (0, b * nt + nt - 1 - t, 0))
    halo = pl.BlockSpec((nf, 16, ft), lambda b, t: (0, jnp.maximum((b * nt + nt - 1 - t) * (tl // 16) - 1, 0), 0))
    gain = pl.BlockSpec((1, d), lambda b, t: (0, 0))
    hid_shape = SDS((nf, bsz * l, ft), bf16)
    return pl.pallas_call(
        body, name=f"ffn_bwd_act_{li}", grid=(bsz, nt),
        out_shape=(SDS((bsz, l, d), f32), hid_shape, hid_shape, hid_shape,
                   SDS((bsz, l, d), bf16), SDS((bsz, l, d), bf16), SDS((8, d), f32), SDS((nf, 8, ft), f32), *h_shape),
        in_specs=[tok, tok, tok, hid, hid, halo, gain, gain, _resident_spec(w_l.shape, 2),
                  pl.BlockSpec((None, 3, f), lambda b, t: (li, 0, 0)),
                  pl.BlockSpec((None, 1, f), lambda b, t: (li, 0, 0)), *h_in],
        out_specs=(tok, hid, hid, hid, tok, tok,
                   pl.BlockSpec((8, d), lambda b, t: (0, 0)), pl.BlockSpec((nf, 8, ft), lambda b, t: (0, 0, 0)),
                   *h_out),
        scratch_shapes=[pltpu.VMEM((tl, d), bf16), pltpu.VMEM((tl, d), bf16), pltpu.VMEM((tl, d), f32),
                        pltpu.VMEM((nf, 8, ft), f32), *h_scr],
        compiler_params=_cparams(2, side_effects=bool(host)),
    )(x_mid, f_sv, dxo, g_sv, v_sv, g_sv, gain_pre, gain_post, w_l, conv_w,
      conv_b.reshape(conv_b.shape[0], 1, f), *h_ops)


def _ffn_bwd_weights(dg, dv, hdn, h, df, li):
    nf, t, ft = dg.shape
    f = nf * ft
    d = h.shape[1]
    tr = min(WG_TR, t)
    nr = t // tr

    def body(dg_ref, dv_ref, hdn_ref, h_ref, df_ref, o_ref, acc):
        r = pl.program_id(1)

        @pl.when(r == 0)
        def _():
            acc[...] = jnp.zeros_like(acc)

        rows = pl.ds(pl.multiple_of(r * tr, tr), tr)
        hb = h_ref[rows, :]
        acc[0] += _dot_tn(dg_ref[...], hb)
        acc[1] += _dot_tn(dv_ref[...], hb)
        acc[2] += _dot_tn(hdn_ref[...], df_ref[rows, :])

        @pl.when(r == nr - 1)
        def _():
            o_ref[...] = acc[...].astype(bf16)

    hid = pl.BlockSpec((None, tr, ft), lambda j, r: (j, r, 0))
    tok = _resident_spec((t, d), 2)
    return pl.pallas_call(
        body, name=f"ffn_bwd_weights_{li}", grid=(nf, nr), out_shape=SDS((3, f, d), bf16),
        in_specs=[hid, hid, hid, tok, tok], out_specs=pl.BlockSpec((3, ft, d), lambda j, r: (0, j, 0)),
        scratch_shapes=[pltpu.VMEM((3, ft, d), f32)], compiler_params=_cparams(2),
    )(dg, dv, hdn, h, df)


def _pool_counts(t0, tl, d):
    gch = d // len(POOL_WINDOWS)
    tpos = (t0 + lax.broadcasted_iota(jnp.int32, (tl, d), 0) + 1).astype(f32)
    lane = lax.broadcasted_iota(jnp.int32, (tl, d), 1)
    win = jnp.full((tl, d), float(POOL_WINDOWS[-1]), f32)
    for gi in range(len(POOL_WINDOWS) - 2, -1, -1):
        win = jnp.where(lane < (gi + 1) * gch, float(POOL_WINDOWS[gi]), win)
    return jnp.minimum(tpos, win)


def _pool_select(parts, tl, d):
    gch = d // len(POOL_WINDOWS)
    lane = lax.broadcasted_iota(jnp.int32, (tl, d), 1)
    out = parts[-1]
    for gi in range(len(parts) - 2, -1, -1):
        out = jnp.where(lane < (gi + 1) * gch, parts[gi], out)
    return out


def _pool_window_sums(u, halo, tl):
    ext = jnp.concatenate([halo, u], axis=0)
    sums, cur = [], ext
    for k in (1, 2, 4, 8):
        cur = cur + pltpu.roll(cur, k, axis=0)
        sums.append(cur[POOL_HALO:POOL_HALO + tl, :])
    return sums


def _pool_mix(u, halo, cnt, pw_ref, scale, tl, d):
    gch = d // len(POOL_WINDOWS)
    pooled = _pool_select(_pool_window_sums(u, halo, tl), tl, d) / cnt
    diff = pooled - u
    outs = [_dot(diff[:, gi * gch:(gi + 1) * gch].astype(bf16), pw_ref[gi]) for gi in range(len(POOL_WINDOWS))]
    return diff, jnp.concatenate(outs, axis=1)


def _pool_fwd(x, gain_pre, gain_post, pw, scale, li):
    bsz, l, d = x.shape
    tl = min(POOL_TL, l)
    nt = l // tl

    def body(x_ref, gpre_ref, gpost_ref, pw_ref, sc_ref, xo_ref, halo):
        t = pl.program_id(1)

        @pl.when(t == 0)
        def _():
            halo[...] = jnp.zeros_like(halo)

        xv = x_ref[...]
        u = _rms_fwd(xv, gpre_ref[...])
        _, out = _pool_mix(u, halo[...], _pool_counts(t * tl, tl, d), pw_ref, sc_ref[...], tl, d)
        halo[...] = u[tl - POOL_HALO:tl, :]
        xo_ref[...] = xv + _rms_fwd(out * sc_ref[...], gpost_ref[...])

    tok = pl.BlockSpec((None, tl, d), lambda b, t: (b, t, 0))
    gain = pl.BlockSpec((1, d), lambda b, t: (0, 0))
    return pl.pallas_call(
        body, name=f"pool_fwd_{li}", grid=(bsz, nt), out_shape=SDS((bsz, l, d), f32),
        in_specs=[tok, gain, gain, pl.BlockSpec(pw.shape, lambda b, t: (0, 0, 0)), gain], out_specs=tok,
        scratch_shapes=[pltpu.VMEM((POOL_HALO, d), f32)], compiler_params=_cparams(2),
    )(x, gain_pre, gain_post, pw, scale)


def _pool_bwd(x, dxo, gain_pre, gain_post, pw, scale, li):
    bsz, l, d = x.shape
    tl = min(POOL_TL, l)
    nt = l // tl
    ng = len(POOL_WINDOWS)
    gch = d // ng
    n_ext = tl + POOL_HALO

    def body(x_ref, xh_ref, dxo_ref, gpre_ref, gpost_ref, pw_ref, sc_ref, dx_ref, dpw_ref, ds_ref, qnext):
        b, t = pl.program_id(0), pl.program_id(1)
        tt = nt - 1 - t

        @pl.when((b == 0) & (t == 0))
        def _():
            dpw_ref[...] = jnp.zeros_like(dpw_ref)
            ds_ref[...] = jnp.zeros_like(ds_ref)

        @pl.when(t == 0)
        def _():
            qnext[...] = jnp.zeros_like(qnext)

        xv = x_ref[...]
        gpre = gpre_ref[...]
        u = _rms_fwd(xv, gpre)
        uh = _rms_fwd(jnp.where(tt > 0, xh_ref[...], 0.0), gpre)
        cnt = _pool_counts(tt * tl, tl, d)
        scale_v = sc_ref[...]
        diff, out = _pool_mix(u, uh, cnt, pw_ref, scale_v, tl, d)
        dxo_v = dxo_ref[...]
        dm, dgpost = _rms_bwd(out * scale_v, gpost_ref[...], dxo_v)
        ds_ref[1:2, :] += dgpost
        ds_ref[2:3, :] += jnp.sum(dm * out, axis=0, keepdims=True)
        dout = (dm * scale_v).astype(bf16)
        ddiffs = []
        for gi in range(ng):
            sl = slice(gi * gch, (gi + 1) * gch)
            dpw_ref[gi] += _dot_tn(diff[:, sl].astype(bf16), dout[:, sl])
            ddiffs.append(_dot_nt(dout[:, sl], pw_ref[gi]))
        ddiff = jnp.concatenate(ddiffs, axis=1)
        q = ddiff / cnt
        ext = jnp.concatenate([q, qnext[...]], axis=0)
        sums, cur = [], ext
        for k in (1, 2, 4, 8):
            cur = cur + pltpu.roll(cur, n_ext - k, axis=0)
            sums.append(cur[0:tl, :])
        du = _pool_select(sums, tl, d) - ddiff
        qnext[...] = q[0:POOL_HALO, :]
        dxp, dgpre = _rms_bwd(xv, gpre, du)
        ds_ref[0:1, :] += dgpre
        dx_ref[...] = dxo_v + dxp

    tok = pl.BlockSpec((None, tl, d), lambda b, t: (b, nt - 1 - t, 0))
    halo = pl.BlockSpec((None, POOL_HALO, d),
                        lambda b, t: (b, jnp.maximum((nt - 1 - t) * (tl // POOL_HALO) - 1, 0), 0))
    gain = pl.BlockSpec((1, d), lambda b, t: (0, 0))
    return pl.pallas_call(
        body, name=f"pool_bwd_{li}", grid=(bsz, nt),
        out_shape=(SDS((bsz, l, d), f32), SDS((ng, gch, gch), f32), SDS((8, d), f32)),
        in_specs=[tok, halo, tok, gain, gain, pl.BlockSpec(pw.shape, lambda b, t: (0, 0, 0)), gain],
        out_specs=(tok, pl.BlockSpec((ng, gch, gch), lambda b, t: (0, 0, 0)), pl.BlockSpec((8, d), lambda b, t: (0, 0))),
        scratch_shapes=[pltpu.VMEM((POOL_HALO, d), f32)], compiler_params=_cparams(2),
    )(x, x, dxo, gain_pre, gain_post, pw, scale)


def _s5_discretise(lam_re, lam_im, log_dt, b_re, b_im):
    def body(lr_ref, li_ref, ld_ref, br_ref, bi_ref, ar_ref, ai_ref, bbr_ref, bbi_ref):
        lr, li = lr_ref[...], li_ref[...]
        dt = jnp.exp(ld_ref[...])
        mag = jnp.exp(lr * dt)
        ar = mag * jnp.cos(li * dt)
        ai = mag * jnp.sin(li * dt)
        den = lr * lr + li * li
        nr, ni = ar - 1.0, ai
        fr = (nr * lr + ni * li) / den
        fi = (ni * lr - nr * li) / den
        br, bi = br_ref[...], bi_ref[...]
        ar_ref[...] = ar
        ai_ref[...] = ai
        bbr_ref[...] = fr * br - fi * bi
        bbi_ref[...] = fr * bi + fi * br

    return pl.pallas_call(
        body, name="s5_discretise",
        out_shape=(SDS(lam_re.shape, f32), SDS(lam_re.shape, f32), SDS(b_re.shape, f32), SDS(b_re.shape, f32)),
        compiler_params=_cparams(),
    )(lam_re, lam_im, log_dt, b_re, b_im)


def _s5_discretise_bwd(lam_re, lam_im, log_dt, b_re, b_im, g_ar, g_ai, g_bbr, g_bbi):
    def body(lr_ref, li_ref, ld_ref, br_ref, bi_ref, gar_ref, gai_ref, gbbr_ref, gbbi_ref,
             dlr_ref, dli_ref, dld_ref, dbr_ref, dbi_ref):
        lr, li = lr_ref[...], li_ref[...]
        dt = jnp.exp(ld_ref[...])
        mag = jnp.exp(lr * dt)
        cs, sn = jnp.cos(li * dt), jnp.sin(li * dt)
        ar, ai = mag * cs, mag * sn
        den = lr * lr + li * li
        nr, ni = ar - 1.0, ai
        fr = (nr * lr + ni * li) / den
        fi = (ni * lr - nr * li) / den
        br, bi = br_ref[...], bi_ref[...]
        gbbr, gbbi = gbbr_ref[...], gbbi_ref[...]
        dbr_ref[...] = fr * gbbr + fi * gbbi
        dbi_ref[...] = fr * gbbi - fi * gbbr
        gfr = jnp.sum(br * gbbr + bi * gbbi, axis=2, keepdims=True)
        gfi = jnp.sum(br * gbbi - bi * gbbr, axis=2, keepdims=True)
        gnr_num, gni_num = gfr / den, gfi / den
        gden = -(gfr * fr + gfi * fi) / den
        g_nr = gnr_num * lr - gni_num * li
        g_ni = gnr_num * li + gni_num * lr
        dlr = gnr_num * nr + gni_num * ni + gden * 2.0 * lr
        dli = gnr_num * ni - gni_num * nr + gden * 2.0 * li
        gar = gar_ref[...] + g_nr
        gai = gai_ref[...] + g_ni
        gq = (gar * cs + gai * sn) * mag
        gth = (gai * cs - gar * sn) * mag
        dlr_ref[...] = dlr + gq * dt
        dli_ref[...] = dli + gth * dt
        dld_ref[...] = jnp.sum(gq * lr + gth * li, axis=3, keepdims=True) * dt

    return pl.pallas_call(
        body, name="s5_discretise_bwd",
        out_shape=(SDS(lam_re.shape, f32), SDS(lam_re.shape, f32), SDS(log_dt.shape, f32),
                   SDS(b_re.shape, f32), SDS(b_re.shape, f32)),
        compiler_params=_cparams(),
    )(lam_re, lam_im, log_dt, b_re, b_im, g_ar, g_ai, g_bbr, g_bbi)


def _fill_powers(pw_r, pw_i, ar, ai, nj, width):
    cb = min(S5_CB, width)
    for c0 in range(0, width, cb):
        sl = pl.ds(c0, cb)
        a_r, a_i = ar[:, c0:c0 + cb], ai[:, c0:c0 + cb]

        def step(i, p):
            pw_r[i, :, sl] = p[0]
            pw_i[i, :, sl] = p[1]
            return _cmul(p[0], p[1], a_r, a_i)

        lax.fori_loop(0, nj, step, (a_r, a_i))


def _interleaved_scan(xr_sc, xi_sc, ar, ai, pw_r, pw_i, carry_r, carry_i, nj, width, reverse, h_sc=None):
    cb = min(S5_CB, width)
    acc_out = []
    for c0 in range(0, width, cb):
        sl = pl.ds(c0, cb)
        a_r, a_i = ar[:, c0:c0 + cb], ai[:, c0:c0 + cb]
        aj_r, aj_i = pw_r[nj - 1, :, sl], pw_i[nj - 1, :, sl]

        def pos(i):
            return nj - 1 - i if reverse else i

        def local_step(i, st):
            j = pos(i)
            hr, hi = _cmul(a_r, a_i, st[0], st[1])
            hr, hi = hr + xr_sc[j, :, sl], hi + xi_sc[j, :, sl]
            xr_sc[j, :, sl] = hr
            xi_sc[j, :, sl] = hi
            return hr, hi

        zero = jnp.zeros((8, cb), f32)
        fin_r, fin_i = lax.fori_loop(0, nj, local_step, (zero, zero), unroll=S5_UNROLL)
        row = lax.broadcasted_iota(jnp.int32, (8, cb), 0)
        c_r, c_i = carry_r[0:1, sl], carry_i[0:1, sl]
        ent_r, ent_i = zero, zero
        order = range(7, -1, -1) if reverse else range(8)
        for s in order:
            ent_r = jnp.where(row == s, c_r, ent_r)
            ent_i = jnp.where(row == s, c_i, ent_i)
            pr, pi_ = _cmul(aj_r[0:1, :], aj_i[0:1, :], c_r, c_i)
            c_r, c_i = fin_r[s:s + 1, :] + pr, fin_i[s:s + 1, :] + pi_
        carry_r[:, sl] = jnp.broadcast_to(c_r, (8, cb))
        carry_i[:, sl] = jnp.broadcast_to(c_i, (8, cb))

        def fix_step(i, st):
            j = pos(i)
            nx_r, nx_i, acc_r, acc_i = st
            cr_, ci_ = _cmul(pw_r[i, :, sl], pw_i[i, :, sl], ent_r, ent_i)
            hr, hi = xr_sc[j, :, sl] + cr_, xi_sc[j, :, sl] + ci_
            xr_sc[j, :, sl] = hr
            xi_sc[j, :, sl] = hi
            if h_sc is not None:
                sr, si = h_sc[0][j, :, sl], h_sc[1][j, :, sl]
                acc_r = acc_r + nx_r * sr + nx_i * si
                acc_i = acc_i + nx_i * sr - nx_r * si
                nx_r, nx_i = hr, hi
            return nx_r, nx_i, acc_r, acc_i

        st = lax.fori_loop(0, nj, fix_step, (ent_r, ent_i, zero, zero), unroll=S5_UNROLL)
        acc_out.append((st[2], st[3]))
    return acc_out


def _diag_mask(n_rep, h, ks):
    assert h & (h - 1) == 0 and (ks // n_rep) & (ks // n_rep - 1) == 0
    r = lax.shift_right_logical(lax.broadcasted_iota(jnp.int32, (n_rep * h, ks), 0), h.bit_length() - 1)
    c = lax.shift_right_logical(lax.broadcasted_iota(jnp.int32, (n_rep * h, ks), 1), (ks // n_rep).bit_length() - 1)
    return r == c


def _expand_block_diag(compact, n_rep):
    h, ks = compact.shape
    full = jnp.concatenate([compact] * n_rep, axis=0)
    return jnp.where(_diag_mask(n_rep, h, ks), full, 0.0).astype(bf16)


def _compact_block_diag(full, n_rep):
    rows, ks = full.shape
    h = rows // n_rep
    return jnp.sum(jnp.where(_diag_mask(n_rep, h, ks), full, 0.0).reshape(n_rep, h, ks), axis=0)


def _s5_scan_fwd(x_il, gain_pre, d_skip, ab_r, ab_i, bbc_r, bbc_i, cc_r, cc_in, li, host=None):
    bsz, nblk, tq, d = x_il.shape
    nkb, gch, ks = bbc_r.shape
    kc = d // nkb
    n_rep = kc // gch
    nj = tq // 8
    h_in, h_shape, h_out, h_scr, h_ops = _host_args(host)

    def body(xf_ref, xk_ref, gk_ref, dk_ref, ar_ref, ai_ref, bbr_ref, bbi_ref, cr_ref, ci_ref, *rest):
        o0 = len(h_in)
        y_ref, hr_ref, hi_ref = rest[o0:o0 + 3]
        s0 = o0 + 3 + len(h_out)
        xr_sc, xi_sc, pw_r, pw_i, carry_r, carry_i, bbr_sc, bbi_sc, crt_sc, cit_sc = rest[s0:s0 + 10]
        k, b, n = pl.program_id(0), pl.program_id(1), pl.program_id(2)
        finish = _hosted(host, rest[:o0], rest[o0 + 3:s0], rest[s0 + 10:],
                         (k == 0) & (b == 0) & (n == 0), (k == nkb - 1) & (b == bsz - 1) & (n == nblk - 1))
        ar = jnp.broadcast_to(ar_ref[...], (8, ks))
        ai = jnp.broadcast_to(ai_ref[...], (8, ks))

        @pl.when((b == 0) & (n == 0))
        def _():
            bbr_sc[...] = _expand_block_diag(bbr_ref[...], n_rep)
            bbi_sc[...] = _expand_block_diag(bbi_ref[...], n_rep)
            crt_sc[...] = _expand_block_diag(cr_ref[...], n_rep)
            cit_sc[...] = _expand_block_diag(ci_ref[...], n_rep)
            _fill_powers(pw_r, pw_i, ar, ai, nj, ks)

        @pl.when(n == 0)
        def _():
            carry_r[...] = jnp.zeros_like(carry_r)
            carry_i[...] = jnp.zeros_like(carry_i)

        u = xk_ref[...] * _rms_scale(xf_ref[...]) * gk_ref[...]
        ub = u.astype(bf16)
        xr_sc[...] = _dot(ub, bbr_sc[...]).reshape(nj, 8, ks)
        xi_sc[...] = _dot(ub, bbi_sc[...]).reshape(nj, 8, ks)
        _interleaved_scan(xr_sc, xi_sc, ar, ai, pw_r, pw_i, carry_r, carry_i, nj, ks, reverse=False)
        hrb = xr_sc[...].reshape(tq, ks).astype(bf16)
        hib = xi_sc[...].reshape(tq, ks).astype(bf16)
        hr_ref[...] = hrb
        hi_ref[...] = hib
        crt, cit, dk = crt_sc[...], cit_sc[...], dk_ref[...]
        for r0 in range(0, tq, tq // 2):
            rows = slice(r0, r0 + tq // 2)
            y_ref[rows, :] = _dot_nt(hrb[rows, :], crt) + _dot_nt(hib[rows, :], cit) + dk * u[rows, :]
        finish()

    full = pl.BlockSpec((None, None, tq, d), lambda k, b, n: (b, n, 0, 0))
    chan = pl.BlockSpec((None, None, tq, kc), lambda k, b, n: (b, n, 0, k))
    stat = pl.BlockSpec((None, None, tq, ks), lambda k, b, n: (b, n, 0, k))
    vec_c = pl.BlockSpec((1, kc), lambda k, b, n: (0, k))
    vec_s = pl.BlockSpec((1, ks), lambda k, b, n: (0, k))
    cmap = pl.BlockSpec((None, gch, ks), lambda k, b, n: (k, 0, 0))
    s_tot = nkb * ks
    return pl.pallas_call(
        body, name=f"s5_scan_fwd_{li}", grid=(nkb, bsz, nblk),
        out_shape=(SDS((bsz, nblk, tq, d), f32), SDS((bsz, nblk, tq, s_tot), bf16), SDS((bsz, nblk, tq, s_tot), bf16),
                   *h_shape),
        in_specs=[full, chan, vec_c, vec_c, vec_s, vec_s, cmap, cmap, cmap, cmap, *h_in],
        out_specs=(chan, stat, stat, *h_out),
        scratch_shapes=[pltpu.VMEM((nj, 8, ks), f32)] * 4 + [pltpu.VMEM((8, ks), f32)] * 2
        + [pltpu.VMEM((kc, ks), bf16)] * 4 + h_scr,
        compiler_params=_cparams(3, side_effects=bool(host)),
    )(x_il, x_il, gain_pre, d_skip, ab_r, ab_i, bbc_r, bbc_i, cc_r, cc_in, *h_ops)


def _s5_glu_fwd(y, x_il, w_glu, b_glu, gain_post, li):
    t, d = y.shape
    tr = min(TOK_TR, t)

    def body(y_ref, x_ref, w_ref, b_ref, gp_ref, xo_ref):
        z = _gelu(y_ref[...])
        a = _dot(z.astype(bf16), w_ref[...]) + b_ref[...]
        xo_ref[...] = x_ref[...] + _rms_fwd(z * _sigmoid(a), gp_ref[...])

    tok = pl.BlockSpec((tr, d), lambda i: (i, 0))
    vec = pl.BlockSpec((1, d), lambda i: (0, 0))
    return pl.pallas_call(
        body, name=f"s5_glu_fwd_{li}", grid=(t // tr,), out_shape=SDS((t, d), f32),
        in_specs=[tok, tok, pl.BlockSpec((d, d), lambda i: (0, 0)), vec, vec], out_specs=tok,
        compiler_params=_cparams(1),
    )(y, x_il, w_glu, b_glu, gain_post)


def _s5_glu_bwd(y, dxo, w_glu, b_glu, gain_post, li):
    t, d = y.shape
    tr = min(TOK_TR, t)

    def body(y_ref, dxo_ref, w_ref, b_ref, gp_ref, dy_ref, dw_ref, ds_ref):
        @pl.when(pl.program_id(0) == 0)
        def _():
            dw_ref[...] = jnp.zeros_like(dw_ref)
            ds_ref[...] = jnp.zeros_like(ds_ref)

        z, zg = _gelu_and_grad(y_ref[...])
        zb = z.astype(bf16)
        w = w_ref[...]
        s = _sigmoid(_dot(zb, w) + b_ref[...])
        dm, dgpost = _rms_bwd(z * s, gp_ref[...], dxo_ref[...])
        da = dm * z * s * (1.0 - s)
        dab = da.astype(bf16)
        dz = dm * s + _dot_nt(dab, w)
        dw_ref[...] += _dot_tn(zb, dab)
        ds_ref[0:1, :] += dgpost
        ds_ref[1:2, :] += jnp.sum(da, axis=0, keepdims=True)
        dy_ref[...] = dz * zg

    tok = pl.BlockSpec((tr, d), lambda i: (i, 0))
    vec = pl.BlockSpec((1, d), lambda i: (0, 0))
    mat = pl.BlockSpec((d, d), lambda i: (0, 0))
    return pl.pallas_call(
        body, name=f"s5_glu_bwd_{li}", grid=(t // tr,),
        out_shape=(SDS((t, d), f32), SDS((d, d), f32), SDS((8, d), f32)),
        in_specs=[tok, tok, mat, vec, vec], out_specs=(tok, mat, pl.BlockSpec((8, d), lambda i: (0, 0))),
        compiler_params=_cparams(1),
    )(y, dxo, w_glu, b_glu, gain_post)


def _s5_scan_bwd(x_il, dy, h_r, h_i, gain_pre, d_skip, ab_r, ab_i, bbc_r, bbc_i, cc_r, cc_in, li, host=None):
    bsz, nblk, tq, d = x_il.shape
    nkb, gch, ks = bbc_r.shape
    kc = d // nkb
    n_rep = kc // gch
    nj = tq // 8
    h_in, h_shape, h_out, h_scr, h_ops = _host_args(host)

    def body(xf_ref, xk_ref, dy_ref, hr_ref, hi_ref, gk_ref, dk_ref, ar_ref, ai_ref, bbr_ref, bbi_ref, cr_ref, ci_ref,
             *rest):
        o0 = len(h_in)
        du_ref, dbbr_ref, dbbi_ref, dcr_ref, dci_ref, dar_ref, dai_ref, dd_ref = rest[o0:o0 + 8]
        s0 = o0 + 8 + len(h_out)
        (gr_sc, gi_sc, hr_sc, hi_sc, pw_r, pw_i, carry_r, carry_i, acc_r, acc_i,
         bbr_sc, bbi_sc, crt_sc, cit_sc, dbbr_acc, dbbi_acc, dcr_acc, dci_acc) = rest[s0:s0 + 18]
        k, b, n = pl.program_id(0), pl.program_id(1), pl.program_id(2)
        finish = _hosted(host, rest[:o0], rest[o0 + 8:s0], rest[s0 + 18:],
                         (k == 0) & (b == 0) & (n == 0), (k == nkb - 1) & (b == bsz - 1) & (n == nblk - 1))
        ar = jnp.broadcast_to(ar_ref[...], (8, ks))
        ai = jnp.broadcast_to(-ai_ref[...], (8, ks))

        @pl.when((b == 0) & (n == 0))
        def _():
            bbr_sc[...] = _expand_block_diag(bbr_ref[...], n_rep)
            bbi_sc[...] = _expand_block_diag(bbi_ref[...], n_rep)
            crt_sc[...] = _expand_block_diag(cr_ref[...], n_rep)
            cit_sc[...] = _expand_block_diag(ci_ref[...], n_rep)
            _fill_powers(pw_r, pw_i, ar, ai, nj, ks)
            for ref in (dar_ref, dai_ref, dd_ref, acc_r, acc_i, dbbr_acc, dbbi_acc, dcr_acc, dci_acc):
                ref[...] = jnp.zeros_like(ref)

        @pl.when(n == 0)
        def _():
            carry_r[...] = jnp.zeros_like(carry_r)
            carry_i[...] = jnp.zeros_like(carry_i)

        u = xk_ref[...] * _rms_scale(xf_ref[...]) * gk_ref[...]
        ub = u.astype(bf16)
        dyv = dy_ref[...]
        dyb = dyv.astype(bf16)
        dd_ref[0:1, :] += jnp.sum(dyv * u, axis=0, keepdims=True)
        hrb, hib = hr_ref[...], hi_ref[...]
        dcr_acc[...] += _dot_tn(dyb, hrb)
        dci_acc[...] += _dot_tn(dyb, hib)
        hr_sc[...] = hrb.astype(f32).reshape(nj, 8, ks)
        hi_sc[...] = hib.astype(f32).reshape(nj, 8, ks)
        gr_sc[...] = _dot(dyb, crt_sc[...]).reshape(nj, 8, ks)
        gi_sc[...] = _dot(dyb, cit_sc[...]).reshape(nj, 8, ks)
        accs = _interleaved_scan(gr_sc, gi_sc, ar, ai, pw_r, pw_i, carry_r, carry_i, nj, ks, reverse=True,
                                 h_sc=(hr_sc, hi_sc))
        cb = min(S5_CB, ks)
        for q, (a_r, a_i) in enumerate(accs):
            acc_r[:, q * cb:(q + 1) * cb] += a_r
            acc_i[:, q * cb:(q + 1) * cb] += a_i
        grb = gr_sc[...].reshape(tq, ks).astype(bf16)
        gib = gi_sc[...].reshape(tq, ks).astype(bf16)
        dbbr_acc[...] += _dot_tn(ub, grb)
        dbbi_acc[...] += _dot_tn(ub, gib)
        bbr, bbi, dk = bbr_sc[...], bbi_sc[...], dk_ref[...]
        for r0 in range(0, tq, tq // 2):
            rows = slice(r0, r0 + tq // 2)
            du_ref[rows, :] = dyv[rows, :] * dk + _dot_nt(grb[rows, :], bbr) + _dot_nt(gib[rows, :], bbi)

        @pl.when((b == bsz - 1) & (n == nblk - 1))
        def _():
            dar_ref[0:1, :] = jnp.sum(acc_r[...], axis=0, keepdims=True)
            dai_ref[0:1, :] = jnp.sum(acc_i[...], axis=0, keepdims=True)
            dbbr_ref[...] = _compact_block_diag(dbbr_acc[...], n_rep)
            dbbi_ref[...] = _compact_block_diag(dbbi_acc[...], n_rep)
            dcr_ref[...] = _compact_block_diag(dcr_acc[...], n_rep)
            dci_ref[...] = _compact_block_diag(dci_acc[...], n_rep)

        finish()

    full = pl.BlockSpec((None, None, tq, d), lambda k, b, n: (b, nblk - 1 - n, 0, 0))
    chan = pl.BlockSpec((None, None, tq, kc), lambda k, b, n: (b, nblk - 1 - n, 0, k))
    stat = pl.BlockSpec((None, None, tq, ks), lambda k, b, n: (b, nblk - 1 - n, 0, k))
    vec_c = pl.BlockSpec((1, kc), lambda k, b, n: (0, k))
    vec_s = pl.BlockSpec((1, ks), lambda k, b, n: (0, k))
    cmap = pl.BlockSpec((None, gch, ks), lambda k, b, n: (k, 0, 0))
    acc_s = pl.BlockSpec((None, 8, ks), lambda k, b, n: (k, 0, 0))
    acc_c = pl.BlockSpec((None, 8, kc), lambda k, b, n: (k, 0, 0))
    cshape = SDS((nkb, gch, ks), f32)
    return pl.pallas_call(
        body, name=f"s5_scan_bwd_{li}", grid=(nkb, bsz, nblk),
        out_shape=(SDS((bsz, nblk, tq, d), f32), cshape, cshape, cshape, cshape, SDS((nkb, 8, ks), f32),
                   SDS((nkb, 8, ks), f32), SDS((nkb, 8, kc), f32), *h_shape),
        in_specs=[full, chan, chan, stat, stat, vec_c, vec_c, vec_s, vec_s, cmap, cmap, cmap, cmap, *h_in],
        out_specs=(chan, cmap, cmap, cmap, cmap, acc_s, acc_s, acc_c, *h_out),
        scratch_shapes=([pltpu.VMEM((nj, 8, ks), f32)] * 6 + [pltpu.VMEM((8, ks), f32)] * 4
                        + [pltpu.VMEM((kc, ks), bf16)] * 4 + [pltpu.VMEM((kc, ks), f32)] * 4 + h_scr),
        compiler_params=_cparams(3, side_effects=bool(host)),
    )(x_il, x_il, dy, h_r, h_i, gain_pre, d_skip, ab_r, ab_i, bbc_r, bbc_i, cc_r, cc_in, *h_ops)


def _norm_residual_bwd(x, du, dxo, gain, name):
    t, d = x.shape
    tr = min(TOK_TR, t)

    def body(x_ref, du_ref, dxo_ref, g_ref, dx_ref, dg_ref):
        @pl.when(pl.program_id(0) == 0)
        def _():
            dg_ref[...] = jnp.zeros_like(dg_ref)

        dxp, dgain = _rms_bwd(x_ref[...], g_ref[...], du_ref[...])
        dx_ref[...] = dxo_ref[...] + dxp
        dg_ref[0:1, :] += dgain

    tok = pl.BlockSpec((tr, d), lambda i: (i, 0))
    return pl.pallas_call(
        body, name=name, grid=(t // tr,), out_shape=(SDS((t, d), f32), SDS((8, d), f32)),
        in_specs=[tok, tok, tok, pl.BlockSpec((1, d), lambda i: (0, 0))],
        out_specs=(tok, pl.BlockSpec((8, d), lambda i: (0, 0))), compiler_params=_cparams(1),
    )(x, du, dxo, gain)


def _loss_head(y, target):
    t, d = y.shape
    tr = min(TOK_TR, t)

    def body(y_ref, t_ref, l_ref, dy_ref):
        @pl.when(pl.program_id(0) == 0)
        def _():
            l_ref[...] = jnp.zeros_like(l_ref)

        err = y_ref[...] - t_ref[...]
        dy_ref[...] = err * (1.0 / d)
        l_ref[...] += jnp.sum(jnp.sum(err * err, axis=1, keepdims=True), axis=0, keepdims=True)

    tok = pl.BlockSpec((tr, d), lambda i: (i, 0))
    return pl.pallas_call(
        body, name="loss_head", grid=(t // tr,), out_shape=(SDS((8, 128), f32), SDS((t, d), f32)),
        in_specs=[tok, tok], out_specs=(pl.BlockSpec((8, 128), lambda i: (0, 0)), tok), compiler_params=_cparams(1),
    )(y, target)


def _interleave(a, tq):
    bsz, l, d = a.shape
    return a.reshape(bsz, l // tq, 8, tq // 8, d).transpose(0, 1, 3, 2, 4).reshape(bsz, l // tq, tq, d)


def _deinterleave(a, l):
    bsz, nblk, tq, d = a.shape
    return a.reshape(bsz, nblk, tq // 8, 8, d).transpose(0, 1, 3, 2, 4).reshape(bsz, l, d)


def _compact_maps(a, nkb):
    ns, g, h, p = a.shape
    gl = g // nkb
    return a.reshape(ns, nkb, gl, h, p).transpose(0, 1, 3, 2, 4).reshape(ns, nkb, h, gl * p)


def _uncompact_maps(a, g):
    nkb, h, cols = a.shape
    gl = g // nkb
    return a.reshape(nkb, h, gl, cols // gl).transpose(0, 2, 1, 3).reshape(g, h, cols // gl)


def _pack_rows(arrs, cols, row_mult):
    flat = jnp.concatenate([a.reshape(-1).astype(f32) for a in arrs])
    rows = -(-flat.shape[0] // cols)
    rows = -(-rows // row_mult) * row_mult
    return jnp.pad(flat, (0, rows * cols - flat.shape[0])).reshape(rows, cols)


def _unpack_rows(buf, shapes):
    flat = buf.reshape(-1)
    out, off = [], 0
    for s in shapes:
        n = math.prod(s)
        out.append(flat[off:off + n].reshape(s))
        off += n
    return out


W_NAMES = ['s5_lambda_re', 's5_lambda_im', 's5_log_dt', 's5_b_re', 's5_b_im', 's5_c_re', 's5_c_im', 's5_d', 's5_w_glu',
           's5_b_glu', 'pool_w', 'pool_scale', 'ffn_w_gate', 'ffn_w_val', 'ffn_conv_w', 'ffn_conv_b', 'ffn_w_down',
           'norm_mix_pre', 'norm_mix_post', 'norm_ffn_pre', 'norm_ffn_post']
BIG = ('s5_w_glu', 'ffn_w_gate', 'ffn_w_val', 'ffn_w_down')


def kernel(x, s5_lambda_re, s5_lambda_im, s5_log_dt, s5_b_re, s5_b_im, s5_c_re, s5_c_im, s5_d, s5_w_glu, s5_b_glu, pool_w, pool_scale, ffn_w_gate, ffn_w_val, ffn_conv_w, ffn_conv_b, ffn_w_down, norm_mix_pre, norm_mix_post, norm_ffn_pre, norm_ffn_post, loss_target, m_s5_lambda_re, m_s5_lambda_im, m_s5_log_dt, m_s5_b_re, m_s5_b_im, m_s5_c_re, m_s5_c_im, m_s5_d, m_s5_w_glu, m_s5_b_glu, m_pool_w, m_pool_scale, m_ffn_w_gate, m_ffn_w_val, m_ffn_conv_w, m_ffn_conv_b, m_ffn_w_down, m_norm_mix_pre, m_norm_mix_post, m_norm_ffn_pre, m_norm_ffn_post, v_s5_lambda_re, v_s5_lambda_im, v_s5_log_dt, v_s5_b_re, v_s5_b_im, v_s5_c_re, v_s5_c_im, v_s5_d, v_s5_w_glu, v_s5_b_glu, v_pool_w, v_pool_scale, v_ffn_w_gate, v_ffn_w_val, v_ffn_conv_w, v_ffn_conv_b, v_ffn_w_down, v_norm_mix_pre, v_norm_mix_post, v_norm_ffn_pre, v_norm_ffn_post):
    w_in = dict(zip(W_NAMES, (s5_lambda_re, s5_lambda_im, s5_log_dt, s5_b_re, s5_b_im, s5_c_re, s5_c_im, s5_d, s5_w_glu,
                              s5_b_glu, pool_w, pool_scale, ffn_w_gate, ffn_w_val, ffn_conv_w, ffn_conv_b, ffn_w_down,
                              norm_mix_pre, norm_mix_post, norm_ffn_pre, norm_ffn_post)))
    m_in = dict(zip(W_NAMES, (m_s5_lambda_re, m_s5_lambda_im, m_s5_log_dt, m_s5_b_re, m_s5_b_im, m_s5_c_re, m_s5_c_im,
                              m_s5_d, m_s5_w_glu, m_s5_b_glu, m_pool_w, m_pool_scale, m_ffn_w_gate, m_ffn_w_val,
                              m_ffn_conv_w, m_ffn_conv_b, m_ffn_w_down, m_norm_mix_pre, m_norm_mix_post,
                              m_norm_ffn_pre, m_norm_ffn_post)))
    v_in = dict(zip(W_NAMES, (v_s5_lambda_re, v_s5_lambda_im, v_s5_log_dt, v_s5_b_re, v_s5_b_im, v_s5_c_re, v_s5_c_im,
                              v_s5_d, v_s5_w_glu, v_s5_b_glu, v_pool_w, v_pool_scale, v_ffn_w_gate, v_ffn_w_val,
                              v_ffn_conv_w, v_ffn_conv_b, v_ffn_w_down, v_norm_mix_pre, v_norm_mix_post,
                              v_norm_ffn_pre, v_norm_ffn_post)))

    bsz, seq, d = x.shape
    depth = ffn_w_gate.shape[0]
    n_s5, n_grp, n_state = s5_lambda_re.shape
    n_gch = s5_b_re.shape[3]
    n_pool = pool_w.shape[0]
    fs = ffn_w_gate.shape[2]
    f = N_CHIPS * fs
    gs = s5_w_glu.shape[1]
    nkb = d // S5_KB_CH
    tq = min(S5_TQ, seq)
    chip = 2 * lax.axis_index("x") + lax.axis_index("y")

    s_ffn = jnp.stack([ffn_w_gate.transpose(0, 2, 1), ffn_w_val.transpose(0, 2, 1), ffn_w_down], axis=1).astype(bf16)
    small_shard = [pool_w, pool_scale, ffn_conv_w]
    first_hosts = [("gather", 1, s_ffn[0]), ("gather", 1, s5_w_glu.astype(bf16)),
                   ("whole", 0, _pack_rows(small_shard, d, 16))]

    lam_r4 = s5_lambda_re.reshape(n_s5, n_grp, 1, n_state)
    lam_i4 = s5_lambda_im.reshape(n_s5, n_grp, 1, n_state)
    ldt4 = s5_log_dt.reshape(n_s5, n_grp, 1, 1)
    b_r4 = s5_b_re.transpose(0, 1, 3, 2)
    b_i4 = s5_b_im.transpose(0, 1, 3, 2)
    ab_r4, ab_i4, bb_r4, bb_i4 = _s5_discretise(lam_r4, lam_i4, ldt4, b_r4, b_i4)
    n_st_tot = n_grp * n_state
    ab_r, ab_i = ab_r4.reshape(n_s5, 1, n_st_tot), ab_i4.reshape(n_s5, 1, n_st_tot)
    bbc_r, bbc_i = _compact_maps(bb_r4, nkb), _compact_maps(bb_i4, nkb)
    cc_r, cc_in = _compact_maps(s5_c_re, nkb), _compact_maps(-s5_c_im, nkb)

    def s5_params(j):
        return dict(ab_r=ab_r[j], ab_i=ab_i[j], bb_r=bbc_r[j], bb_i=bbc_i[j], c_r=cc_r[j], c_in=cc_in[j],
                    d_skip=s5_d[j].reshape(1, d), b_glu=s5_b_glu[j].reshape(1, d))

    def row(a, i):
        return a[i].reshape(1, -1)

    assert depth >= 1 and N_CHIPS == 4
    saved = []
    xc = x
    w_layer = [None] * depth
    for i in range(depth):
        j = i // 2
        sv = dict(x_in=xc)
        if i % 2 == 0:
            sp = s5_params(j)
            x_il = _interleave(xc, tq)
            res = _s5_scan_fwd(x_il, row(norm_mix_pre, i), sp["d_skip"], sp["ab_r"], sp["ab_i"], sp["bb_r"], sp["bb_i"],
                               sp["c_r"], sp["c_in"], i, host=first_hosts if i == 0 else None)
            y_il, h_r, h_i = res[:3]
            if i == 0:
                w_layer[0], g_glu, g_small = res[3:6]
                parts = [_unpack_rows(g_small[k], [a.shape for a in small_shard]) for k in range(N_CHIPS)]
                pool_w_full = jnp.concatenate([p[0] for p in parts], axis=2).astype(bf16)
                pool_scale_full = jnp.concatenate([p[1] for p in parts], axis=1)
                conv_w_full = jnp.concatenate([p[2] for p in parts], axis=2)
            sp["w_glu"] = g_glu[j]
            xo_il = _s5_glu_fwd(y_il.reshape(bsz * seq, d), x_il.reshape(bsz * seq, d), sp["w_glu"], sp["b_glu"],
                                row(norm_mix_post, i), i)
            x_mid = _deinterleave(xo_il.reshape(x_il.shape), seq)
            sv.update(sp=sp, x_il=x_il, y_il=y_il, h_r=h_r, h_i=h_i)
        else:
            x_mid = _pool_fwd(xc, row(norm_mix_pre, i), row(norm_mix_post, i), pool_w_full[j], row(pool_scale_full, j), i)
        res = _ffn_fwd(x_mid, row(norm_ffn_pre, i), row(norm_ffn_post, i), w_layer[i], conv_w_full, ffn_conv_b, i,
                       host=[("gather", 1, s_ffn[i + 1])] if i + 1 < depth else None)
        xc, g_sv, v_sv, f_sv = res[:4]
        if i + 1 < depth:
            w_layer[i + 1] = res[4]
        sv.update(x_mid=x_mid, g=g_sv, v=v_sv, f=f_sv)
        saved.append(sv)

    sq, dy = _loss_head(xc.reshape(bsz * seq, d), loss_target.reshape(bsz * seq, d))
    loss = lax.psum(sq[0, 0] * (0.5 / d), ("x", "y", "c"))

    dx = dy.reshape(bsz, seq, d)
    dw_layers = [None] * depth
    r_layers = [None] * depth
    g_small_params = {n: [None] * w_in[n].shape[0] for n in W_NAMES if n not in BIG}
    dglu = [None] * n_s5
    for i in range(depth - 1, -1, -1):
        j = i // 2
        sv = saved[i]
        res = _ffn_bwd_act(sv["x_mid"], sv["f"], sv["g"], sv["v"], dx, row(norm_ffn_pre, i), row(norm_ffn_post, i),
                           w_layer[i], conv_w_full, ffn_conv_b, i,
                           host=[("slab", 1, dw_layers[i + 1])] if i + 1 < depth else None)
        dx, dg, dv, hdn, hb, dfb, dgain, dconv = res[:8]
        if i + 1 < depth:
            r_layers[i + 1] = res[8]
        dw_layers[i] = _ffn_bwd_weights(dg, dv, hdn, hb.reshape(bsz * seq, d), dfb.reshape(bsz * seq, d), i)
        g_small_params["norm_ffn_pre"][i] = dgain[0]
        g_small_params["norm_ffn_post"][i] = dgain[1]
        dconv = dconv.transpose(1, 0, 2).reshape(8, f)
        g_small_params["ffn_conv_w"][i] = dconv[0:3]
        g_small_params["ffn_conv_b"][i] = dconv[3]
        if i % 2 == 0:
            sp = sv["sp"]
            dxo_il = _interleave(dx, tq)
            dy_s, dglu[j], ds_glu = _s5_glu_bwd(sv["y_il"].reshape(bsz * seq, d), dxo_il.reshape(bsz * seq, d),
                                                sp["w_glu"], sp["b_glu"], row(norm_mix_post, i), i)
            res = _s5_scan_bwd(sv["x_il"], dy_s.reshape(sv["x_il"].shape), sv["h_r"], sv["h_i"], row(norm_mix_pre, i),
                               sp["d_skip"], sp["ab_r"], sp["ab_i"], sp["bb_r"], sp["bb_i"], sp["c_r"], sp["c_in"], i,
                               host=[("slab", 1, dw_layers[0])] if i == 0 else None)
            du, dbb_r, dbb_i, dc_r, dc_in, dab_r, dab_i, dd = res[:8]
            if i == 0:
                r_layers[0] = res[8]
            dx_il, dgpre = _norm_residual_bwd(sv["x_il"].reshape(bsz * seq, d), du.reshape(bsz * seq, d),
                                              dxo_il.reshape(bsz * seq, d), row(norm_mix_pre, i), f"s5_pre_bwd_{i}")
            dx = _deinterleave(dx_il.reshape(sv["x_il"].shape), seq)
            g_small_params["norm_mix_pre"][i] = dgpre[0]
            g_small_params["norm_mix_post"][i] = ds_glu[0]
            g_small_params["s5_b_glu"][j] = ds_glu[1]
            g_small_params["s5_d"][j] = dd[:, 0, :].reshape(d)
            g_small_params["s5_c_re"][j] = _uncompact_maps(dc_r, n_grp)
            g_small_params["s5_c_im"][j] = -_uncompact_maps(dc_in, n_grp)
            sv["g_ab"] = (dab_r[:, 0, :].reshape(n_grp, 1, n_state), dab_i[:, 0, :].reshape(n_grp, 1, n_state),
                          _uncompact_maps(dbb_r, n_grp), _uncompact_maps(dbb_i, n_grp))
        else:
            dx, dpw, dsm = _pool_bwd(sv["x_in"], dx, row(norm_mix_pre, i), row(norm_mix_post, i), pool_w_full[j],
                                     row(pool_scale_full, j), i)
            g_small_params["norm_mix_pre"][i] = dsm[0]
            g_small_params["norm_mix_post"][i] = dsm[1]
            g_small_params["pool_scale"][j] = dsm[2]
            g_small_params["pool_w"][j] = dpw
    grad_x = dx

    g_ab = [saved[2 * j]["g_ab"] for j in range(n_s5)]
    d_lr, d_li, d_ld, d_br, d_bi = _s5_discretise_bwd(
        lam_r4, lam_i4, ldt4, b_r4, b_i4, jnp.stack([g[0] for g in g_ab]), jnp.stack([g[1] for g in g_ab]),
        jnp.stack([g[2] for g in g_ab]), jnp.stack([g[3] for g in g_ab]))
    small_full = {n: (jnp.stack(v) if v[0] is not None else None) for n, v in g_small_params.items()}
    small_full["s5_lambda_re"] = d_lr.reshape(n_s5, n_grp, n_state)
    small_full["s5_lambda_im"] = d_li.reshape(n_s5, n_grp, n_state)
    small_full["s5_log_dt"] = d_ld.reshape(n_s5, n_grp)
    small_full["s5_b_re"] = d_br.transpose(0, 1, 3, 2)
    small_full["s5_b_im"] = d_bi.transpose(0, 1, 3, 2)
    bc_names = ["s5_b_re", "s5_b_im", "s5_c_re", "s5_c_im"]
    small_names = [n for n in W_NAMES if n not in BIG and n not in bc_names and n != "pool_w"]
    q = _pack_rows([small_full[n] for n in small_names], d, 16)
    q_bc = _pack_rows([small_full[n] for n in bc_names], d, 16).astype(bf16)

    r_glu, r_pw, r_q, r_bc = _chip_exchange(
        [("slab", 1, jnp.stack(dglu).astype(bf16)), ("slab", 2, small_full["pool_w"]), ("whole", 0, q), ("whole", 0, q_bc)],
        "exchange_small_grads")
    p_ffn = _sum_slots_layers([r.reshape(N_CHIPS, 3 * fs, d) for r in r_layers], "sum_slots_ffn").reshape(depth * 3 * fs, d)
    p_glu = _sum_slots(r_glu.reshape(N_CHIPS, n_s5 * gs, d), "sum_slots_glu")
    pw_rows, pw_cols = math.prod(pool_w.shape[:3]), pool_w.shape[3]
    p_pw = _sum_slots(r_pw.reshape(N_CHIPS, pw_rows, pw_cols), "sum_slots_pool_w")
    p_q = _sum_slots(r_q, "sum_slots_small")
    p_bc = _sum_slots(r_bc, "sum_slots_bc")
    o_ffn, o_glu, o_pw, o_q, o_bc = _exchange_with_sibling([p_ffn, p_glu, p_pw, p_q, p_bc])

    outs = {}

    def put(name, res, shape):
        outs[name] = tuple(r.reshape(shape) for r in res)

    pf, of = p_ffn.reshape(depth, 3, fs, d), o_ffn.reshape(depth, 3, fs, d)
    for kind, name in ((0, "ffn_w_gate"), (1, "ffn_w_val")):
        parts_g = [a[:, kind].transpose(0, 2, 1).reshape(depth * d, fs) for a in (pf, of)]
        put(name, _adamw(parts_g, w_in[name].reshape(depth * d, fs), m_in[name].reshape(depth * d, fs),
                         v_in[name].reshape(depth * d, fs), f"adamw_{name}"), w_in[name].shape)
    put("ffn_w_down", _adamw([pf[:, 2].reshape(depth * fs, d), of[:, 2].reshape(depth * fs, d)],
                             ffn_w_down.reshape(depth * fs, d), m_ffn_w_down.reshape(depth * fs, d),
                             v_ffn_w_down.reshape(depth * fs, d), "adamw_ffn_w_down"), ffn_w_down.shape)
    put("s5_w_glu", _adamw([p_glu, o_glu], s5_w_glu.reshape(n_s5 * gs, d), m_s5_w_glu.reshape(n_s5 * gs, d),
                           v_s5_w_glu.reshape(n_s5 * gs, d), "adamw_s5_w_glu"), s5_w_glu.shape)

    q_tot, bc_tot, pw_tot = _add_pairs([(p_q, o_q), (p_bc, o_bc), (p_pw, o_pw)], "sum_small")
    g_small = dict(zip(small_names, _unpack_rows(q_tot, [small_full[n].shape for n in small_names])))
    g_small.update(zip(bc_names, _unpack_rows(bc_tot, [small_full[n].shape for n in bc_names])))
    g_small["pool_w"] = pw_tot.reshape(pool_w.shape)
    g_small["pool_scale"] = lax.dynamic_slice_in_dim(g_small["pool_scale"], chip * pool_scale.shape[1],
                                                     pool_scale.shape[1], axis=1)
    g_small["ffn_conv_w"] = lax.dynamic_slice_in_dim(g_small["ffn_conv_w"], chip * fs, fs, axis=2)
    all_small = [n for n in W_NAMES if n not in BIG]
    local_shapes = [w_in[n].shape for n in all_small]
    res = _adamw([_pack_rows([g_small[n] for n in all_small], d, 64)],
                 _pack_rows([w_in[n] for n in all_small], d, 64), _pack_rows([m_in[n] for n in all_small], d, 64),
                 _pack_rows([v_in[n] for n in all_small], d, 64), "adamw_small")
    unpacked = [_unpack_rows(r, local_shapes) for r in res]
    for idx, n in enumerate(all_small):
        outs[n] = (g_small[n], unpacked[1][idx], unpacked[2][idx], unpacked[3][idx])

    return (loss, grad_x, *[outs[n][0] for n in W_NAMES], *[outs[n][1] for n in W_NAMES],
            *[outs[n][2] for n in W_NAMES], *[outs[n][3] for n in W_NAMES])
```

```python
import math

import jax
import jax.numpy as jnp
from jax import lax
from jax.experimental import pallas as pl
from jax.experimental.pallas import tpu as pltpu

f32, bf16 = jnp.float32, jnp.bfloat16
SDS = jax.ShapeDtypeStruct
MESH = pl.DeviceIdType.MESH

RMS_EPS = 1e-6
GELU_C = math.sqrt(2.0 / math.pi)
GELU_K = 0.044715
ADAM_LR, ADAM_B1, ADAM_B2, ADAM_EPS, ADAM_WD, ADAM_STEP = 0.001, 0.9, 0.999, 1e-08, 0.01, 10
POOL_WINDOWS = (2, 4, 8, 16)
POOL_HALO = 16
S5_GROUP_CH = 16
S5_STATE = 64
S5_KB_CH = 256
N_CHIPS = 4

FFN_TL = 512
FFN_TL_BWD = 256
FFN_FT = 256
FFN_CH = 16
FFN_SPLIT = 2
WG_TR = 2048
POOL_TL = 512
S5_TQ = 512
S5_CB = 1024
S5_UNROLL = 2
TOK_TR = 512
VMEM_LIMIT = 56 * 1024 * 1024

HBM = pl.BlockSpec(memory_space=pltpu.HBM)


def _cparams(n_axes=0, side_effects=False):
    kw = dict(vmem_limit_bytes=VMEM_LIMIT)
    if n_axes:
        kw["dimension_semantics"] = ("arbitrary",) * n_axes
    if side_effects:
        kw["has_side_effects"] = True
    return pltpu.CompilerParams(**kw)


def _dot(a, b):
    return jnp.dot(a, b, preferred_element_type=f32)


def _dot_nt(a, b):
    return lax.dot_general(a, b, (((1,), (1,)), ((), ())), preferred_element_type=f32)


def _dot_tn(a, b):
    return lax.dot_general(a, b, (((0,), (0,)), ((), ())), preferred_element_type=f32)


def _rms_scale(x):
    return lax.rsqrt(jnp.mean(x * x, axis=-1, keepdims=True) + RMS_EPS)


def _rms_fwd(x, gain):
    return x * _rms_scale(x) * gain


def _rms_bwd(x, gain, dy):
    r = _rms_scale(x)
    xn = x * r
    dgain = jnp.sum(dy * xn, axis=0, keepdims=True)
    dxn = dy * gain
    dx = r * (dxn - xn * jnp.mean(dxn * xn, axis=-1, keepdims=True))
    return dx, dgain


def _gelu(x):
    t = jnp.tanh(x * (GELU_C + (GELU_C * GELU_K) * (x * x)))
    hx = 0.5 * x
    return hx + hx * t


def _gelu_and_grad(x):
    x2 = x * x
    t = jnp.tanh(x * (GELU_C + (GELU_C * GELU_K) * x2))
    hx = 0.5 * x
    return hx + hx * t, (0.5 + 0.5 * t) + hx * (1.0 - t * t) * (GELU_C + (3.0 * GELU_C * GELU_K) * x2)


def _sigmoid(x):
    return 1.0 / (1.0 + jnp.exp(-x))


def _cmul(ar, ai, br, bi):
    return ar * br - ai * bi, ar * bi + ai * br


def _row_block(n, cap):
    best = None
    for d in range(16, min(n, cap) + 1, 16):
        if n % d == 0:
            best = d
    assert best is not None, n
    return best


def _mesh_pos():
    return lax.axis_index("x"), lax.axis_index("y"), lax.axis_index("c")


def _other_chips(x, y):
    return [(1 - x, y), (x, 1 - y), (1 - x, 1 - y)]


def _slab_index(ndim, axis, start, size):
    return tuple(pl.ds(start, size) if a == axis else slice(None) for a in range(ndim))


class _ChipExchange:
    def __init__(self, kind, axis, src, dst, send_sems, recv_sems, loc_sem):
        x, y, c = _mesh_pos()
        me = 2 * x + y
        nd = len(src.shape)
        size = src.shape[axis] if kind == "gather" else src.shape[axis] // N_CHIPS

        def src_for(kk):
            return src.at[_slab_index(nd, axis, kk * size, size)] if kind == "slab" else src

        def dst_for(kk):
            return dst.at[_slab_index(nd, axis, kk * size, size)] if kind == "gather" else dst.at[kk]

        self.own = pltpu.make_async_copy(src_for(me), dst_for(me), loc_sem)
        self.sends, self.recvs = [], []
        for j, chip in enumerate(_other_chips(x, y)):
            kk = 2 * chip[0] + chip[1]
            peer = dict(send_sem=send_sems.at[j], recv_sem=recv_sems.at[j], device_id=(chip[0], chip[1], c),
                        device_id_type=MESH)
            self.sends.append(pltpu.make_async_remote_copy(src_ref=src_for(kk), dst_ref=dst_for(me), **peer))
            self.recvs.append(pltpu.make_async_remote_copy(src_ref=src_for(me), dst_ref=dst_for(kk), **peer))

    def start(self):
        self.own.start()
        for cp in self.sends:
            cp.start()

    def finish(self):
        for cp in self.recvs:
            cp.wait_recv()
        for cp in self.sends:
            cp.wait_send()
        self.own.wait()


def _exchange_out_shape(kind, axis, src):
    if kind == "gather":
        shape = tuple(N_CHIPS * n if a == axis else n for a, n in enumerate(src.shape))
    elif kind == "slab":
        shape = (N_CHIPS,) + tuple(n // N_CHIPS if a == axis else n for a, n in enumerate(src.shape))
    else:
        shape = (N_CHIPS,) + tuple(src.shape)
    return SDS(shape, src.dtype)


EXCHANGE_SEMS = [pltpu.SemaphoreType.DMA((3,)), pltpu.SemaphoreType.DMA((3,)), pltpu.SemaphoreType.DMA((1,))]


def _chip_exchange(items, name):
    n = len(items)

    def body(*refs):
        ins, outs, sems = refs[:n], refs[n:2 * n], refs[2 * n:]
        exs = [_ChipExchange(kind, axis, ins[i], outs[i], sems[3 * i], sems[3 * i + 1], sems[3 * i + 2].at[0])
               for i, (kind, axis, _) in enumerate(items)]
        for ex in exs:
            ex.start()
        for ex in exs:
            ex.finish()

    return pl.pallas_call(
        body, name=name, out_shape=tuple(_exchange_out_shape(k, a, arr) for k, a, arr in items),
        in_specs=[HBM] * n, out_specs=tuple([HBM] * n), scratch_shapes=EXCHANGE_SEMS * n,
        compiler_params=_cparams(side_effects=True),
    )(*[arr for _, _, arr in items])


def _hosted(hosts, host_ins, host_outs, sems, first, last):
    if not hosts:
        return lambda: None
    exs = [_ChipExchange(h[0], h[1], host_ins[i], host_outs[i], sems[3 * i], sems[3 * i + 1], sems[3 * i + 2].at[0])
           for i, h in enumerate(hosts)]

    @pl.when(first)
    def _():
        for ex in exs:
            ex.start()

    def finish():
        @pl.when(last)
        def _():
            for ex in exs:
                ex.finish()

    return finish


def _host_args(hosts):
    hosts = hosts or []
    n = len(hosts)
    return [HBM] * n, [_exchange_out_shape(*h) for h in hosts], [HBM] * n, EXCHANGE_SEMS * n, [h[2] for h in hosts]


def _exchange_with_sibling(parts):
    n = len(parts)

    def body(*refs):
        ins, outs = refs[:n], refs[n:2 * n]
        send_sems, recv_sems = refs[2 * n:]
        x, y, c = _mesh_pos()
        cps = [pltpu.make_async_remote_copy(src_ref=s, dst_ref=t, send_sem=send_sems.at[i], recv_sem=recv_sems.at[i],
                                            device_id=(x, y, 1 - c), device_id_type=MESH)
               for i, (s, t) in enumerate(zip(ins, outs))]
        for cp in cps:
            cp.start()
        for cp in cps:
            cp.wait()

    return pl.pallas_call(
        body, name="exchange_with_sibling",
        out_shape=tuple(SDS(p.shape, p.dtype) for p in parts),
        in_specs=[HBM] * n, out_specs=tuple([HBM] * n),
        scratch_shapes=[pltpu.SemaphoreType.DMA((n,)), pltpu.SemaphoreType.DMA((n,))],
        compiler_params=_cparams(side_effects=True),
    )(*parts)


def _sum_slots(r, name):
    _, rows, cols = r.shape
    tr = _row_block(rows, 512)

    def body(r_ref, o_ref):
        o_ref[...] = ((r_ref[0].astype(f32) + r_ref[1].astype(f32)) + r_ref[2].astype(f32)) + r_ref[3].astype(f32)

    return pl.pallas_call(
        body, name=name, grid=(rows // tr,), out_shape=SDS((rows, cols), f32),
        in_specs=[pl.BlockSpec((N_CHIPS, tr, cols), lambda i: (0, i, 0))],
        out_specs=pl.BlockSpec((tr, cols), lambda i: (i, 0)),
        compiler_params=_cparams(1),
    )(r)


def _sum_slots_layers(rs, name):
    _, rows, cols = rs[0].shape
    nl = len(rs)
    tr = _row_block(rows, 256)

    def body(*refs):
        o_ref = refs[nl]
        for l in range(nl):
            r = refs[l]
            o_ref[l] = ((r[0].astype(f32) + r[1].astype(f32)) + r[2].astype(f32)) + r[3].astype(f32)

    return pl.pallas_call(
        body, name=name, grid=(rows // tr,), out_shape=SDS((nl, rows, cols), f32),
        in_specs=[pl.BlockSpec((N_CHIPS, tr, cols), lambda i: (0, i, 0))] * nl,
        out_specs=pl.BlockSpec((nl, tr, cols), lambda i: (0, i, 0)),
        compiler_params=_cparams(1),
    )(*rs)


def _add_pairs(pairs, name):
    n = len(pairs)

    def body(*refs):
        for i in range(n):
            refs[2 * n + i][...] = refs[2 * i][...] + refs[2 * i + 1][...]

    return pl.pallas_call(body, name=name, out_shape=tuple(SDS(a.shape, f32) for a, _ in pairs),
                          compiler_params=_cparams())(*[t for pair in pairs for t in pair])


def _adamw(g_parts, w, m, v, name):
    rows, cols = w.shape[-2:]
    tr = _row_block(rows, 512)
    n_g = len(g_parts)
    c1 = 1.0 / (1.0 - ADAM_B1 ** ADAM_STEP)
    c2 = 1.0 / (1.0 - ADAM_B2 ** ADAM_STEP)

    def body(*refs):
        g_refs = refs[:n_g]
        w_ref, m_ref, v_ref, go_ref, d_ref, mo_ref, vo_ref = refs[n_g:]
        g = g_refs[0][...]
        for r in g_refs[1:]:
            g = g + r[...]
        mn = ADAM_B1 * m_ref[...] + (1.0 - ADAM_B1) * g
        vn = ADAM_B2 * v_ref[...] + (1.0 - ADAM_B2) * (g * g)
        go_ref[...] = g
        mo_ref[...] = mn
        vo_ref[...] = vn
        d_ref[...] = -ADAM_LR * ((mn * c1) / (jnp.sqrt(vn * c2) + ADAM_EPS) + ADAM_WD * w_ref[...])

    if w.ndim == 2:
        grid, spec = (rows // tr,), pl.BlockSpec((tr, cols), lambda i: (i, 0))
    else:
        grid, spec = (w.shape[0], rows // tr), pl.BlockSpec((None, tr, cols), lambda a, i: (a, i, 0))
    return pl.pallas_call(
        body, name=name, grid=grid, out_shape=tuple(SDS(w.shape, f32) for _ in range(4)),
        in_specs=[spec] * (n_g + 3), out_specs=(spec,) * 4, compiler_params=_cparams(len(grid)),
    )(*g_parts, w, m, v)


def _resident_spec(shape, n_grid):
    zeros = (0,) * len(shape)
    return pl.BlockSpec(tuple(shape), {2: lambda a, b: zeros, 3: lambda a, b, c: zeros}[n_grid],
                        pipeline_mode=pl.Buffered(1))


def _shifted_rows(ext):
    return ext[8:, :], pltpu.roll(ext, 1, axis=0)[8:, :], pltpu.roll(ext, 2, axis=0)[8:, :]


def _later_rows(ext):
    n = ext.shape[0]
    return pltpu.roll(ext, n - 1, axis=0)[0:n - 8, :], pltpu.roll(ext, n - 2, axis=0)[0:n - 8, :]


def _ffn_fwd(x_mid, gain_pre, gain_post, w_l, conv_w, conv_b, li, host=None):
    bsz, l, d = x_mid.shape
    f = w_l.shape[1]
    tl, ft = min(FFN_TL, l), min(FFN_FT, f)
    nt, nf = l // tl, f // ft
    n_sub = FFN_SPLIT if tl % (FFN_SPLIT * FFN_CH) == 0 else 1
    sub = tl // n_sub
    h_in, h_shape, h_out, h_scr, h_ops = _host_args(host)

    def body(x_ref, gpre_ref, gpost_ref, w_ref, cw_ref, cb_ref, *rest):
        xo_ref, g_ref, v_ref, f_ref = rest[len(h_in):len(h_in) + 4]
        s0 = len(h_in) + 4 + len(h_out)
        h_sc, facc, gprev = rest[s0:s0 + 3]
        b, t = pl.program_id(0), pl.program_id(1)
        finish = _hosted(host, rest[:len(h_in)], rest[len(h_in) + 4:s0], rest[s0 + 3:],
                         (b == 0) & (t == 0), (b == bsz - 1) & (t == nt - 1))
        h_sc[...] = _rms_fwd(x_ref[...], gpre_ref[...]).astype(bf16)

        @pl.when(t == 0)
        def _():
            gprev[...] = jnp.zeros_like(gprev)

        def gate_val(j):
            wg, wv = w_ref[0, j * ft:(j + 1) * ft, :], w_ref[1, j * ft:(j + 1) * ft, :]
            return ([_dot_nt(h_sc[s * sub:(s + 1) * sub, :], wg) for s in range(n_sub)],
                    [_dot_nt(h_sc[s * sub:(s + 1) * sub, :], wv) for s in range(n_sub)])

        ahead = gate_val(0)
        for j in range(nf):
            gs, vs = ahead
            if j + 1 < nf:
                ahead = gate_val(j + 1)
            cols = slice(j * ft, (j + 1) * ft)
            wd = w_ref[2, cols, :]
            w0, w1, w2, bias = (jnp.broadcast_to(r, (FFN_CH, ft))
                                for r in (cw_ref[0:1, cols], cw_ref[1:2, cols], cw_ref[2:3, cols], cb_ref[:, cols]))
            hist = gprev[j]
            for s in range(n_sub):
                rows = slice(s * sub, (s + 1) * sub)
                g_ref[j, rows, :] = gs[s].astype(bf16)
                v_ref[j, rows, :] = vs[s].astype(bf16)
                hdn = []
                for r0 in range(0, sub, FFN_CH):
                    g0, g1, g2 = _shifted_rows(jnp.concatenate([hist, gs[s][r0:r0 + FFN_CH, :]], axis=0))
                    hist = g0[FFN_CH - 8:, :]
                    hdn.append((_gelu(bias + w0 * g2 + w1 * g1 + w2 * g0) * vs[s][r0:r0 + FFN_CH, :]).astype(bf16))
                part = _dot(jnp.concatenate(hdn, axis=0), wd)
                if j == 0:
                    facc[rows, :] = part
                else:
                    facc[rows, :] += part
            gprev[j] = hist

        fv = facc[...]
        f_ref[...] = fv
        xo_ref[...] = x_ref[...] + _rms_fwd(fv, gpost_ref[...])
        finish()

    tok = pl.BlockSpec((None, tl, d), lambda b, t: (b, t, 0))
    hid = pl.BlockSpec((nf, tl, ft), lambda b, t: (0, b * nt + t, 0))
    gain = pl.BlockSpec((1, d), lambda b, t: (0, 0))
    hid_shape = SDS((nf, bsz * l, ft), bf16)
    return pl.pallas_call(
        body, name=f"ffn_fwd_{li}", grid=(bsz, nt),
        out_shape=(SDS((bsz, l, d), f32), hid_shape, hid_shape, SDS((bsz, l, d), f32), *h_shape),
        in_specs=[tok, gain, gain, _resident_spec(w_l.shape, 2),
                  pl.BlockSpec((None, 3, f), lambda b, t: (li, 0, 0)),
                  pl.BlockSpec((None, 1, f), lambda b, t: (li, 0, 0)), *h_in],
        out_specs=(tok, hid, hid, tok, *h_out),
        scratch_shapes=[pltpu.VMEM((tl, d), bf16), pltpu.VMEM((tl, d), f32), pltpu.VMEM((nf, 8, ft), f32), *h_scr],
        compiler_params=_cparams(2, side_effects=bool(host)),
    )(x_mid, gain_pre, gain_post, w_l, conv_w, conv_b.reshape(conv_b.shape[0], 1, f), *h_ops)


def _ffn_bwd_act(x_mid, f_sv, g_sv, v_sv, dxo, gain_pre, gain_post, w_l, conv_w, conv_b, li, host=None):
    bsz, l, d = x_mid.shape
    f = w_l.shape[1]
    tl, ft = min(FFN_TL_BWD, l), min(FFN_FT, f)
    nt, nf = l // tl, f // ft
    h_in, h_shape, h_out, h_scr, h_ops = _host_args(host)

    def body(x_ref, f_ref, dxo_ref, g_ref, v_ref, gh_ref, gpre_ref, gpost_ref, w_ref, cw_ref, cb_ref, *rest):
        o0 = len(h_in)
        dx_ref, dg_ref, dv_ref, hdn_ref, h_ref, df_ref, dgain_ref, dconv_ref = rest[o0:o0 + 8]
        s0 = o0 + 8 + len(h_out)
        h_sc, df_sc, dh_acc, dgc_next = rest[s0:s0 + 4]
        b, t = pl.program_id(0), pl.program_id(1)
        tt = nt - 1 - t
        finish = _hosted(host, rest[:o0], rest[o0 + 8:s0], rest[s0 + 4:],
                         (b == 0) & (t == 0), (b == bsz - 1) & (t == nt - 1))

        @pl.when((b == 0) & (t == 0))
        def _():
            dgain_ref[...] = jnp.zeros_like(dgain_ref)
            dconv_ref[...] = jnp.zeros_like(dconv_ref)

        @pl.when(t == 0)
        def _():
            dgc_next[...] = jnp.zeros_like(dgc_next)

        hb = _rms_fwd(x_ref[...], gpre_ref[...]).astype(bf16)
        h_sc[...] = hb
        h_ref[...] = hb
        df, dgp = _rms_bwd(f_ref[...], gpost_ref[...], dxo_ref[...])
        dfb = df.astype(bf16)
        df_sc[...] = dfb
        df_ref[...] = dfb
        dgain_ref[1:2, :] += dgp

        def fold(a):
            return a.reshape(FFN_CH // 8, 8, ft).sum(axis=0)

        ahead = _dot_nt(df_sc[...], w_ref[2, 0:ft, :])
        for j in range(nf):
            dhdn = ahead
            if j + 1 < nf:
                ahead = _dot_nt(df_sc[...], w_ref[2, (j + 1) * ft:(j + 2) * ft, :])
            cols = slice(j * ft, (j + 1) * ft)
            w0, w1, w2, bias = (jnp.broadcast_to(r, (FFN_CH, ft))
                                for r in (cw_ref[0:1, cols], cw_ref[1:2, cols], cw_ref[2:3, cols], cb_ref[:, cols]))

            def conv_bwd(dgc, after):
                d1, d2 = _later_rows(jnp.concatenate([dgc, after], axis=0))
                return (w2 * dgc + w1 * d1 + w0 * d2).astype(bf16)

            g = g_ref[j].astype(f32)
            v = v_ref[j].astype(f32)
            hist = jnp.where(tt > 0, gh_ref[j].astype(f32)[8:16, :], 0.0)
            acc = [jnp.zeros((8, ft), f32)] * 4
            first, pending, hdn, dvs, dgs = None, None, [], [], []
            for r0 in range(0, tl, FFN_CH):
                g0, g1, g2 = _shifted_rows(jnp.concatenate([hist, g[r0:r0 + FFN_CH, :]], axis=0))
                hist = g0[FFN_CH - 8:, :]
                vc, dc = v[r0:r0 + FFN_CH, :], dhdn[r0:r0 + FFN_CH, :]
                u, ug = _gelu_and_grad(bias + w0 * g2 + w1 * g1 + w2 * g0)
                hdn.append((u * vc).astype(bf16))
                dvs.append((dc * u).astype(bf16))
                dgc = dc * vc * ug
                acc = [acc[0] + fold(dgc * g2), acc[1] + fold(dgc * g1), acc[2] + fold(dgc * g0), acc[3] + fold(dgc)]
                if pending is None:
                    first = dgc[0:8, :]
                else:
                    dgs.append(conv_bwd(pending, dgc[0:8, :]))
                pending = dgc
            dgs.append(conv_bwd(pending, dgc_next[j]))
            dgc_next[j] = first
            dgb, dvb = jnp.concatenate(dgs, axis=0), jnp.concatenate(dvs, axis=0)
            hdn_ref[j] = jnp.concatenate(hdn, axis=0)
            dg_ref[j] = dgb
            dv_ref[j] = dvb
            part = _dot(dgb, w_ref[0, cols, :]) + _dot(dvb, w_ref[1, cols, :])
            if j == 0:
                dh_acc[...] = part
            else:
                dh_acc[...] += part
            for k in range(4):
                dconv_ref[j, k:k + 1, :] += jnp.sum(acc[k], axis=0, keepdims=True)

        dxp, dgp = _rms_bwd(x_ref[...], gpre_ref[...], dh_acc[...])
        dx_ref[...] = dxo_ref[...] + dxp
        dgain_ref[0:1, :] += dgp
        finish()

    tok = pl.BlockSpec((None, tl, d), lambda b, t: (b, nt - 1 - t, 0))
    hid = pl.BlockSpec((nf, tl, ft), lambda b, t: (0, b * nt + nt - 1 - t, 0))
    halo = pl.BlockSpec((nf, 16, ft), lambda b, t: (0, jnp.maximum((b * nt + nt - 1 - t) * (tl // 16) - 1, 0), 0))
    gain = pl.BlockSpec((1, d), lambda b, t: (0, 0))
    hid_shape = SDS((nf, bsz * l, ft), bf16)
    return pl.pallas_call(
        body, name=f"ffn_bwd_act_{li}", grid=(bsz, nt),
        out_shape=(SDS((bsz, l, d), f32), hid_shape, hid_shape, hid_shape,
                   SDS((bsz, l, d), bf16), SDS((bsz, l, d), bf16), SDS((8, d), f32), SDS((nf, 8, ft), f32), *h_shape),
        in_specs=[tok, tok, tok, hid, hid, halo, gain, gain, _resident_spec(w_l.shape, 2),
                  pl.BlockSpec((None, 3, f), lambda b, t: (li, 0, 0)),
                  pl.BlockSpec((None, 1, f), lambda b, t: (li, 0, 0)), *h_in],
        out_specs=(tok, hid, hid, hid, tok, tok,
                   pl.BlockSpec((8, d), lambda b, t: (0, 0)), pl.BlockSpec((nf, 8, ft), lambda b, t: (0, 0, 0)),
                   *h_out),
        scratch_shapes=[pltpu.VMEM((tl, d), bf16), pltpu.VMEM((tl, d), bf16), pltpu.VMEM((tl, d), f32),
                        pltpu.VMEM((nf, 8, ft), f32), *h_scr],
        compiler_params=_cparams(2, side_effects=bool(host)),
    )(x_mid, f_sv, dxo, g_sv, v_sv, g_sv, gain_pre, gain_post, w_l, conv_w,
      conv_b.reshape(conv_b.shape[0], 1, f), *h_ops)


def _ffn_bwd_weights(dg, dv, hdn, h, df, li):
    nf, t, ft = dg.shape
    f = nf * ft
    d = h.shape[1]
    tr = min(WG_TR, t)
    nr = t // tr

    def body(dg_ref, dv_ref, hdn_ref, h_ref, df_ref, o_ref, acc):
        r = pl.program_id(1)

        @pl.when(r == 0)
        def _():
            acc[...] = jnp.zeros_like(acc)

        rows = pl.ds(pl.multiple_of(r * tr, tr), tr)
        hb = h_ref[rows, :]
        acc[0] += _dot_tn(dg_ref[...], hb)
        acc[1] += _dot_tn(dv_ref[...], hb)
        acc[2] += _dot_tn(hdn_ref[...], df_ref[rows, :])

        @pl.when(r == nr - 1)
        def _():
            o_ref[...] = acc[...].astype(bf16)

    hid = pl.BlockSpec((None, tr, ft), lambda j, r: (j, r, 0))
    tok = _resident_spec((t, d), 2)
    return pl.pallas_call(
        body, name=f"ffn_bwd_weights_{li}", grid=(nf, nr), out_shape=SDS((3, f, d), bf16),
        in_specs=[hid, hid, hid, tok, tok], out_specs=pl.BlockSpec((3, ft, d), lambda j, r: (0, j, 0)),
        scratch_shapes=[pltpu.VMEM((3, ft, d), f32)], compiler_params=_cparams(2),
    )(dg, dv, hdn, h, df)


def _pool_counts(t0, tl, d):
    gch = d // len(POOL_WINDOWS)
    tpos = (t0 + lax.broadcasted_iota(jnp.int32, (tl, d), 0) + 1).astype(f32)
    lane = lax.broadcasted_iota(jnp.int32, (tl, d), 1)
    win = jnp.full((tl, d), float(POOL_WINDOWS[-1]), f32)
    for gi in range(len(POOL_WINDOWS) - 2, -1, -1):
        win = jnp.where(lane < (gi + 1) * gch, float(POOL_WINDOWS[gi]), win)
    return jnp.minimum(tpos, win)


def _pool_select(parts, tl, d):
    gch = d // len(POOL_WINDOWS)
    lane = lax.broadcasted_iota(jnp.int32, (tl, d), 1)
    out = parts[-1]
    for gi in range(len(parts) - 2, -1, -1):
        out = jnp.where(lane < (gi + 1) * gch, parts[gi], out)
    return out


def _pool_window_sums(u, halo, tl):
    ext = jnp.concatenate([halo, u], axis=0)
    sums, cur = [], ext
    for k in (1, 2, 4, 8):
        cur = cur + pltpu.roll(cur, k, axis=0)
        sums.append(cur[POOL_HALO:POOL_HALO + tl, :])
    return sums


def _pool_mix(u, halo, cnt, pw_ref, scale, tl, d):
    gch = d // len(POOL_WINDOWS)
    pooled = _pool_select(_pool_window_sums(u, halo, tl), tl, d) / cnt
    diff = pooled - u
    outs = [_dot(diff[:, gi * gch:(gi + 1) * gch].astype(bf16), pw_ref[gi]) for gi in range(len(POOL_WINDOWS))]
    return diff, jnp.concatenate(outs, axis=1)


def _pool_fwd(x, gain_pre, gain_post, pw, scale, li):
    bsz, l, d = x.shape
    tl = min(POOL_TL, l)
    nt = l // tl

    def body(x_ref, gpre_ref, gpost_ref, pw_ref, sc_ref, xo_ref, halo):
        t = pl.program_id(1)

        @pl.when(t == 0)
        def _():
            halo[...] = jnp.zeros_like(halo)

        xv = x_ref[...]
        u = _rms_fwd(xv, gpre_ref[...])
        _, out = _pool_mix(u, halo[...], _pool_counts(t * tl, tl, d), pw_ref, sc_ref[...], tl, d)
        halo[...] = u[tl - POOL_HALO:tl, :]
        xo_ref[...] = xv + _rms_fwd(out * sc_ref[...], gpost_ref[...])

    tok = pl.BlockSpec((None, tl, d), lambda b, t: (b, t, 0))
    gain = pl.BlockSpec((1, d), lambda b, t: (0, 0))
    return pl.pallas_call(
        body, name=f"pool_fwd_{li}", grid=(bsz, nt), out_shape=SDS((bsz, l, d), f32),
        in_specs=[tok, gain, gain, pl.BlockSpec(pw.shape, lambda b, t: (0, 0, 0)), gain], out_specs=tok,
        scratch_shapes=[pltpu.VMEM((POOL_HALO, d), f32)], compiler_params=_cparams(2),
    )(x, gain_pre, gain_post, pw, scale)


def _pool_bwd(x, dxo, gain_pre, gain_post, pw, scale, li):
    bsz, l, d = x.shape
    tl = min(POOL_TL, l)
    nt = l // tl
    ng = len(POOL_WINDOWS)
    gch = d // ng
    n_ext = tl + POOL_HALO

    def body(x_ref, xh_ref, dxo_ref, gpre_ref, gpost_ref, pw_ref, sc_ref, dx_ref, dpw_ref, ds_ref, qnext):
        b, t = pl.program_id(0), pl.program_id(1)
        tt = nt - 1 - t

        @pl.when((b == 0) & (t == 0))
        def _():
            dpw_ref[...] = jnp.zeros_like(dpw_ref)
            ds_ref[...] = jnp.zeros_like(ds_ref)

        @pl.when(t == 0)
        def _():
            qnext[...] = jnp.zeros_like(qnext)

        xv = x_ref[...]
        gpre = gpre_ref[...]
        u = _rms_fwd(xv, gpre)
        uh = _rms_fwd(jnp.where(tt > 0, xh_ref[...], 0.0), gpre)
        cnt = _pool_counts(tt * tl, tl, d)
        scale_v = sc_ref[...]
        diff, out = _pool_mix(u, uh, cnt, pw_ref, scale_v, tl, d)
        dxo_v = dxo_ref[...]
        dm, dgpost = _rms_bwd(out * scale_v, gpost_ref[...], dxo_v)
        ds_ref[1:2, :] += dgpost
        ds_ref[2:3, :] += jnp.sum(dm * out, axis=0, keepdims=True)
        dout = (dm * scale_v).astype(bf16)
        ddiffs = []
        for gi in range(ng):
            sl = slice(gi * gch, (gi + 1) * gch)
            dpw_ref[gi] += _dot_tn(diff[:, sl].astype(bf16), dout[:, sl])
            ddiffs.append(_dot_nt(dout[:, sl], pw_ref[gi]))
        ddiff = jnp.concatenate(ddiffs, axis=1)
        q = ddiff / cnt
        ext = jnp.concatenate([q, qnext[...]], axis=0)
        sums, cur = [], ext
        for k in (1, 2, 4, 8):
            cur = cur + pltpu.roll(cur, n_ext - k, axis=0)
            sums.append(cur[0:tl, :])
        du = _pool_select(sums, tl, d) - ddiff
        qnext[...] = q[0:POOL_HALO, :]
        dxp, dgpre = _rms_bwd(xv, gpre, du)
        ds_ref[0:1, :] += dgpre
        dx_ref[...] = dxo_v + dxp

    tok = pl.BlockSpec((None, tl, d), lambda b, t: (b, nt - 1 - t, 0))
    halo = pl.BlockSpec((None, POOL_HALO, d),
                        lambda b, t: (b, jnp.maximum((nt - 1 - t) * (tl // POOL_HALO) - 1, 0), 0))
    gain = pl.BlockSpec((1, d), lambda b, t: (0, 0))
    return pl.pallas_call(
        body, name=f"pool_bwd_{li}", grid=(bsz, nt),
        out_shape=(SDS((bsz, l, d), f32), SDS((ng, gch, gch), f32), SDS((8, d), f32)),
        in_specs=[tok, halo, tok, gain, gain, pl.BlockSpec(pw.shape, lambda b, t: (0, 0, 0)), gain],
        out_specs=(tok, pl.BlockSpec((ng, gch, gch), lambda b, t: (0, 0, 0)), pl.BlockSpec((8, d), lambda b, t: (0, 0))),
        scratch_shapes=[pltpu.VMEM((POOL_HALO, d), f32)], compiler_params=_cparams(2),
    )(x, x, dxo, gain_pre, gain_post, pw, scale)


def _s5_discretise(lam_re, lam_im, log_dt, b_re, b_im):
    def body(lr_ref, li_ref, ld_ref, br_ref, bi_ref, ar_ref, ai_ref, bbr_ref, bbi_ref):
        lr, li = lr_ref[...], li_ref[...]
        dt = jnp.exp(ld_ref[...])
        mag = jnp.exp(lr * dt)
        ar = mag * jnp.cos(li * dt)
        ai = mag * jnp.sin(li * dt)
        den = lr * lr + li * li
        nr, ni = ar - 1.0, ai
        fr = (nr * lr + ni * li) / den
        fi = (ni * lr - nr * li) / den
        br, bi = br_ref[...], bi_ref[...]
        ar_ref[...] = ar
        ai_ref[...] = ai
        bbr_ref[...] = fr * br - fi * bi
        bbi_ref[...] = fr * bi + fi * br

    return pl.pallas_call(
        body, name="s5_discretise",
        out_shape=(SDS(lam_re.shape, f32), SDS(lam_re.shape, f32), SDS(b_re.shape, f32), SDS(b_re.shape, f32)),
        compiler_params=_cparams(),
    )(lam_re, lam_im, log_dt, b_re, b_im)


def _s5_discretise_bwd(lam_re, lam_im, log_dt, b_re, b_im, g_ar, g_ai, g_bbr, g_bbi):
    def body(lr_ref, li_ref, ld_ref, br_ref, bi_ref, gar_ref, gai_ref, gbbr_ref, gbbi_ref,
             dlr_ref, dli_ref, dld_ref, dbr_ref, dbi_ref):
        lr, li = lr_ref[...], li_ref[...]
        dt = jnp.exp(ld_ref[...])
        mag = jnp.exp(lr * dt)
        cs, sn = jnp.cos(li * dt), jnp.sin(li * dt)
        ar, ai = mag * cs, mag * sn
        den = lr * lr + li * li
        nr, ni = ar - 1.0, ai
        fr = (nr * lr + ni * li) / den
        fi = (ni * lr - nr * li) / den
        br, bi = br_ref[...], bi_ref[...]
        gbbr, gbbi = gbbr_ref[...], gbbi_ref[...]
        dbr_ref[...] = fr * gbbr + fi * gbbi
        dbi_ref[...] = fr * gbbi - fi * gbbr
        gfr = jnp.sum(br * gbbr + bi * gbbi, axis=2, keepdims=True)
        gfi = jnp.sum(br * gbbi - bi * gbbr, axis=2, keepdims=True)
        gnr_num, gni_num = gfr / den, gfi / den
        gden = -(gfr * fr + gfi * fi) / den
        g_nr = gnr_num * lr - gni_num * li
        g_ni = gnr_num * li + gni_num * lr
        dlr = gnr_num * nr + gni_num * ni + gden * 2.0 * lr
        dli = gnr_num * ni - gni_num * nr + gden * 2.0 * li
        gar = gar_ref[...] + g_nr
        gai = gai_ref[...] + g_ni
        gq = (gar * cs + gai * sn) * mag
        gth = (gai * cs - gar * sn) * mag
        dlr_ref[...] = dlr + gq * dt
        dli_ref[...] = dli + gth * dt
        dld_ref[...] = jnp.sum(gq * lr + gth * li, axis=3, keepdims=True) * dt

    return pl.pallas_call(
        body, name="s5_discretise_bwd",
        out_shape=(SDS(lam_re.shape, f32), SDS(lam_re.shape, f32), SDS(log_dt.shape, f32),
                   SDS(b_re.shape, f32), SDS(b_re.shape, f32)),
        compiler_params=_cparams(),
    )(lam_re, lam_im, log_dt, b_re, b_im, g_ar, g_ai, g_bbr, g_bbi)


def _fill_powers(pw_r, pw_i, ar, ai, nj, width):
    cb = min(S5_CB, width)
    for c0 in range(0, width, cb):
        sl = pl.ds(c0, cb)
        a_r, a_i = ar[:, c0:c0 + cb], ai[:, c0:c0 + cb]

        def step(i, p):
            pw_r[i, :, sl] = p[0]
            pw_i[i, :, sl] = p[1]
            return _cmul(p[0], p[1], a_r, a_i)

        lax.fori_loop(0, nj, step, (a_r, a_i))


def _interleaved_scan(xr_sc, xi_sc, ar, ai, pw_r, pw_i, carry_r, carry_i, nj, width, reverse, h_sc=None):
    cb = min(S5_CB, width)
    acc_out = []
    for c0 in range(0, width, cb):
        sl = pl.ds(c0, cb)
        a_r, a_i = ar[:, c0:c0 + cb], ai[:, c0:c0 + cb]
        aj_r, aj_i = pw_r[nj - 1, :, sl], pw_i[nj - 1, :, sl]

        def pos(i):
            return nj - 1 - i if reverse else i

        def local_step(i, st):
            j = pos(i)
            hr, hi = _cmul(a_r, a_i, st[0], st[1])
            hr, hi = hr + xr_sc[j, :, sl], hi + xi_sc[j, :, sl]
            xr_sc[j, :, sl] = hr
            xi_sc[j, :, sl] = hi
            return hr, hi

        zero = jnp.zeros((8, cb), f32)
        fin_r, fin_i = lax.fori_loop(0, nj, local_step, (zero, zero), unroll=S5_UNROLL)
        row = lax.broadcasted_iota(jnp.int32, (8, cb), 0)
        c_r, c_i = carry_r[0:1, sl], carry_i[0:1, sl]
        ent_r, ent_i = zero, zero
        order = range(7, -1, -1) if reverse else range(8)
        for s in order:
            ent_r = jnp.where(row == s, c_r, ent_r)
            ent_i = jnp.where(row == s, c_i, ent_i)
            pr, pi_ = _cmul(aj_r[0:1, :], aj_i[0:1, :], c_r, c_i)
            c_r, c_i = fin_r[s:s + 1, :] + pr, fin_i[s:s + 1, :] + pi_
        carry_r[:, sl] = jnp.broadcast_to(c_r, (8, cb))
        carry_i[:, sl] = jnp.broadcast_to(c_i, (8, cb))

        def fix_step(i, st):
            j = pos(i)
            nx_r, nx_i, acc_r, acc_i = st
            cr_, ci_ = _cmul(pw_r[i, :, sl], pw_i[i, :, sl], ent_r, ent_i)
            hr, hi = xr_sc[j, :, sl] + cr_, xi_sc[j, :, sl] + ci_
            xr_sc[j, :, sl] = hr
            xi_sc[j, :, sl] = hi
            if h_sc is not None:
                sr, si = h_sc[0][j, :, sl], h_sc[1][j, :, sl]
                acc_r = acc_r + nx_r * sr + nx_i * si
                acc_i = acc_i + nx_i * sr - nx_r * si
                nx_r, nx_i = hr, hi
            return nx_r, nx_i, acc_r, acc_i

        st = lax.fori_loop(0, nj, fix_step, (ent_r, ent_i, zero, zero), unroll=S5_UNROLL)
        acc_out.append((st[2], st[3]))
    return acc_out


def _diag_mask(n_rep, h, ks):
    assert h & (h - 1) == 0 and (ks // n_rep) & (ks // n_rep - 1) == 0
    r = lax.shift_right_logical(lax.broadcasted_iota(jnp.int32, (n_rep * h, ks), 0), h.bit_length() - 1)
    c = lax.shift_right_logical(lax.broadcasted_iota(jnp.int32, (n_rep * h, ks), 1), (ks // n_rep).bit_length() - 1)
    return r == c


def _expand_block_diag(compact, n_rep):
    h, ks = compact.shape
    full = jnp.concatenate([compact] * n_rep, axis=0)
    return jnp.where(_diag_mask(n_rep, h, ks), full, 0.0).astype(bf16)


def _compact_block_diag(full, n_rep):
    rows, ks = full.shape
    h = rows // n_rep
    return jnp.sum(jnp.where(_diag_mask(n_rep, h, ks), full, 0.0).reshape(n_rep, h, ks), axis=0)


def _s5_scan_fwd(x_il, gain_pre, d_skip, ab_r, ab_i, bbc_r, bbc_i, cc_r, cc_in, li, host=None):
    bsz, nblk, tq, d = x_il.shape
    nkb, gch, ks = bbc_r.shape
    kc = d // nkb
    n_rep = kc // gch
    nj = tq // 8
    h_in, h_shape, h_out, h_scr, h_ops = _host_args(host)

    def body(xf_ref, xk_ref, gk_ref, dk_ref, ar_ref, ai_ref, bbr_ref, bbi_ref, cr_ref, ci_ref, *rest):
        o0 = len(h_in)
        y_ref, hr_ref, hi_ref = rest[o0:o0 + 3]
        s0 = o0 + 3 + len(h_out)
        xr_sc, xi_sc, pw_r, pw_i, carry_r, carry_i, bbr_sc, bbi_sc, crt_sc, cit_sc = rest[s0:s0 + 10]
        k, b, n = pl.program_id(0), pl.program_id(1), pl.program_id(2)
        finish = _hosted(host, rest[:o0], rest[o0 + 3:s0], rest[s0 + 10:],
                         (k == 0) & (b == 0) & (n == 0), (k == nkb - 1) & (b == bsz - 1) & (n == nblk - 1))
        ar = jnp.broadcast_to(ar_ref[...], (8, ks))
        ai = jnp.broadcast_to(ai_ref[...], (8, ks))

        @pl.when((b == 0) & (n == 0))
        def _():
            bbr_sc[...] = _expand_block_diag(bbr_ref[...], n_rep)
            bbi_sc[...] = _expand_block_diag(bbi_ref[...], n_rep)
            crt_sc[...] = _expand_block_diag(cr_ref[...], n_rep)
            cit_sc[...] = _expand_block_diag(ci_ref[...], n_rep)
            _fill_powers(pw_r, pw_i, ar, ai, nj, ks)

        @pl.when(n == 0)
        def _():
            carry_r[...] = jnp.zeros_like(carry_r)
            carry_i[...] = jnp.zeros_like(carry_i)

        u = xk_ref[...] * _rms_scale(xf_ref[...]) * gk_ref[...]
        ub = u.astype(bf16)
        xr_sc[...] = _dot(ub, bbr_sc[...]).reshape(nj, 8, ks)
        xi_sc[...] = _dot(ub, bbi_sc[...]).reshape(nj, 8, ks)
        _interleaved_scan(xr_sc, xi_sc, ar, ai, pw_r, pw_i, carry_r, carry_i, nj, ks, reverse=False)
        hrb = xr_sc[...].reshape(tq, ks).astype(bf16)
        hib = xi_sc[...].reshape(tq, ks).astype(bf16)
        hr_ref[...] = hrb
        hi_ref[...] = hib
        crt, cit, dk = crt_sc[...], cit_sc[...], dk_ref[...]
        for r0 in range(0, tq, tq // 2):
            rows = slice(r0, r0 + tq // 2)
            y_ref[rows, :] = _dot_nt(hrb[rows, :], crt) + _dot_nt(hib[rows, :], cit) + dk * u[rows, :]
        finish()

    full = pl.BlockSpec((None, None, tq, d), lambda k, b, n: (b, n, 0, 0))
    chan = pl.BlockSpec((None, None, tq, kc), lambda k, b, n: (b, n, 0, k))
    stat = pl.BlockSpec((None, None, tq, ks), lambda k, b, n: (b, n, 0, k))
    vec_c = pl.BlockSpec((1, kc), lambda k, b, n: (0, k))
    vec_s = pl.BlockSpec((1, ks), lambda k, b, n: (0, k))
    cmap = pl.BlockSpec((None, gch, ks), lambda k, b, n: (k, 0, 0))
    s_tot = nkb * ks
    return pl.pallas_call(
        body, name=f"s5_scan_fwd_{li}", grid=(nkb, bsz, nblk),
        out_shape=(SDS((bsz, nblk, tq, d), f32), SDS((bsz, nblk, tq, s_tot), bf16), SDS((bsz, nblk, tq, s_tot), bf16),
                   *h_shape),
        in_specs=[full, chan, vec_c, vec_c, vec_s, vec_s, cmap, cmap, cmap, cmap, *h_in],
        out_specs=(chan, stat, stat, *h_out),
        scratch_shapes=[pltpu.VMEM((nj, 8, ks), f32)] * 4 + [pltpu.VMEM((8, ks), f32)] * 2
        + [pltpu.VMEM((kc, ks), bf16)] * 4 + h_scr,
        compiler_params=_cparams(3, side_effects=bool(host)),
    )(x_il, x_il, gain_pre, d_skip, ab_r, ab_i, bbc_r, bbc_i, cc_r, cc_in, *h_ops)


def _s5_glu_fwd(y, x_il, w_glu, b_glu, gain_post, li):
    t, d = y.shape
    tr = min(TOK_TR, t)

    def body(y_ref, x_ref, w_ref, b_ref, gp_ref, xo_ref):
        z = _gelu(y_ref[...])
        a = _dot(z.astype(bf16), w_ref[...]) + b_ref[...]
        xo_ref[...] = x_ref[...] + _rms_fwd(z * _sigmoid(a), gp_ref[...])

    tok = pl.BlockSpec((tr, d), lambda i: (i, 0))
    vec = pl.BlockSpec((1, d), lambda i: (0, 0))
    return pl.pallas_call(
        body, name=f"s5_glu_fwd_{li}", grid=(t // tr,), out_shape=SDS((t, d), f32),
        in_specs=[tok, tok, pl.BlockSpec((d, d), lambda i: (0, 0)), vec, vec], out_specs=tok,
        compiler_params=_cparams(1),
    )(y, x_il, w_glu, b_glu, gain_post)


def _s5_glu_bwd(y, dxo, w_glu, b_glu, gain_post, li):
    t, d = y.shape
    tr = min(TOK_TR, t)

    def body(y_ref, dxo_ref, w_ref, b_ref, gp_ref, dy_ref, dw_ref, ds_ref):
        @pl.when(pl.program_id(0) == 0)
        def _():
            dw_ref[...] = jnp.zeros_like(dw_ref)
            ds_ref[...] = jnp.zeros_like(ds_ref)

        z, zg = _gelu_and_grad(y_ref[...])
        zb = z.astype(bf16)
        w = w_ref[...]
        s = _sigmoid(_dot(zb, w) + b_ref[...])
        dm, dgpost = _rms_bwd(z * s, gp_ref[...], dxo_ref[...])
        da = dm * z * s * (1.0 - s)
        dab = da.astype(bf16)
        dz = dm * s + _dot_nt(dab, w)
        dw_ref[...] += _dot_tn(zb, dab)
        ds_ref[0:1, :] += dgpost
        ds_ref[1:2, :] += jnp.sum(da, axis=0, keepdims=True)
        dy_ref[...] = dz * zg

    tok = pl.BlockSpec((tr, d), lambda i: (i, 0))
    vec = pl.BlockSpec((1, d), lambda i: (0, 0))
    mat = pl.BlockSpec((d, d), lambda i: (0, 0))
    return pl.pallas_call(
        body, name=f"s5_glu_bwd_{li}", grid=(t // tr,),
        out_shape=(SDS((t, d), f32), SDS((d, d), f32), SDS((8, d), f32)),
        in_specs=[tok, tok, mat, vec, vec], out_specs=(tok, mat, pl.BlockSpec((8, d), lambda i: (0, 0))),
        compiler_params=_cparams(1),
    )(y, dxo, w_glu, b_glu, gain_post)


def _s5_scan_bwd(x_il, dy, h_r, h_i, gain_pre, d_skip, ab_r, ab_i, bbc_r, bbc_i, cc_r, cc_in, li, host=None):
    bsz, nblk, tq, d = x_il.shape
    nkb, gch, ks = bbc_r.shape
    kc = d // nkb
    n_rep = kc // gch
    nj = tq // 8
    h_in, h_shape, h_out, h_scr, h_ops = _host_args(host)

    def body(xf_ref, xk_ref, dy_ref, hr_ref, hi_ref, gk_ref, dk_ref, ar_ref, ai_ref, bbr_ref, bbi_ref, cr_ref, ci_ref,
             *rest):
        o0 = len(h_in)
        du_ref, dbbr_ref, dbbi_ref, dcr_ref, dci_ref, dar_ref, dai_ref, dd_ref = rest[o0:o0 + 8]
        s0 = o0 + 8 + len(h_out)
        (gr_sc, gi_sc, hr_sc, hi_sc, pw_r, pw_i, carry_r, carry_i, acc_r, acc_i,
         bbr_sc, bbi_sc, crt_sc, cit_sc, dbbr_acc, dbbi_acc, dcr_acc, dci_acc) = rest[s0:s0 + 18]
        k, b, n = pl.program_id(0), pl.program_id(1), pl.program_id(2)
        finish = _hosted(host, rest[:o0], rest[o0 + 8:s0], rest[s0 + 18:],
                         (k == 0) & (b == 0) & (n == 0), (k == nkb - 1) & (b == bsz - 1) & (n == nblk - 1))
        ar = jnp.broadcast_to(ar_ref[...], (8, ks))
        ai = jnp.broadcast_to(-ai_ref[...], (8, ks))

        @pl.when((b == 0) & (n == 0))
        def _():
            bbr_sc[...] = _expand_block_diag(bbr_ref[...], n_rep)
            bbi_sc[...] = _expand_block_diag(bbi_ref[...], n_rep)
            crt_sc[...] = _expand_block_diag(cr_ref[...], n_rep)
            cit_sc[...] = _expand_block_diag(ci_ref[...], n_rep)
            _fill_powers(pw_r, pw_i, ar, ai, nj, ks)
            for ref in (dar_ref, dai_ref, dd_ref, acc_r, acc_i, dbbr_acc, dbbi_acc, dcr_acc, dci_acc):
                ref[...] = jnp.zeros_like(ref)

        @pl.when(n == 0)
        def _():
            carry_r[...] = jnp.zeros_like(carry_r)
            carry_i[...] = jnp.zeros_like(carry_i)

        u = xk_ref[...] * _rms_scale(xf_ref[...]) * gk_ref[...]
        ub = u.astype(bf16)
        dyv = dy_ref[...]
        dyb = dyv.astype(bf16)
        dd_ref[0:1, :] += jnp.sum(dyv * u, axis=0, keepdims=True)
        hrb, hib = hr_ref[...], hi_ref[...]
        dcr_acc[...] += _dot_tn(dyb, hrb)
        dci_acc[...] += _dot_tn(dyb, hib)
        hr_sc[...] = hrb.astype(f32).reshape(nj, 8, ks)
        hi_sc[...] = hib.astype(f32).reshape(nj, 8, ks)
        gr_sc[...] = _dot(dyb, crt_sc[...]).reshape(nj, 8, ks)
        gi_sc[...] = _dot(dyb, cit_sc[...]).reshape(nj, 8, ks)
        accs = _interleaved_scan(gr_sc, gi_sc, ar, ai, pw_r, pw_i, carry_r, carry_i, nj, ks, reverse=True,
                                 h_sc=(hr_sc, hi_sc))
        cb = min(S5_CB, ks)
        for q, (a_r, a_i) in enumerate(accs):
            acc_r[:, q * cb:(q + 1) * cb] += a_r
            acc_i[:, q * cb:(q + 1) * cb] += a_i
        grb = gr_sc[...].reshape(tq, ks).astype(bf16)
        gib = gi_sc[...].reshape(tq, ks).astype(bf16)
        dbbr_acc[...] += _dot_tn(ub, grb)
        dbbi_acc[...] += _dot_tn(ub, gib)
        bbr, bbi, dk = bbr_sc[...], bbi_sc[...], dk_ref[...]
        for r0 in range(0, tq, tq // 2):
            rows = slice(r0, r0 + tq // 2)
            du_ref[rows, :] = dyv[rows, :] * dk + _dot_nt(grb[rows, :], bbr) + _dot_nt(gib[rows, :], bbi)

        @pl.when((b == bsz - 1) & (n == nblk - 1))
        def _():
            dar_ref[0:1, :] = jnp.sum(acc_r[...], axis=0, keepdims=True)
            dai_ref[0:1, :] = jnp.sum(acc_i[...], axis=0, keepdims=True)
            dbbr_ref[...] = _compact_block_diag(dbbr_acc[...], n_rep)
            dbbi_ref[...] = _compact_block_diag(dbbi_acc[...], n_rep)
            dcr_ref[...] = _compact_block_diag(dcr_acc[...], n_rep)
            dci_ref[...] = _compact_block_diag(dci_acc[...], n_rep)

        finish()

    full = pl.BlockSpec((None, None, tq, d), lambda k, b, n: (b, nblk - 1 - n, 0, 0))
    chan = pl.BlockSpec((None, None, tq, kc), lambda k, b, n: (b, nblk - 1 - n, 0, k))
    stat = pl.BlockSpec((None, None, tq, ks), lambda k, b, n: (b, nblk - 1 - n, 0, k))
    vec_c = pl.BlockSpec((1, kc), lambda k, b, n: (0, k))
    vec_s = pl.BlockSpec((1, ks), lambda k, b, n: (0, k))
    cmap = pl.BlockSpec((None, gch, ks), lambda k, b, n: (k, 0, 0))
    acc_s = pl.BlockSpec((None, 8, ks), lambda k, b, n: (k, 0, 0))
    acc_c = pl.BlockSpec((None, 8, kc), lambda k, b, n: (k, 0, 0))
    cshape = SDS((nkb, gch, ks), f32)
    return pl.pallas_call(
        body, name=f"s5_scan_bwd_{li}", grid=(nkb, bsz, nblk),
        out_shape=(SDS((bsz, nblk, tq, d), f32), cshape, cshape, cshape, cshape, SDS((nkb, 8, ks), f32),
                   SDS((nkb, 8, ks), f32), SDS((nkb, 8, kc), f32), *h_shape),
        in_specs=[full, chan, chan, stat, stat, vec_c, vec_c, vec_s, vec_s, cmap, cmap, cmap, cmap, *h_in],
        out_specs=(chan, cmap, cmap, cmap, cmap, acc_s, acc_s, acc_c, *h_out),
        scratch_shapes=([pltpu.VMEM((nj, 8, ks), f32)] * 6 + [pltpu.VMEM((8, ks), f32)] * 4
                        + [pltpu.VMEM((kc, ks), bf16)] * 4 + [pltpu.VMEM((kc, ks), f32)] * 4 + h_scr),
        compiler_params=_cparams(3, side_effects=bool(host)),
    )(x_il, x_il, dy, h_r, h_i, gain_pre, d_skip, ab_r, ab_i, bbc_r, bbc_i, cc_r, cc_in, *h_ops)


def _norm_residual_bwd(x, du, dxo, gain, name):
    t, d = x.shape
    tr = min(TOK_TR, t)

    def body(x_ref, du_ref, dxo_ref, g_ref, dx_ref, dg_ref):
        @pl.when(pl.program_id(0) == 0)
        def _():
            dg_ref[...] = jnp.zeros_like(dg_ref)

        dxp, dgain = _rms_bwd(x_ref[...], g_ref[...], du_ref[...])
        dx_ref[...] = dxo_ref[...] + dxp
        dg_ref[0:1, :] += dgain

    tok = pl.BlockSpec((tr, d), lambda i: (i, 0))
    return pl.pallas_call(
        body, name=name, grid=(t // tr,), out_shape=(SDS((t, d), f32), SDS((8, d), f32)),
        in_specs=[tok, tok, tok, pl.BlockSpec((1, d), lambda i: (0, 0))],
        out_specs=(tok, pl.BlockSpec((8, d), lambda i: (0, 0))), compiler_params=_cparams(1),
    )(x, du, dxo, gain)


def _loss_head(y, target):
    t, d = y.shape
    tr = min(TOK_TR, t)

    def body(y_ref, t_ref, l_ref, dy_ref):
        @pl.when(pl.program_id(0) == 0)
        def _():
            l_ref[...] = jnp.zeros_like(l_ref)

        err = y_ref[...] - t_ref[...]
        dy_ref[...] = err * (1.0 / d)
        l_ref[...] += jnp.sum(jnp.sum(err * err, axis=1, keepdims=True), axis=0, keepdims=True)

    tok = pl.BlockSpec((tr, d), lambda i: (i, 0))
    return pl.pallas_call(
        body, name="loss_head", grid=(t // tr,), out_shape=(SDS((8, 128), f32), SDS((t, d), f32)),
        in_specs=[tok, tok], out_specs=(pl.BlockSpec((8, 128), lambda i: (0, 0)), tok), compiler_params=_cparams(1),
    )(y, target)


def _interleave(a, tq):
    bsz, l, d = a.shape
    return a.reshape(bsz, l // tq, 8, tq // 8, d).transpose(0, 1, 3, 2, 4).reshape(bsz, l // tq, tq, d)


def _deinterleave(a, l):
    bsz, nblk, tq, d = a.shape
    return a.reshape(bsz, nblk, tq // 8, 8, d).transpose(0, 1, 3, 2, 4).reshape(bsz, l, d)


def _compact_maps(a, nkb):
    ns, g, h, p = a.shape
    gl = g // nkb
    return a.reshape(ns, nkb, gl, h, p).transpose(0, 1, 3, 2, 4).reshape(ns, nkb, h, gl * p)


def _uncompact_maps(a, g):
    nkb, h, cols = a.shape
    gl = g // nkb
    return a.reshape(nkb, h, gl, cols // gl).transpose(0, 2, 1, 3).reshape(g, h, cols // gl)


def _pack_rows(arrs, cols, row_mult):
    flat = jnp.concatenate([a.reshape(-1).astype(f32) for a in arrs])
    rows = -(-flat.shape[0] // cols)
    rows = -(-rows // row_mult) * row_mult
    return jnp.pad(flat, (0, rows * cols - flat.shape[0])).reshape(rows, cols)


def _unpack_rows(buf, shapes):
    flat = buf.reshape(-1)
    out, off = [], 0
    for s in shapes:
        n = math.prod(s)
        out.append(flat[off:off + n].reshape(s))
        off += n
    return out


W_NAMES = ['s5_lambda_re', 's5_lambda_im', 's5_log_dt', 's5_b_re', 's5_b_im', 's5_c_re', 's5_c_im', 's5_d', 's5_w_glu',
           's5_b_glu', 'pool_w', 'pool_scale', 'ffn_w_gate', 'ffn_w_val', 'ffn_conv_w', 'ffn_conv_b', 'ffn_w_down',
           'norm_mix_pre', 'norm_mix_post', 'norm_ffn_pre', 'norm_ffn_post']
BIG = ('s5_w_glu', 'ffn_w_gate', 'ffn_w_val', 'ffn_w_down')


def kernel(x, s5_lambda_re, s5_lambda_im, s5_log_dt, s5_b_re, s5_b_im, s5_c_re, s5_c_im, s5_d, s5_w_glu, s5_b_glu, pool_w, pool_scale, ffn_w_gate, ffn_w_val, ffn_conv_w, ffn_conv_b, ffn_w_down, norm_mix_pre, norm_mix_post, norm_ffn_pre, norm_ffn_post, loss_target, m_s5_lambda_re, m_s5_lambda_im, m_s5_log_dt, m_s5_b_re, m_s5_b_im, m_s5_c_re, m_s5_c_im, m_s5_d, m_s5_w_glu, m_s5_b_glu, m_pool_w, m_pool_scale, m_ffn_w_gate, m_ffn_w_val, m_ffn_conv_w, m_ffn_conv_b, m_ffn_w_down, m_norm_mix_pre, m_norm_mix_post, m_norm_ffn_pre, m_norm_ffn_post, v_s5_lambda_re, v_s5_lambda_im, v_s5_log_dt, v_s5_b_re, v_s5_b_im, v_s5_c_re, v_s5_c_im, v_s5_d, v_s5_w_glu, v_s5_b_glu, v_pool_w, v_pool_scale, v_ffn_w_gate, v_ffn_w_val, v_ffn_conv_w, v_ffn_conv_b, v_ffn_w_down, v_norm_mix_pre, v_norm_mix_post, v_norm_ffn_pre, v_norm_ffn_post):
    w_in = dict(zip(W_NAMES, (s5_lambda_re, s5_lambda_im, s5_log_dt, s5_b_re, s5_b_im, s5_c_re, s5_c_im, s5_d, s5_w_glu,
                              s5_b_glu, pool_w, pool_scale, ffn_w_gate, ffn_w_val, ffn_conv_w, ffn_conv_b, ffn_w_down,
                              norm_mix_pre, norm_mix_post, norm_ffn_pre, norm_ffn_post)))
    m_in = dict(zip(W_NAMES, (m_s5_lambda_re, m_s5_lambda_im, m_s5_log_dt, m_s5_b_re, m_s5_b_im, m_s5_c_re, m_s5_c_im,
                              m_s5_d, m_s5_w_glu, m_s5_b_glu, m_pool_w, m_pool_scale, m_ffn_w_gate, m_ffn_w_val,
                              m_ffn_conv_w, m_ffn_conv_b, m_ffn_w_down, m_norm_mix_pre, m_norm_mix_post,
                              m_norm_ffn_pre, m_norm_ffn_post)))
    v_in = dict(zip(W_NAMES, (v_s5_lambda_re, v_s5_lambda_im, v_s5_log_dt, v_s5_b_re, v_s5_b_im, v_s5_c_re, v_s5_c_im,
                              v_s5_d, v_s5_w_glu, v_s5_b_glu, v_pool_w, v_pool_scale, v_ffn_w_gate, v_ffn_w_val,
                              v_ffn_conv_w, v_ffn_conv_b, v_ffn_w_down, v_norm_mix_pre, v_norm_mix_post,
                              v_norm_ffn_pre, v_norm_ffn_post)))

    bsz, seq, d = x.shape
    depth = ffn_w_gate.shape[0]
    n_s5, n_grp, n_state = s5_lambda_re.shape
    n_gch = s5_b_re.shape[3]
    n_pool = pool_w.shape[0]
    fs = ffn_w_gate.shape[2]
    f = N_CHIPS * fs
    gs = s5_w_glu.shape[1]
    nkb = d // S5_KB_CH
    tq = min(S5_TQ, seq)
    chip = 2 * lax.axis_index("x") + lax.axis_index("y")

    s_ffn = jnp.stack([ffn_w_gate.transpose(0, 2, 1), ffn_w_val.transpose(0, 2, 1), ffn_w_down], axis=1).astype(bf16)
    small_shard = [pool_w, pool_scale, ffn_conv_w]
    first_hosts = [("gather", 1, s_ffn[0]), ("gather", 1, s5_w_glu.astype(bf16)),
                   ("whole", 0, _pack_rows(small_shard, d, 16))]

    lam_r4 = s5_lambda_re.reshape(n_s5, n_grp, 1, n_state)
    lam_i4 = s5_lambda_im.reshape(n_s5, n_grp, 1, n_state)
    ldt4 = s5_log_dt.reshape(n_s5, n_grp, 1, 1)
    b_r4 = s5_b_re.transpose(0, 1, 3, 2)
    b_i4 = s5_b_im.transpose(0, 1, 3, 2)
    ab_r4, ab_i4, bb_r4, bb_i4 = _s5_discretise(lam_r4, lam_i4, ldt4, b_r4, b_i4)
    n_st_tot = n_grp * n_state
    ab_r, ab_i = ab_r4.reshape(n_s5, 1, n_st_tot), ab_i4.reshape(n_s5, 1, n_st_tot)
    bbc_r, bbc_i = _compact_maps(bb_r4, nkb), _compact_maps(bb_i4, nkb)
    cc_r, cc_in = _compact_maps(s5_c_re, nkb), _compact_maps(-s5_c_im, nkb)

    def s5_params(j):
        return dict(ab_r=ab_r[j], ab_i=ab_i[j], bb_r=bbc_r[j], bb_i=bbc_i[j], c_r=cc_r[j], c_in=cc_in[j],
                    d_skip=s5_d[j].reshape(1, d), b_glu=s5_b_glu[j].reshape(1, d))

    def row(a, i):
        return a[i].reshape(1, -1)

    assert depth >= 1 and N_CHIPS == 4
    saved = []
    xc = x
    w_layer = [None] * depth
    for i in range(depth):
        j = i // 2
        sv = dict(x_in=xc)
        if i % 2 == 0:
            sp = s5_params(j)
            x_il = _interleave(xc, tq)
            res = _s5_scan_fwd(x_il, row(norm_mix_pre, i), sp["d_skip"], sp["ab_r"], sp["ab_i"], sp["bb_r"], sp["bb_i"],
                               sp["c_r"], sp["c_in"], i, host=first_hosts if i == 0 else None)
            y_il, h_r, h_i = res[:3]
            if i == 0:
                w_layer[0], g_glu, g_small = res[3:6]
                parts = [_unpack_rows(g_small[k], [a.shape for a in small_shard]) for k in range(N_CHIPS)]
                pool_w_full = jnp.concatenate([p[0] for p in parts], axis=2).astype(bf16)
                pool_scale_full = jnp.concatenate([p[1] for p in parts], axis=1)
                conv_w_full = jnp.concatenate([p[2] for p in parts], axis=2)
            sp["w_glu"] = g_glu[j]
            xo_il = _s5_glu_fwd(y_il.reshape(bsz * seq, d), x_il.reshape(bsz * seq, d), sp["w_glu"], sp["b_glu"],
                                row(norm_mix_post, i), i)
            x_mid = _deinterleave(xo_il.reshape(x_il.shape), seq)
            sv.update(sp=sp, x_il=x_il, y_il=y_il, h_r=h_r, h_i=h_i)
        else:
            x_mid = _pool_fwd(xc, row(norm_mix_pre, i), row(norm_mix_post, i), pool_w_full[j], row(pool_scale_full, j), i)
        res = _ffn_fwd(x_mid, row(norm_ffn_pre, i), row(norm_ffn_post, i), w_layer[i], conv_w_full, ffn_conv_b, i,
                       host=[("gather", 1, s_ffn[i + 1])] if i + 1 < depth else None)
        xc, g_sv, v_sv, f_sv = res[:4]
        if i + 1 < depth:
            w_layer[i + 1] = res[4]
        sv.update(x_mid=x_mid, g=g_sv, v=v_sv, f=f_sv)
        saved.append(sv)

    sq, dy = _loss_head(xc.reshape(bsz * seq, d), loss_target.reshape(bsz * seq, d))
    loss = lax.psum(sq[0, 0] * (0.5 / d), ("x", "y", "c"))

    dx = dy.reshape(bsz, seq, d)
    dw_layers = [None] * depth
    r_layers = [None] * depth
    g_small_params = {n: [None] * w_in[n].shape[0] for n in W_NAMES if n not in BIG}
    dglu = [None] * n_s5
    for i in range(depth - 1, -1, -1):
        j = i // 2
        sv = saved[i]
        res = _ffn_bwd_act(sv["x_mid"], sv["f"], sv["g"], sv["v"], dx, row(norm_ffn_pre, i), row(norm_ffn_post, i),
                           w_layer[i], conv_w_full, ffn_conv_b, i,
                           host=[("slab", 1, dw_layers[i + 1])] if i + 1 < depth else None)
        dx, dg, dv, hdn, hb, dfb, dgain, dconv = res[:8]
        if i + 1 < depth:
            r_layers[i + 1] = res[8]
        dw_layers[i] = _ffn_bwd_weights(dg, dv, hdn, hb.reshape(bsz * seq, d), dfb.reshape(bsz * seq, d), i)
        g_small_params["norm_ffn_pre"][i] = dgain[0]
        g_small_params["norm_ffn_post"][i] = dgain[1]
        dconv = dconv.transpose(1, 0, 2).reshape(8, f)
        g_small_params["ffn_conv_w"][i] = dconv[0:3]
        g_small_params["ffn_conv_b"][i] = dconv[3]
        if i % 2 == 0:
            sp = sv["sp"]
            dxo_il = _interleave(dx, tq)
            dy_s, dglu[j], ds_glu = _s5_glu_bwd(sv["y_il"].reshape(bsz * seq, d), dxo_il.reshape(bsz * seq, d),
                                                sp["w_glu"], sp["b_glu"], row(norm_mix_post, i), i)
            hosts = None
            if i == 0:
                assert all(g is not None for g in dglu) and all(g is not None for g in g_small_params["pool_w"])
                hosts = [("slab", 1, dw_layers[0]), ("slab", 1, jnp.stack(dglu).astype(bf16)),
                         ("slab", 2, jnp.stack(g_small_params["pool_w"]))]
            res = _s5_scan_bwd(sv["x_il"], dy_s.reshape(sv["x_il"].shape), sv["h_r"], sv["h_i"], row(norm_mix_pre, i),
                               sp["d_skip"], sp["ab_r"], sp["ab_i"], sp["bb_r"], sp["bb_i"], sp["c_r"], sp["c_in"], i,
                               host=hosts)
            du, dbb_r, dbb_i, dc_r, dc_in, dab_r, dab_i, dd = res[:8]
            if i == 0:
                r_layers[0], r_glu, r_pw = res[8:11]
            dx_il, dgpre = _norm_residual_bwd(sv["x_il"].reshape(bsz * seq, d), du.reshape(bsz * seq, d),
                                              dxo_il.reshape(bsz * seq, d), row(norm_mix_pre, i), f"s5_pre_bwd_{i}")
            dx = _deinterleave(dx_il.reshape(sv["x_il"].shape), seq)
            g_small_params["norm_mix_pre"][i] = dgpre[0]
            g_small_params["norm_mix_post"][i] = ds_glu[0]
            g_small_params["s5_b_glu"][j] = ds_glu[1]
            g_small_params["s5_d"][j] = dd[:, 0, :].reshape(d)
            g_small_params["s5_c_re"][j] = _uncompact_maps(dc_r, n_grp)
            g_small_params["s5_c_im"][j] = -_uncompact_maps(dc_in, n_grp)
            sv["g_ab"] = (dab_r[:, 0, :].reshape(n_grp, 1, n_state), dab_i[:, 0, :].reshape(n_grp, 1, n_state),
                          _uncompact_maps(dbb_r, n_grp), _uncompact_maps(dbb_i, n_grp))
        else:
            dx, dpw, dsm = _pool_bwd(sv["x_in"], dx, row(norm_mix_pre, i), row(norm_mix_post, i), pool_w_full[j],
                                     row(pool_scale_full, j), i)
            g_small_params["norm_mix_pre"][i] = dsm[0]
            g_small_params["norm_mix_post"][i] = dsm[1]
            g_small_params["pool_scale"][j] = dsm[2]
            g_small_params["pool_w"][j] = dpw
    grad_x = dx

    g_ab = [saved[2 * j]["g_ab"] for j in range(n_s5)]
    d_lr, d_li, d_ld, d_br, d_bi = _s5_discretise_bwd(
        lam_r4, lam_i4, ldt4, b_r4, b_i4, jnp.stack([g[0] for g in g_ab]), jnp.stack([g[1] for g in g_ab]),
        jnp.stack([g[2] for g in g_ab]), jnp.stack([g[3] for g in g_ab]))
    small_full = {n: (jnp.stack(v) if v[0] is not None else None) for n, v in g_small_params.items()}
    small_full["s5_lambda_re"] = d_lr.reshape(n_s5, n_grp, n_state)
    small_full["s5_lambda_im"] = d_li.reshape(n_s5, n_grp, n_state)
    small_full["s5_log_dt"] = d_ld.reshape(n_s5, n_grp)
    small_full["s5_b_re"] = d_br.transpose(0, 1, 3, 2)
    small_full["s5_b_im"] = d_bi.transpose(0, 1, 3, 2)
    bc_names = ["s5_b_re", "s5_b_im", "s5_c_re", "s5_c_im"]
    small_names = [n for n in W_NAMES if n not in BIG and n not in bc_names and n != "pool_w"]
    q = _pack_rows([small_full[n] for n in small_names], d, 16)
    q_bc = _pack_rows([small_full[n] for n in bc_names], d, 16).astype(bf16)

    r_q, r_bc = _chip_exchange([("whole", 0, q), ("whole", 0, q_bc)], "exchange_small_grads")
    p_ffn = _sum_slots_layers([r.reshape(N_CHIPS, 3 * fs, d) for r in r_layers], "sum_slots_ffn").reshape(depth * 3 * fs, d)
    p_glu = _sum_slots(r_glu.reshape(N_CHIPS, n_s5 * gs, d), "sum_slots_glu")
    pw_rows, pw_cols = math.prod(pool_w.shape[:3]), pool_w.shape[3]
    p_pw = _sum_slots(r_pw.reshape(N_CHIPS, pw_rows, pw_cols), "sum_slots_pool_w")
    p_q = _sum_slots(r_q, "sum_slots_small")
    p_bc = _sum_slots(r_bc, "sum_slots_bc")
    o_ffn, o_glu, o_pw, o_q, o_bc = _exchange_with_sibling([p_ffn, p_glu, p_pw, p_q, p_bc])

    outs = {}

    def put(name, res, shape):
        outs[name] = tuple(r.reshape(shape) for r in res)

    pf, of = p_ffn.reshape(depth, 3, fs, d), o_ffn.reshape(depth, 3, fs, d)
    for kind, name in ((0, "ffn_w_gate"), (1, "ffn_w_val")):
        parts_g = [a[:, kind].transpose(0, 2, 1).reshape(depth * d, fs) for a in (pf, of)]
        put(name, _adamw(parts_g, w_in[name].reshape(depth * d, fs), m_in[name].reshape(depth * d, fs),
                         v_in[name].reshape(depth * d, fs), f"adamw_{name}"), w_in[name].shape)
    put("ffn_w_down", _adamw([pf[:, 2].reshape(depth * fs, d), of[:, 2].reshape(depth * fs, d)],
                             ffn_w_down.reshape(depth * fs, d), m_ffn_w_down.reshape(depth * fs, d),
                             v_ffn_w_down.reshape(depth * fs, d), "adamw_ffn_w_down"), ffn_w_down.shape)
    put("s5_w_glu", _adamw([p_glu, o_glu], s5_w_glu.reshape(n_s5 * gs, d), m_s5_w_glu.reshape(n_s5 * gs, d),
                           v_s5_w_glu.reshape(n_s5 * gs, d), "adamw_s5_w_glu"), s5_w_glu.shape)

    q_tot, bc_tot, pw_tot = _add_pairs([(p_q, o_q), (p_bc, o_bc), (p_pw, o_pw)], "sum_small")
    g_small = dict(zip(small_names, _unpack_rows(q_tot, [small_full[n].shape for n in small_names])))
    g_small.update(zip(bc_names, _unpack_rows(bc_tot, [small_full[n].shape for n in bc_names])))
    g_small["pool_w"] = pw_tot.reshape(pool_w.shape)
    g_small["pool_scale"] = lax.dynamic_slice_in_dim(g_small["pool_scale"], chip * pool_scale.shape[1],
                                                     pool_scale.shape[1], axis=1)
    g_small["ffn_conv_w"] = lax.dynamic_slice_in_dim(g_small["ffn_conv_w"], chip * fs, fs, axis=2)
    all_small = [n for n in W_NAMES if n not in BIG]
    local_shapes = [w_in[n].shape for n in all_small]
    res = _adamw([_pack_rows([g_small[n] for n in all_small], d, 64)],
                 _pack_rows([w_in[n] for n in all_small], d, 64), _pack_rows([m_in[n] for n in all_small], d, 64),
                 _pack_rows([v_in[n] for n in all_small], d, 64), "adamw_small")
    unpacked = [_unpack_rows(r, local_shapes) for r in res]
    for idx, n in enumerate(all_small):
        outs[n] = (g_small[n], unpacked[1][idx], unpacked[2][idx], unpacked[3][idx])

    return (loss, grad_x, *[outs[n][0] for n in W_NAMES], *[outs[n][1] for n in W_NAMES],
            *[outs[n][2] for n in W_NAMES], *[outs[n][3] for n in W_NAMES])
```

```python
import math

import jax
import jax.numpy as jnp
from jax import lax
from jax.experimental import pallas as pl
from jax.experimental.pallas import tpu as pltpu

f32, bf16 = jnp.float32, jnp.bfloat16
SDS = jax.ShapeDtypeStruct
MESH = pl.DeviceIdType.MESH

RMS_EPS = 1e-6
GELU_C = math.sqrt(2.0 / math.pi)
GELU_K = 0.044715
ADAM_LR, ADAM_B1, ADAM_B2, ADAM_EPS, ADAM_WD, ADAM_STEP = 0.001, 0.9, 0.999, 1e-08, 0.01, 10
POOL_WINDOWS = (2, 4, 8, 16)
POOL_HALO = 16
S5_GROUP_CH = 16
S5_STATE = 64
S5_KB_CH = 256
N_CHIPS = 4

FFN_TL = 512
FFN_TL_BWD = 256
FFN_FT = 256
FFN_CH = 16
FFN_SPLIT = 2
WG_TR = 2048
POOL_TL = 512
S5_TQ = 512
S5_CB = 1024
S5_UNROLL = 2
TOK_TR = 512
VMEM_LIMIT = 56 * 1024 * 1024

HBM = pl.BlockSpec(memory_space=pltpu.HBM)


def _cparams(n_axes=0, side_effects=False):
    kw = dict(vmem_limit_bytes=VMEM_LIMIT)
    if n_axes:
        kw["dimension_semantics"] = ("arbitrary",) * n_axes
    if side_effects:
        kw["has_side_effects"] = True
    return pltpu.CompilerParams(**kw)


def _dot(a, b):
    return jnp.dot(a, b, preferred_element_type=f32)


def _dot_nt(a, b):
    return lax.dot_general(a, b, (((1,), (1,)), ((), ())), preferred_element_type=f32)


def _dot_tn(a, b):
    return lax.dot_general(a, b, (((0,), (0,)), ((), ())), preferred_element_type=f32)


def _rms_scale(x):
    return lax.rsqrt(jnp.mean(x * x, axis=-1, keepdims=True) + RMS_EPS)


def _rms_fwd(x, gain):
    return x * _rms_scale(x) * gain


def _rms_bwd(x, gain, dy):
    r = _rms_scale(x)
    xn = x * r
    dgain = jnp.sum(dy * xn, axis=0, keepdims=True)
    dxn = dy * gain
    dx = r * (dxn - xn * jnp.mean(dxn * xn, axis=-1, keepdims=True))
    return dx, dgain


def _gelu(x):
    t = jnp.tanh(x * (GELU_C + (GELU_C * GELU_K) * (x * x)))
    hx = 0.5 * x
    return hx + hx * t


def _gelu_and_grad(x):
    x2 = x * x
    t = jnp.tanh(x * (GELU_C + (GELU_C * GELU_K) * x2))
    hx = 0.5 * x
    return hx + hx * t, (0.5 + 0.5 * t) + hx * (1.0 - t * t) * (GELU_C + (3.0 * GELU_C * GELU_K) * x2)


def _sigmoid(x):
    return 1.0 / (1.0 + jnp.exp(-x))


def _cmul(ar, ai, br, bi):
    return ar * br - ai * bi, ar * bi + ai * br


def _row_block(n, cap):
    best = None
    for d in range(16, min(n, cap) + 1, 16):
        if n % d == 0:
            best = d
    assert best is not None, n
    return best


def _mesh_pos():
    return lax.axis_index("x"), lax.axis_index("y"), lax.axis_index("c")


def _other_chips(x, y):
    return [(1 - x, y), (x, 1 - y), (1 - x, 1 - y)]


def _slab_index(ndim, axis, start, size):
    return tuple(pl.ds(start, size) if a == axis else slice(None) for a in range(ndim))


class _ChipExchange:
    def __init__(self, kind, axis, src, dst, send_sems, recv_sems, loc_sem):
        x, y, c = _mesh_pos()
        me = 2 * x + y
        nd = len(src.shape)
        size = src.shape[axis] if kind == "gather" else src.shape[axis] // N_CHIPS

        def src_for(kk):
            return src.at[_slab_index(nd, axis, kk * size, size)] if kind == "slab" else src

        def dst_for(kk):
            return dst.at[_slab_index(nd, axis, kk * size, size)] if kind == "gather" else dst.at[kk]

        self.own = pltpu.make_async_copy(src_for(me), dst_for(me), loc_sem)
        self.sends, self.recvs = [], []
        for j, chip in enumerate(_other_chips(x, y)):
            kk = 2 * chip[0] + chip[1]
            peer = dict(send_sem=send_sems.at[j], recv_sem=recv_sems.at[j], device_id=(chip[0], chip[1], c),
                        device_id_type=MESH)
            self.sends.append(pltpu.make_async_remote_copy(src_ref=src_for(kk), dst_ref=dst_for(me), **peer))
            self.recvs.append(pltpu.make_async_remote_copy(src_ref=src_for(me), dst_ref=dst_for(kk), **peer))

    def start(self):
        self.own.start()
        for cp in self.sends:
            cp.start()

    def finish(self):
        for cp in self.recvs:
            cp.wait_recv()
        for cp in self.sends:
            cp.wait_send()
        self.own.wait()


def _exchange_out_shape(kind, axis, src):
    if kind == "gather":
        shape = tuple(N_CHIPS * n if a == axis else n for a, n in enumerate(src.shape))
    elif kind == "slab":
        shape = (N_CHIPS,) + tuple(n // N_CHIPS if a == axis else n for a, n in enumerate(src.shape))
    else:
        shape = (N_CHIPS,) + tuple(src.shape)
    return SDS(shape, src.dtype)


EXCHANGE_SEMS = [pltpu.SemaphoreType.DMA((3,)), pltpu.SemaphoreType.DMA((3,)), pltpu.SemaphoreType.DMA((1,))]


def _chip_exchange(items, name, to_sibling=()):
    n, ns = len(items), len(to_sibling)

    def body(*refs):
        ins, sib_ins = refs[:n], refs[n:n + ns]
        outs, sib_outs = refs[n + ns:2 * n + ns], refs[2 * n + ns:2 * (n + ns)]
        sems = refs[2 * (n + ns):]
        exs = [_ChipExchange(kind, axis, ins[i], outs[i], sems[3 * i], sems[3 * i + 1], sems[3 * i + 2].at[0])
               for i, (kind, axis, _) in enumerate(items)]
        x, y, c = _mesh_pos()
        sibs = [pltpu.make_async_remote_copy(src_ref=s, dst_ref=t, send_sem=sems[3 * n].at[i], recv_sem=sems[3 * n + 1].at[i],
                                             device_id=(x, y, 1 - c), device_id_type=MESH)
                for i, (s, t) in enumerate(zip(sib_ins, sib_outs))]
        for ex in exs:
            ex.start()
        for cp in sibs:
            cp.start()
        for ex in exs:
            ex.finish()
        for cp in sibs:
            cp.wait()

    sib_sems = [pltpu.SemaphoreType.DMA((ns,)), pltpu.SemaphoreType.DMA((ns,))] if ns else []
    return pl.pallas_call(
        body, name=name,
        out_shape=tuple(_exchange_out_shape(k, a, arr) for k, a, arr in items)
        + tuple(SDS(p.shape, p.dtype) for p in to_sibling),
        in_specs=[HBM] * (n + ns), out_specs=tuple([HBM] * (n + ns)), scratch_shapes=EXCHANGE_SEMS * n + sib_sems,
        compiler_params=_cparams(side_effects=True),
    )(*[arr for _, _, arr in items], *to_sibling)


def _hosted(hosts, host_ins, host_outs, sems, first, last):
    if not hosts:
        return lambda: None
    exs = [_ChipExchange(h[0], h[1], host_ins[i], host_outs[i], sems[3 * i], sems[3 * i + 1], sems[3 * i + 2].at[0])
           for i, h in enumerate(hosts)]

    @pl.when(first)
    def _():
        for ex in exs:
            ex.start()

    def finish():
        @pl.when(last)
        def _():
            for ex in exs:
                ex.finish()

    return finish


def _host_args(hosts):
    hosts = hosts or []
    n = len(hosts)
    return [HBM] * n, [_exchange_out_shape(*h) for h in hosts], [HBM] * n, EXCHANGE_SEMS * n, [h[2] for h in hosts]


def _exchange_with_sibling(parts):
    n = len(parts)

    def body(*refs):
        ins, outs = refs[:n], refs[n:2 * n]
        send_sems, recv_sems = refs[2 * n:]
        x, y, c = _mesh_pos()
        cps = [pltpu.make_async_remote_copy(src_ref=s, dst_ref=t, send_sem=send_sems.at[i], recv_sem=recv_sems.at[i],
                                            device_id=(x, y, 1 - c), device_id_type=MESH)
               for i, (s, t) in enumerate(zip(ins, outs))]
        for cp in cps:
            cp.start()
        for cp in cps:
            cp.wait()

    return pl.pallas_call(
        body, name="exchange_with_sibling",
        out_shape=tuple(SDS(p.shape, p.dtype) for p in parts),
        in_specs=[HBM] * n, out_specs=tuple([HBM] * n),
        scratch_shapes=[pltpu.SemaphoreType.DMA((n,)), pltpu.SemaphoreType.DMA((n,))],
        compiler_params=_cparams(side_effects=True),
    )(*parts)


def _sum_slots(r, name):
    _, rows, cols = r.shape
    tr = _row_block(rows, 512)

    def body(r_ref, o_ref):
        o_ref[...] = ((r_ref[0].astype(f32) + r_ref[1].astype(f32)) + r_ref[2].astype(f32)) + r_ref[3].astype(f32)

    return pl.pallas_call(
        body, name=name, grid=(rows // tr,), out_shape=SDS((rows, cols), f32),
        in_specs=[pl.BlockSpec((N_CHIPS, tr, cols), lambda i: (0, i, 0))],
        out_specs=pl.BlockSpec((tr, cols), lambda i: (i, 0)),
        compiler_params=_cparams(1),
    )(r)


def _sum_slots_layers(rs, name):
    _, rows, cols = rs[0].shape
    nl = len(rs)
    tr = _row_block(rows, 256)

    def body(*refs):
        o_ref = refs[nl]
        for l in range(nl):
            r = refs[l]
            o_ref[l] = ((r[0].astype(f32) + r[1].astype(f32)) + r[2].astype(f32)) + r[3].astype(f32)

    return pl.pallas_call(
        body, name=name, grid=(rows // tr,), out_shape=SDS((nl, rows, cols), f32),
        in_specs=[pl.BlockSpec((N_CHIPS, tr, cols), lambda i: (0, i, 0))] * nl,
        out_specs=pl.BlockSpec((nl, tr, cols), lambda i: (0, i, 0)),
        compiler_params=_cparams(1),
    )(*rs)


def _add_pairs(pairs, name):
    n = len(pairs)

    def body(*refs):
        for i in range(n):
            refs[2 * n + i][...] = refs[2 * i][...] + refs[2 * i + 1][...]

    return pl.pallas_call(body, name=name, out_shape=tuple(SDS(a.shape, f32) for a, _ in pairs),
                          compiler_params=_cparams())(*[t for pair in pairs for t in pair])


def _adamw(g_parts, w, m, v, name):
    rows, cols = w.shape[-2:]
    tr = _row_block(rows, 512)
    n_g = len(g_parts)
    c1 = 1.0 / (1.0 - ADAM_B1 ** ADAM_STEP)
    c2 = 1.0 / (1.0 - ADAM_B2 ** ADAM_STEP)

    def body(*refs):
        g_refs = refs[:n_g]
        w_ref, m_ref, v_ref, go_ref, d_ref, mo_ref, vo_ref = refs[n_g:]
        g = g_refs[0][...]
        for r in g_refs[1:]:
            g = g + r[...]
        mn = ADAM_B1 * m_ref[...] + (1.0 - ADAM_B1) * g
        vn = ADAM_B2 * v_ref[...] + (1.0 - ADAM_B2) * (g * g)
        go_ref[...] = g
        mo_ref[...] = mn
        vo_ref[...] = vn
        d_ref[...] = -ADAM_LR * ((mn * c1) / (jnp.sqrt(vn * c2) + ADAM_EPS) + ADAM_WD * w_ref[...])

    if w.ndim == 2:
        grid, spec = (rows // tr,), pl.BlockSpec((tr, cols), lambda i: (i, 0))
    else:
        grid, spec = (w.shape[0], rows // tr), pl.BlockSpec((None, tr, cols), lambda a, i: (a, i, 0))
    return pl.pallas_call(
        body, name=name, grid=grid, out_shape=tuple(SDS(w.shape, f32) for _ in range(4)),
        in_specs=[spec] * (n_g + 3), out_specs=(spec,) * 4, compiler_params=_cparams(len(grid)),
    )(*g_parts, w, m, v)


def _resident_spec(shape, n_grid):
    zeros = (0,) * len(shape)
    return pl.BlockSpec(tuple(shape), {2: lambda a, b: zeros, 3: lambda a, b, c: zeros}[n_grid],
                        pipeline_mode=pl.Buffered(1))


def _shifted_rows(ext):
    return ext[8:, :], pltpu.roll(ext, 1, axis=0)[8:, :], pltpu.roll(ext, 2, axis=0)[8:, :]


def _later_rows(ext):
    n = ext.shape[0]
    return pltpu.roll(ext, n - 1, axis=0)[0:n - 8, :], pltpu.roll(ext, n - 2, axis=0)[0:n - 8, :]


def _ffn_fwd(x_mid, gain_pre, gain_post, w_l, conv_w, conv_b, li, host=None):
    bsz, l, d = x_mid.shape
    f = w_l.shape[1]
    tl, ft = min(FFN_TL, l), min(FFN_FT, f)
    nt, nf = l // tl, f // ft
    n_sub = FFN_SPLIT if tl % (FFN_SPLIT * FFN_CH) == 0 else 1
    sub = tl // n_sub
    h_in, h_shape, h_out, h_scr, h_ops = _host_args(host)

    def body(x_ref, gpre_ref, gpost_ref, w_ref, cw_ref, cb_ref, *rest):
        xo_ref, g_ref, v_ref, f_ref = rest[len(h_in):len(h_in) + 4]
        s0 = len(h_in) + 4 + len(h_out)
        h_sc, facc, gprev = rest[s0:s0 + 3]
        b, t = pl.program_id(0), pl.program_id(1)
        finish = _hosted(host, rest[:len(h_in)], rest[len(h_in) + 4:s0], rest[s0 + 3:],
                         (b == 0) & (t == 0), (b == bsz - 1) & (t == nt - 1))
        h_sc[...] = _rms_fwd(x_ref[...], gpre_ref[...]).astype(bf16)

        @pl.when(t == 0)
        def _():
            gprev[...] = jnp.zeros_like(gprev)

        def gate_val(j):
            wg, wv = w_ref[0, j * ft:(j + 1) * ft, :], w_ref[1, j * ft:(j + 1) * ft, :]
            return ([_dot_nt(h_sc[s * sub:(s + 1) * sub, :], wg) for s in range(n_sub)],
                    [_dot_nt(h_sc[s * sub:(s + 1) * sub, :], wv) for s in range(n_sub)])

        ahead = gate_val(0)
        for j in range(nf):
            gs, vs = ahead
            if j + 1 < nf:
                ahead = gate_val(j + 1)
            cols = slice(j * ft, (j + 1) * ft)
            wd = w_ref[2, cols, :]
            w0, w1, w2, bias = (jnp.broadcast_to(r, (FFN_CH, ft))
                                for r in (cw_ref[0:1, cols], cw_ref[1:2, cols], cw_ref[2:3, cols], cb_ref[:, cols]))
            hist = gprev[j]
            for s in range(n_sub):
                rows = slice(s * sub, (s + 1) * sub)
                g_ref[j, rows, :] = gs[s].astype(bf16)
                v_ref[j, rows, :] = vs[s].astype(bf16)
                hdn = []
                for r0 in range(0, sub, FFN_CH):
                    g0, g1, g2 = _shifted_rows(jnp.concatenate([hist, gs[s][r0:r0 + FFN_CH, :]], axis=0))
                    hist = g0[FFN_CH - 8:, :]
                    hdn.append((_gelu(bias + w0 * g2 + w1 * g1 + w2 * g0) * vs[s][r0:r0 + FFN_CH, :]).astype(bf16))
                part = _dot(jnp.concatenate(hdn, axis=0), wd)
                if j == 0:
                    facc[rows, :] = part
                else:
                    facc[rows, :] += part
            gprev[j] = hist

        fv = facc[...]
        f_ref[...] = fv
        xo_ref[...] = x_ref[...] + _rms_fwd(fv, gpost_ref[...])
        finish()

    tok = pl.BlockSpec((None, tl, d), lambda b, t: (b, t, 0))
    hid = pl.BlockSpec((nf, tl, ft), lambda b, t: (0, b * nt + t, 0))
    gain = pl.BlockSpec((1, d), lambda b, t: (0, 0))
    hid_shape = SDS((nf, bsz * l, ft), bf16)
    return pl.pallas_call(
        body, name=f"ffn_fwd_{li}", grid=(bsz, nt),
        out_shape=(SDS((bsz, l, d), f32), hid_shape, hid_shape, SDS((bsz, l, d), f32), *h_shape),
        in_specs=[tok, gain, gain, _resident_spec(w_l.shape, 2),
                  pl.BlockSpec((None, 3, f), lambda b, t: (li, 0, 0)),
                  pl.BlockSpec((None, 1, f), lambda b, t: (li, 0, 0)), *h_in],
        out_specs=(tok, hid, hid, tok, *h_out),
        scratch_shapes=[pltpu.VMEM((tl, d), bf16), pltpu.VMEM((tl, d), f32), pltpu.VMEM((nf, 8, ft), f32), *h_scr],
        compiler_params=_cparams(2, side_effects=bool(host)),
    )(x_mid, gain_pre, gain_post, w_l, conv_w, conv_b.reshape(conv_b.shape[0], 1, f), *h_ops)


def _ffn_bwd_act(x_mid, f_sv, g_sv, v_sv, dxo, gain_pre, gain_post, w_l, conv_w, conv_b, li, host=None):
    bsz, l, d = x_mid.shape
    f = w_l.shape[1]
    tl, ft = min(FFN_TL_BWD, l), min(FFN_FT, f)
    nt, nf = l // tl, f // ft
    h_in, h_shape, h_out, h_scr, h_ops = _host_args(host)

    def body(x_ref, f_ref, dxo_ref, g_ref, v_ref, gh_ref, gpre_ref, gpost_ref, w_ref, cw_ref, cb_ref, *rest):
        o0 = len(h_in)
        dx_ref, dg_ref, dv_ref, hdn_ref, h_ref, df_ref, dgain_ref, dconv_ref = rest[o0:o0 + 8]
        s0 = o0 + 8 + len(h_out)
        h_sc, df_sc, dh_acc, dgc_next = rest[s0:s0 + 4]
        b, t = pl.program_id(0), pl.program_id(1)
        tt = nt - 1 - t
        finish = _hosted(host, rest[:o0], rest[o0 + 8:s0], rest[s0 + 4:],
                         (b == 0) & (t == 0), (b == bsz - 1) & (t == nt - 1))

        @pl.when((b == 0) & (t == 0))
        def _():
            dgain_ref[...] = jnp.zeros_like(dgain_ref)
            dconv_ref[...] = jnp.zeros_like(dconv_ref)

        @pl.when(t == 0)
        def _():
            dgc_next[...] = jnp.zeros_like(dgc_next)

        hb = _rms_fwd(x_ref[...], gpre_ref[...]).astype(bf16)
        h_sc[...] = hb
        h_ref[...] = hb
        df, dgp = _rms_bwd(f_ref[...], gpost_ref[...], dxo_ref[...])
        dfb = df.astype(bf16)
        df_sc[...] = dfb
        df_ref[...] = dfb
        dgain_ref[1:2, :] += dgp

        def fold(a):
            return a.reshape(FFN_CH // 8, 8, ft).sum(axis=0)

        ahead = _dot_nt(df_sc[...], w_ref[2, 0:ft, :])
        for j in range(nf):
            dhdn = ahead
            if j + 1 < nf:
                ahead = _dot_nt(df_sc[...], w_ref[2, (j + 1) * ft:(j + 2) * ft, :])
            cols = slice(j * ft, (j + 1) * ft)
            w0, w1, w2, bias = (jnp.broadcast_to(r, (FFN_CH, ft))
                                for r in (cw_ref[0:1, cols], cw_ref[1:2, cols], cw_ref[2:3, cols], cb_ref[:, cols]))

            def conv_bwd(dgc, after):
                d1, d2 = _later_rows(jnp.concatenate([dgc, after], axis=0))
                return (w2 * dgc + w1 * d1 + w0 * d2).astype(bf16)

            g = g_ref[j].astype(f32)
            v = v_ref[j].astype(f32)
            hist = jnp.where(tt > 0, gh_ref[j].astype(f32)[8:16, :], 0.0)
            acc = [jnp.zeros((8, ft), f32)] * 4
            first, pending, hdn, dvs, dgs = None, None, [], [], []
            for r0 in range(0, tl, FFN_CH):
                g0, g1, g2 = _shifted_rows(jnp.concatenate([hist, g[r0:r0 + FFN_CH, :]], axis=0))
                hist = g0[FFN_CH - 8:, :]
                vc, dc = v[r0:r0 + FFN_CH, :], dhdn[r0:r0 + FFN_CH, :]
                u, ug = _gelu_and_grad(bias + w0 * g2 + w1 * g1 + w2 * g0)
                hdn.append((u * vc).astype(bf16))
                dvs.append((dc * u).astype(bf16))
                dgc = dc * vc * ug
                acc = [acc[0] + fold(dgc * g2), acc[1] + fold(dgc * g1), acc[2] + fold(dgc * g0), acc[3] + fold(dgc)]
                if pending is None:
                    first = dgc[0:8, :]
                else:
                    dgs.append(conv_bwd(pending, dgc[0:8, :]))
                pending = dgc
            dgs.append(conv_bwd(pending, dgc_next[j]))
            dgc_next[j] = first
            dgb, dvb = jnp.concatenate(dgs, axis=0), jnp.concatenate(dvs, axis=0)
            hdn_ref[j] = jnp.concatenate(hdn, axis=0)
            dg_ref[j] = dgb
            dv_ref[j] = dvb
            part = _dot(dgb, w_ref[0, cols, :]) + _dot(dvb, w_ref[1, cols, :])
            if j == 0:
                dh_acc[...] = part
            else:
                dh_acc[...] += part
            for k in range(4):
                dconv_ref[j, k:k + 1, :] += jnp.sum(acc[k], axis=0, keepdims=True)

        dxp, dgp = _rms_bwd(x_ref[...], gpre_ref[...], dh_acc[...])
        dx_ref[...] = dxo_ref[...] + dxp
        dgain_ref[0:1, :] += dgp
        finish()

    tok = pl.BlockSpec((None, tl, d), lambda b, t: (b, nt - 1 - t, 0))
    hid = pl.BlockSpec((nf, tl, ft), lambda b, t: (0, b * nt + nt - 1 - t, 0))
    halo = pl.BlockSpec((nf, 16, ft), lambda b, t: (0, jnp.maximum((b * nt + nt - 1 - t) * (tl // 16) - 1, 0), 0))
    gain = pl.BlockSpec((1, d), lambda b, t: (0, 0))
    hid_shape = SDS((nf, bsz * l, ft), bf16)
    return pl.pallas_call(
        body, name=f"ffn_bwd_act_{li}", grid=(bsz, nt),
        out_shape=(SDS((bsz, l, d), f32), hid_shape, hid_shape, hid_shape,
                   SDS((bsz, l, d), bf16), SDS((bsz, l, d), bf16), SDS((8, d), f32), SDS((nf, 8, ft), f32), *h_shape),
        in_specs=[tok, tok, tok, hid, hid, halo, gain, gain, _resident_spec(w_l.shape, 2),
                  pl.BlockSpec((None, 3, f), lambda b, t: (li, 0, 0)),
                  pl.BlockSpec((None, 1, f), lambda b, t: (li, 0, 0)), *h_in],
        out_specs=(tok, hid, hid, hid, tok, tok,
                   pl.BlockSpec((8, d), lambda b, t: (0, 0)), pl.BlockSpec((nf, 8, ft), lambda b, t: (0, 0, 0)),
                   *h_out),
        scratch_shapes=[pltpu.VMEM((tl, d), bf16), pltpu.VMEM((tl, d), bf16), pltpu.VMEM((tl, d), f32),
                        pltpu.VMEM((nf, 8, ft), f32), *h_scr],
        compiler_params=_cparams(2, side_effects=bool(host)),
    )(x_mid, f_sv, dxo, g_sv, v_sv, g_sv, gain_pre, gain_post, w_l, conv_w,
      conv_b.reshape(conv_b.shape[0], 1, f), *h_ops)


def _ffn_bwd_weights(dg, dv, hdn, h, df, li):
    nf, t, ft = dg.shape
    f = nf * ft
    d = h.shape[1]
    tr = min(WG_TR, t)
    nr = t // tr

    def body(dg_ref, dv_ref, hdn_ref, h_ref, df_ref, o_ref, acc):
        r = pl.program_id(1)

        @pl.when(r == 0)
        def _():
            acc[...] = jnp.zeros_like(acc)

        rows = pl.ds(pl.multiple_of(r * tr, tr), tr)
        hb = h_ref[rows, :]
        acc[0] += _dot_tn(dg_ref[...], hb)
        acc[1] += _dot_tn(dv_ref[...], hb)
        acc[2] += _dot_tn(hdn_ref[...], df_ref[rows, :])

        @pl.when(r == nr - 1)
        def _():
            o_ref[...] = acc[...].astype(bf16)

    hid = pl.BlockSpec((None, tr, ft), lambda j, r: (j, r, 0))
    tok = _resident_spec((t, d), 2)
    return pl.pallas_call(
        body, name=f"ffn_bwd_weights_{li}", grid=(nf, nr), out_shape=SDS((3, f, d), bf16),
        in_specs=[hid, hid, hid, tok, tok], out_specs=pl.BlockSpec((3, ft, d), lambda j, r: (0, j, 0)),
        scratch_shapes=[pltpu.VMEM((3, ft, d), f32)], compiler_params=_cparams(2),
    )(dg, dv, hdn, h, df)


def _pool_counts(t0, tl, d):
    gch = d // len(POOL_WINDOWS)
    tpos = (t0 + lax.broadcasted_iota(jnp.int32, (tl, d), 0) + 1).astype(f32)
    lane = lax.broadcasted_iota(jnp.int32, (tl, d), 1)
    win = jnp.full((tl, d), float(POOL_WINDOWS[-1]), f32)
    for gi in range(len(POOL_WINDOWS) - 2, -1, -1):
        win = jnp.where(lane < (gi + 1) * gch, float(POOL_WINDOWS[gi]), win)
    return jnp.minimum(tpos, win)


def _pool_select(parts, tl, d):
    gch = d // len(POOL_WINDOWS)
    lane = lax.broadcasted_iota(jnp.int32, (tl, d), 1)
    out = parts[-1]
    for gi in range(len(parts) - 2, -1, -1):
        out = jnp.where(lane < (gi + 1) * gch, parts[gi], out)
    return out


def _pool_window_sums(u, halo, tl):
    ext = jnp.concatenate([halo, u], axis=0)
    sums, cur = [], ext
    for k in (1, 2, 4, 8):
        cur = cur + pltpu.roll(cur, k, axis=0)
        sums.append(cur[POOL_HALO:POOL_HALO + tl, :])
    return sums


def _pool_mix(u, halo, cnt, pw_ref, scale, tl, d):
    gch = d // len(POOL_WINDOWS)
    pooled = _pool_select(_pool_window_sums(u, halo, tl), tl, d) / cnt
    diff = pooled - u
    outs = [_dot(diff[:, gi * gch:(gi + 1) * gch].astype(bf16), pw_ref[gi]) for gi in range(len(POOL_WINDOWS))]
    return diff, jnp.concatenate(outs, axis=1)


def _pool_fwd(x, gain_pre, gain_post, pw, scale, li):
    bsz, l, d = x.shape
    tl = min(POOL_TL, l)
    nt = l // tl

    def body(x_ref, gpre_ref, gpost_ref, pw_ref, sc_ref, xo_ref, halo):
        t = pl.program_id(1)

        @pl.when(t == 0)
        def _():
            halo[...] = jnp.zeros_like(halo)

        xv = x_ref[...]
        u = _rms_fwd(xv, gpre_ref[...])
        _, out = _pool_mix(u, halo[...], _pool_counts(t * tl, tl, d), pw_ref, sc_ref[...], tl, d)
        halo[...] = u[tl - POOL_HALO:tl, :]
        xo_ref[...] = xv + _rms_fwd(out * sc_ref[...], gpost_ref[...])

    tok = pl.BlockSpec((None, tl, d), lambda b, t: (b, t, 0))
    gain = pl.BlockSpec((1, d), lambda b, t: (0, 0))
    return pl.pallas_call(
        body, name=f"pool_fwd_{li}", grid=(bsz, nt), out_shape=SDS((bsz, l, d), f32),
        in_specs=[tok, gain, gain, pl.BlockSpec(pw.shape, lambda b, t: (0, 0, 0)), gain], out_specs=tok,
        scratch_shapes=[pltpu.VMEM((POOL_HALO, d), f32)], compiler_params=_cparams(2),
    )(x, gain_pre, gain_post, pw, scale)


def _pool_bwd(x, dxo, gain_pre, gain_post, pw, scale, li):
    bsz, l, d = x.shape
    tl = min(POOL_TL, l)
    nt = l // tl
    ng = len(POOL_WINDOWS)
    gch = d // ng
    n_ext = tl + POOL_HALO

    def body(x_ref, xh_ref, dxo_ref, gpre_ref, gpost_ref, pw_ref, sc_ref, dx_ref, dpw_ref, ds_ref, qnext):
        b, t = pl.program_id(0), pl.program_id(1)
        tt = nt - 1 - t

        @pl.when((b == 0) & (t == 0))
        def _():
            dpw_ref[...] = jnp.zeros_like(dpw_ref)
            ds_ref[...] = jnp.zeros_like(ds_ref)

        @pl.when(t == 0)
        def _():
            qnext[...] = jnp.zeros_like(qnext)

        xv = x_ref[...]
        gpre = gpre_ref[...]
        u = _rms_fwd(xv, gpre)
        uh = _rms_fwd(jnp.where(tt > 0, xh_ref[...], 0.0), gpre)
        cnt = _pool_counts(tt * tl, tl, d)
        scale_v = sc_ref[...]
        diff, out = _pool_mix(u, uh, cnt, pw_ref, scale_v, tl, d)
        dxo_v = dxo_ref[...]
        dm, dgpost = _rms_bwd(out * scale_v, gpost_ref[...], dxo_v)
        ds_ref[1:2, :] += dgpost
        ds_ref[2:3, :] += jnp.sum(dm * out, axis=0, keepdims=True)
        dout = (dm * scale_v).astype(bf16)
        ddiffs = []
        for gi in range(ng):
            sl = slice(gi * gch, (gi + 1) * gch)
            dpw_ref[gi] += _dot_tn(diff[:, sl].astype(bf16), dout[:, sl])
            ddiffs.append(_dot_nt(dout[:, sl], pw_ref[gi]))
        ddiff = jnp.concatenate(ddiffs, axis=1)
        q = ddiff / cnt
        ext = jnp.concatenate([q, qnext[...]], axis=0)
        sums, cur = [], ext
        for k in (1, 2, 4, 8):
            cur = cur + pltpu.roll(cur, n_ext - k, axis=0)
            sums.append(cur[0:tl, :])
        du = _pool_select(sums, tl, d) - ddiff
        qnext[...] = q[0:POOL_HALO, :]
        dxp, dgpre = _rms_bwd(xv, gpre, du)
        ds_ref[0:1, :] += dgpre
        dx_ref[...] = dxo_v + dxp

    tok = pl.BlockSpec((None, tl, d), lambda b, t: (b, nt - 1 - t, 0))
    halo = pl.BlockSpec((None, POOL_HALO, d),
                        lambda b, t: (b, jnp.maximum((nt - 1 - t) * (tl // POOL_HALO) - 1, 0), 0))
    gain = pl.BlockSpec((1, d), lambda b, t: (0, 0))
    return pl.pallas_call(
        body, name=f"pool_bwd_{li}", grid=(bsz, nt),
        out_shape=(SDS((bsz, l, d), f32), SDS((ng, gch, gch), f32), SDS((8, d), f32)),
        in_specs=[tok, halo, tok, gain, gain, pl.BlockSpec(pw.shape, lambda b, t: (0, 0, 0)), gain],
        out_specs=(tok, pl.BlockSpec((ng, gch, gch), lambda b, t: (0, 0, 0)), pl.BlockSpec((8, d), lambda b, t: (0, 0))),
        scratch_shapes=[pltpu.VMEM((POOL_HALO, d), f32)], compiler_params=_cparams(2),
    )(x, x, dxo, gain_pre, gain_post, pw, scale)


def _s5_discretise(lam_re, lam_im, log_dt, b_re, b_im):
    def body(lr_ref, li_ref, ld_ref, br_ref, bi_ref, ar_ref, ai_ref, bbr_ref, bbi_ref):
        lr, li = lr_ref[...], li_ref[...]
        dt = jnp.exp(ld_ref[...])
        mag = jnp.exp(lr * dt)
        ar = mag * jnp.cos(li * dt)
        ai = mag * jnp.sin(li * dt)
        den = lr * lr + li * li
        nr, ni = ar - 1.0, ai
        fr = (nr * lr + ni * li) / den
        fi = (ni * lr - nr * li) / den
        br, bi = br_ref[...], bi_ref[...]
        ar_ref[...] = ar
        ai_ref[...] = ai
        bbr_ref[...] = fr * br - fi * bi
        bbi_ref[...] = fr * bi + fi * br

    return pl.pallas_call(
        body, name="s5_discretise",
        out_shape=(SDS(lam_re.shape, f32), SDS(lam_re.shape, f32), SDS(b_re.shape, f32), SDS(b_re.shape, f32)),
        compiler_params=_cparams(),
    )(lam_re, lam_im, log_dt, b_re, b_im)


def _s5_discretise_bwd(lam_re, lam_im, log_dt, b_re, b_im, g_ar, g_ai, g_bbr, g_bbi):
    def body(lr_ref, li_ref, ld_ref, br_ref, bi_ref, gar_ref, gai_ref, gbbr_ref, gbbi_ref,
             dlr_ref, dli_ref, dld_ref, dbr_ref, dbi_ref):
        lr, li = lr_ref[...], li_ref[...]
        dt = jnp.exp(ld_ref[...])
        mag = jnp.exp(lr * dt)
        cs, sn = jnp.cos(li * dt), jnp.sin(li * dt)
        ar, ai = mag * cs, mag * sn
        den = lr * lr + li * li
        nr, ni = ar - 1.0, ai
        fr = (nr * lr + ni * li) / den
        fi = (ni * lr - nr * li) / den
        br, bi = br_ref[...], bi_ref[...]
        gbbr, gbbi = gbbr_ref[...], gbbi_ref[...]
        dbr_ref[...] = fr * gbbr + fi * gbbi
        dbi_ref[...] = fr * gbbi - fi * gbbr
        gfr = jnp.sum(br * gbbr + bi * gbbi, axis=2, keepdims=True)
        gfi = jnp.sum(br * gbbi - bi * gbbr, axis=2, keepdims=True)
        gnr_num, gni_num = gfr / den, gfi / den
        gden = -(gfr * fr + gfi * fi) / den
        g_nr = gnr_num * lr - gni_num * li
        g_ni = gnr_num * li + gni_num * lr
        dlr = gnr_num * nr + gni_num * ni + gden * 2.0 * lr
        dli = gnr_num * ni - gni_num * nr + gden * 2.0 * li
        gar = gar_ref[...] + g_nr
        gai = gai_ref[...] + g_ni
        gq = (gar * cs + gai * sn) * mag
        gth = (gai * cs - gar * sn) * mag
        dlr_ref[...] = dlr + gq * dt
        dli_ref[...] = dli + gth * dt
        dld_ref[...] = jnp.sum(gq * lr + gth * li, axis=3, keepdims=True) * dt

    return pl.pallas_call(
        body, name="s5_discretise_bwd",
        out_shape=(SDS(lam_re.shape, f32), SDS(lam_re.shape, f32), SDS(log_dt.shape, f32),
                   SDS(b_re.shape, f32), SDS(b_re.shape, f32)),
        compiler_params=_cparams(),
    )(lam_re, lam_im, log_dt, b_re, b_im, g_ar, g_ai, g_bbr, g_bbi)


def _fill_powers(pw_r, pw_i, ar, ai, nj, width):
    cb = min(S5_CB, width)
    for c0 in range(0, width, cb):
        sl = pl.ds(c0, cb)
        a_r, a_i = ar[:, c0:c0 + cb], ai[:, c0:c0 + cb]

        def step(i, p):
            pw_r[i, :, sl] = p[0]
            pw_i[i, :, sl] = p[1]
            return _cmul(p[0], p[1], a_r, a_i)

        lax.fori_loop(0, nj, step, (a_r, a_i))


def _interleaved_scan(xr_sc, xi_sc, ar, ai, pw_r, pw_i, carry_r, carry_i, nj, width, reverse, h_sc=None):
    cb = min(S5_CB, width)
    acc_out = []
    for c0 in range(0, width, cb):
        sl = pl.ds(c0, cb)
        a_r, a_i = ar[:, c0:c0 + cb], ai[:, c0:c0 + cb]
        aj_r, aj_i = pw_r[nj - 1, :, sl], pw_i[nj - 1, :, sl]

        def pos(i):
            return nj - 1 - i if reverse else i

        def local_step(i, st):
            j = pos(i)
            hr, hi = _cmul(a_r, a_i, st[0], st[1])
            hr, hi = hr + xr_sc[j, :, sl], hi + xi_sc[j, :, sl]
            xr_sc[j, :, sl] = hr
            xi_sc[j, :, sl] = hi
            return hr, hi

        zero = jnp.zeros((8, cb), f32)
        fin_r, fin_i = lax.fori_loop(0, nj, local_step, (zero, zero), unroll=S5_UNROLL)
        row = lax.broadcasted_iota(jnp.int32, (8, cb), 0)
        c_r, c_i = carry_r[0:1, sl], carry_i[0:1, sl]
        ent_r, ent_i = zero, zero
        order = range(7, -1, -1) if reverse else range(8)
        for s in order:
            ent_r = jnp.where(row == s, c_r, ent_r)
            ent_i = jnp.where(row == s, c_i, ent_i)
            pr, pi_ = _cmul(aj_r[0:1, :], aj_i[0:1, :], c_r, c_i)
            c_r, c_i = fin_r[s:s + 1, :] + pr, fin_i[s:s + 1, :] + pi_
        carry_r[:, sl] = jnp.broadcast_to(c_r, (8, cb))
        carry_i[:, sl] = jnp.broadcast_to(c_i, (8, cb))

        def fix_step(i, st):
            j = pos(i)
            nx_r, nx_i, acc_r, acc_i = st
            cr_, ci_ = _cmul(pw_r[i, :, sl], pw_i[i, :, sl], ent_r, ent_i)
            hr, hi = xr_sc[j, :, sl] + cr_, xi_sc[j, :, sl] + ci_
            xr_sc[j, :, sl] = hr
            xi_sc[j, :, sl] = hi
            if h_sc is not None:
                sr, si = h_sc[0][j, :, sl], h_sc[1][j, :, sl]
                acc_r = acc_r + nx_r * sr + nx_i * si
                acc_i = acc_i + nx_i * sr - nx_r * si
                nx_r, nx_i = hr, hi
            return nx_r, nx_i, acc_r, acc_i

        st = lax.fori_loop(0, nj, fix_step, (ent_r, ent_i, zero, zero), unroll=S5_UNROLL)
        acc_out.append((st[2], st[3]))
    return acc_out


def _diag_mask(n_rep, h, ks):
    assert h & (h - 1) == 0 and (ks // n_rep) & (ks // n_rep - 1) == 0
    r = lax.shift_right_logical(lax.broadcasted_iota(jnp.int32, (n_rep * h, ks), 0), h.bit_length() - 1)
    c = lax.shift_right_logical(lax.broadcasted_iota(jnp.int32, (n_rep * h, ks), 1), (ks // n_rep).bit_length() - 1)
    return r == c


def _expand_block_diag(compact, n_rep):
    h, ks = compact.shape
    full = jnp.concatenate([compact] * n_rep, axis=0)
    return jnp.where(_diag_mask(n_rep, h, ks), full, 0.0).astype(bf16)


def _compact_block_diag(full, n_rep):
    rows, ks = full.shape
    h = rows // n_rep
    return jnp.sum(jnp.where(_diag_mask(n_rep, h, ks), full, 0.0).reshape(n_rep, h, ks), axis=0)


def _s5_scan_fwd(x_il, gain_pre, d_skip, ab_r, ab_i, bbc_r, bbc_i, cc_r, cc_in, li, host=None):
    bsz, nblk, tq, d = x_il.shape
    nkb, gch, ks = bbc_r.shape
    kc = d // nkb
    n_rep = kc // gch
    nj = tq // 8
    h_in, h_shape, h_out, h_scr, h_ops = _host_args(host)

    def body(xf_ref, xk_ref, gk_ref, dk_ref, ar_ref, ai_ref, bbr_ref, bbi_ref, cr_ref, ci_ref, *rest):
        o0 = len(h_in)
        y_ref, hr_ref, hi_ref = rest[o0:o0 + 3]
        s0 = o0 + 3 + len(h_out)
        xr_sc, xi_sc, pw_r, pw_i, carry_r, carry_i, bbr_sc, bbi_sc, crt_sc, cit_sc = rest[s0:s0 + 10]
        k, b, n = pl.program_id(0), pl.program_id(1), pl.program_id(2)
        finish = _hosted(host, rest[:o0], rest[o0 + 3:s0], rest[s0 + 10:],
                         (k == 0) & (b == 0) & (n == 0), (k == nkb - 1) & (b == bsz - 1) & (n == nblk - 1))
        ar = jnp.broadcast_to(ar_ref[...], (8, ks))
        ai = jnp.broadcast_to(ai_ref[...], (8, ks))

        @pl.when((b == 0) & (n == 0))
        def _():
            bbr_sc[...] = _expand_block_diag(bbr_ref[...], n_rep)
            bbi_sc[...] = _expand_block_diag(bbi_ref[...], n_rep)
            crt_sc[...] = _expand_block_diag(cr_ref[...], n_rep)
            cit_sc[...] = _expand_block_diag(ci_ref[...], n_rep)
            _fill_powers(pw_r, pw_i, ar, ai, nj, ks)

        @pl.when(n == 0)
        def _():
            carry_r[...] = jnp.zeros_like(carry_r)
            carry_i[...] = jnp.zeros_like(carry_i)

        u = xk_ref[...] * _rms_scale(xf_ref[...]) * gk_ref[...]
        ub = u.astype(bf16)
        xr_sc[...] = _dot(ub, bbr_sc[...]).reshape(nj, 8, ks)
        xi_sc[...] = _dot(ub, bbi_sc[...]).reshape(nj, 8, ks)
        _interleaved_scan(xr_sc, xi_sc, ar, ai, pw_r, pw_i, carry_r, carry_i, nj, ks, reverse=False)
        hrb = xr_sc[...].reshape(tq, ks).astype(bf16)
        hib = xi_sc[...].reshape(tq, ks).astype(bf16)
        hr_ref[...] = hrb
        hi_ref[...] = hib
        crt, cit, dk = crt_sc[...], cit_sc[...], dk_ref[...]
        for r0 in range(0, tq, tq // 2):
            rows = slice(r0, r0 + tq // 2)
            y_ref[rows, :] = _dot_nt(hrb[rows, :], crt) + _dot_nt(hib[rows, :], cit) + dk * u[rows, :]
        finish()

    full = pl.BlockSpec((None, None, tq, d), lambda k, b, n: (b, n, 0, 0))
    chan = pl.BlockSpec((None, None, tq, kc), lambda k, b, n: (b, n, 0, k))
    stat = pl.BlockSpec((None, None, tq, ks), lambda k, b, n: (b, n, 0, k))
    vec_c = pl.BlockSpec((1, kc), lambda k, b, n: (0, k))
    vec_s = pl.BlockSpec((1, ks), lambda k, b, n: (0, k))
    cmap = pl.BlockSpec((None, gch, ks), lambda k, b, n: (k, 0, 0))
    s_tot = nkb * ks
    return pl.pallas_call(
        body, name=f"s5_scan_fwd_{li}", grid=(nkb, bsz, nblk),
        out_shape=(SDS((bsz, nblk, tq, d), f32), SDS((bsz, nblk, tq, s_tot), bf16), SDS((bsz, nblk, tq, s_tot), bf16),
                   *h_shape),
        in_specs=[full, chan, vec_c, vec_c, vec_s, vec_s, cmap, cmap, cmap, cmap, *h_in],
        out_specs=(chan, stat, stat, *h_out),
        scratch_shapes=[pltpu.VMEM((nj, 8, ks), f32)] * 4 + [pltpu.VMEM((8, ks), f32)] * 2
        + [pltpu.VMEM((kc, ks), bf16)] * 4 + h_scr,
        compiler_params=_cparams(3, side_effects=bool(host)),
    )(x_il, x_il, gain_pre, d_skip, ab_r, ab_i, bbc_r, bbc_i, cc_r, cc_in, *h_ops)


def _s5_glu_fwd(y, x_il, w_glu, b_glu, gain_post, li):
    t, d = y.shape
    tr = min(TOK_TR, t)

    def body(y_ref, x_ref, w_ref, b_ref, gp_ref, xo_ref):
        z = _gelu(y_ref[...])
        a = _dot(z.astype(bf16), w_ref[...]) + b_ref[...]
        xo_ref[...] = x_ref[...] + _rms_fwd(z * _sigmoid(a), gp_ref[...])

    tok = pl.BlockSpec((tr, d), lambda i: (i, 0))
    vec = pl.BlockSpec((1, d), lambda i: (0, 0))
    return pl.pallas_call(
        body, name=f"s5_glu_fwd_{li}", grid=(t // tr,), out_shape=SDS((t, d), f32),
        in_specs=[tok, tok, pl.BlockSpec((d, d), lambda i: (0, 0)), vec, vec], out_specs=tok,
        compiler_params=_cparams(1),
    )(y, x_il, w_glu, b_glu, gain_post)


def _s5_glu_bwd(y, dxo, w_glu, b_glu, gain_post, li):
    t, d = y.shape
    tr = min(TOK_TR, t)

    def body(y_ref, dxo_ref, w_ref, b_ref, gp_ref, dy_ref, dw_ref, ds_ref):
        @pl.when(pl.program_id(0) == 0)
        def _():
            dw_ref[...] = jnp.zeros_like(dw_ref)
            ds_ref[...] = jnp.zeros_like(ds_ref)

        z, zg = _gelu_and_grad(y_ref[...])
        zb = z.astype(bf16)
        w = w_ref[...]
        s = _sigmoid(_dot(zb, w) + b_ref[...])
        dm, dgpost = _rms_bwd(z * s, gp_ref[...], dxo_ref[...])
        da = dm * z * s * (1.0 - s)
        dab = da.astype(bf16)
        dz = dm * s + _dot_nt(dab, w)
        dw_ref[...] += _dot_tn(zb, dab)
        ds_ref[0:1, :] += dgpost
        ds_ref[1:2, :] += jnp.sum(da, axis=0, keepdims=True)
        dy_ref[...] = dz * zg

    tok = pl.BlockSpec((tr, d), lambda i: (i, 0))
    vec = pl.BlockSpec((1, d), lambda i: (0, 0))
    mat = pl.BlockSpec((d, d), lambda i: (0, 0))
    return pl.pallas_call(
        body, name=f"s5_glu_bwd_{li}", grid=(t // tr,),
        out_shape=(SDS((t, d), f32), SDS((d, d), f32), SDS((8, d), f32)),
        in_specs=[tok, tok, mat, vec, vec], out_specs=(tok, mat, pl.BlockSpec((8, d), lambda i: (0, 0))),
        compiler_params=_cparams(1),
    )(y, dxo, w_glu, b_glu, gain_post)


def _s5_scan_bwd(x_il, dy, h_r, h_i, gain_pre, d_skip, ab_r, ab_i, bbc_r, bbc_i, cc_r, cc_in, li, host=None):
    bsz, nblk, tq, d = x_il.shape
    nkb, gch, ks = bbc_r.shape
    kc = d // nkb
    n_rep = kc // gch
    nj = tq // 8
    h_in, h_shape, h_out, h_scr, h_ops = _host_args(host)

    def body(xf_ref, xk_ref, dy_ref, hr_ref, hi_ref, gk_ref, dk_ref, ar_ref, ai_ref, bbr_ref, bbi_ref, cr_ref, ci_ref,
             *rest):
        o0 = len(h_in)
        du_ref, dbbr_ref, dbbi_ref, dcr_ref, dci_ref, dar_ref, dai_ref, dd_ref = rest[o0:o0 + 8]
        s0 = o0 + 8 + len(h_out)
        (gr_sc, gi_sc, hr_sc, hi_sc, pw_r, pw_i, carry_r, carry_i, acc_r, acc_i,
         bbr_sc, bbi_sc, crt_sc, cit_sc, dbbr_acc, dbbi_acc, dcr_acc, dci_acc) = rest[s0:s0 + 18]
        k, b, n = pl.program_id(0), pl.program_id(1), pl.program_id(2)
        finish = _hosted(host, rest[:o0], rest[o0 + 8:s0], rest[s0 + 18:],
                         (k == 0) & (b == 0) & (n == 0), (k == nkb - 1) & (b == bsz - 1) & (n == nblk - 1))
        ar = jnp.broadcast_to(ar_ref[...], (8, ks))
        ai = jnp.broadcast_to(-ai_ref[...], (8, ks))

        @pl.when((b == 0) & (n == 0))
        def _():
            bbr_sc[...] = _expand_block_diag(bbr_ref[...], n_rep)
            bbi_sc[...] = _expand_block_diag(bbi_ref[...], n_rep)
            crt_sc[...] = _expand_block_diag(cr_ref[...], n_rep)
            cit_sc[...] = _expand_block_diag(ci_ref[...], n_rep)
            _fill_powers(pw_r, pw_i, ar, ai, nj, ks)
            for ref in (dar_ref, dai_ref, dd_ref, acc_r, acc_i, dbbr_acc, dbbi_acc, dcr_acc, dci_acc):
                ref[...] = jnp.zeros_like(ref)

        @pl.when(n == 0)
        def _():
            carry_r[...] = jnp.zeros_like(carry_r)
            carry_i[...] = jnp.zeros_like(carry_i)

        u = xk_ref[...] * _rms_scale(xf_ref[...]) * gk_ref[...]
        ub = u.astype(bf16)
        dyv = dy_ref[...]
        dyb = dyv.astype(bf16)
        dd_ref[0:1, :] += jnp.sum(dyv * u, axis=0, keepdims=True)
        hrb, hib = hr_ref[...], hi_ref[...]
        dcr_acc[...] += _dot_tn(dyb, hrb)
        dci_acc[...] += _dot_tn(dyb, hib)
        hr_sc[...] = hrb.astype(f32).reshape(nj, 8, ks)
        hi_sc[...] = hib.astype(f32).reshape(nj, 8, ks)
        gr_sc[...] = _dot(dyb, crt_sc[...]).reshape(nj, 8, ks)
        gi_sc[...] = _dot(dyb, cit_sc[...]).reshape(nj, 8, ks)
        accs = _interleaved_scan(gr_sc, gi_sc, ar, ai, pw_r, pw_i, carry_r, carry_i, nj, ks, reverse=True,
                                 h_sc=(hr_sc, hi_sc))
        cb = min(S5_CB, ks)
        for q, (a_r, a_i) in enumerate(accs):
            acc_r[:, q * cb:(q + 1) * cb] += a_r
            acc_i[:, q * cb:(q + 1) * cb] += a_i
        grb = gr_sc[...].reshape(tq, ks).astype(bf16)
        gib = gi_sc[...].reshape(tq, ks).astype(bf16)
        dbbr_acc[...] += _dot_tn(ub, grb)
        dbbi_acc[...] += _dot_tn(ub, gib)
        bbr, bbi, dk = bbr_sc[...], bbi_sc[...], dk_ref[...]
        for r0 in range(0, tq, tq // 2):
            rows = slice(r0, r0 + tq // 2)
            du_ref[rows, :] = dyv[rows, :] * dk + _dot_nt(grb[rows, :], bbr) + _dot_nt(gib[rows, :], bbi)

        @pl.when((b == bsz - 1) & (n == nblk - 1))
        def _():
            dar_ref[0:1, :] = jnp.sum(acc_r[...], axis=0, keepdims=True)
            dai_ref[0:1, :] = jnp.sum(acc_i[...], axis=0, keepdims=True)
            dbbr_ref[...] = _compact_block_diag(dbbr_acc[...], n_rep)
            dbbi_ref[...] = _compact_block_diag(dbbi_acc[...], n_rep)
            dcr_ref[...] = _compact_block_diag(dcr_acc[...], n_rep)
            dci_ref[...] = _compact_block_diag(dci_acc[...], n_rep)

        finish()

    full = pl.BlockSpec((None, None, tq, d), lambda k, b, n: (b, nblk - 1 - n, 0, 0))
    chan = pl.BlockSpec((None, None, tq, kc), lambda k, b, n: (b, nblk - 1 - n, 0, k))
    stat = pl.BlockSpec((None, None, tq, ks), lambda k, b, n: (b, nblk - 1 - n, 0, k))
    vec_c = pl.BlockSpec((1, kc), lambda k, b, n: (0, k))
    vec_s = pl.BlockSpec((1, ks), lambda k, b, n: (0, k))
    cmap = pl.BlockSpec((None, gch, ks), lambda k, b, n: (k, 0, 0))
    acc_s = pl.BlockSpec((None, 8, ks), lambda k, b, n: (k, 0, 0))
    acc_c = pl.BlockSpec((None, 8, kc), lambda k, b, n: (k, 0, 0))
    cshape = SDS((nkb, gch, ks), f32)
    return pl.pallas_call(
        body, name=f"s5_scan_bwd_{li}", grid=(nkb, bsz, nblk),
        out_shape=(SDS((bsz, nblk, tq, d), f32), cshape, cshape, cshape, cshape, SDS((nkb, 8, ks), f32),
                   SDS((nkb, 8, ks), f32), SDS((nkb, 8, kc), f32), *h_shape),
        in_specs=[full, chan, chan, stat, stat, vec_c, vec_c, vec_s, vec_s, cmap, cmap, cmap, cmap, *h_in],
        out_specs=(chan, cmap, cmap, cmap, cmap, acc_s, acc_s, acc_c, *h_out),
        scratch_shapes=([pltpu.VMEM((nj, 8, ks), f32)] * 6 + [pltpu.VMEM((8, ks), f32)] * 4
                        + [pltpu.VMEM((kc, ks), bf16)] * 4 + [pltpu.VMEM((kc, ks), f32)] * 4 + h_scr),
        compiler_params=_cparams(3, side_effects=bool(host)),
    )(x_il, x_il, dy, h_r, h_i, gain_pre, d_skip, ab_r, ab_i, bbc_r, bbc_i, cc_r, cc_in, *h_ops)


def _norm_residual_bwd(x, du, dxo, gain, name):
    t, d = x.shape
    tr = min(TOK_TR, t)

    def body(x_ref, du_ref, dxo_ref, g_ref, dx_ref, dg_ref):
        @pl.when(pl.program_id(0) == 0)
        def _():
            dg_ref[...] = jnp.zeros_like(dg_ref)

        dxp, dgain = _rms_bwd(x_ref[...], g_ref[...], du_ref[...])
        dx_ref[...] = dxo_ref[...] + dxp
        dg_ref[0:1, :] += dgain

    tok = pl.BlockSpec((tr, d), lambda i: (i, 0))
    return pl.pallas_call(
        body, name=name, grid=(t // tr,), out_shape=(SDS((t, d), f32), SDS((8, d), f32)),
        in_specs=[tok, tok, tok, pl.BlockSpec((1, d), lambda i: (0, 0))],
        out_specs=(tok, pl.BlockSpec((8, d), lambda i: (0, 0))), compiler_params=_cparams(1),
    )(x, du, dxo, gain)


def _loss_head(y, target):
    t, d = y.shape
    tr = min(TOK_TR, t)

    def body(y_ref, t_ref, l_ref, dy_ref):
        @pl.when(pl.program_id(0) == 0)
        def _():
            l_ref[...] = jnp.zeros_like(l_ref)

        err = y_ref[...] - t_ref[...]
        dy_ref[...] = err * (1.0 / d)
        l_ref[...] += jnp.sum(jnp.sum(err * err, axis=1, keepdims=True), axis=0, keepdims=True)

    tok = pl.BlockSpec((tr, d), lambda i: (i, 0))
    return pl.pallas_call(
        body, name="loss_head", grid=(t // tr,), out_shape=(SDS((8, 128), f32), SDS((t, d), f32)),
        in_specs=[tok, tok], out_specs=(pl.BlockSpec((8, 128), lambda i: (0, 0)), tok), compiler_params=_cparams(1),
    )(y, target)


def _interleave(a, tq):
    bsz, l, d = a.shape
    return a.reshape(bsz, l // tq, 8, tq // 8, d).transpose(0, 1, 3, 2, 4).reshape(bsz, l // tq, tq, d)


def _deinterleave(a, l):
    bsz, nblk, tq, d = a.shape
    return a.reshape(bsz, nblk, tq // 8, 8, d).transpose(0, 1, 3, 2, 4).reshape(bsz, l, d)


def _compact_maps(a, nkb):
    ns, g, h, p = a.shape
    gl = g // nkb
    return a.reshape(ns, nkb, gl, h, p).transpose(0, 1, 3, 2, 4).reshape(ns, nkb, h, gl * p)


def _uncompact_maps(a, g):
    nkb, h, cols = a.shape
    gl = g // nkb
    return a.reshape(nkb, h, gl, cols // gl).transpose(0, 2, 1, 3).reshape(g, h, cols // gl)


def _pack_rows(arrs, cols, row_mult):
    flat = jnp.concatenate([a.reshape(-1).astype(f32) for a in arrs])
    rows = -(-flat.shape[0] // cols)
    rows = -(-rows // row_mult) * row_mult
    return jnp.pad(flat, (0, rows * cols - flat.shape[0])).reshape(rows, cols)


def _unpack_rows(buf, shapes):
    flat = buf.reshape(-1)
    out, off = [], 0
    for s in shapes:
        n = math.prod(s)
        out.append(flat[off:off + n].reshape(s))
        off += n
    return out


W_NAMES = ['s5_lambda_re', 's5_lambda_im', 's5_log_dt', 's5_b_re', 's5_b_im', 's5_c_re', 's5_c_im', 's5_d', 's5_w_glu',
           's5_b_glu', 'pool_w', 'pool_scale', 'ffn_w_gate', 'ffn_w_val', 'ffn_conv_w', 'ffn_conv_b', 'ffn_w_down',
           'norm_mix_pre', 'norm_mix_post', 'norm_ffn_pre', 'norm_ffn_post']
BIG = ('s5_w_glu', 'ffn_w_gate', 'ffn_w_val', 'ffn_w_down')


def kernel(x, s5_lambda_re, s5_lambda_im, s5_log_dt, s5_b_re, s5_b_im, s5_c_re, s5_c_im, s5_d, s5_w_glu, s5_b_glu, pool_w, pool_scale, ffn_w_gate, ffn_w_val, ffn_conv_w, ffn_conv_b, ffn_w_down, norm_mix_pre, norm_mix_post, norm_ffn_pre, norm_ffn_post, loss_target, m_s5_lambda_re, m_s5_lambda_im, m_s5_log_dt, m_s5_b_re, m_s5_b_im, m_s5_c_re, m_s5_c_im, m_s5_d, m_s5_w_glu, m_s5_b_glu, m_pool_w, m_pool_scale, m_ffn_w_gate, m_ffn_w_val, m_ffn_conv_w, m_ffn_conv_b, m_ffn_w_down, m_norm_mix_pre, m_norm_mix_post, m_norm_ffn_pre, m_norm_ffn_post, v_s5_lambda_re, v_s5_lambda_im, v_s5_log_dt, v_s5_b_re, v_s5_b_im, v_s5_c_re, v_s5_c_im, v_s5_d, v_s5_w_glu, v_s5_b_glu, v_pool_w, v_pool_scale, v_ffn_w_gate, v_ffn_w_val, v_ffn_conv_w, v_ffn_conv_b, v_ffn_w_down, v_norm_mix_pre, v_norm_mix_post, v_norm_ffn_pre, v_norm_ffn_post):
    w_in = dict(zip(W_NAMES, (s5_lambda_re, s5_lambda_im, s5_log_dt, s5_b_re, s5_b_im, s5_c_re, s5_c_im, s5_d, s5_w_glu,
                              s5_b_glu, pool_w, pool_scale, ffn_w_gate, ffn_w_val, ffn_conv_w, ffn_conv_b, ffn_w_down,
                              norm_mix_pre, norm_mix_post, norm_ffn_pre, norm_ffn_post)))
    m_in = dict(zip(W_NAMES, (m_s5_lambda_re, m_s5_lambda_im, m_s5_log_dt, m_s5_b_re, m_s5_b_im, m_s5_c_re, m_s5_c_im,
                              m_s5_d, m_s5_w_glu, m_s5_b_glu, m_pool_w, m_pool_scale, m_ffn_w_gate, m_ffn_w_val,
                              m_ffn_conv_w, m_ffn_conv_b, m_ffn_w_down, m_norm_mix_pre, m_norm_mix_post,
                              m_norm_ffn_pre, m_norm_ffn_post)))
    v_in = dict(zip(W_NAMES, (v_s5_lambda_re, v_s5_lambda_im, v_s5_log_dt, v_s5_b_re, v_s5_b_im, v_s5_c_re, v_s5_c_im,
                              v_s5_d, v_s5_w_glu, v_s5_b_glu, v_pool_w, v_pool_scale, v_ffn_w_gate, v_ffn_w_val,
                              v_ffn_conv_w, v_ffn_conv_b, v_ffn_w_down, v_norm_mix_pre, v_norm_mix_post,
                              v_norm_ffn_pre, v_norm_ffn_post)))

    bsz, seq, d = x.shape
    depth = ffn_w_gate.shape[0]
    n_s5, n_grp, n_state = s5_lambda_re.shape
    n_gch = s5_b_re.shape[3]
    n_pool = pool_w.shape[0]
    fs = ffn_w_gate.shape[2]
    f = N_CHIPS * fs
    gs = s5_w_glu.shape[1]
    nkb = d // S5_KB_CH
    tq = min(S5_TQ, seq)
    chip = 2 * lax.axis_index("x") + lax.axis_index("y")

    s_ffn = jnp.stack([ffn_w_gate.transpose(0, 2, 1), ffn_w_val.transpose(0, 2, 1), ffn_w_down], axis=1).astype(bf16)
    small_shard = [pool_w, pool_scale, ffn_conv_w]
    first_hosts = [("gather", 1, s_ffn[0]), ("gather", 1, s5_w_glu.astype(bf16)),
                   ("whole", 0, _pack_rows(small_shard, d, 16))]

    lam_r4 = s5_lambda_re.reshape(n_s5, n_grp, 1, n_state)
    lam_i4 = s5_lambda_im.reshape(n_s5, n_grp, 1, n_state)
    ldt4 = s5_log_dt.reshape(n_s5, n_grp, 1, 1)
    b_r4 = s5_b_re.transpose(0, 1, 3, 2)
    b_i4 = s5_b_im.transpose(0, 1, 3, 2)
    ab_r4, ab_i4, bb_r4, bb_i4 = _s5_discretise(lam_r4, lam_i4, ldt4, b_r4, b_i4)
    n_st_tot = n_grp * n_state
    ab_r, ab_i = ab_r4.reshape(n_s5, 1, n_st_tot), ab_i4.reshape(n_s5, 1, n_st_tot)
    bbc_r, bbc_i = _compact_maps(bb_r4, nkb), _compact_maps(bb_i4, nkb)
    cc_r, cc_in = _compact_maps(s5_c_re, nkb), _compact_maps(-s5_c_im, nkb)

    def s5_params(j):
        return dict(ab_r=ab_r[j], ab_i=ab_i[j], bb_r=bbc_r[j], bb_i=bbc_i[j], c_r=cc_r[j], c_in=cc_in[j],
                    d_skip=s5_d[j].reshape(1, d), b_glu=s5_b_glu[j].reshape(1, d))

    def row(a, i):
        return a[i].reshape(1, -1)

    assert depth >= 1 and N_CHIPS == 4
    saved = []
    xc = x
    w_layer = [None] * depth
    for i in range(depth):
        j = i // 2
        sv = dict(x_in=xc)
        if i % 2 == 0:
            sp = s5_params(j)
            x_il = _interleave(xc, tq)
            res = _s5_scan_fwd(x_il, row(norm_mix_pre, i), sp["d_skip"], sp["ab_r"], sp["ab_i"], sp["bb_r"], sp["bb_i"],
                               sp["c_r"], sp["c_in"], i, host=first_hosts if i == 0 else None)
            y_il, h_r, h_i = res[:3]
            if i == 0:
                w_layer[0], g_glu, g_small = res[3:6]
                parts = [_unpack_rows(g_small[k], [a.shape for a in small_shard]) for k in range(N_CHIPS)]
                pool_w_full = jnp.concatenate([p[0] for p in parts], axis=2).astype(bf16)
                pool_scale_full = jnp.concatenate([p[1] for p in parts], axis=1)
                conv_w_full = jnp.concatenate([p[2] for p in parts], axis=2)
            sp["w_glu"] = g_glu[j]
            xo_il = _s5_glu_fwd(y_il.reshape(bsz * seq, d), x_il.reshape(bsz * seq, d), sp["w_glu"], sp["b_glu"],
                                row(norm_mix_post, i), i)
            x_mid = _deinterleave(xo_il.reshape(x_il.shape), seq)
            sv.update(sp=sp, x_il=x_il, y_il=y_il, h_r=h_r, h_i=h_i)
        else:
            x_mid = _pool_fwd(xc, row(norm_mix_pre, i), row(norm_mix_post, i), pool_w_full[j], row(pool_scale_full, j), i)
        res = _ffn_fwd(x_mid, row(norm_ffn_pre, i), row(norm_ffn_post, i), w_layer[i], conv_w_full, ffn_conv_b, i,
                       host=[("gather", 1, s_ffn[i + 1])] if i + 1 < depth else None)
        xc, g_sv, v_sv, f_sv = res[:4]
        if i + 1 < depth:
            w_layer[i + 1] = res[4]
        sv.update(x_mid=x_mid, g=g_sv, v=v_sv, f=f_sv)
        saved.append(sv)

    sq, dy = _loss_head(xc.reshape(bsz * seq, d), loss_target.reshape(bsz * seq, d))
    loss = lax.psum(sq[0, 0] * (0.5 / d), ("x", "y", "c"))

    dx = dy.reshape(bsz, seq, d)
    dw_layers = [None] * depth
    r_layers = [None] * depth
    g_small_params = {n: [None] * w_in[n].shape[0] for n in W_NAMES if n not in BIG}
    dglu = [None] * n_s5
    for i in range(depth - 1, -1, -1):
        j = i // 2
        sv = saved[i]
        res = _ffn_bwd_act(sv["x_mid"], sv["f"], sv["g"], sv["v"], dx, row(norm_ffn_pre, i), row(norm_ffn_post, i),
                           w_layer[i], conv_w_full, ffn_conv_b, i,
                           host=[("slab", 1, dw_layers[i + 1])] if i + 1 < depth else None)
        dx, dg, dv, hdn, hb, dfb, dgain, dconv = res[:8]
        if i + 1 < depth:
            r_layers[i + 1] = res[8]
        dw_layers[i] = _ffn_bwd_weights(dg, dv, hdn, hb.reshape(bsz * seq, d), dfb.reshape(bsz * seq, d), i)
        g_small_params["norm_ffn_pre"][i] = dgain[0]
        g_small_params["norm_ffn_post"][i] = dgain[1]
        dconv = dconv.transpose(1, 0, 2).reshape(8, f)
        g_small_params["ffn_conv_w"][i] = dconv[0:3]
        g_small_params["ffn_conv_b"][i] = dconv[3]
        if i % 2 == 0:
            sp = sv["sp"]
            dxo_il = _interleave(dx, tq)
            dy_s, dglu[j], ds_glu = _s5_glu_bwd(sv["y_il"].reshape(bsz * seq, d), dxo_il.reshape(bsz * seq, d),
                                                sp["w_glu"], sp["b_glu"], row(norm_mix_post, i), i)
            hosts = None
            if i == 0:
                assert all(g is not None for g in dglu) and all(g is not None for g in g_small_params["pool_w"])
                hosts = [("slab", 1, dw_layers[0]), ("slab", 1, jnp.stack(dglu).astype(bf16)),
                         ("slab", 2, jnp.stack(g_small_params["pool_w"]))]
            res = _s5_scan_bwd(sv["x_il"], dy_s.reshape(sv["x_il"].shape), sv["h_r"], sv["h_i"], row(norm_mix_pre, i),
                               sp["d_skip"], sp["ab_r"], sp["ab_i"], sp["bb_r"], sp["bb_i"], sp["c_r"], sp["c_in"], i,
                               host=hosts)
            du, dbb_r, dbb_i, dc_r, dc_in, dab_r, dab_i, dd = res[:8]
            if i == 0:
                r_layers[0], r_glu, r_pw = res[8:11]
            dx_il, dgpre = _norm_residual_bwd(sv["x_il"].reshape(bsz * seq, d), du.reshape(bsz * seq, d),
                                              dxo_il.reshape(bsz * seq, d), row(norm_mix_pre, i), f"s5_pre_bwd_{i}")
            dx = _deinterleave(dx_il.reshape(sv["x_il"].shape), seq)
            g_small_params["norm_mix_pre"][i] = dgpre[0]
            g_small_params["norm_mix_post"][i] = ds_glu[0]
            g_small_params["s5_b_glu"][j] = ds_glu[1]
            g_small_params["s5_d"][j] = dd[:, 0, :].reshape(d)
            g_small_params["s5_c_re"][j] = _uncompact_maps(dc_r, n_grp)
            g_small_params["s5_c_im"][j] = -_uncompact_maps(dc_in, n_grp)
            sv["g_ab"] = (dab_r[:, 0, :].reshape(n_grp, 1, n_state), dab_i[:, 0, :].reshape(n_grp, 1, n_state),
                          _uncompact_maps(dbb_r, n_grp), _uncompact_maps(dbb_i, n_grp))
        else:
            dx, dpw, dsm = _pool_bwd(sv["x_in"], dx, row(norm_mix_pre, i), row(norm_mix_post, i), pool_w_full[j],
                                     row(pool_scale_full, j), i)
            g_small_params["norm_mix_pre"][i] = dsm[0]
            g_small_params["norm_mix_post"][i] = dsm[1]
            g_small_params["pool_scale"][j] = dsm[2]
            g_small_params["pool_w"][j] = dpw
    grad_x = dx

    g_ab = [saved[2 * j]["g_ab"] for j in range(n_s5)]
    d_lr, d_li, d_ld, d_br, d_bi = _s5_discretise_bwd(
        lam_r4, lam_i4, ldt4, b_r4, b_i4, jnp.stack([g[0] for g in g_ab]), jnp.stack([g[1] for g in g_ab]),
        jnp.stack([g[2] for g in g_ab]), jnp.stack([g[3] for g in g_ab]))
    small_full = {n: (jnp.stack(v) if v[0] is not None else None) for n, v in g_small_params.items()}
    small_full["s5_lambda_re"] = d_lr.reshape(n_s5, n_grp, n_state)
    small_full["s5_lambda_im"] = d_li.reshape(n_s5, n_grp, n_state)
    small_full["s5_log_dt"] = d_ld.reshape(n_s5, n_grp)
    small_full["s5_b_re"] = d_br.transpose(0, 1, 3, 2)
    small_full["s5_b_im"] = d_bi.transpose(0, 1, 3, 2)
    bc_names = ["s5_b_re", "s5_b_im", "s5_c_re", "s5_c_im"]
    small_names = [n for n in W_NAMES if n not in BIG and n not in bc_names and n != "pool_w"]
    q = _pack_rows([small_full[n] for n in small_names], d, 16)
    q_bc = _pack_rows([small_full[n] for n in bc_names], d, 16).astype(bf16)

    p_ffn = _sum_slots_layers([r.reshape(N_CHIPS, 3 * fs, d) for r in r_layers], "sum_slots_ffn").reshape(depth * 3 * fs, d)
    p_glu = _sum_slots(r_glu.reshape(N_CHIPS, n_s5 * gs, d), "sum_slots_glu")
    pw_rows, pw_cols = math.prod(pool_w.shape[:3]), pool_w.shape[3]
    p_pw = _sum_slots(r_pw.reshape(N_CHIPS, pw_rows, pw_cols), "sum_slots_pool_w")
    r_q, r_bc, o_ffn, o_glu, o_pw = _chip_exchange([("whole", 0, q), ("whole", 0, q_bc)], "exchange_small_grads",
                                                   to_sibling=[p_ffn, p_glu, p_pw])
    p_q = _sum_slots(r_q, "sum_slots_small")
    p_bc = _sum_slots(r_bc, "sum_slots_bc")
    o_q, o_bc = _exchange_with_sibling([p_q, p_bc])

    outs = {}

    def put(name, res, shape):
        outs[name] = tuple(r.reshape(shape) for r in res)

    pf, of = p_ffn.reshape(depth, 3, fs, d), o_ffn.reshape(depth, 3, fs, d)
    for kind, name in ((0, "ffn_w_gate"), (1, "ffn_w_val")):
        parts_g = [a[:, kind].transpose(0, 2, 1).reshape(depth * d, fs) for a in (pf, of)]
        put(name, _adamw(parts_g, w_in[name].reshape(depth * d, fs), m_in[name].reshape(depth * d, fs),
                         v_in[name].reshape(depth * d, fs), f"adamw_{name}"), w_in[name].shape)
    put("ffn_w_down", _adamw([pf[:, 2].reshape(depth * fs, d), of[:, 2].reshape(depth * fs, d)],
                             ffn_w_down.reshape(depth * fs, d), m_ffn_w_down.reshape(depth * fs, d),
                             v_ffn_w_down.reshape(depth * fs, d), "adamw_ffn_w_down"), ffn_w_down.shape)
    put("s5_w_glu", _adamw([p_glu, o_glu], s5_w_glu.reshape(n_s5 * gs, d), m_s5_w_glu.reshape(n_s5 * gs, d),
                           v_s5_w_glu.reshape(n_s5 * gs, d), "adamw_s5_w_glu"), s5_w_glu.shape)

    q_tot, bc_tot, pw_tot = _add_pairs([(p_q, o_q), (p_bc, o_bc), (p_pw, o_pw)], "sum_small")
    g_small = dict(zip(small_names, _unpack_rows(q_tot, [small_full[n].shape for n in small_names])))
    g_small.update(zip(bc_names, _unpack_rows(bc_tot, [small_full[n].shape for n in bc_names])))
    g_small["pool_w"] = pw_tot.reshape(pool_w.shape)
    g_small["pool_scale"] = lax.dynamic_slice_in_dim(g_small["pool_scale"], chip * pool_scale.shape[1],
                                                     pool_scale.shape[1], axis=1)
    g_small["ffn_conv_w"] = lax.dynamic_slice_in_dim(g_small["ffn_conv_w"], chip * fs, fs, axis=2)
    all_small = [n for n in W_NAMES if n not in BIG]
    local_shapes = [w_in[n].shape for n in all_small]
    res = _adamw([_pack_rows([g_small[n] for n in all_small], d, 64)],
                 _pack_rows([w_in[n] for n in all_small], d, 64), _pack_rows([m_in[n] for n in all_small], d, 64),
                 _pack_rows([v_in[n] for n in all_small], d, 64), "adamw_small")
    unpacked = [_unpack_rows(r, local_shapes) for r in res]
    for idx, n in enumerate(all_small):
        outs[n] = (g_small[n], unpacked[1][idx], unpacked[2][idx], unpacked[3][idx])

    return (loss, grad_x, *[outs[n][0] for n in W_NAMES], *[outs[n][1] for n in W_NAMES],
            *[outs[n][2] for n in W_NAMES], *[outs[n][3] for n in W_NAMES])
```

```python
import math

import jax
import jax.numpy as jnp
from jax import lax
from jax.experimental import pallas as pl
from jax.experimental.pallas import tpu as pltpu

f32, bf16 = jnp.float32, jnp.bfloat16
SDS = jax.ShapeDtypeStruct
MESH = pl.DeviceIdType.MESH

RMS_EPS = 1e-6
GELU_C = math.sqrt(2.0 / math.pi)
GELU_K = 0.044715
ADAM_LR, ADAM_B1, ADAM_B2, ADAM_EPS, ADAM_WD, ADAM_STEP = 0.001, 0.9, 0.999, 1e-08, 0.01, 10
POOL_WINDOWS = (2, 4, 8, 16)
POOL_HALO = 16
S5_GROUP_CH = 16
S5_STATE = 64
S5_KB_CH = 256
N_CHIPS = 4

FFN_TL = 512
FFN_TL_BWD = 256
FFN_FT = 256
FFN_CH = 16
FFN_SPLIT = 2
WG_TR = 2048
POOL_TL = 512
S5_TQ = 512
S5_CB = 1024
S5_UNROLL = 2
TOK_TR = 512
VMEM_LIMIT = 56 * 1024 * 1024

HBM = pl.BlockSpec(memory_space=pltpu.HBM)


def _cparams(n_axes=0, side_effects=False):
    kw = dict(vmem_limit_bytes=VMEM_LIMIT)
    if n_axes:
        kw["dimension_semantics"] = ("arbitrary",) * n_axes
    if side_effects:
        kw["has_side_effects"] = True
    return pltpu.CompilerParams(**kw)


def _dot(a, b):
    return jnp.dot(a, b, preferred_element_type=f32)


def _dot_nt(a, b):
    return lax.dot_general(a, b, (((1,), (1,)), ((), ())), preferred_element_type=f32)


def _dot_tn(a, b):
    return lax.dot_general(a, b, (((0,), (0,)), ((), ())), preferred_element_type=f32)


def _rms_scale(x):
    return lax.rsqrt(jnp.mean(x * x, axis=-1, keepdims=True) + RMS_EPS)


def _rms_fwd(x, gain):
    return x * _rms_scale(x) * gain


def _rms_bwd(x, gain, dy):
    r = _rms_scale(x)
    xn = x * r
    dgain = jnp.sum(dy * xn, axis=0, keepdims=True)
    dxn = dy * gain
    dx = r * (dxn - xn * jnp.mean(dxn * xn, axis=-1, keepdims=True))
    return dx, dgain


def _gelu(x):
    t = jnp.tanh(x * (GELU_C + (GELU_C * GELU_K) * (x * x)))
    hx = 0.5 * x
    return hx + hx * t


def _gelu_and_grad(x):
    x2 = x * x
    t = jnp.tanh(x * (GELU_C + (GELU_C * GELU_K) * x2))
    hx = 0.5 * x
    return hx + hx * t, (0.5 + 0.5 * t) + hx * (1.0 - t * t) * (GELU_C + (3.0 * GELU_C * GELU_K) * x2)


def _sigmoid(x):
    return 1.0 / (1.0 + jnp.exp(-x))


def _cmul(ar, ai, br, bi):
    return ar * br - ai * bi, ar * bi + ai * br


def _row_block(n, cap):
    best = None
    for d in range(16, min(n, cap) + 1, 16):
        if n % d == 0:
            best = d
    assert best is not None, n
    return best


def _mesh_pos():
    return lax.axis_index("x"), lax.axis_index("y"), lax.axis_index("c")


def _other_chips(x, y):
    return [(1 - x, y), (x, 1 - y), (1 - x, 1 - y)]


def _slab_index(ndim, axis, start, size):
    return tuple(pl.ds(start, size) if a == axis else slice(None) for a in range(ndim))


class _ChipExchange:
    def __init__(self, kind, axis, src, dst, send_sems, recv_sems, loc_sem):
        x, y, c = _mesh_pos()
        me = 2 * x + y
        nd = len(src.shape)
        size = src.shape[axis] if kind == "gather" else src.shape[axis] // N_CHIPS

        def src_for(kk):
            return src.at[_slab_index(nd, axis, kk * size, size)] if kind == "slab" else src

        def dst_for(kk):
            return dst.at[_slab_index(nd, axis, kk * size, size)] if kind == "gather" else dst.at[kk]

        self.own = pltpu.make_async_copy(src_for(me), dst_for(me), loc_sem)
        self.sends, self.recvs = [], []
        for j, chip in enumerate(_other_chips(x, y)):
            kk = 2 * chip[0] + chip[1]
            peer = dict(send_sem=send_sems.at[j], recv_sem=recv_sems.at[j], device_id=(chip[0], chip[1], c),
                        device_id_type=MESH)
            self.sends.append(pltpu.make_async_remote_copy(src_ref=src_for(kk), dst_ref=dst_for(me), **peer))
            self.recvs.append(pltpu.make_async_remote_copy(src_ref=src_for(me), dst_ref=dst_for(kk), **peer))

    def start(self):
        self.own.start()
        for cp in self.sends:
            cp.start()

    def finish(self):
        for cp in self.recvs:
            cp.wait_recv()
        for cp in self.sends:
            cp.wait_send()
        self.own.wait()


def _exchange_out_shape(kind, axis, src):
    if kind == "gather":
        shape = tuple(N_CHIPS * n if a == axis else n for a, n in enumerate(src.shape))
    elif kind == "slab":
        shape = (N_CHIPS,) + tuple(n // N_CHIPS if a == axis else n for a, n in enumerate(src.shape))
    else:
        shape = (N_CHIPS,) + tuple(src.shape)
    return SDS(shape, src.dtype)


EXCHANGE_SEMS = [pltpu.SemaphoreType.DMA((3,)), pltpu.SemaphoreType.DMA((3,)), pltpu.SemaphoreType.DMA((1,))]


def _chip_exchange(items, name, to_sibling=()):
    n, ns = len(items), len(to_sibling)

    def body(*refs):
        ins, sib_ins = refs[:n], refs[n:n + ns]
        outs, sib_outs = refs[n + ns:2 * n + ns], refs[2 * n + ns:2 * (n + ns)]
        sems = refs[2 * (n + ns):]
        exs = [_ChipExchange(kind, axis, ins[i], outs[i], sems[3 * i], sems[3 * i + 1], sems[3 * i + 2].at[0])
               for i, (kind, axis, _) in enumerate(items)]
        x, y, c = _mesh_pos()
        sibs = [pltpu.make_async_remote_copy(src_ref=s, dst_ref=t, send_sem=sems[3 * n].at[i], recv_sem=sems[3 * n + 1].at[i],
                                             device_id=(x, y, 1 - c), device_id_type=MESH)
                for i, (s, t) in enumerate(zip(sib_ins, sib_outs))]
        for ex in exs:
            ex.start()
        for cp in sibs:
            cp.start()
        for ex in exs:
            ex.finish()
        for cp in sibs:
            cp.wait()

    sib_sems = [pltpu.SemaphoreType.DMA((ns,)), pltpu.SemaphoreType.DMA((ns,))] if ns else []
    return pl.pallas_call(
        body, name=name,
        out_shape=tuple(_exchange_out_shape(k, a, arr) for k, a, arr in items)
        + tuple(SDS(p.shape, p.dtype) for p in to_sibling),
        in_specs=[HBM] * (n + ns), out_specs=tuple([HBM] * (n + ns)), scratch_shapes=EXCHANGE_SEMS * n + sib_sems,
        compiler_params=_cparams(side_effects=True),
    )(*[arr for _, _, arr in items], *to_sibling)


def _hosted(hosts, host_ins, host_outs, sems, first, last):
    if not hosts:
        return lambda: None
    exs = [_ChipExchange(h[0], h[1], host_ins[i], host_outs[i], sems[3 * i], sems[3 * i + 1], sems[3 * i + 2].at[0])
           for i, h in enumerate(hosts)]

    @pl.when(first)
    def _():
        for ex in exs:
            ex.start()

    def finish():
        @pl.when(last)
        def _():
            for ex in exs:
                ex.finish()

    return finish


def _host_args(hosts):
    hosts = hosts or []
    n = len(hosts)
    return [HBM] * n, [_exchange_out_shape(*h) for h in hosts], [HBM] * n, EXCHANGE_SEMS * n, [h[2] for h in hosts]


def _exchange_with_sibling(parts):
    n = len(parts)

    def body(*refs):
        ins, outs = refs[:n], refs[n:2 * n]
        send_sems, recv_sems = refs[2 * n:]
        x, y, c = _mesh_pos()
        cps = [pltpu.make_async_remote_copy(src_ref=s, dst_ref=t, send_sem=send_sems.at[i], recv_sem=recv_sems.at[i],
                                            device_id=(x, y, 1 - c), device_id_type=MESH)
               for i, (s, t) in enumerate(zip(ins, outs))]
        for cp in cps:
            cp.start()
        for cp in cps:
            cp.wait()

    return pl.pallas_call(
        body, name="exchange_with_sibling",
        out_shape=tuple(SDS(p.shape, p.dtype) for p in parts),
        in_specs=[HBM] * n, out_specs=tuple([HBM] * n),
        scratch_shapes=[pltpu.SemaphoreType.DMA((n,)), pltpu.SemaphoreType.DMA((n,))],
        compiler_params=_cparams(side_effects=True),
    )(*parts)


def _sum_slots(r, name):
    _, rows, cols = r.shape
    tr = _row_block(rows, 512)

    def body(r_ref, o_ref):
        o_ref[...] = ((r_ref[0].astype(f32) + r_ref[1].astype(f32)) + r_ref[2].astype(f32)) + r_ref[3].astype(f32)

    return pl.pallas_call(
        body, name=name, grid=(rows // tr,), out_shape=SDS((rows, cols), f32),
        in_specs=[pl.BlockSpec((N_CHIPS, tr, cols), lambda i: (0, i, 0))],
        out_specs=pl.BlockSpec((tr, cols), lambda i: (i, 0)),
        compiler_params=_cparams(1),
    )(r)


def _sum_slots_layers(rs, name):
    _, rows, cols = rs[0].shape
    nl = len(rs)
    tr = _row_block(rows, 256)

    def body(*refs):
        o_ref = refs[nl]
        for l in range(nl):
            r = refs[l]
            o_ref[l] = ((r[0].astype(f32) + r[1].astype(f32)) + r[2].astype(f32)) + r[3].astype(f32)

    return pl.pallas_call(
        body, name=name, grid=(rows // tr,), out_shape=SDS((nl, rows, cols), f32),
        in_specs=[pl.BlockSpec((N_CHIPS, tr, cols), lambda i: (0, i, 0))] * nl,
        out_specs=pl.BlockSpec((nl, tr, cols), lambda i: (0, i, 0)),
        compiler_params=_cparams(1),
    )(*rs)


def _add_pairs(pairs, name):
    n = len(pairs)

    def body(*refs):
        for i in range(n):
            refs[2 * n + i][...] = refs[2 * i][...] + refs[2 * i + 1][...]

    return pl.pallas_call(body, name=name, out_shape=tuple(SDS(a.shape, f32) for a, _ in pairs),
                          compiler_params=_cparams())(*[t for pair in pairs for t in pair])


def _adamw(g_parts, w, m, v, name):
    rows, cols = w.shape[-2:]
    tr = _row_block(rows, 512)
    n_g = len(g_parts)
    c1 = 1.0 / (1.0 - ADAM_B1 ** ADAM_STEP)
    c2 = 1.0 / (1.0 - ADAM_B2 ** ADAM_STEP)

    def body(*refs):
        g_refs = refs[:n_g]
        w_ref, m_ref, v_ref, go_ref, d_ref, mo_ref, vo_ref = refs[n_g:]
        g = g_refs[0][...]
        for r in g_refs[1:]:
            g = g + r[...]
        mn = ADAM_B1 * m_ref[...] + (1.0 - ADAM_B1) * g
        vn = ADAM_B2 * v_ref[...] + (1.0 - ADAM_B2) * (g * g)
        go_ref[...] = g
        mo_ref[...] = mn
        vo_ref[...] = vn
        d_ref[...] = -ADAM_LR * ((mn * c1) / (jnp.sqrt(vn * c2) + ADAM_EPS) + ADAM_WD * w_ref[...])

    if w.ndim == 2:
        grid, spec = (rows // tr,), pl.BlockSpec((tr, cols), lambda i: (i, 0))
    else:
        grid, spec = (w.shape[0], rows // tr), pl.BlockSpec((None, tr, cols), lambda a, i: (a, i, 0))
    return pl.pallas_call(
        body, name=name, grid=grid, out_shape=tuple(SDS(w.shape, f32) for _ in range(4)),
        in_specs=[spec] * (n_g + 3), out_specs=(spec,) * 4, compiler_params=_cparams(len(grid)),
    )(*g_parts, w, m, v)


def _resident_spec(shape, n_grid):
    zeros = (0,) * len(shape)
    return pl.BlockSpec(tuple(shape), {2: lambda a, b: zeros, 3: lambda a, b, c: zeros}[n_grid],
                        pipeline_mode=pl.Buffered(1))


def _shifted_rows(ext):
    return ext[8:, :], pltpu.roll(ext, 1, axis=0)[8:, :], pltpu.roll(ext, 2, axis=0)[8:, :]


def _later_rows(ext):
    n = ext.shape[0]
    return pltpu.roll(ext, n - 1, axis=0)[0:n - 8, :], pltpu.roll(ext, n - 2, axis=0)[0:n - 8, :]


def _ffn_fwd(x_mid, gain_pre, gain_post, w_l, conv_w, conv_b, li, host=None):
    bsz, l, d = x_mid.shape
    f = w_l.shape[1]
    tl, ft = min(FFN_TL, l), min(FFN_FT, f)
    nt, nf = l // tl, f // ft
    n_sub = FFN_SPLIT if tl % (FFN_SPLIT * FFN_CH) == 0 else 1
    sub = tl // n_sub
    h_in, h_shape, h_out, h_scr, h_ops = _host_args(host)

    def body(x_ref, gpre_ref, gpost_ref, w_ref, cw_ref, cb_ref, *rest):
        xo_ref, g_ref, v_ref, f_ref = rest[len(h_in):len(h_in) + 4]
        s0 = len(h_in) + 4 + len(h_out)
        h_sc, facc, gprev = rest[s0:s0 + 3]
        b, t = pl.program_id(0), pl.program_id(1)
        finish = _hosted(host, rest[:len(h_in)], rest[len(h_in) + 4:s0], rest[s0 + 3:],
                         (b == 0) & (t == 0), (b == bsz - 1) & (t == nt - 1))
        h_sc[...] = _rms_fwd(x_ref[...], gpre_ref[...]).astype(bf16)

        @pl.when(t == 0)
        def _():
            gprev[...] = jnp.zeros_like(gprev)

        def gate_val(j):
            wg, wv = w_ref[0, j * ft:(j + 1) * ft, :], w_ref[1, j * ft:(j + 1) * ft, :]
            return ([_dot_nt(h_sc[s * sub:(s + 1) * sub, :], wg) for s in range(n_sub)],
                    [_dot_nt(h_sc[s * sub:(s + 1) * sub, :], wv) for s in range(n_sub)])

        ahead = gate_val(0)
        for j in range(nf):
            gs, vs = ahead
            if j + 1 < nf:
                ahead = gate_val(j + 1)
            cols = slice(j * ft, (j + 1) * ft)
            wd = w_ref[2, cols, :]
            w0, w1, w2, bias = (jnp.broadcast_to(r, (FFN_CH, ft))
                                for r in (cw_ref[0:1, cols], cw_ref[1:2, cols], cw_ref[2:3, cols], cb_ref[:, cols]))
            hist = gprev[j]
            for s in range(n_sub):
                rows = slice(s * sub, (s + 1) * sub)
                g_ref[j, rows, :] = gs[s].astype(bf16)
                v_ref[j, rows, :] = vs[s].astype(bf16)
                hdn = []
                for r0 in range(0, sub, FFN_CH):
                    g0, g1, g2 = _shifted_rows(jnp.concatenate([hist, gs[s][r0:r0 + FFN_CH, :]], axis=0))
                    hist = g0[FFN_CH - 8:, :]
                    hdn.append((_gelu(bias + w0 * g2 + w1 * g1 + w2 * g0) * vs[s][r0:r0 + FFN_CH, :]).astype(bf16))
                part = _dot(jnp.concatenate(hdn, axis=0), wd)
                if j == 0:
                    facc[rows, :] = part
                else:
                    facc[rows, :] += part
            gprev[j] = hist

        fv = facc[...]
        f_ref[...] = fv
        xo_ref[...] = x_ref[...] + _rms_fwd(fv, gpost_ref[...])
        finish()

    tok = pl.BlockSpec((None, tl, d), lambda b, t: (b, t, 0))
    hid = pl.BlockSpec((nf, tl, ft), lambda b, t: (0, b * nt + t, 0))
    gain = pl.BlockSpec((1, d), lambda b, t: (0, 0))
    hid_shape = SDS((nf, bsz * l, ft), bf16)
    return pl.pallas_call(
        body, name=f"ffn_fwd_{li}", grid=(bsz, nt),
        out_shape=(SDS((bsz, l, d), f32), hid_shape, hid_shape, SDS((bsz, l, d), f32), *h_shape),
        in_specs=[tok, gain, gain, _resident_spec(w_l.shape, 2),
                  pl.BlockSpec((None, 3, f), lambda b, t: (li, 0, 0)),
                  pl.BlockSpec((None, 1, f), lambda b, t: (li, 0, 0)), *h_in],
        out_specs=(tok, hid, hid, tok, *h_out),
        scratch_shapes=[pltpu.VMEM((tl, d), bf16), pltpu.VMEM((tl, d), f32), pltpu.VMEM((nf, 8, ft), f32), *h_scr],
        compiler_params=_cparams(2, side_effects=bool(host)),
    )(x_mid, gain_pre, gain_post, w_l, conv_w, conv_b.reshape(conv_b.shape[0], 1, f), *h_ops)


def _ffn_bwd_act(x_mid, f_sv, g_sv, v_sv, dxo, gain_pre, gain_post, w_l, conv_w, conv_b, li, host=None):
    bsz, l, d = x_mid.shape
    f = w_l.shape[1]
    tl, ft = min(FFN_TL_BWD, l), min(FFN_FT, f)
    nt, nf = l // tl, f // ft
    h_in, h_shape, h_out, h_scr, h_ops = _host_args(host)

    def body(x_ref, f_ref, dxo_ref, g_ref, v_ref, gh_ref, gpre_ref, gpost_ref, w_ref, cw_ref, cb_ref, *rest):
        o0 = len(h_in)
        dx_ref, dg_ref, dv_ref, hdn_ref, h_ref, df_ref, dgain_ref, dconv_ref = rest[o0:o0 + 8]
        s0 = o0 + 8 + len(h_out)
        h_sc, df_sc, dh_acc, dgc_next = rest[s0:s0 + 4]
        b, t = pl.program_id(0), pl.program_id(1)
        tt = nt - 1 - t
        finish = _hosted(host, rest[:o0], rest[o0 + 8:s0], rest[s0 + 4:],
                         (b == 0) & (t == 0), (b == bsz - 1) & (t == nt - 1))

        @pl.when((b == 0) & (t == 0))
        def _():
            dgain_ref[...] = jnp.zeros_like(dgain_ref)
            dconv_ref[...] = jnp.zeros_like(dconv_ref)

        @pl.when(t == 0)
        def _():
            dgc_next[...] = jnp.zeros_like(dgc_next)

        hb = _rms_fwd(x_ref[...], gpre_ref[...]).astype(bf16)
        h_sc[...] = hb
        h_ref[...] = hb
        df, dgp = _rms_bwd(f_ref[...], gpost_ref[...], dxo_ref[...])
        dfb = df.astype(bf16)
        df_sc[...] = dfb
        df_ref[...] = dfb
        dgain_ref[1:2, :] += dgp

        def fold(a):
            return a.reshape(FFN_CH // 8, 8, ft).sum(axis=0)

        ahead = _dot_nt(df_sc[...], w_ref[2, 0:ft, :])
        for j in range(nf):
            dhdn = ahead
            if j + 1 < nf:
                ahead = _dot_nt(df_sc[...], w_ref[2, (j + 1) * ft:(j + 2) * ft, :])
            cols = slice(j * ft, (j + 1) * ft)
            w0, w1, w2, bias = (jnp.broadcast_to(r, (FFN_CH, ft))
                                for r in (cw_ref[0:1, cols], cw_ref[1:2, cols], cw_ref[2:3, cols], cb_ref[:, cols]))

            def conv_bwd(dgc, after):
                d1, d2 = _later_rows(jnp.concatenate([dgc, after], axis=0))
                return (w2 * dgc + w1 * d1 + w0 * d2).astype(bf16)

            g = g_ref[j].astype(f32)
            v = v_ref[j].astype(f32)
            hist = jnp.where(tt > 0, gh_ref[j].astype(f32)[8:16, :], 0.0)
            acc = [jnp.zeros((8, ft), f32)] * 4
            first, pending, hdn, dvs, dgs = None, None, [], [], []
            for r0 in range(0, tl, FFN_CH):
                g0, g1, g2 = _shifted_rows(jnp.concatenate([hist, g[r0:r0 + FFN_CH, :]], axis=0))
                hist = g0[FFN_CH - 8:, :]
                vc, dc = v[r0:r0 + FFN_CH, :], dhdn[r0:r0 + FFN_CH, :]
                u, ug = _gelu_and_grad(bias + w0 * g2 + w1 * g1 + w2 * g0)
                hdn.append((u * vc).astype(bf16))
                dvs.append((dc * u).astype(bf16))
                dgc = dc * vc * ug
                acc = [acc[0] + fold(dgc * g2), acc[1] + fold(dgc * g1), acc[2] + fold(dgc * g0), acc[3] + fold(dgc)]
                if pending is None:
                    first = dgc[0:8, :]
                else:
                    dgs.append(conv_bwd(pending, dgc[0:8, :]))
                pending = dgc
            dgs.append(conv_bwd(pending, dgc_next[j]))
            dgc_next[j] = first
            dgb, dvb = jnp.concatenate(dgs, axis=0), jnp.concatenate(dvs, axis=0)
            hdn_ref[j] = jnp.concatenate(hdn, axis=0)
            dg_ref[j] = dgb
            dv_ref[j] = dvb
            part = _dot(dgb, w_ref[0, cols, :]) + _dot(dvb, w_ref[1, cols, :])
            if j == 0:
                dh_acc[...] = part
            else:
                dh_acc[...] += part
            for k in range(4):
                dconv_ref[j, k:k + 1, :] += jnp.sum(acc[k], axis=0, keepdims=True)

        dxp, dgp = _rms_bwd(x_ref[...], gpre_ref[...], dh_acc[...])
        dx_ref[...] = dxo_ref[...] + dxp
        dgain_ref[0:1, :] += dgp
        finish()

    tok = pl.BlockSpec((None, tl, d), lambda b, t: (b, nt - 1 - t, 0))
    hid = pl.BlockSpec((nf, tl, ft), lambda b, t: (0, b * nt + nt - 1 - t, 0))
    halo = pl.BlockSpec((nf, 16, ft), lambda b, t: (0, jnp.maximum((b * nt + nt - 1 - t) * (tl // 16) - 1, 0), 0))
    gain = pl.BlockSpec((1, d), lambda b, t: (0, 0))
    hid_shape = SDS((nf, bsz * l, ft), bf16)
    return pl.pallas_call(
        body, name=f"ffn_bwd_act_{li}", grid=(bsz, nt),
        out_shape=(SDS((bsz, l, d), f32), hid_shape, hid_shape, hid_shape,
                   SDS((bsz, l, d), bf16), SDS((bsz, l, d), bf16), SDS((8, d), f32), SDS((nf, 8, ft), f32), *h_shape),
        in_specs=[tok, tok, tok, hid, hid, halo, gain, gain, _resident_spec(w_l.shape, 2),
                  pl.BlockSpec((None, 3, f), lambda b, t: (li, 0, 0)),
                  pl.BlockSpec((None, 1, f), lambda b, t: (li, 0, 0)), *h_in],
        out_specs=(tok, hid, hid, hid, tok, tok,
                   pl.BlockSpec((8, d), lambda b, t: (0, 0)), pl.BlockSpec((nf, 8, ft), lambda b, t: (0, 0, 0)),
                   *h_out),
        scratch_shapes=[pltpu.VMEM((tl, d), bf16), pltpu.VMEM((tl, d), bf16), pltpu.VMEM((tl, d), f32),
                        pltpu.VMEM((nf, 8, ft), f32), *h_scr],
        compiler_params=_cparams(2, side_effects=bool(host)),
    )(x_mid, f_sv, dxo, g_sv, v_sv, g_sv, gain_pre, gain_post, w_l, conv_w,
      conv_b.reshape(conv_b.shape[0], 1, f), *h_ops)


def _ffn_bwd_weights(dg, dv, hdn, h, df, li):
    nf, t, ft = dg.shape
    f = nf * ft
    d = h.shape[1]
    tr = min(WG_TR, t)
    nr = t // tr

    def body(dg_ref, dv_ref, hdn_ref, h_ref, df_ref, o_ref, acc):
        r = pl.program_id(1)

        @pl.when(r == 0)
        def _():
            acc[...] = jnp.zeros_like(acc)

        rows = pl.ds(pl.multiple_of(r * tr, tr), tr)
        hb = h_ref[rows, :]
        acc[0] += _dot_tn(dg_ref[...], hb)
        acc[1] += _dot_tn(dv_ref[...], hb)
        acc[2] += _dot_tn(hdn_ref[...], df_ref[rows, :])

        @pl.when(r == nr - 1)
        def _():
            o_ref[...] = acc[...].astype(bf16)

    hid = pl.BlockSpec((None, tr, ft), lambda j, r: (j, r, 0))
    tok = _resident_spec((t, d), 2)
    return pl.pallas_call(
        body, name=f"ffn_bwd_weights_{li}", grid=(nf, nr), out_shape=SDS((3, f, d), bf16),
        in_specs=[hid, hid, hid, tok, tok], out_specs=pl.BlockSpec((3, ft, d), lambda j, r: (0, j, 0)),
        scratch_shapes=[pltpu.VMEM((3, ft, d), f32)], compiler_params=_cparams(2),
    )(dg, dv, hdn, h, df)


def _pool_counts(t0, tl, d):
    gch = d // len(POOL_WINDOWS)
    tpos = (t0 + lax.broadcasted_iota(jnp.int32, (tl, d), 0) + 1).astype(f32)
    lane = lax.broadcasted_iota(jnp.int32, (tl, d), 1)
    win = jnp.full((tl, d), float(POOL_WINDOWS[-1]), f32)
    for gi in range(len(POOL_WINDOWS) - 2, -1, -1):
        win = jnp.where(lane < (gi + 1) * gch, float(POOL_WINDOWS[gi]), win)
    return jnp.minimum(tpos, win)


def _pool_select(parts, tl, d):
    gch = d // len(POOL_WINDOWS)
    lane = lax.broadcasted_iota(jnp.int32, (tl, d), 1)
    out = parts[-1]
    for gi in range(len(parts) - 2, -1, -1):
        out = jnp.where(lane < (gi + 1) * gch, parts[gi], out)
    return out


def _pool_window_sums(u, halo, tl):
    ext = jnp.concatenate([halo, u], axis=0)
    sums, cur = [], ext
    for k in (1, 2, 4, 8):
        cur = cur + pltpu.roll(cur, k, axis=0)
        sums.append(cur[POOL_HALO:POOL_HALO + tl, :])
    return sums


def _pool_mix(u, halo, cnt, pw_ref, scale, tl, d):
    gch = d // len(POOL_WINDOWS)
    ext = jnp.concatenate([halo, u], axis=0)
    sums = []
    for gi in range(len(POOL_WINDOWS)):
        cur = ext[:, gi * gch:(gi + 1) * gch]
        for k in (1, 2, 4, 8)[:gi + 1]:
            cur = cur + pltpu.roll(cur, k, axis=0)
        sums.append(cur[POOL_HALO:POOL_HALO + tl, :])
    pooled = jnp.concatenate(sums, axis=1) / cnt
    diff = pooled - u
    outs = [_dot(diff[:, gi * gch:(gi + 1) * gch].astype(bf16), pw_ref[gi]) for gi in range(len(POOL_WINDOWS))]
    return diff, jnp.concatenate(outs, axis=1)


def _pool_fwd(x, gain_pre, gain_post, pw, scale, li):
    bsz, l, d = x.shape
    tl = min(POOL_TL, l)
    nt = l // tl

    def body(x_ref, gpre_ref, gpost_ref, pw_ref, sc_ref, xo_ref, halo):
        t = pl.program_id(1)

        @pl.when(t == 0)
        def _():
            halo[...] = jnp.zeros_like(halo)

        xv = x_ref[...]
        u = _rms_fwd(xv, gpre_ref[...])
        _, out = _pool_mix(u, halo[...], _pool_counts(t * tl, tl, d), pw_ref, sc_ref[...], tl, d)
        halo[...] = u[tl - POOL_HALO:tl, :]
        xo_ref[...] = xv + _rms_fwd(out * sc_ref[...], gpost_ref[...])

    tok = pl.BlockSpec((None, tl, d), lambda b, t: (b, t, 0))
    gain = pl.BlockSpec((1, d), lambda b, t: (0, 0))
    return pl.pallas_call(
        body, name=f"pool_fwd_{li}", grid=(bsz, nt), out_shape=SDS((bsz, l, d), f32),
        in_specs=[tok, gain, gain, pl.BlockSpec(pw.shape, lambda b, t: (0, 0, 0)), gain], out_specs=tok,
        scratch_shapes=[pltpu.VMEM((POOL_HALO, d), f32)], compiler_params=_cparams(2),
    )(x, gain_pre, gain_post, pw, scale)


def _pool_bwd(x, dxo, gain_pre, gain_post, pw, scale, li):
    bsz, l, d = x.shape
    tl = min(POOL_TL, l)
    nt = l // tl
    ng = len(POOL_WINDOWS)
    gch = d // ng
    n_ext = tl + POOL_HALO

    def body(x_ref, xh_ref, dxo_ref, gpre_ref, gpost_ref, pw_ref, sc_ref, dx_ref, dpw_ref, ds_ref, qnext):
        b, t = pl.program_id(0), pl.program_id(1)
        tt = nt - 1 - t

        @pl.when((b == 0) & (t == 0))
        def _():
            dpw_ref[...] = jnp.zeros_like(dpw_ref)
            ds_ref[...] = jnp.zeros_like(ds_ref)

        @pl.when(t == 0)
        def _():
            qnext[...] = jnp.zeros_like(qnext)

        xv = x_ref[...]
        gpre = gpre_ref[...]
        u = _rms_fwd(xv, gpre)
        uh = _rms_fwd(jnp.where(tt > 0, xh_ref[...], 0.0), gpre)
        cnt = _pool_counts(tt * tl, tl, d)
        scale_v = sc_ref[...]
        diff, out = _pool_mix(u, uh, cnt, pw_ref, scale_v, tl, d)
        dxo_v = dxo_ref[...]
        dm, dgpost = _rms_bwd(out * scale_v, gpost_ref[...], dxo_v)
        ds_ref[1:2, :] += dgpost
        ds_ref[2:3, :] += jnp.sum(dm * out, axis=0, keepdims=True)
        dout = (dm * scale_v).astype(bf16)
        ddiffs = []
        for gi in range(ng):
            sl = slice(gi * gch, (gi + 1) * gch)
            dpw_ref[gi] += _dot_tn(diff[:, sl].astype(bf16), dout[:, sl])
            ddiffs.append(_dot_nt(dout[:, sl], pw_ref[gi]))
        ddiff = jnp.concatenate(ddiffs, axis=1)
        q = ddiff / cnt
        ext = jnp.concatenate([q, qnext[...]], axis=0)
        sums, cur = [], ext
        for k in (1, 2, 4, 8):
            cur = cur + pltpu.roll(cur, n_ext - k, axis=0)
            sums.append(cur[0:tl, :])
        du = _pool_select(sums, tl, d) - ddiff
        qnext[...] = q[0:POOL_HALO, :]
        dxp, dgpre = _rms_bwd(xv, gpre, du)
        ds_ref[0:1, :] += dgpre
        dx_ref[...] = dxo_v + dxp

    tok = pl.BlockSpec((None, tl, d), lambda b, t: (b, nt - 1 - t, 0))
    halo = pl.BlockSpec((None, POOL_HALO, d),
                        lambda b, t: (b, jnp.maximum((nt - 1 - t) * (tl // POOL_HALO) - 1, 0), 0))
    gain = pl.BlockSpec((1, d), lambda b, t: (0, 0))
    return pl.pallas_call(
        body, name=f"pool_bwd_{li}", grid=(bsz, nt),
        out_shape=(SDS((bsz, l, d), f32), SDS((ng, gch, gch), f32), SDS((8, d), f32)),
        in_specs=[tok, halo, tok, gain, gain, pl.BlockSpec(pw.shape, lambda b, t: (0, 0, 0)), gain],
        out_specs=(tok, pl.BlockSpec((ng, gch, gch), lambda b, t: (0, 0, 0)), pl.BlockSpec((8, d), lambda b, t: (0, 0))),
        scratch_shapes=[pltpu.VMEM((POOL_HALO, d), f32)], compiler_params=_cparams(2),
    )(x, x, dxo, gain_pre, gain_post, pw, scale)


def _s5_discretise(lam_re, lam_im, log_dt, b_re, b_im):
    def body(lr_ref, li_ref, ld_ref, br_ref, bi_ref, ar_ref, ai_ref, bbr_ref, bbi_ref):
        lr, li = lr_ref[...], li_ref[...]
        dt = jnp.exp(ld_ref[...])
        mag = jnp.exp(lr * dt)
        ar = mag * jnp.cos(li * dt)
        ai = mag * jnp.sin(li * dt)
        den = lr * lr + li * li
        nr, ni = ar - 1.0, ai
        fr = (nr * lr + ni * li) / den
        fi = (ni * lr - nr * li) / den
        br, bi = br_ref[...], bi_ref[...]
        ar_ref[...] = ar
        ai_ref[...] = ai
        bbr_ref[...] = fr * br - fi * bi
        bbi_ref[...] = fr * bi + fi * br

    return pl.pallas_call(
        body, name="s5_discretise",
        out_shape=(SDS(lam_re.shape, f32), SDS(lam_re.shape, f32), SDS(b_re.shape, f32), SDS(b_re.shape, f32)),
        compiler_params=_cparams(),
    )(lam_re, lam_im, log_dt, b_re, b_im)


def _s5_discretise_bwd(lam_re, lam_im, log_dt, b_re, b_im, g_ar, g_ai, g_bbr, g_bbi):
    def body(lr_ref, li_ref, ld_ref, br_ref, bi_ref, gar_ref, gai_ref, gbbr_ref, gbbi_ref,
             dlr_ref, dli_ref, dld_ref, dbr_ref, dbi_ref):
        lr, li = lr_ref[...], li_ref[...]
        dt = jnp.exp(ld_ref[...])
        mag = jnp.exp(lr * dt)
        cs, sn = jnp.cos(li * dt), jnp.sin(li * dt)
        ar, ai = mag * cs, mag * sn
        den = lr * lr + li * li
        nr, ni = ar - 1.0, ai
        fr = (nr * lr + ni * li) / den
        fi = (ni * lr - nr * li) / den
        br, bi = br_ref[...], bi_ref[...]
        gbbr, gbbi = gbbr_ref[...], gbbi_ref[...]
        dbr_ref[...] = fr * gbbr + fi * gbbi
        dbi_ref[...] = fr * gbbi - fi * gbbr
        gfr = jnp.sum(br * gbbr + bi * gbbi, axis=2, keepdims=True)
        gfi = jnp.sum(br * gbbi - bi * gbbr, axis=2, keepdims=True)
        gnr_num, gni_num = gfr / den, gfi / den
        gden = -(gfr * fr + gfi * fi) / den
        g_nr = gnr_num * lr - gni_num * li
        g_ni = gnr_num * li + gni_num * lr
        dlr = gnr_num * nr + gni_num * ni + gden * 2.0 * lr
        dli = gnr_num * ni - gni_num * nr + gden * 2.0 * li
        gar = gar_ref[...] + g_nr
        gai = gai_ref[...] + g_ni
        gq = (gar * cs + gai * sn) * mag
        gth = (gai * cs - gar * sn) * mag
        dlr_ref[...] = dlr + gq * dt
        dli_ref[...] = dli + gth * dt
        dld_ref[...] = jnp.sum(gq * lr + gth * li, axis=3, keepdims=True) * dt

    return pl.pallas_call(
        body, name="s5_discretise_bwd",
        out_shape=(SDS(lam_re.shape, f32), SDS(lam_re.shape, f32), SDS(log_dt.shape, f32),
                   SDS(b_re.shape, f32), SDS(b_re.shape, f32)),
        compiler_params=_cparams(),
    )(lam_re, lam_im, log_dt, b_re, b_im, g_ar, g_ai, g_bbr, g_bbi)


def _fill_powers(pw_r, pw_i, ar, ai, nj, width):
    cb = min(S5_CB, width)
    for c0 in range(0, width, cb):
        sl = pl.ds(c0, cb)
        a_r, a_i = ar[:, c0:c0 + cb], ai[:, c0:c0 + cb]

        def step(i, p):
            pw_r[i, :, sl] = p[0]
            pw_i[i, :, sl] = p[1]
            return _cmul(p[0], p[1], a_r, a_i)

        lax.fori_loop(0, nj, step, (a_r, a_i))


def _interleaved_scan(xr_sc, xi_sc, ar, ai, pw_r, pw_i, carry_r, carry_i, nj, width, reverse, h_sc=None):
    cb = min(S5_CB, width)
    acc_out = []
    for c0 in range(0, width, cb):
        sl = pl.ds(c0, cb)
        a_r, a_i = ar[:, c0:c0 + cb], ai[:, c0:c0 + cb]
        aj_r, aj_i = pw_r[nj - 1, :, sl], pw_i[nj - 1, :, sl]

        def pos(i):
            return nj - 1 - i if reverse else i

        def local_step(i, st):
            j = pos(i)
            hr, hi = _cmul(a_r, a_i, st[0], st[1])
            hr, hi = hr + xr_sc[j, :, sl], hi + xi_sc[j, :, sl]
            xr_sc[j, :, sl] = hr
            xi_sc[j, :, sl] = hi
            return hr, hi

        zero = jnp.zeros((8, cb), f32)
        fin_r, fin_i = lax.fori_loop(0, nj, local_step, (zero, zero), unroll=S5_UNROLL)
        row = lax.broadcasted_iota(jnp.int32, (8, cb), 0)
        c_r, c_i = carry_r[0:1, sl], carry_i[0:1, sl]
        ent_r, ent_i = zero, zero
        order = range(7, -1, -1) if reverse else range(8)
        for s in order:
            ent_r = jnp.where(row == s, c_r, ent_r)
            ent_i = jnp.where(row == s, c_i, ent_i)
            pr, pi_ = _cmul(aj_r[0:1, :], aj_i[0:1, :], c_r, c_i)
            c_r, c_i = fin_r[s:s + 1, :] + pr, fin_i[s:s + 1, :] + pi_
        carry_r[:, sl] = jnp.broadcast_to(c_r, (8, cb))
        carry_i[:, sl] = jnp.broadcast_to(c_i, (8, cb))

        def fix_step(i, st):
            j = pos(i)
            nx_r, nx_i, acc_r, acc_i = st
            cr_, ci_ = _cmul(pw_r[i, :, sl], pw_i[i, :, sl], ent_r, ent_i)
            hr, hi = xr_sc[j, :, sl] + cr_, xi_sc[j, :, sl] + ci_
            xr_sc[j, :, sl] = hr
            xi_sc[j, :, sl] = hi
            if h_sc is not None:
                sr, si = h_sc[0][j, :, sl], h_sc[1][j, :, sl]
                acc_r = acc_r + nx_r * sr + nx_i * si
                acc_i = acc_i + nx_i * sr - nx_r * si
                nx_r, nx_i = hr, hi
            return nx_r, nx_i, acc_r, acc_i

        st = lax.fori_loop(0, nj, fix_step, (ent_r, ent_i, zero, zero), unroll=S5_UNROLL)
        acc_out.append((st[2], st[3]))
    return acc_out


def _diag_mask(n_rep, h, ks):
    assert h & (h - 1) == 0 and (ks // n_rep) & (ks // n_rep - 1) == 0
    r = lax.shift_right_logical(lax.broadcasted_iota(jnp.int32, (n_rep * h, ks), 0), h.bit_length() - 1)
    c = lax.shift_right_logical(lax.broadcasted_iota(jnp.int32, (n_rep * h, ks), 1), (ks // n_rep).bit_length() - 1)
    return r == c


def _expand_block_diag(compact, n_rep):
    h, ks = compact.shape
    full = jnp.concatenate([compact] * n_rep, axis=0)
    return jnp.where(_diag_mask(n_rep, h, ks), full, 0.0).astype(bf16)


def _compact_block_diag(full, n_rep):
    rows, ks = full.shape
    h = rows // n_rep
    return jnp.sum(jnp.where(_diag_mask(n_rep, h, ks), full, 0.0).reshape(n_rep, h, ks), axis=0)


def _s5_scan_fwd(x_il, gain_pre, d_skip, ab_r, ab_i, bbc_r, bbc_i, cc_r, cc_in, li, host=None):
    bsz, nblk, tq, d = x_il.shape
    nkb, gch, ks = bbc_r.shape
    kc = d // nkb
    n_rep = kc // gch
    nj = tq // 8
    h_in, h_shape, h_out, h_scr, h_ops = _host_args(host)

    def body(xf_ref, xk_ref, gk_ref, dk_ref, ar_ref, ai_ref, bbr_ref, bbi_ref, cr_ref, ci_ref, *rest):
        o0 = len(h_in)
        y_ref, hr_ref, hi_ref = rest[o0:o0 + 3]
        s0 = o0 + 3 + len(h_out)
        xr_sc, xi_sc, pw_r, pw_i, carry_r, carry_i, bbr_sc, bbi_sc, crt_sc, cit_sc = rest[s0:s0 + 10]
        k, b, n = pl.program_id(0), pl.program_id(1), pl.program_id(2)
        finish = _hosted(host, rest[:o0], rest[o0 + 3:s0], rest[s0 + 10:],
                         (k == 0) & (b == 0) & (n == 0), (k == nkb - 1) & (b == bsz - 1) & (n == nblk - 1))
        ar = jnp.broadcast_to(ar_ref[...], (8, ks))
        ai = jnp.broadcast_to(ai_ref[...], (8, ks))

        @pl.when((b == 0) & (n == 0))
        def _():
            bbr_sc[...] = _expand_block_diag(bbr_ref[...], n_rep)
            bbi_sc[...] = _expand_block_diag(bbi_ref[...], n_rep)
            crt_sc[...] = _expand_block_diag(cr_ref[...], n_rep)
            cit_sc[...] = _expand_block_diag(ci_ref[...], n_rep)
            _fill_powers(pw_r, pw_i, ar, ai, nj, ks)

        @pl.when(n == 0)
        def _():
            carry_r[...] = jnp.zeros_like(carry_r)
            carry_i[...] = jnp.zeros_like(carry_i)

        u = xk_ref[...] * _rms_scale(xf_ref[...]) * gk_ref[...]
        ub = u.astype(bf16)
        xr_sc[...] = _dot(ub, bbr_sc[...]).reshape(nj, 8, ks)
        xi_sc[...] = _dot(ub, bbi_sc[...]).reshape(nj, 8, ks)
        _interleaved_scan(xr_sc, xi_sc, ar, ai, pw_r, pw_i, carry_r, carry_i, nj, ks, reverse=False)
        hrb = xr_sc[...].reshape(tq, ks).astype(bf16)
        hib = xi_sc[...].reshape(tq, ks).astype(bf16)
        hr_ref[...] = hrb
        hi_ref[...] = hib
        crt, cit, dk = crt_sc[...], cit_sc[...], dk_ref[...]
        for r0 in range(0, tq, tq // 2):
            rows = slice(r0, r0 + tq // 2)
            y_ref[rows, :] = _dot_nt(hrb[rows, :], crt) + _dot_nt(hib[rows, :], cit) + dk * u[rows, :]
        finish()

    full = pl.BlockSpec((None, None, tq, d), lambda k, b, n: (b, n, 0, 0))
    chan = pl.BlockSpec((None, None, tq, kc), lambda k, b, n: (b, n, 0, k))
    stat = pl.BlockSpec((None, None, tq, ks), lambda k, b, n: (b, n, 0, k))
    vec_c = pl.BlockSpec((1, kc), lambda k, b, n: (0, k))
    vec_s = pl.BlockSpec((1, ks), lambda k, b, n: (0, k))
    cmap = pl.BlockSpec((None, gch, ks), lambda k, b, n: (k, 0, 0))
    s_tot = nkb * ks
    return pl.pallas_call(
        body, name=f"s5_scan_fwd_{li}", grid=(nkb, bsz, nblk),
        out_shape=(SDS((bsz, nblk, tq, d), f32), SDS((bsz, nblk, tq, s_tot), bf16), SDS((bsz, nblk, tq, s_tot), bf16),
                   *h_shape),
        in_specs=[full, chan, vec_c, vec_c, vec_s, vec_s, cmap, cmap, cmap, cmap, *h_in],
        out_specs=(chan, stat, stat, *h_out),
        scratch_shapes=[pltpu.VMEM((nj, 8, ks), f32)] * 4 + [pltpu.VMEM((8, ks), f32)] * 2
        + [pltpu.VMEM((kc, ks), bf16)] * 4 + h_scr,
        compiler_params=_cparams(3, side_effects=bool(host)),
    )(x_il, x_il, gain_pre, d_skip, ab_r, ab_i, bbc_r, bbc_i, cc_r, cc_in, *h_ops)


def _s5_glu_fwd(y, x_il, w_glu, b_glu, gain_post, li):
    t, d = y.shape
    tr = min(TOK_TR, t)

    def body(y_ref, x_ref, w_ref, b_ref, gp_ref, xo_ref):
        z = _gelu(y_ref[...])
        a = _dot(z.astype(bf16), w_ref[...]) + b_ref[...]
        xo_ref[...] = x_ref[...] + _rms_fwd(z * _sigmoid(a), gp_ref[...])

    tok = pl.BlockSpec((tr, d), lambda i: (i, 0))
    vec = pl.BlockSpec((1, d), lambda i: (0, 0))
    return pl.pallas_call(
        body, name=f"s5_glu_fwd_{li}", grid=(t // tr,), out_shape=SDS((t, d), f32),
        in_specs=[tok, tok, pl.BlockSpec((d, d), lambda i: (0, 0)), vec, vec], out_specs=tok,
        compiler_params=_cparams(1),
    )(y, x_il, w_glu, b_glu, gain_post)


def _s5_glu_bwd(y, dxo, w_glu, b_glu, gain_post, li):
    t, d = y.shape
    tr = min(TOK_TR, t)

    def body(y_ref, dxo_ref, w_ref, b_ref, gp_ref, dy_ref, dw_ref, ds_ref):
        @pl.when(pl.program_id(0) == 0)
        def _():
            dw_ref[...] = jnp.zeros_like(dw_ref)
            ds_ref[...] = jnp.zeros_like(ds_ref)

        z, zg = _gelu_and_grad(y_ref[...])
        zb = z.astype(bf16)
        w = w_ref[...]
        s = _sigmoid(_dot(zb, w) + b_ref[...])
        dm, dgpost = _rms_bwd(z * s, gp_ref[...], dxo_ref[...])
        da = dm * z * s * (1.0 - s)
        dab = da.astype(bf16)
        dz = dm * s + _dot_nt(dab, w)
        dw_ref[...] += _dot_tn(zb, dab)
        ds_ref[0:1, :] += dgpost
        ds_ref[1:2, :] += jnp.sum(da, axis=0, keepdims=True)
        dy_ref[...] = dz * zg

    tok = pl.BlockSpec((tr, d), lambda i: (i, 0))
    vec = pl.BlockSpec((1, d), lambda i: (0, 0))
    mat = pl.BlockSpec((d, d), lambda i: (0, 0))
    return pl.pallas_call(
        body, name=f"s5_glu_bwd_{li}", grid=(t // tr,),
        out_shape=(SDS((t, d), f32), SDS((d, d), f32), SDS((8, d), f32)),
        in_specs=[tok, tok, mat, vec, vec], out_specs=(tok, mat, pl.BlockSpec((8, d), lambda i: (0, 0))),
        compiler_params=_cparams(1),
    )(y, dxo, w_glu, b_glu, gain_post)


def _s5_scan_bwd(x_il, dy, h_r, h_i, gain_pre, d_skip, ab_r, ab_i, bbc_r, bbc_i, cc_r, cc_in, li, host=None):
    bsz, nblk, tq, d = x_il.shape
    nkb, gch, ks = bbc_r.shape
    kc = d // nkb
    n_rep = kc // gch
    nj = tq // 8
    h_in, h_shape, h_out, h_scr, h_ops = _host_args(host)

    def body(xf_ref, xk_ref, dy_ref, hr_ref, hi_ref, gk_ref, dk_ref, ar_ref, ai_ref, bbr_ref, bbi_ref, cr_ref, ci_ref,
             *rest):
        o0 = len(h_in)
        du_ref, dbbr_ref, dbbi_ref, dcr_ref, dci_ref, dar_ref, dai_ref, dd_ref = rest[o0:o0 + 8]
        s0 = o0 + 8 + len(h_out)
        (gr_sc, gi_sc, hr_sc, hi_sc, pw_r, pw_i, carry_r, carry_i, acc_r, acc_i,
         bbr_sc, bbi_sc, crt_sc, cit_sc, dbbr_acc, dbbi_acc, dcr_acc, dci_acc) = rest[s0:s0 + 18]
        k, b, n = pl.program_id(0), pl.program_id(1), pl.program_id(2)
        finish = _hosted(host, rest[:o0], rest[o0 + 8:s0], rest[s0 + 18:],
                         (k == 0) & (b == 0) & (n == 0), (k == nkb - 1) & (b == bsz - 1) & (n == nblk - 1))
        ar = jnp.broadcast_to(ar_ref[...], (8, ks))
        ai = jnp.broadcast_to(-ai_ref[...], (8, ks))

        @pl.when((b == 0) & (n == 0))
        def _():
            bbr_sc[...] = _expand_block_diag(bbr_ref[...], n_rep)
            bbi_sc[...] = _expand_block_diag(bbi_ref[...], n_rep)
            crt_sc[...] = _expand_block_diag(cr_ref[...], n_rep)
            cit_sc[...] = _expand_block_diag(ci_ref[...], n_rep)
            _fill_powers(pw_r, pw_i, ar, ai, nj, ks)
            for ref in (dar_ref, dai_ref, dd_ref, acc_r, acc_i, dbbr_acc, dbbi_acc, dcr_acc, dci_acc):
                ref[...] = jnp.zeros_like(ref)

        @pl.when(n == 0)
        def _():
            carry_r[...] = jnp.zeros_like(carry_r)
            carry_i[...] = jnp.zeros_like(carry_i)

        u = xk_ref[...] * _rms_scale(xf_ref[...]) * gk_ref[...]
        ub = u.astype(bf16)
        dyv = dy_ref[...]
        dyb = dyv.astype(bf16)
        dd_ref[0:1, :] += jnp.sum(dyv * u, axis=0, keepdims=True)
        hrb, hib = hr_ref[...], hi_ref[...]
        dcr_acc[...] += _dot_tn(dyb, hrb)
        dci_acc[...] += _dot_tn(dyb, hib)
        hr_sc[...] = hrb.astype(f32).reshape(nj, 8, ks)
        hi_sc[...] = hib.astype(f32).reshape(nj, 8, ks)
        gr_sc[...] = _dot(dyb, crt_sc[...]).reshape(nj, 8, ks)
        gi_sc[...] = _dot(dyb, cit_sc[...]).reshape(nj, 8, ks)
        accs = _interleaved_scan(gr_sc, gi_sc, ar, ai, pw_r, pw_i, carry_r, carry_i, nj, ks, reverse=True,
                                 h_sc=(hr_sc, hi_sc))
        cb = min(S5_CB, ks)
        for q, (a_r, a_i) in enumerate(accs):
            acc_r[:, q * cb:(q + 1) * cb] += a_r
            acc_i[:, q * cb:(q + 1) * cb] += a_i
        grb = gr_sc[...].reshape(tq, ks).astype(bf16)
        gib = gi_sc[...].reshape(tq, ks).astype(bf16)
        dbbr_acc[...] += _dot_tn(ub, grb)
        dbbi_acc[...] += _dot_tn(ub, gib)
        bbr, bbi, dk = bbr_sc[...], bbi_sc[...], dk_ref[...]
        for r0 in range(0, tq, tq // 2):
            rows = slice(r0, r0 + tq // 2)
            du_ref[rows, :] = dyv[rows, :] * dk + _dot_nt(grb[rows, :], bbr) + _dot_nt(gib[rows, :], bbi)

        @pl.when((b == bsz - 1) & (n == nblk - 1))
        def _():
            dar_ref[0:1, :] = jnp.sum(acc_r[...], axis=0, keepdims=True)
            dai_ref[0:1, :] = jnp.sum(acc_i[...], axis=0, keepdims=True)
            dbbr_ref[...] = _compact_block_diag(dbbr_acc[...], n_rep)
            dbbi_ref[...] = _compact_block_diag(dbbi_acc[...], n_rep)
            dcr_ref[...] = _compact_block_diag(dcr_acc[...], n_rep)
            dci_ref[...] = _compact_block_diag(dci_acc[...], n_rep)

        finish()

    full = pl.BlockSpec((None, None, tq, d), lambda k, b, n: (b, nblk - 1 - n, 0, 0))
    chan = pl.BlockSpec((None, None, tq, kc), lambda k, b, n: (b, nblk - 1 - n, 0, k))
    stat = pl.BlockSpec((None, None, tq, ks), lambda k, b, n: (b, nblk - 1 - n, 0, k))
    vec_c = pl.BlockSpec((1, kc), lambda k, b, n: (0, k))
    vec_s = pl.BlockSpec((1, ks), lambda k, b, n: (0, k))
    cmap = pl.BlockSpec((None, gch, ks), lambda k, b, n: (k, 0, 0))
    acc_s = pl.BlockSpec((None, 8, ks), lambda k, b, n: (k, 0, 0))
    acc_c = pl.BlockSpec((None, 8, kc), lambda k, b, n: (k, 0, 0))
    cshape = SDS((nkb, gch, ks), f32)
    return pl.pallas_call(
        body, name=f"s5_scan_bwd_{li}", grid=(nkb, bsz, nblk),
        out_shape=(SDS((bsz, nblk, tq, d), f32), cshape, cshape, cshape, cshape, SDS((nkb, 8, ks), f32),
                   SDS((nkb, 8, ks), f32), SDS((nkb, 8, kc), f32), *h_shape),
        in_specs=[full, chan, chan, stat, stat, vec_c, vec_c, vec_s, vec_s, cmap, cmap, cmap, cmap, *h_in],
        out_specs=(chan, cmap, cmap, cmap, cmap, acc_s, acc_s, acc_c, *h_out),
        scratch_shapes=([pltpu.VMEM((nj, 8, ks), f32)] * 6 + [pltpu.VMEM((8, ks), f32)] * 4
                        + [pltpu.VMEM((kc, ks), bf16)] * 4 + [pltpu.VMEM((kc, ks), f32)] * 4 + h_scr),
        compiler_params=_cparams(3, side_effects=bool(host)),
    )(x_il, x_il, dy, h_r, h_i, gain_pre, d_skip, ab_r, ab_i, bbc_r, bbc_i, cc_r, cc_in, *h_ops)


def _norm_residual_bwd(x, du, dxo, gain, name):
    t, d = x.shape
    tr = min(TOK_TR, t)

    def body(x_ref, du_ref, dxo_ref, g_ref, dx_ref, dg_ref):
        @pl.when(pl.program_id(0) == 0)
        def _():
            dg_ref[...] = jnp.zeros_like(dg_ref)

        dxp, dgain = _rms_bwd(x_ref[...], g_ref[...], du_ref[...])
        dx_ref[...] = dxo_ref[...] + dxp
        dg_ref[0:1, :] += dgain

    tok = pl.BlockSpec((tr, d), lambda i: (i, 0))
    return pl.pallas_call(
        body, name=name, grid=(t // tr,), out_shape=(SDS((t, d), f32), SDS((8, d), f32)),
        in_specs=[tok, tok, tok, pl.BlockSpec((1, d), lambda i: (0, 0))],
        out_specs=(tok, pl.BlockSpec((8, d), lambda i: (0, 0))), compiler_params=_cparams(1),
    )(x, du, dxo, gain)


def _loss_head(y, target):
    t, d = y.shape
    tr = min(TOK_TR, t)

    def body(y_ref, t_ref, l_ref, dy_ref):
        @pl.when(pl.program_id(0) == 0)
        def _():
            l_ref[...] = jnp.zeros_like(l_ref)

        err = y_ref[...] - t_ref[...]
        dy_ref[...] = err * (1.0 / d)
        l_ref[...] += jnp.sum(jnp.sum(err * err, axis=1, keepdims=True), axis=0, keepdims=True)

    tok = pl.BlockSpec((tr, d), lambda i: (i, 0))
    return pl.pallas_call(
        body, name="loss_head", grid=(t // tr,), out_shape=(SDS((8, 128), f32), SDS((t, d), f32)),
        in_specs=[tok, tok], out_specs=(pl.BlockSpec((8, 128), lambda i: (0, 0)), tok), compiler_params=_cparams(1),
    )(y, target)


def _interleave(a, tq):
    bsz, l, d = a.shape
    return a.reshape(bsz, l // tq, 8, tq // 8, d).transpose(0, 1, 3, 2, 4).reshape(bsz, l // tq, tq, d)


def _deinterleave(a, l):
    bsz, nblk, tq, d = a.shape
    return a.reshape(bsz, nblk, tq // 8, 8, d).transpose(0, 1, 3, 2, 4).reshape(bsz, l, d)


def _compact_maps(a, nkb):
    ns, g, h, p = a.shape
    gl = g // nkb
    return a.reshape(ns, nkb, gl, h, p).transpose(0, 1, 3, 2, 4).reshape(ns, nkb, h, gl * p)


def _uncompact_maps(a, g):
    nkb, h, cols = a.shape
    gl = g // nkb
    return a.reshape(nkb, h, gl, cols // gl).transpose(0, 2, 1, 3).reshape(g, h, cols // gl)


def _pack_rows(arrs, cols, row_mult):
    flat = jnp.concatenate([a.reshape(-1).astype(f32) for a in arrs])
    rows = -(-flat.shape[0] // cols)
    rows = -(-rows // row_mult) * row_mult
    return jnp.pad(flat, (0, rows * cols - flat.shape[0])).reshape(rows, cols)


def _unpack_rows(buf, shapes):
    flat = buf.reshape(-1)
    out, off = [], 0
    for s in shapes:
        n = math.prod(s)
        out.append(flat[off:off + n].reshape(s))
        off += n
    return out


W_NAMES = ['s5_lambda_re', 's5_lambda_im', 's5_log_dt', 's5_b_re', 's5_b_im', 's5_c_re', 's5_c_im', 's5_d', 's5_w_glu',
           's5_b_glu', 'pool_w', 'pool_scale', 'ffn_w_gate', 'ffn_w_val', 'ffn_conv_w', 'ffn_conv_b', 'ffn_w_down',
           'norm_mix_pre', 'norm_mix_post', 'norm_ffn_pre', 'norm_ffn_post']
BIG = ('s5_w_glu', 'ffn_w_gate', 'ffn_w_val', 'ffn_w_down')


def kernel(x, s5_lambda_re, s5_lambda_im, s5_log_dt, s5_b_re, s5_b_im, s5_c_re, s5_c_im, s5_d, s5_w_glu, s5_b_glu, pool_w, pool_scale, ffn_w_gate, ffn_w_val, ffn_conv_w, ffn_conv_b, ffn_w_down, norm_mix_pre, norm_mix_post, norm_ffn_pre, norm_ffn_post, loss_target, m_s5_lambda_re, m_s5_lambda_im, m_s5_log_dt, m_s5_b_re, m_s5_b_im, m_s5_c_re, m_s5_c_im, m_s5_d, m_s5_w_glu, m_s5_b_glu, m_pool_w, m_pool_scale, m_ffn_w_gate, m_ffn_w_val, m_ffn_conv_w, m_ffn_conv_b, m_ffn_w_down, m_norm_mix_pre, m_norm_mix_post, m_norm_ffn_pre, m_norm_ffn_post, v_s5_lambda_re, v_s5_lambda_im, v_s5_log_dt, v_s5_b_re, v_s5_b_im, v_s5_c_re, v_s5_c_im, v_s5_d, v_s5_w_glu, v_s5_b_glu, v_pool_w, v_pool_scale, v_ffn_w_gate, v_ffn_w_val, v_ffn_conv_w, v_ffn_conv_b, v_ffn_w_down, v_norm_mix_pre, v_norm_mix_post, v_norm_ffn_pre, v_norm_ffn_post):
    w_in = dict(zip(W_NAMES, (s5_lambda_re, s5_lambda_im, s5_log_dt, s5_b_re, s5_b_im, s5_c_re, s5_c_im, s5_d, s5_w_glu,
                              s5_b_glu, pool_w, pool_scale, ffn_w_gate, ffn_w_val, ffn_conv_w, ffn_conv_b, ffn_w_down,
                              norm_mix_pre, norm_mix_post, norm_ffn_pre, norm_ffn_post)))
    m_in = dict(zip(W_NAMES, (m_s5_lambda_re, m_s5_lambda_im, m_s5_log_dt, m_s5_b_re, m_s5_b_im, m_s5_c_re, m_s5_c_im,
                              m_s5_d, m_s5_w_glu, m_s5_b_glu, m_pool_w, m_pool_scale, m_ffn_w_gate, m_ffn_w_val,
                              m_ffn_conv_w, m_ffn_conv_b, m_ffn_w_down, m_norm_mix_pre, m_norm_mix_post,
                              m_norm_ffn_pre, m_norm_ffn_post)))
    v_in = dict(zip(W_NAMES, (v_s5_lambda_re, v_s5_lambda_im, v_s5_log_dt, v_s5_b_re, v_s5_b_im, v_s5_c_re, v_s5_c_im,
                              v_s5_d, v_s5_w_glu, v_s5_b_glu, v_pool_w, v_pool_scale, v_ffn_w_gate, v_ffn_w_val,
                              v_ffn_conv_w, v_ffn_conv_b, v_ffn_w_down, v_norm_mix_pre, v_norm_mix_post,
                              v_norm_ffn_pre, v_norm_ffn_post)))

    bsz, seq, d = x.shape
    depth = ffn_w_gate.shape[0]
    n_s5, n_grp, n_state = s5_lambda_re.shape
    n_gch = s5_b_re.shape[3]
    n_pool = pool_w.shape[0]
    fs = ffn_w_gate.shape[2]
    f = N_CHIPS * fs
    gs = s5_w_glu.shape[1]
    nkb = d // S5_KB_CH
    tq = min(S5_TQ, seq)
    chip = 2 * lax.axis_index("x") + lax.axis_index("y")

    s_ffn = jnp.stack([ffn_w_gate.transpose(0, 2, 1), ffn_w_val.transpose(0, 2, 1), ffn_w_down], axis=1).astype(bf16)
    small_shard = [pool_w, pool_scale, ffn_conv_w]
    first_hosts = [("gather", 1, s_ffn[0]), ("gather", 1, s5_w_glu.astype(bf16)),
                   ("whole", 0, _pack_rows(small_shard, d, 16))]

    lam_r4 = s5_lambda_re.reshape(n_s5, n_grp, 1, n_state)
    lam_i4 = s5_lambda_im.reshape(n_s5, n_grp, 1, n_state)
    ldt4 = s5_log_dt.reshape(n_s5, n_grp, 1, 1)
    b_r4 = s5_b_re.transpose(0, 1, 3, 2)
    b_i4 = s5_b_im.transpose(0, 1, 3, 2)
    ab_r4, ab_i4, bb_r4, bb_i4 = _s5_discretise(lam_r4, lam_i4, ldt4, b_r4, b_i4)
    n_st_tot = n_grp * n_state
    ab_r, ab_i = ab_r4.reshape(n_s5, 1, n_st_tot), ab_i4.reshape(n_s5, 1, n_st_tot)
    bbc_r, bbc_i = _compact_maps(bb_r4, nkb), _compact_maps(bb_i4, nkb)
    cc_r, cc_in = _compact_maps(s5_c_re, nkb), _compact_maps(-s5_c_im, nkb)

    def s5_params(j):
        return dict(ab_r=ab_r[j], ab_i=ab_i[j], bb_r=bbc_r[j], bb_i=bbc_i[j], c_r=cc_r[j], c_in=cc_in[j],
                    d_skip=s5_d[j].reshape(1, d), b_glu=s5_b_glu[j].reshape(1, d))

    def row(a, i):
        return a[i].reshape(1, -1)

    assert depth >= 1 and N_CHIPS == 4
    saved = []
    xc = x
    w_layer = [None] * depth
    for i in range(depth):
        j = i // 2
        sv = dict(x_in=xc)
        if i % 2 == 0:
            sp = s5_params(j)
            x_il = _interleave(xc, tq)
            res = _s5_scan_fwd(x_il, row(norm_mix_pre, i), sp["d_skip"], sp["ab_r"], sp["ab_i"], sp["bb_r"], sp["bb_i"],
                               sp["c_r"], sp["c_in"], i, host=first_hosts if i == 0 else None)
            y_il, h_r, h_i = res[:3]
            if i == 0:
                w_layer[0], g_glu, g_small = res[3:6]
                parts = [_unpack_rows(g_small[k], [a.shape for a in small_shard]) for k in range(N_CHIPS)]
                pool_w_full = jnp.concatenate([p[0] for p in parts], axis=2).astype(bf16)
                pool_scale_full = jnp.concatenate([p[1] for p in parts], axis=1)
                conv_w_full = jnp.concatenate([p[2] for p in parts], axis=2)
            sp["w_glu"] = g_glu[j]
            xo_il = _s5_glu_fwd(y_il.reshape(bsz * seq, d), x_il.reshape(bsz * seq, d), sp["w_glu"], sp["b_glu"],
                                row(norm_mix_post, i), i)
            x_mid = _deinterleave(xo_il.reshape(x_il.shape), seq)
            sv.update(sp=sp, x_il=x_il, y_il=y_il, h_r=h_r, h_i=h_i)
        else:
            x_mid = _pool_fwd(xc, row(norm_mix_pre, i), row(norm_mix_post, i), pool_w_full[j], row(pool_scale_full, j), i)
        res = _ffn_fwd(x_mid, row(norm_ffn_pre, i), row(norm_ffn_post, i), w_layer[i], conv_w_full, ffn_conv_b, i,
                       host=[("gather", 1, s_ffn[i + 1])] if i + 1 < depth else None)
        xc, g_sv, v_sv, f_sv = res[:4]
        if i + 1 < depth:
            w_layer[i + 1] = res[4]
        sv.update(x_mid=x_mid, g=g_sv, v=v_sv, f=f_sv)
        saved.append(sv)

    sq, dy = _loss_head(xc.reshape(bsz * seq, d), loss_target.reshape(bsz * seq, d))
    loss = lax.psum(sq[0, 0] * (0.5 / d), ("x", "y", "c"))

    dx = dy.reshape(bsz, seq, d)
    dw_layers = [None] * depth
    r_layers = [None] * depth
    g_small_params = {n: [None] * w_in[n].shape[0] for n in W_NAMES if n not in BIG}
    dglu = [None] * n_s5
    for i in range(depth - 1, -1, -1):
        j = i // 2
        sv = saved[i]
        res = _ffn_bwd_act(sv["x_mid"], sv["f"], sv["g"], sv["v"], dx, row(norm_ffn_pre, i), row(norm_ffn_post, i),
                           w_layer[i], conv_w_full, ffn_conv_b, i,
                           host=[("slab", 1, dw_layers[i + 1])] if i + 1 < depth else None)
        dx, dg, dv, hdn, hb, dfb, dgain, dconv = res[:8]
        if i + 1 < depth:
            r_layers[i + 1] = res[8]
        dw_layers[i] = _ffn_bwd_weights(dg, dv, hdn, hb.reshape(bsz * seq, d), dfb.reshape(bsz * seq, d), i)
        g_small_params["norm_ffn_pre"][i] = dgain[0]
        g_small_params["norm_ffn_post"][i] = dgain[1]
        dconv = dconv.transpose(1, 0, 2).reshape(8, f)
        g_small_params["ffn_conv_w"][i] = dconv[0:3]
        g_small_params["ffn_conv_b"][i] = dconv[3]
        if i % 2 == 0:
            sp = sv["sp"]
            dxo_il = _interleave(dx, tq)
            dy_s, dglu[j], ds_glu = _s5_glu_bwd(sv["y_il"].reshape(bsz * seq, d), dxo_il.reshape(bsz * seq, d),
                                                sp["w_glu"], sp["b_glu"], row(norm_mix_post, i), i)
            hosts = None
            if i == 0:
                assert all(g is not None for g in dglu) and all(g is not None for g in g_small_params["pool_w"])
                hosts = [("slab", 1, dw_layers[0]), ("slab", 1, jnp.stack(dglu).astype(bf16)),
                         ("slab", 2, jnp.stack(g_small_params["pool_w"]))]
            res = _s5_scan_bwd(sv["x_il"], dy_s.reshape(sv["x_il"].shape), sv["h_r"], sv["h_i"], row(norm_mix_pre, i),
                               sp["d_skip"], sp["ab_r"], sp["ab_i"], sp["bb_r"], sp["bb_i"], sp["c_r"], sp["c_in"], i,
                               host=hosts)
            du, dbb_r, dbb_i, dc_r, dc_in, dab_r, dab_i, dd = res[:8]
            if i == 0:
                r_layers[0], r_glu, r_pw = res[8:11]
            dx_il, dgpre = _norm_residual_bwd(sv["x_il"].reshape(bsz * seq, d), du.reshape(bsz * seq, d),
                                              dxo_il.reshape(bsz * seq, d), row(norm_mix_pre, i), f"s5_pre_bwd_{i}")
            dx = _deinterleave(dx_il.reshape(sv["x_il"].shape), seq)
            g_small_params["norm_mix_pre"][i] = dgpre[0]
            g_small_params["norm_mix_post"][i] = ds_glu[0]
            g_small_params["s5_b_glu"][j] = ds_glu[1]
            g_small_params["s5_d"][j] = dd[:, 0, :].reshape(d)
            g_small_params["s5_c_re"][j] = _uncompact_maps(dc_r, n_grp)
            g_small_params["s5_c_im"][j] = -_uncompact_maps(dc_in, n_grp)
            sv["g_ab"] = (dab_r[:, 0, :].reshape(n_grp, 1, n_state), dab_i[:, 0, :].reshape(n_grp, 1, n_state),
                          _uncompact_maps(dbb_r, n_grp), _uncompact_maps(dbb_i, n_grp))
        else:
            dx, dpw, dsm = _pool_bwd(sv["x_in"], dx, row(norm_mix_pre, i), row(norm_mix_post, i), pool_w_full[j],
                                     row(pool_scale_full, j), i)
            g_small_params["norm_mix_pre"][i] = dsm[0]
            g_small_params["norm_mix_post"][i] = dsm[1]
            g_small_params["pool_scale"][j] = dsm[2]
            g_small_params["pool_w"][j] = dpw
    grad_x = dx

    g_ab = [saved[2 * j]["g_ab"] for j in range(n_s5)]
    d_lr, d_li, d_ld, d_br, d_bi = _s5_discretise_bwd(
        lam_r4, lam_i4, ldt4, b_r4, b_i4, jnp.stack([g[0] for g in g_ab]), jnp.stack([g[1] for g in g_ab]),
        jnp.stack([g[2] for g in g_ab]), jnp.stack([g[3] for g in g_ab]))
    small_full = {n: (jnp.stack(v) if v[0] is not None else None) for n, v in g_small_params.items()}
    small_full["s5_lambda_re"] = d_lr.reshape(n_s5, n_grp, n_state)
    small_full["s5_lambda_im"] = d_li.reshape(n_s5, n_grp, n_state)
    small_full["s5_log_dt"] = d_ld.reshape(n_s5, n_grp)
    small_full["s5_b_re"] = d_br.transpose(0, 1, 3, 2)
    small_full["s5_b_im"] = d_bi.transpose(0, 1, 3, 2)
    bc_names = ["s5_b_re", "s5_b_im", "s5_c_re", "s5_c_im"]
    small_names = [n for n in W_NAMES if n not in BIG and n not in bc_names and n != "pool_w"]
    q = _pack_rows([small_full[n] for n in small_names], d, 16)
    q_bc = _pack_rows([small_full[n] for n in bc_names], d, 16).astype(bf16)

    p_ffn = _sum_slots_layers([r.reshape(N_CHIPS, 3 * fs, d) for r in r_layers], "sum_slots_ffn").reshape(depth * 3 * fs, d)
    p_glu = _sum_slots(r_glu.reshape(N_CHIPS, n_s5 * gs, d), "sum_slots_glu")
    pw_rows, pw_cols = math.prod(pool_w.shape[:3]), pool_w.shape[3]
    p_pw = _sum_slots(r_pw.reshape(N_CHIPS, pw_rows, pw_cols), "sum_slots_pool_w")
    r_q, r_bc, o_ffn, o_glu, o_pw = _chip_exchange([("whole", 0, q), ("whole", 0, q_bc)], "exchange_small_grads",
                                                   to_sibling=[p_ffn, p_glu, p_pw])
    p_q = _sum_slots(r_q, "sum_slots_small")
    p_bc = _sum_slots(r_bc, "sum_slots_bc")
    o_q, o_bc = _exchange_with_sibling([p_q, p_bc])

    outs = {}

    def put(name, res, shape):
        outs[name] = tuple(r.reshape(shape) for r in res)

    pf, of = p_ffn.reshape(depth, 3, fs, d), o_ffn.reshape(depth, 3, fs, d)
    for kind, name in ((0, "ffn_w_gate"), (1, "ffn_w_val")):
        parts_g = [a[:, kind].transpose(0, 2, 1).reshape(depth * d, fs) for a in (pf, of)]
        put(name, _adamw(parts_g, w_in[name].reshape(depth * d, fs), m_in[name].reshape(depth * d, fs),
                         v_in[name].reshape(depth * d, fs), f"adamw_{name}"), w_in[name].shape)
    put("ffn_w_down", _adamw([pf[:, 2].reshape(depth * fs, d), of[:, 2].reshape(depth * fs, d)],
                             ffn_w_down.reshape(depth * fs, d), m_ffn_w_down.reshape(depth * fs, d),
                             v_ffn_w_down.reshape(depth * fs, d), "adamw_ffn_w_down"), ffn_w_down.shape)
    put("s5_w_glu", _adamw([p_glu, o_glu], s5_w_glu.reshape(n_s5 * gs, d), m_s5_w_glu.reshape(n_s5 * gs, d),
                           v_s5_w_glu.reshape(n_s5 * gs, d), "adamw_s5_w_glu"), s5_w_glu.shape)

    q_tot, bc_tot, pw_tot = _add_pairs([(p_q, o_q), (p_bc, o_bc), (p_pw, o_pw)], "sum_small")
    g_small = dict(zip(small_names, _unpack_rows(q_tot, [small_full[n].shape for n in small_names])))
    g_small.update(zip(bc_names, _unpack_rows(bc_tot, [small_full[n].shape for n in bc_names])))
    g_small["pool_w"] = pw_tot.reshape(pool_w.shape)
    g_small["pool_scale"] = lax.dynamic_slice_in_dim(g_small["pool_scale"], chip * pool_scale.shape[1],
                                                     pool_scale.shape[1], axis=1)
    g_small["ffn_conv_w"] = lax.dynamic_slice_in_dim(g_small["ffn_conv_w"], chip * fs, fs, axis=2)
    all_small = [n for n in W_NAMES if n not in BIG]
    local_shapes = [w_in[n].shape for n in all_small]
    res = _adamw([_pack_rows([g_small[n] for n in all_small], d, 64)],
                 _pack_rows([w_in[n] for n in all_small], d, 64), _pack_rows([m_in[n] for n in all_small], d, 64),
                 _pack_rows([v_in[n] for n in all_small], d, 64), "adamw_small")
    unpacked = [_unpack_rows(r, local_shapes) for r in res]
    for idx, n in enumerate(all_small):
        outs[n] = (g_small[n], unpacked[1][idx], unpacked[2][idx], unpacked[3][idx])

    return (loss, grad_x, *[outs[n][0] for n in W_NAMES], *[outs[n][1] for n in W_NAMES],
            *[outs[n][2] for n in W_NAMES], *[outs[n][3] for n in W_NAMES])
```
